```python
import math
import jax, jax.numpy as jnp
from jax import lax
import numpy as np

D_MODEL = 1024
BATCH = 8
SEQ = 4096
DEPTH = 1

N_MEM = 256
ROPE_THETA = 500000.0
Q_BLOCK = 128
NEG_INF = -1e30
LN_EPS = 1e-5
RMS_EPS = 1e-6

DIFF_HEADS = 4
DIFF_QK_DIM = 64
DIFF_V_DIM = 128
DIFF_ROT_DIM = DIFF_QK_DIM // 4
MLA_HEADS = 4
MLA_Q_RANK = 256
MLA_KV_RANK = 128
MLA_NOPE_DIM = 128
MLA_ROPE_DIM = 64
MLA_V_DIM = 128
MIX_WIDTH = DIFF_HEADS * DIFF_V_DIM + MLA_HEADS * MLA_V_DIM
DIFF_Q_COLS = DIFF_HEADS * 2 * DIFF_QK_DIM
DIFF_K_COLS = DIFF_HEADS * 2 * DIFF_QK_DIM
DIFF_V_COLS = DIFF_HEADS * DIFF_V_DIM
IN_COLS = DIFF_Q_COLS + DIFF_K_COLS + DIFF_V_COLS + MLA_Q_RANK + MLA_KV_RANK + MLA_ROPE_DIM
IN_SPLITS = (DIFF_Q_COLS,
             DIFF_Q_COLS + DIFF_K_COLS,
             DIFF_Q_COLS + DIFF_K_COLS + DIFF_V_COLS,
             DIFF_Q_COLS + DIFF_K_COLS + DIFF_V_COLS + MLA_Q_RANK,
             DIFF_Q_COLS + DIFF_K_COLS + DIFF_V_COLS + MLA_Q_RANK + MLA_KV_RANK)
MEM_HEADS = 4
MEM_HEAD_DIM = 128
N_EXPERTS = 64
TOP_K = 8
N_GROUPS = 8
TOPK_GROUPS = 4
EXPERT_DIM = 256
SHARED_DIM = 256
ROUTED_SCALE = 2.5
DEEPNORM_ALPHA = (2.0 * DEPTH) ** 0.25
DEEPNORM_BETA = (8.0 * DEPTH) ** -0.25

kernel_name = 'hybrid_diffattn_mla_moe_deepnorm'


def layer_norm(x, g, b):
    xf = x.astype(jnp.float32)
    mu = jnp.mean(xf, axis=-1, keepdims=True)
    var = jnp.mean(jnp.square(xf - mu), axis=-1, keepdims=True)
    return ((xf - mu) * lax.rsqrt(var + LN_EPS)).astype(x.dtype) * g + b


def rms_norm(x, g):
    xf = x.astype(jnp.float32)
    return (xf * lax.rsqrt(jnp.mean(jnp.square(xf), axis=-1, keepdims=True) + RMS_EPS)).astype(x.dtype) * g


def rope_tables(positions, dim):
    inv_freq = ROPE_THETA ** (-jnp.arange(0, dim, 2, dtype=jnp.float32) / dim)
    ang = positions.astype(jnp.float32)[..., None] * inv_freq
    return jnp.cos(ang)[:, :, None, :], jnp.sin(ang)[:, :, None, :]


def apply_rope(x, cos, sin):
    x1, x2 = jnp.split(x.astype(jnp.float32), 2, axis=-1)
    return jnp.concatenate([x1 * cos - x2 * sin, x2 * cos + x1 * sin], axis=-1).astype(x.dtype)


def partial_rope(x, cos, sin, rot_dim):
    return jnp.concatenate([apply_rope(x[..., :rot_dim], cos, sin), x[..., rot_dim:]], axis=-1)


def causal_attention(q, k, v, scale):
    B, S, H, Dk = q.shape
    Dv = v.shape[-1]
    nb = S // Q_BLOCK
    q_blocks = q.reshape(B, nb, Q_BLOCK, H, Dk).transpose(1, 0, 2, 3, 4)
    key_idx = jnp.arange(S)

    def one_block(args):
        qb, b_idx = args
        s = jnp.einsum('bqhd,bkhd->bhqk', qb, k, preferred_element_type=jnp.float32) * scale
        q_idx = b_idx * Q_BLOCK + jnp.arange(Q_BLOCK)
        s = jnp.where(key_idx[None, :] <= q_idx[:, None], s, NEG_INF)
        p = jax.nn.softmax(s, axis=-1).astype(v.dtype)
        return jnp.einsum('bhqk,bkhd->bqhd', p, v)

    out = lax.map(one_block, (q_blocks, jnp.arange(nb)))
    return out.transpose(1, 0, 2, 3, 4).reshape(B, S, H, Dv)


def hybrid_mixer(x, cos_d, sin_d, cos_m, sin_m, layer_idx, w_in, lam_q1, lam_k1, lam_q2, lam_k2,
                 diff_subln_g, mla_q_norm_g, mla_w_uq, mla_kv_norm_g, mla_w_ukv, mla_out_norm_g, w_o):
    B, S, _ = x.shape
    proj = jnp.einsum('bsd,dc->bsc', x, w_in)
    d_q, d_k, d_v, c_q, c_kv, k_rope = jnp.split(proj, IN_SPLITS, axis=-1)

    d_q = partial_rope(d_q.reshape(B, S, 2 * DIFF_HEADS, DIFF_QK_DIM), cos_d, sin_d, DIFF_ROT_DIM)
    d_k = partial_rope(d_k.reshape(B, S, 2 * DIFF_HEADS, DIFF_QK_DIM), cos_d, sin_d, DIFF_ROT_DIM)
    d_q = d_q.reshape(B, S, DIFF_HEADS, 2, DIFF_QK_DIM)
    d_k = d_k.reshape(B, S, DIFF_HEADS, 2, DIFF_QK_DIM)
    d_v = d_v.reshape(B, S, DIFF_HEADS, DIFF_V_DIM)
    lam_init = 0.8 - 0.6 * math.exp(-0.3 * layer_idx)
    lam = (jnp.exp(jnp.sum(lam_q1.astype(jnp.float32) * lam_k1.astype(jnp.float32)))
           - jnp.exp(jnp.sum(lam_q2.astype(jnp.float32) * lam_k2.astype(jnp.float32))) + lam_init)
    scale_d = DIFF_QK_DIM ** -0.5
    a1 = causal_attention(d_q[:, :, :, 0], d_k[:, :, :, 0], d_v, scale_d)
    a2 = causal_attention(d_q[:, :, :, 1], d_k[:, :, :, 1], d_v, scale_d)
    diff_out = rms_norm(a1 - lam.astype(a1.dtype) * a2, diff_subln_g) * (1.0 - lam_init)

    q = jnp.einsum('bsr,rc->bsc', rms_norm(c_q, mla_q_norm_g), mla_w_uq)
    q = q.reshape(B, S, MLA_HEADS, MLA_NOPE_DIM + MLA_ROPE_DIM)
    q_nope, q_pe = jnp.split(q, [MLA_NOPE_DIM], axis=-1)
    kv = jnp.einsum('bsr,rc->bsc', rms_norm(c_kv, mla_kv_norm_g), mla_w_ukv)
    kv = kv.reshape(B, S, MLA_HEADS, MLA_NOPE_DIM + MLA_V_DIM)
    k_nope, v_m = jnp.split(kv, [MLA_NOPE_DIM], axis=-1)
    k_pe = apply_rope(k_rope[:, :, None, :], cos_m, sin_m)
    q_m = jnp.concatenate([q_nope, apply_rope(q_pe, cos_m, sin_m)], axis=-1)
    k_m = jnp.concatenate([k_nope, jnp.broadcast_to(k_pe, (B, S, MLA_HEADS, MLA_ROPE_DIM))], axis=-1)
    mla_out = causal_attention(q_m, k_m, v_m, (MLA_NOPE_DIM + MLA_ROPE_DIM) ** -0.5)
    mla_out = rms_norm(mla_out, mla_out_norm_g)

    mixed = jnp.concatenate([diff_out.reshape(B, S, DIFF_HEADS * DIFF_V_DIM),
                             mla_out.reshape(B, S, MLA_HEADS * MLA_V_DIM)], axis=-1)
    return jnp.einsum('bsc,cd->bsd', mixed, w_o)


def memory_attention(x, mem, w_q, w_k, w_v, w_o):
    B, S, _ = x.shape
    q = jnp.einsum('bsd,dc->bsc', x, w_q).reshape(B, S, MEM_HEADS, MEM_HEAD_DIM)
    k = jnp.einsum('bmd,dc->bmc', mem, w_k).reshape(B, N_MEM, MEM_HEADS, MEM_HEAD_DIM)
    v = jnp.einsum('bmd,dc->bmc', mem, w_v).reshape(B, N_MEM, MEM_HEADS, MEM_HEAD_DIM)
    s = jnp.einsum('bshd,bmhd->bhsm', q, k, preferred_element_type=jnp.float32) * MEM_HEAD_DIM ** -0.5
    p = jax.nn.softmax(s, axis=-1).astype(v.dtype)
    o = jnp.einsum('bhsm,bmhd->bshd', p, v).reshape(B, S, MEM_HEADS * MEM_HEAD_DIM)
    return jnp.einsum('bsc,cd->bsd', o, w_o)


def moe_ffn(x, router_w, router_bias, w_gate, w_up, w_down, sh_gate, sh_up, sh_down):
    B, S, D = x.shape
    t = x.reshape(B * S, D)
    T = t.shape[0]
    scores = jax.nn.sigmoid(jnp.dot(t, router_w, preferred_element_type=jnp.float32))
    choice = scores + router_bias.astype(jnp.float32)
    group_score = lax.top_k(choice.reshape(T, N_GROUPS, N_EXPERTS // N_GROUPS), 2)[0].sum(-1)
    _, top_groups = lax.top_k(group_score, TOPK_GROUPS)
    group_mask = jnp.sum(jax.nn.one_hot(top_groups, N_GROUPS, dtype=jnp.float32), axis=1) > 0
    expert_mask = jnp.repeat(group_mask, N_EXPERTS // N_GROUPS, axis=-1)
    _, top_idx = lax.top_k(jnp.where(expert_mask, choice, NEG_INF), TOP_K)
    top_w = jnp.take_along_axis(scores, top_idx, axis=-1)
    top_w = top_w / jnp.sum(top_w, axis=-1, keepdims=True) * ROUTED_SCALE
    gates = jnp.zeros((T, N_EXPERTS), jnp.float32).at[jnp.arange(T)[:, None], top_idx].set(top_w)
    gates = gates.astype(t.dtype)
    out = jnp.dot(jax.nn.silu(t @ sh_gate) * (t @ sh_up), sh_down)
    for e in range(N_EXPERTS):
        h = jax.nn.silu(t @ w_gate[e]) * (t @ w_up[e])
        out = out + gates[:, e:e + 1] * (h @ w_down[e])
    return out.reshape(B, S, D)


def setup_inputs(seed: int = 0) -> dict:
    key = jax.random.key(seed)
    ks = iter(jax.random.split(key, 40))
    L = DEPTH

    def nrm(shape):
        return jax.random.normal(next(ks), shape, jnp.float32)

    def w(shape, fan_in, scale=1.0):
        return nrm(shape) * (scale * fan_in ** -0.5)

    def gain(shape):
        return 1.0 + 0.02 * nrm(shape)

    x = nrm((BATCH, SEQ, D_MODEL))
    mem = nrm((BATCH, N_MEM, D_MODEL))
    offset = jax.random.randint(next(ks), (BATCH, 1), 0, 1024, dtype=jnp.int32)
    positions = (offset + jnp.arange(SEQ, dtype=jnp.int32)[None, :]).astype(jnp.int32)
    return {
        'x': x,
        'mem': mem,
        'positions': positions,
        'w_in': w((L, D_MODEL, IN_COLS), D_MODEL),
        'lam_q1': 0.1 * nrm((L, DIFF_QK_DIM)),
        'lam_k1': 0.1 * nrm((L, DIFF_QK_DIM)),
        'lam_q2': 0.1 * nrm((L, DIFF_QK_DIM)),
        'lam_k2': 0.1 * nrm((L, DIFF_QK_DIM)),
        'diff_subln_g': gain((L, DIFF_V_DIM)),
        'mla_q_norm_g': gain((L, MLA_Q_RANK)),
        'mla_w_uq': w((L, MLA_Q_RANK, MLA_HEADS * (MLA_NOPE_DIM + MLA_ROPE_DIM)), MLA_Q_RANK),
        'mla_kv_norm_g': gain((L, MLA_KV_RANK)),
        'mla_w_ukv': w((L, MLA_KV_RANK, MLA_HEADS * (MLA_NOPE_DIM + MLA_V_DIM)), MLA_KV_RANK),
        'mla_out_norm_g': gain((L, MLA_V_DIM)),
        'w_o': w((L, MIX_WIDTH, D_MODEL), MIX_WIDTH, DEEPNORM_BETA),
        'ln1_g': gain((L, D_MODEL)),
        'ln1_b': 0.02 * nrm((L, D_MODEL)),
        'mem_w_q': w((L, D_MODEL, MEM_HEADS * MEM_HEAD_DIM), D_MODEL),
        'mem_w_k': w((L, D_MODEL, MEM_HEADS * MEM_HEAD_DIM), D_MODEL),
        'mem_w_v': w((L, D_MODEL, MEM_HEADS * MEM_HEAD_DIM), D_MODEL),
        'mem_w_o': w((L, MEM_HEADS * MEM_HEAD_DIM, D_MODEL), MEM_HEADS * MEM_HEAD_DIM, DEEPNORM_BETA),
        'ln2_g': gain((L, D_MODEL)),
        'ln2_b': 0.02 * nrm((L, D_MODEL)),
        'router_w': w((L, D_MODEL, N_EXPERTS), D_MODEL),
        'router_bias': 0.01 * nrm((L, N_EXPERTS)),
        'exp_w_gate': w((L, N_EXPERTS, D_MODEL, EXPERT_DIM), D_MODEL),
        'exp_w_up': w((L, N_EXPERTS, D_MODEL, EXPERT_DIM), D_MODEL),
        'exp_w_down': w((L, N_EXPERTS, EXPERT_DIM, D_MODEL), EXPERT_DIM, DEEPNORM_BETA),
        'sh_w_gate': w((L, D_MODEL, SHARED_DIM), D_MODEL),
        'sh_w_up': w((L, D_MODEL, SHARED_DIM), D_MODEL),
        'sh_w_down': w((L, SHARED_DIM, D_MODEL), SHARED_DIM, DEEPNORM_BETA),
        'ln3_g': gain((L, D_MODEL)),
        'ln3_b': 0.02 * nrm((L, D_MODEL)),
    }


def reference(x, mem, positions, w_in, lam_q1, lam_k1, lam_q2, lam_k2, diff_subln_g,
              mla_q_norm_g, mla_w_uq, mla_kv_norm_g, mla_w_ukv, mla_out_norm_g, w_o, ln1_g, ln1_b,
              mem_w_q, mem_w_k, mem_w_v, mem_w_o, ln2_g, ln2_b,
              router_w, router_bias, exp_w_gate, exp_w_up, exp_w_down,
              sh_w_gate, sh_w_up, sh_w_down, ln3_g, ln3_b):
    cos_d, sin_d = rope_tables(positions, DIFF_ROT_DIM)
    cos_m, sin_m = rope_tables(positions, MLA_ROPE_DIM)
    h = x
    for l in range(DEPTH):
        mix = hybrid_mixer(h, cos_d, sin_d, cos_m, sin_m, l, w_in[l], lam_q1[l], lam_k1[l], lam_q2[l],
                           lam_k2[l], diff_subln_g[l], mla_q_norm_g[l], mla_w_uq[l], mla_kv_norm_g[l],
                           mla_w_ukv[l], mla_out_norm_g[l], w_o[l])
        h = layer_norm(DEEPNORM_ALPHA * h + mix, ln1_g[l], ln1_b[l])
        xat = memory_attention(h, mem, mem_w_q[l], mem_w_k[l], mem_w_v[l], mem_w_o[l])
        h = layer_norm(DEEPNORM_ALPHA * h + xat, ln2_g[l], ln2_b[l])
        ffn = moe_ffn(h, router_w[l], router_bias[l], exp_w_gate[l], exp_w_up[l], exp_w_down[l],
                      sh_w_gate[l], sh_w_up[l], sh_w_down[l])
        h = layer_norm(DEEPNORM_ALPHA * h + ffn, ln3_g[l], ln3_b[l])
    return h
```

```python
import functools
import math

import jax
import jax.numpy as jnp
from jax import lax
from jax.experimental import pallas as pl
from jax.experimental.pallas import tpu as pltpu

F32 = jnp.float32
BF16 = jnp.bfloat16

D_MODEL = 1024
N_MEM = 256
ROPE_THETA = 500000.0
NEG_INF = -1e30
LN_EPS = 1e-5
RMS_EPS = 1e-6
DIFF_HEADS = 4
DIFF_QK_DIM = 64
DIFF_V_DIM = 128
DIFF_ROT_DIM = DIFF_QK_DIM // 4
MLA_HEADS = 4
MLA_Q_RANK = 256
MLA_KV_RANK = 128
MLA_NOPE_DIM = 128
MLA_ROPE_DIM = 64
MLA_V_DIM = 128
MLA_QK_PAD = 256
MEM_HEADS = 4
MEM_HEAD_DIM = 128
N_EXPERTS = 64
TOP_K = 8
N_GROUPS = 8
TOPK_GROUPS = 4
GROUP_SIZE = N_EXPERTS // N_GROUPS
EXPERT_DIM = 256
SHARED_DIM = 256
ROUTED_SCALE = 2.5
DEPTH = 1
DEEPNORM_ALPHA = (2.0 * DEPTH) ** 0.25

V7X_LANES = 128
V7X_VMEM_LIMIT_BYTES = 56 * 1024 * 1024


def _cparams(sem):
    return pltpu.CompilerParams(dimension_semantics=sem, vmem_limit_bytes=V7X_VMEM_LIMIT_BYTES)


def _dot(a, b):
    return jnp.dot(a, b, preferred_element_type=F32)


def _dot_nt(a, b):
    return lax.dot_general(a, b, (((1,), (1,)), ((), ())), preferred_element_type=F32)


def _rms(x, g):
    return x * lax.rsqrt(jnp.mean(jnp.square(x), axis=-1, keepdims=True) + RMS_EPS) * g


def _layer_norm(x, g, b):
    mu = jnp.mean(x, axis=-1, keepdims=True)
    xc = x - mu
    var = jnp.mean(jnp.square(xc), axis=-1, keepdims=True)
    return xc * lax.rsqrt(var + LN_EPS) * g + b


def _rope(x, c, s1, s2, shift):
    w = x.shape[1]
    reps = w // V7X_LANES
    if reps > 1:
        c = jnp.concatenate([c] * reps, axis=1)
        s1 = jnp.concatenate([s1] * reps, axis=1)
        s2 = jnp.concatenate([s2] * reps, axis=1)
    return x * c + pltpu.roll(x, w - shift, 1) * s1 + pltpu.roll(x, shift, 1) * s2


def _proj_kernel(x_ref, pos_ref, invf_ref, w_in_ref, gq_ref, wuq_ref, gkv_ref, wukv_ref,
                 dq_ref, dk_ref, dv_ref, mq_ref, mk_ref, mv_ref):
    tm = x_ref.shape[0]
    xb = x_ref[...].astype(BF16)

    ang = pos_ref[...].astype(F32) * invf_ref[...]
    cosv = jnp.cos(ang)
    sinv = jnp.sin(ang)
    lane = lax.broadcasted_iota(jnp.int32, (tm, V7X_LANES), 1)
    half_m = MLA_ROPE_DIM // 2
    c_m = jnp.where(lane < MLA_ROPE_DIM, cosv, 0.0)
    s1_m = jnp.where(lane < half_m, -sinv, 0.0)
    s2_m = jnp.where((lane >= half_m) & (lane < MLA_ROPE_DIM), sinv, 0.0)
    cos_d = jnp.where(lane < 64, pltpu.roll(cosv, 64, 1), cosv)
    sin_d = jnp.where(lane < 64, pltpu.roll(sinv, 64, 1), sinv)
    l64 = lane & (DIFF_QK_DIM - 1)
    half_d = DIFF_ROT_DIM // 2
    s1_d = jnp.where(l64 < half_d, -sin_d, 0.0)
    s2_d = jnp.where((l64 >= half_d) & (l64 < DIFF_ROT_DIM), sin_d, 0.0)

    n_dq = DIFF_HEADS * 2 * DIFF_QK_DIM
    n_dv = DIFF_HEADS * DIFF_V_DIM
    o_dk = n_dq
    o_dv = 2 * n_dq
    o_cq = o_dv + n_dv
    o_ckv = o_cq + MLA_Q_RANK
    o_kr = o_ckv + MLA_KV_RANK

    dq = _dot(xb, w_in_ref[:, 0:n_dq])
    dq_ref[...] = (_rope(dq, cos_d, s1_d, s2_d, half_d) * (DIFF_QK_DIM ** -0.5)).astype(BF16)
    dk = _dot(xb, w_in_ref[:, o_dk:o_dk + n_dq])
    dk_ref[...] = _rope(dk, cos_d, s1_d, s2_d, half_d).astype(BF16)
    dv_ref[...] = _dot(xb, w_in_ref[:, o_dv:o_dv + n_dv]).astype(BF16)

    c_q = _dot(xb, w_in_ref[:, o_cq:o_cq + MLA_Q_RANK])
    q = _dot(_rms(c_q, gq_ref[...]).astype(BF16), wuq_ref[...])
    q = q * ((MLA_NOPE_DIM + MLA_ROPE_DIM) ** -0.5)
    for h in range(MLA_HEADS):
        o = h * MLA_QK_PAD
        mq_ref[:, o:o + MLA_NOPE_DIM] = q[:, o:o + MLA_NOPE_DIM].astype(BF16)
        pe = _rope(q[:, o + MLA_NOPE_DIM:o + MLA_QK_PAD], c_m, s1_m, s2_m, half_m)
        mq_ref[:, o + MLA_NOPE_DIM:o + MLA_QK_PAD] = pe.astype(BF16)

    c_kv = _dot(xb, w_in_ref[:, o_ckv:o_ckv + MLA_KV_RANK])
    kv = _dot(_rms(c_kv, gkv_ref[...]).astype(BF16), wukv_ref[...])
    k_rope = _dot(xb, w_in_ref[:, o_kr:o_kr + V7X_LANES])
    k_pe = _rope(k_rope, c_m, s1_m, s2_m, half_m).astype(BF16)
    for h in range(MLA_HEADS):
        o = h * MLA_QK_PAD
        mk_ref[:, o:o + MLA_NOPE_DIM] = kv[:, h * MLA_NOPE_DIM:(h + 1) * MLA_NOPE_DIM].astype(BF16)
        mk_ref[:, o + MLA_NOPE_DIM:o + MLA_QK_PAD] = k_pe
    mv_ref[...] = kv[:, MLA_HEADS * MLA_NOPE_DIM:].astype(BF16)


def _proj_call(x2, pos2, invf, w_in_r, gq, wuq_r, gkv, wukv_r, tm):
    t = x2.shape[0]
    n_in = w_in_r.shape[1]
    row = lambda w: pl.BlockSpec((tm, w), lambda i: (i, 0))
    full = lambda a: pl.BlockSpec(a.shape, lambda i: (0, 0))
    outs = [(t, 512), (t, 512), (t, 512), (t, MLA_HEADS * MLA_QK_PAD), (t, MLA_HEADS * MLA_QK_PAD), (t, 512)]
    return pl.pallas_call(
        _proj_kernel,
        out_shape=[jax.ShapeDtypeStruct(s, BF16) for s in outs],
        grid=(t // tm,),
        in_specs=[row(D_MODEL), row(1), full(invf), full(w_in_r), full(gq), full(wuq_r), full(gkv), full(wukv_r)],
        out_specs=[row(s[1]) for s in outs],
        compiler_params=_cparams(("parallel",)),
        name="proj_rope",
    )(x2, pos2, invf, w_in_r, gq, wuq_r, gkv, wukv_r)


def _flash(qq, k_ref, v_ref, q_start, tq, tk):
    m_rows = qq.shape[0]
    dv = v_ref.shape[1]
    row_pos = q_start + (lax.broadcasted_iota(jnp.int32, (m_rows, tk), 0) % tq)
    col = lax.broadcasted_iota(jnp.int32, (m_rows, tk), 1)

    def step(j, carry, masked):
        m, l, acc = carry
        off = pl.multiple_of(j * tk, tk)
        kj = k_ref[pl.ds(off, tk), :]
        vj = v_ref[pl.ds(off, tk), :]
        s = _dot_nt(qq, kj)
        if masked:
            s = jnp.where(col + off <= row_pos, s, NEG_INF)
        m_new = jnp.maximum(m, jnp.max(s, axis=1, keepdims=True))
        alpha = jnp.exp(m - m_new)
        p = jnp.exp(s - m_new)
        l = alpha * l + jnp.sum(p, axis=1, keepdims=True)
        acc = alpha * acc + _dot(p.astype(BF16), vj)
        return m_new, l, acc

    n_full = q_start // tk
    n_all = (q_start + tq + tk - 1) // tk
    init = (jnp.full((m_rows, 1), NEG_INF, F32), jnp.zeros((m_rows, 1), F32), jnp.zeros((m_rows, dv), F32))
    carry = lax.fori_loop(0, n_full, functools.partial(step, masked=False), init)
    m, l, acc = lax.fori_loop(n_full, n_all, functools.partial(step, masked=True), carry)
    return acc / l


def _diff_attn_kernel(q_ref, k_ref, v_ref, lam_ref, g_ref, o_ref, *, tq, tk, lam_init):
    i = pl.program_id(2)
    q = q_ref[...]
    lane = lax.broadcasted_iota(jnp.int32, q.shape, 1)
    zero = jnp.zeros_like(q)
    qq = jnp.concatenate([jnp.where(lane < DIFF_QK_DIM, q, zero), jnp.where(lane >= DIFF_QK_DIM, q, zero)], axis=0)
    o = _flash(qq, k_ref, v_ref, i * tq, tq, tk)
    lv = lam_ref[...]
    lam = (jnp.exp(jnp.sum(lv[0:1] * lv[1:2], axis=1, keepdims=True))
           - jnp.exp(jnp.sum(lv[2:3] * lv[3:4], axis=1, keepdims=True)) + lam_init)
    d = o[:tq] - lam * o[tq:]
    o_ref[...] = (_rms(d, g_ref[...]) * (1.0 - lam_init)).astype(o_ref.dtype)


def _mla_attn_kernel(q_ref, k_ref, v_ref, g_ref, o_ref, *, tq, tk):
    i = pl.program_id(2)
    o = _flash(q_ref[...], k_ref, v_ref, i * tq, tq, tk)
    o_ref[...] = _rms(o, g_ref[...]).astype(o_ref.dtype)


def _attn_call(kernel, name, q, k, v, extras, b, s, heads, dq_w, dk_w, dv_w, tq):
    nq = s // tq
    t = b * s
    small = [pl.BlockSpec(a.shape, lambda bi, h, i: (0, 0)) for a in extras]
    return pl.pallas_call(
        kernel,
        out_shape=jax.ShapeDtypeStruct((t, heads * dv_w), BF16),
        grid=(b, heads, nq),
        in_specs=[pl.BlockSpec((tq, dq_w), lambda bi, h, i: (bi * nq + i, h)),
                  pl.BlockSpec((s, dk_w), lambda bi, h, i: (bi, h)),
                  pl.BlockSpec((s, dv_w), lambda bi, h, i: (bi, h))] + small,
        out_specs=pl.BlockSpec((tq, dv_w), lambda bi, h, i: (bi * nq + i, h)),
        compiler_params=_cparams(("parallel", "parallel", "arbitrary")),
        name=name,
    )(q, k, v, *extras)


def _memkv_kernel(mem_ref, wk_ref, wv_ref, k_ref, v_ref):
    mb = mem_ref[...].astype(BF16)
    k_ref[...] = _dot(mb, wk_ref[...]).astype(BF16)
    v_ref[...] = _dot(mb, wv_ref[...]).astype(BF16)


def _memkv_call(mem2, wk, wv):
    n = mem2.shape[0]
    w = wk.shape[1]
    tm = N_MEM
    return pl.pallas_call(
        _memkv_kernel,
        out_shape=[jax.ShapeDtypeStruct((n, w), BF16)] * 2,
        grid=(n // tm,),
        in_specs=[pl.BlockSpec((tm, D_MODEL), lambda i: (i, 0)),
                  pl.BlockSpec(wk.shape, lambda i: (0, 0)),
                  pl.BlockSpec(wv.shape, lambda i: (0, 0))],
        out_specs=[pl.BlockSpec((tm, w), lambda i: (i, 0))] * 2,
        compiler_params=_cparams(("parallel",)),
        name="mem_kv",
    )(mem2, wk, wv)


def _post_kernel(da_ref, ma_ref, x_ref, wo_ref, g1_ref, b1_ref, wq_ref, km_ref, vm_ref, wmo_ref,
                 g2_ref, b2_ref, h_ref):
    n_d = DIFF_HEADS * DIFF_V_DIM
    mix = _dot(da_ref[...], wo_ref[0:n_d, :]) + _dot(ma_ref[...], wo_ref[n_d:, :])
    h1 = _layer_norm(DEEPNORM_ALPHA * x_ref[...] + mix, g1_ref[...], b1_ref[...])
    q = (_dot(h1.astype(BF16), wq_ref[...]) * (MEM_HEAD_DIM ** -0.5)).astype(BF16)
    outs = []
    for h in range(MEM_HEADS):
        sl = slice(h * MEM_HEAD_DIM, (h + 1) * MEM_HEAD_DIM)
        s = _dot_nt(q[:, sl], km_ref[:, sl])
        m = jnp.max(s, axis=1, keepdims=True)
        p = jnp.exp(s - m)
        p = p / jnp.sum(p, axis=1, keepdims=True)
        outs.append(_dot(p.astype(BF16), vm_ref[:, sl]))
    o = jnp.concatenate(outs, axis=1).astype(BF16)
    xat = _dot(o, wmo_ref[...])
    h_ref[...] = _layer_norm(DEEPNORM_ALPHA * h1 + xat, g2_ref[...], b2_ref[...])


def _post_call(diff_o, mla_o, x2, wo, g1, b1, wq, kmem, vmem, wmo, g2, b2, s, tm):
    t = x2.shape[0]
    per_b = s // tm
    row = lambda w: pl.BlockSpec((tm, w), lambda i: (i, 0))
    full = lambda a: pl.BlockSpec(a.shape, lambda i: (0, 0))
    memb = pl.BlockSpec((N_MEM, kmem.shape[1]), lambda i: (i // per_b, 0))
    return pl.pallas_call(
        _post_kernel,
        out_shape=jax.ShapeDtypeStruct((t, D_MODEL), F32),
        grid=(t // tm,),
        in_specs=[row(diff_o.shape[1]), row(mla_o.shape[1]), row(D_MODEL), full(wo), full(g1), full(b1),
                  full(wq), memb, memb, full(wmo), full(g2), full(b2)],
        out_specs=row(D_MODEL),
        compiler_params=_cparams(("parallel",)),
        name="post_mix_xattn",
    )(diff_o, mla_o, x2, wo, g1, b1, wq, kmem, vmem, wmo, g2, b2)


def _router_kernel(h_ref, rwt_ref, bias_ref, gates_ref):
    tm = h_ref.shape[0]
    logits = lax.dot_general(rwt_ref[...], h_ref[...], (((1,), (1,)), ((), ())),
                             precision=lax.Precision.HIGHEST, preferred_element_type=F32)
    scores = jax.nn.sigmoid(logits)
    choice = scores + bias_ref[...]
    neg = float("-inf")

    c3 = choice.reshape(N_GROUPS, GROUP_SIZE, tm)
    mi = lax.broadcasted_iota(jnp.int32, c3.shape, 1)
    top1 = jnp.max(c3, axis=1, keepdims=True)
    first = jnp.min(jnp.where(c3 == top1, mi, GROUP_SIZE), axis=1, keepdims=True)
    top2 = jnp.max(jnp.where(mi == first, neg, c3), axis=1, keepdims=True)
    gs = (top1 + top2).reshape(N_GROUPS, tm)

    gi = lax.broadcasted_iota(jnp.int32, gs.shape, 0)
    rank = jnp.zeros(gs.shape, jnp.int32)
    for g in range(N_GROUPS):
        rowg = gs[g:g + 1, :]
        beats = (rowg > gs) | ((rowg == gs) & (g < gi))
        rank = rank + beats.astype(jnp.int32)
    gmask = rank < TOPK_GROUPS
    emask = jnp.broadcast_to(gmask.reshape(N_GROUPS, 1, tm), c3.shape).reshape(N_EXPERTS, tm)

    work = jnp.where(emask, choice, NEG_INF)
    ei = lax.broadcasted_iota(jnp.int32, work.shape, 0)
    sel = jnp.zeros(work.shape, jnp.bool_)
    for _ in range(TOP_K):
        mx = jnp.max(work, axis=0, keepdims=True)
        fi = jnp.min(jnp.where(work == mx, ei, N_EXPERTS), axis=0, keepdims=True)
        pick = ei == fi
        sel = sel | pick
        work = jnp.where(pick, neg, work)
    top_w = jnp.where(sel, scores, 0.0)
    gates = top_w / jnp.sum(top_w, axis=0, keepdims=True) * ROUTED_SCALE
    gates_ref[...] = jnp.concatenate([gates, jnp.zeros_like(gates)], axis=0).T


def _router_call(h2, rwt, bias, tm):
    t = h2.shape[0]
    return pl.pallas_call(
        _router_kernel,
        out_shape=jax.ShapeDtypeStruct((t, 2 * N_EXPERTS), F32),
        grid=(t // tm,),
        in_specs=[pl.BlockSpec((tm, D_MODEL), lambda i: (i, 0)),
                  pl.BlockSpec(rwt.shape, lambda i: (0, 0)),
                  pl.BlockSpec(bias.shape, lambda i: (0, 0))],
        out_specs=pl.BlockSpec((tm, 2 * N_EXPERTS), lambda i: (i, 0)),
        compiler_params=_cparams(("parallel",)),
        name="router",
    )(h2, rwt, bias)


def _moe_kernel(h_ref, gates_ref, wg_ref, wu_ref, wd_ref, sg_ref, su_ref, sd_ref, g3_ref, b3_ref,
                o_ref, xb_ref, acc_ref):
    e = pl.program_id(1)

    def ffn(xb, wg, wu, wd):
        a = _dot(xb, wg)
        u = _dot(xb, wu)
        return _dot((jax.nn.silu(a) * u).astype(BF16), wd)

    @pl.when(e == 0)
    def _():
        xb_ref[...] = h_ref[...].astype(BF16)
        acc_ref[...] = ffn(xb_ref[...], sg_ref[...], su_ref[...], sd_ref[...])

    y = ffn(xb_ref[...], wg_ref[0], wu_ref[0], wd_ref[0])
    gates = gates_ref[...]
    lane = lax.broadcasted_iota(jnp.int32, gates.shape, 1)
    gcol = jnp.sum(jnp.where(lane == e, gates, 0.0), axis=1, keepdims=True)
    acc_ref[...] += gcol * y

    @pl.when(e == pl.num_programs(1) - 1)
    def _():
        o_ref[...] = _layer_norm(DEEPNORM_ALPHA * h_ref[...] + acc_ref[...], g3_ref[...], b3_ref[...])


def _moe_call(h2, gates, wg, wu, wd, sg, su, sd, g3, b3, tm):
    t = h2.shape[0]
    full = lambda a: pl.BlockSpec(a.shape, lambda i, e: (0, 0))
    return pl.pallas_call(
        _moe_kernel,
        out_shape=jax.ShapeDtypeStruct((t, D_MODEL), F32),
        grid=(t // tm, N_EXPERTS),
        in_specs=[pl.BlockSpec((tm, D_MODEL), lambda i, e: (i, 0)),
                  pl.BlockSpec((tm, gates.shape[1]), lambda i, e: (i, 0)),
                  pl.BlockSpec((1, D_MODEL, EXPERT_DIM), lambda i, e: (e, 0, 0)),
                  pl.BlockSpec((1, D_MODEL, EXPERT_DIM), lambda i, e: (e, 0, 0)),
                  pl.BlockSpec((1, EXPERT_DIM, D_MODEL), lambda i, e: (e, 0, 0)),
                  full(sg), full(su), full(sd), full(g3), full(b3)],
        out_specs=pl.BlockSpec((tm, D_MODEL), lambda i, e: (i, 0)),
        scratch_shapes=[pltpu.VMEM((tm, D_MODEL), BF16), pltpu.VMEM((tm, D_MODEL), F32)],
        compiler_params=_cparams(("parallel", "arbitrary")),
        name="moe_experts",
    )(h2, gates, wg, wu, wd, sg, su, sd, g3, b3)


def _prep_w_in(w_in):
    pad = jnp.zeros((D_MODEL, V7X_LANES - MLA_ROPE_DIM), w_in.dtype)
    return jnp.concatenate([w_in, pad], axis=1).astype(BF16)


def _prep_w_uq(w_uq):
    w = w_uq.reshape(MLA_Q_RANK, MLA_HEADS, MLA_NOPE_DIM + MLA_ROPE_DIM)
    pad = jnp.zeros((MLA_Q_RANK, MLA_HEADS, MLA_QK_PAD - MLA_NOPE_DIM - MLA_ROPE_DIM), w.dtype)
    return jnp.concatenate([w, pad], axis=2).reshape(MLA_Q_RANK, MLA_HEADS * MLA_QK_PAD).astype(BF16)


def _prep_w_ukv(w_ukv):
    w = w_ukv.reshape(MLA_KV_RANK, MLA_HEADS, MLA_NOPE_DIM + MLA_V_DIM)
    k = w[:, :, :MLA_NOPE_DIM].reshape(MLA_KV_RANK, MLA_HEADS * MLA_NOPE_DIM)
    v = w[:, :, MLA_NOPE_DIM:].reshape(MLA_KV_RANK, MLA_HEADS * MLA_V_DIM)
    return jnp.concatenate([k, v], axis=1).astype(BF16)


def _rope_inv_freq():
    f_m = ROPE_THETA ** (-jnp.arange(0, MLA_ROPE_DIM, 2, dtype=F32) / MLA_ROPE_DIM)
    f_d = ROPE_THETA ** (-jnp.arange(0, DIFF_ROT_DIM, 2, dtype=F32) / DIFF_ROT_DIM)
    zeros = jnp.zeros((V7X_LANES - 2 * f_m.shape[0] - 2 * f_d.shape[0],), F32)
    return jnp.concatenate([f_m, f_m, f_d, f_d, zeros]).reshape(1, V7X_LANES)


def _layer(x2, mem2, pos2, b, s, l, w_in, lam_q1, lam_k1, lam_q2, lam_k2, diff_subln_g, mla_q_norm_g,
           mla_w_uq, mla_kv_norm_g, mla_w_ukv, mla_out_norm_g, w_o, ln1_g, ln1_b, mem_w_q, mem_w_k, mem_w_v,
           mem_w_o, ln2_g, ln2_b, router_w, router_bias, exp_w_gate, exp_w_up, exp_w_down, sh_w_gate,
           sh_w_up, sh_w_down, ln3_g, ln3_b):
    row = lambda a: a.reshape(1, -1)
    dq, dk, dv, mq, mk, mv = _proj_call(
        x2, pos2, _rope_inv_freq(), _prep_w_in(w_in), row(mla_q_norm_g), _prep_w_uq(mla_w_uq),
        row(mla_kv_norm_g), _prep_w_ukv(mla_w_ukv), tm=min(512, s))

    lam_init = 0.8 - 0.6 * math.exp(-0.3 * l)
    lam_vecs = jnp.stack([lam_q1, lam_k1, lam_q2, lam_k2]).astype(F32)
    tq = min(256, s)
    diff_o = _attn_call(
        functools.partial(_diff_attn_kernel, tq=tq, tk=tq, lam_init=lam_init), "diff_attn",
        dq, dk, dv, [lam_vecs, row(diff_subln_g)], b, s, DIFF_HEADS, 2 * DIFF_QK_DIM, 2 * DIFF_QK_DIM,
        DIFF_V_DIM, tq)
    mla_o = _attn_call(
        functools.partial(_mla_attn_kernel, tq=tq, tk=tq), "mla_attn",
        mq, mk, mv, [row(mla_out_norm_g)], b, s, MLA_HEADS, MLA_QK_PAD, MLA_QK_PAD, MLA_V_DIM, tq)

    kmem, vmem = _memkv_call(mem2, mem_w_k.astype(BF16), mem_w_v.astype(BF16))
    h2 = _post_call(diff_o, mla_o, x2, w_o.astype(BF16), row(ln1_g), row(ln1_b), mem_w_q.astype(BF16),
                    kmem, vmem, mem_w_o.astype(BF16), row(ln2_g), row(ln2_b), s, tm=min(512, s))

    gates = _router_call(h2, router_w.T, router_bias.reshape(-1, 1), tm=min(512, s))
    return _moe_call(h2, gates, exp_w_gate.astype(BF16), exp_w_up.astype(BF16), exp_w_down.astype(BF16),
                     sh_w_gate.astype(BF16), sh_w_up.astype(BF16), sh_w_down.astype(BF16),
                     row(ln3_g), row(ln3_b), tm=min(1024, s))


def kernel(x, mem, positions, w_in, lam_q1, lam_k1, lam_q2, lam_k2, diff_subln_g, mla_q_norm_g, mla_w_uq,
           mla_kv_norm_g, mla_w_ukv, mla_out_norm_g, w_o, ln1_g, ln1_b, mem_w_q, mem_w_k, mem_w_v, mem_w_o,
           ln2_g, ln2_b, router_w, router_bias, exp_w_gate, exp_w_up, exp_w_down, sh_w_gate, sh_w_up,
           sh_w_down, ln3_g, ln3_b):
    b, s, d = x.shape
    h = x.reshape(b * s, d)
    mem2 = mem.reshape(b * mem.shape[1], d)
    pos2 = positions.reshape(b * s, 1)
    params = (w_in, lam_q1, lam_k1, lam_q2, lam_k2, diff_subln_g, mla_q_norm_g, mla_w_uq, mla_kv_norm_g,
              mla_w_ukv, mla_out_norm_g, w_o, ln1_g, ln1_b, mem_w_q, mem_w_k, mem_w_v, mem_w_o, ln2_g, ln2_b,
              router_w, router_bias, exp_w_gate, exp_w_up, exp_w_down, sh_w_gate, sh_w_up, sh_w_down,
              ln3_g, ln3_b)
    for l in range(w_in.shape[0]):
        h = _layer(h, mem2, pos2, b, s, l, *[p[l] for p in params])
    return h.reshape(b, s, d)
```

```python
import functools
import math

import jax
import jax.numpy as jnp
from jax import lax
from jax.experimental import pallas as pl
from jax.experimental.pallas import tpu as pltpu

F32 = jnp.float32
BF16 = jnp.bfloat16

D_MODEL = 1024
N_MEM = 256
ROPE_THETA = 500000.0
NEG_INF = -1e30
LN_EPS = 1e-5
RMS_EPS = 1e-6
DIFF_HEADS = 4
DIFF_QK_DIM = 64
DIFF_V_DIM = 128
DIFF_ROT_DIM = DIFF_QK_DIM // 4
MLA_HEADS = 4
MLA_Q_RANK = 256
MLA_KV_RANK = 128
MLA_NOPE_DIM = 128
MLA_ROPE_DIM = 64
MLA_V_DIM = 128
MLA_QK_PAD = 256
MEM_HEADS = 4
MEM_HEAD_DIM = 128
N_EXPERTS = 64
TOP_K = 8
N_GROUPS = 8
TOPK_GROUPS = 4
GROUP_SIZE = N_EXPERTS // N_GROUPS
EXPERT_DIM = 256
SHARED_DIM = 256
ROUTED_SCALE = 2.5
DEPTH = 1
DEEPNORM_ALPHA = (2.0 * DEPTH) ** 0.25
LOG2E = math.log2(math.e)

V7X_LANES = 128
V7X_SUBLANES = 8
V7X_VMEM_LIMIT_BYTES = 56 * 1024 * 1024

N_DQ = DIFF_HEADS * 2 * DIFF_QK_DIM
N_DV = DIFF_HEADS * DIFF_V_DIM
O_DK = N_DQ
O_DV = 2 * N_DQ
O_CQ = O_DV + N_DV
O_CKV = O_CQ + MLA_Q_RANK
O_KR = O_CKV + MLA_KV_RANK
N_IN_PAD = O_KR + V7X_LANES


def _cparams(sem):
    return pltpu.CompilerParams(dimension_semantics=sem, vmem_limit_bytes=V7X_VMEM_LIMIT_BYTES)


def _dot(a, b):
    return jnp.dot(a, b, preferred_element_type=F32)


def _dot_nt(a, b):
    return lax.dot_general(a, b, (((1,), (1,)), ((), ())), preferred_element_type=F32)


def _rms(x, g):
    return x * lax.rsqrt(jnp.mean(jnp.square(x), axis=-1, keepdims=True) + RMS_EPS) * g


def _rms_rows(x, g):
    return x * lax.rsqrt(jnp.mean(jnp.square(x), axis=0, keepdims=True) + RMS_EPS) * g


def _layer_norm(x, g, b):
    mu = jnp.mean(x, axis=-1, keepdims=True)
    xc = x - mu
    var = jnp.mean(jnp.square(xc), axis=-1, keepdims=True)
    return xc * lax.rsqrt(var + LN_EPS) * g + b


def _rope_rows(x, cos, sin, period):
    h = cos.shape[0]
    pieces = []
    for base in range(0, x.shape[0], period):
        x1 = x[base:base + h]
        x2 = x[base + h:base + 2 * h]
        pieces.append(x1 * cos - x2 * sin)
        pieces.append(x2 * cos + x1 * sin)
        if period > 2 * h:
            pieces.append(x[base + 2 * h:base + period])
    return jnp.concatenate(pieces, axis=0)


def _proj_kernel(x_ref, pos_ref, fd_ref, fm_ref, w_in_ref, gq_ref, wuq_ref, gkv_ref, wukv_ref,
                 dq_ref, dk_ref, dv_ref, mq_ref, mk_ref, mv_ref):
    xb = x_ref[...].astype(BF16)
    pos = pos_ref[...].astype(F32)
    ang_d = fd_ref[...] * pos
    ang_m = fm_ref[...] * pos
    cos_d, sin_d = jnp.cos(ang_d), jnp.sin(ang_d)
    cos_m, sin_m = jnp.cos(ang_m), jnp.sin(ang_m)

    dq = _rope_rows(_dot_nt(w_in_ref[0:N_DQ, :], xb), cos_d, sin_d, DIFF_QK_DIM)
    dq_ref[...] = (dq * (DIFF_QK_DIM ** -0.5 * LOG2E)).astype(BF16)
    dk = _rope_rows(_dot_nt(w_in_ref[O_DK:O_DK + N_DQ, :], xb), cos_d, sin_d, DIFF_QK_DIM)
    dk_ref[...] = dk.T.astype(BF16)
    dv_ref[...] = _dot_nt(w_in_ref[O_DV:O_DV + N_DV, :], xb).astype(BF16)

    c_q = _dot_nt(w_in_ref[O_CQ:O_CQ + MLA_Q_RANK, :], xb)
    q = _dot(wuq_ref[...], _rms_rows(c_q, gq_ref[...]).astype(BF16))
    q = q * ((MLA_NOPE_DIM + MLA_ROPE_DIM) ** -0.5 * LOG2E)
    pieces = []
    for h in range(MLA_HEADS):
        o = h * MLA_QK_PAD
        pieces.append(q[o:o + MLA_NOPE_DIM])
        pieces.append(_rope_rows(q[o + MLA_NOPE_DIM:o + MLA_NOPE_DIM + MLA_ROPE_DIM], cos_m, sin_m, MLA_ROPE_DIM))
        pieces.append(q[o + MLA_NOPE_DIM + MLA_ROPE_DIM:o + MLA_QK_PAD])
    mq_ref[...] = jnp.concatenate(pieces, axis=0).astype(BF16)

    c_kv = _dot_nt(w_in_ref[O_CKV:O_CKV + MLA_KV_RANK, :], xb)
    kv = _dot(wukv_ref[...], _rms_rows(c_kv, gkv_ref[...]).astype(BF16))
    n_kn = MLA_HEADS * MLA_NOPE_DIM
    mv_ref[...] = kv[n_kn:].astype(BF16)
    k_rope = _dot_nt(w_in_ref[O_KR:O_KR + V7X_LANES, :], xb)
    k_pe = jnp.concatenate([_rope_rows(k_rope[:MLA_ROPE_DIM], cos_m, sin_m, MLA_ROPE_DIM),
                            k_rope[MLA_ROPE_DIM:]], axis=0)
    k_nope_t = kv[:n_kn].T.astype(BF16)
    k_pe_t = k_pe.T.astype(BF16)
    for h in range(MLA_HEADS):
        o = h * MLA_QK_PAD
        mk_ref[:, o:o + MLA_NOPE_DIM] = k_nope_t[:, h * MLA_NOPE_DIM:(h + 1) * MLA_NOPE_DIM]
        mk_ref[:, o + MLA_NOPE_DIM:o + MLA_QK_PAD] = k_pe_t


def _proj_call(x2, pos_row, fd, fm, w_in_t, gq, wuq_t, gkv, wukv_t, tm):
    t = x2.shape[0]
    full = lambda a: pl.BlockSpec(a.shape, lambda i: (0, 0))
    col_blk = lambda r: pl.BlockSpec((r, tm), lambda i: (0, i))
    row_blk = lambda w: pl.BlockSpec((tm, w), lambda i: (i, 0))
    n_mq = MLA_HEADS * MLA_QK_PAD
    return pl.pallas_call(
        _proj_kernel,
        out_shape=[jax.ShapeDtypeStruct((N_DQ, t), BF16), jax.ShapeDtypeStruct((t, N_DQ), BF16),
                   jax.ShapeDtypeStruct((N_DV, t), BF16), jax.ShapeDtypeStruct((n_mq, t), BF16),
                   jax.ShapeDtypeStruct((t, n_mq), BF16), jax.ShapeDtypeStruct((MLA_HEADS * MLA_V_DIM, t), BF16)],
        grid=(t // tm,),
        in_specs=[row_blk(D_MODEL), col_blk(1), full(fd), full(fm), full(w_in_t), full(gq), full(wuq_t),
                  full(gkv), full(wukv_t)],
        out_specs=[col_blk(N_DQ), row_blk(N_DQ), col_blk(N_DV), col_blk(n_mq), row_blk(n_mq),
                   col_blk(MLA_HEADS * MLA_V_DIM)],
        compiler_params=_cparams(("parallel",)),
        name="proj_rope",
    )(x2, pos_row, fd, fm, w_in_t, gq, wuq_t, gkv, wukv_t)


def _flash_t(qq, k_ref, vt_ref, s_refs, p_refs, acc_ref, i, tq):
    n = qq.shape[1]
    tk = s_refs[0].shape[0]
    assert tq == 2 * tk
    q_pos = i * tq + (lax.broadcasted_iota(jnp.int32, (tk, n), 1) & (tq - 1))
    key_row = lax.broadcasted_iota(jnp.int32, (tk, n), 0)

    def scores(c, s_ref):
        off = pl.multiple_of(c * tk, tk)
        s = _dot(k_ref[pl.ds(off, tk), :], qq)
        s_ref[...] = s
        return jnp.max(s, axis=0, keepdims=True)

    def softmax(c, s_ref, p_ref, cmax, m, l, masked):
        s = s_ref[...]
        if masked:
            s = jnp.where(key_row + c * tk <= q_pos, s, NEG_INF)
            cmax = jnp.max(s, axis=0, keepdims=True)
        m_new = jnp.maximum(m, cmax)
        alpha = jnp.exp2(m - m_new)
        p = jnp.exp2(s - m_new)
        p_ref[...] = p.astype(BF16)
        return m_new, alpha * l + jnp.sum(p, axis=0, keepdims=True), alpha

    def values(c, p_ref, alpha):
        off = pl.multiple_of(jnp.maximum(c, 0) * tk, tk)
        acc_ref[...] = alpha * acc_ref[...] + _dot(vt_ref[:, pl.ds(off, tk)], p_ref[...])

    def pair(u, carry, masked_second):
        cm1, alpha0, m, l = carry
        c = 2 * u
        cm0 = scores(c, s_refs[0])
        values(c - 2, p_refs[0], alpha0)
        m, l, alpha1 = softmax(c - 1, s_refs[1], p_refs[1], cm1, m, l, False)
        cm1 = scores(c + 1, s_refs[1])
        values(c - 1, p_refs[1], alpha1)
        m, l, alpha0 = softmax(c, s_refs[0], p_refs[0], cm0, m, l, masked_second)
        return cm1, alpha0, m, l

    s_refs[1][...] = jnp.full(s_refs[1].shape, 2 * NEG_INF, F32)
    p_refs[0][...] = jnp.zeros(p_refs[0].shape, BF16)
    acc_ref[...] = jnp.zeros(acc_ref.shape, F32)
    init = (jnp.full((1, n), 2 * NEG_INF, F32), jnp.ones((1, n), F32),
            jnp.full((1, n), NEG_INF, F32), jnp.zeros((1, n), F32))
    carry = lax.fori_loop(0, i, functools.partial(pair, masked_second=False), init)
    cm1, alpha0, m, l = pair(i, carry, True)
    values(2 * i, p_refs[0], alpha0)
    m, l, alpha1 = softmax(2 * i + 1, s_refs[1], p_refs[1], cm1, m, l, True)
    values(2 * i + 1, p_refs[1], alpha1)
    return acc_ref[...] / l


def _diff_attn_kernel(qt_ref, k_ref, vt_ref, lam_ref, g_ref, o_ref, s0, s1, p0, p1, acc_ref, *, tq, lam_init):
    i = pl.program_id(2)
    qt = qt_ref[...]
    row = lax.broadcasted_iota(jnp.int32, qt.shape, 0)
    zero = jnp.zeros_like(qt)
    qq = jnp.concatenate([jnp.where(row < DIFF_QK_DIM, qt, zero), jnp.where(row >= DIFF_QK_DIM, qt, zero)], axis=1)
    o = _flash_t(qq, k_ref, vt_ref, (s0, s1), (p0, p1), acc_ref, i, tq)
    lv = lam_ref[...]
    lam = (jnp.exp(jnp.sum(lv[0:1] * lv[1:2], axis=1, keepdims=True))
           - jnp.exp(jnp.sum(lv[2:3] * lv[3:4], axis=1, keepdims=True)) + lam_init)
    d = o[:, :tq] - lam * o[:, tq:]
    o_ref[...] = (_rms_rows(d, g_ref[...]) * (1.0 - lam_init)).T.astype(o_ref.dtype)


def _mla_attn_kernel(qt_ref, k_ref, vt_ref, g_ref, o_ref, s0, s1, p0, p1, acc_ref, *, tq):
    i = pl.program_id(2)
    o = _flash_t(qt_ref[...], k_ref, vt_ref, (s0, s1), (p0, p1), acc_ref, i, tq)
    o_ref[...] = _rms_rows(o, g_ref[...]).T.astype(o_ref.dtype)


def _attn_call(kernel, name, qt, k, vt, extras, b, s, heads, dk_w, dv_w, tq, n_maps):
    nq = s // tq
    t = b * s
    tk = tq // 2
    n = n_maps * tq
    small = [pl.BlockSpec(a.shape, lambda bi, h, i: (0, 0)) for a in extras]
    return pl.pallas_call(
        kernel,
        out_shape=jax.ShapeDtypeStruct((t, heads * dv_w), BF16),
        grid=(b, heads, nq),
        in_specs=[pl.BlockSpec((dk_w, tq), lambda bi, h, i: (h, bi * nq + i)),
                  pl.BlockSpec((s, dk_w), lambda bi, h, i: (bi, h)),
                  pl.BlockSpec((dv_w, s), lambda bi, h, i: (h, bi))] + small,
        out_specs=pl.BlockSpec((tq, dv_w), lambda bi, h, i: (bi * nq + i, h)),
        scratch_shapes=[pltpu.VMEM((tk, n), F32), pltpu.VMEM((tk, n), F32),
                        pltpu.VMEM((tk, n), BF16), pltpu.VMEM((tk, n), BF16), pltpu.VMEM((dv_w, n), F32)],
        compiler_params=_cparams(("parallel", "parallel", "arbitrary")),
        name=name,
    )(qt, k, vt, *extras)


def _memkv_kernel(mem_ref, wk_ref, wv_ref, k_ref, v_ref):
    mb = mem_ref[...].astype(BF16)
    k_ref[...] = _dot(mb, wk_ref[...]).astype(BF16)
    v_ref[...] = _dot(mb, wv_ref[...]).astype(BF16)


def _memkv_call(mem2, wk, wv):
    n = mem2.shape[0]
    w = wk.shape[1]
    tm = N_MEM
    return pl.pallas_call(
        _memkv_kernel,
        out_shape=[jax.ShapeDtypeStruct((n, w), BF16)] * 2,
        grid=(n // tm,),
        in_specs=[pl.BlockSpec((tm, D_MODEL), lambda i: (i, 0)),
                  pl.BlockSpec(wk.shape, lambda i: (0, 0)),
                  pl.BlockSpec(wv.shape, lambda i: (0, 0))],
        out_specs=[pl.BlockSpec((tm, w), lambda i: (i, 0))] * 2,
        compiler_params=_cparams(("parallel",)),
        name="mem_kv",
    )(mem2, wk, wv)


def _post_kernel(da_ref, ma_ref, x_ref, wo_ref, g1_ref, b1_ref, wq_ref, km_ref, vm_ref, wmo_ref,
                 g2_ref, b2_ref, h_ref):
    n_d = DIFF_HEADS * DIFF_V_DIM
    mix = _dot(da_ref[...], wo_ref[0:n_d, :]) + _dot(ma_ref[...], wo_ref[n_d:, :])
    h1 = _layer_norm(DEEPNORM_ALPHA * x_ref[...] + mix, g1_ref[...], b1_ref[...])
    q = (_dot(h1.astype(BF16), wq_ref[...]) * (MEM_HEAD_DIM ** -0.5)).astype(BF16)
    outs = []
    for h in range(MEM_HEADS):
        sl = slice(h * MEM_HEAD_DIM, (h + 1) * MEM_HEAD_DIM)
        s = _dot_nt(q[:, sl], km_ref[:, sl])
        m = jnp.max(s, axis=1, keepdims=True)
        p = jnp.exp(s - m)
        p = p / jnp.sum(p, axis=1, keepdims=True)
        outs.append(_dot(p.astype(BF16), vm_ref[:, sl]))
    o = jnp.concatenate(outs, axis=1).astype(BF16)
    xat = _dot(o, wmo_ref[...])
    h_ref[...] = _layer_norm(DEEPNORM_ALPHA * h1 + xat, g2_ref[...], b2_ref[...])


def _post_call(diff_o, mla_o, x2, wo, g1, b1, wq, kmem, vmem, wmo, g2, b2, s, tm):
    t = x2.shape[0]
    per_b = s // tm
    row = lambda w: pl.BlockSpec((tm, w), lambda i: (i, 0))
    full = lambda a: pl.BlockSpec(a.shape, lambda i: (0, 0))
    memb = pl.BlockSpec((N_MEM, kmem.shape[1]), lambda i: (i // per_b, 0))
    return pl.pallas_call(
        _post_kernel,
        out_shape=jax.ShapeDtypeStruct((t, D_MODEL), F32),
        grid=(t // tm,),
        in_specs=[row(diff_o.shape[1]), row(mla_o.shape[1]), row(D_MODEL), full(wo), full(g1), full(b1),
                  full(wq), memb, memb, full(wmo), full(g2), full(b2)],
        out_specs=row(D_MODEL),
        compiler_params=_cparams(("parallel",)),
        name="post_mix_xattn",
    )(diff_o, mla_o, x2, wo, g1, b1, wq, kmem, vmem, wmo, g2, b2)


def _router_kernel(h_ref, rwt_ref, bias_ref, gates_ref):
    tm = h_ref.shape[0]
    logits = lax.dot_general(rwt_ref[...], h_ref[...], (((1,), (1,)), ((), ())),
                             precision=lax.Precision.HIGHEST, preferred_element_type=F32)
    scores = jax.nn.sigmoid(logits)
    choice = scores + bias_ref[...]
    neg = float("-inf")

    c3 = choice.reshape(N_GROUPS, GROUP_SIZE, tm)
    mi = lax.broadcasted_iota(jnp.int32, c3.shape, 1)
    top1 = jnp.max(c3, axis=1, keepdims=True)
    first = jnp.min(jnp.where(c3 == top1, mi, GROUP_SIZE), axis=1, keepdims=True)
    top2 = jnp.max(jnp.where(mi == first, neg, c3), axis=1, keepdims=True)
    gs = (top1 + top2).reshape(N_GROUPS, tm)

    gi = lax.broadcasted_iota(jnp.int32, gs.shape, 0)
    rank = jnp.zeros(gs.shape, jnp.int32)
    for g in range(N_GROUPS):
        rowg = gs[g:g + 1, :]
        beats = (rowg > gs) | ((rowg == gs) & (g < gi))
        rank = rank + beats.astype(jnp.int32)
    gmask = rank < TOPK_GROUPS
    emask = jnp.broadcast_to(gmask.reshape(N_GROUPS, 1, tm), c3.shape).reshape(N_EXPERTS, tm)

    work = jnp.where(emask, choice, NEG_INF)
    ei = lax.broadcasted_iota(jnp.int32, work.shape, 0)
    sel = jnp.zeros(work.shape, jnp.bool_)
    for _ in range(TOP_K):
        mx = jnp.max(work, axis=0, keepdims=True)
        fi = jnp.min(jnp.where(work == mx, ei, N_EXPERTS), axis=0, keepdims=True)
        pick = ei == fi
        sel = sel | pick
        work = jnp.where(pick, neg, work)
    top_w = jnp.where(sel, scores, 0.0)
    gates = top_w / jnp.sum(top_w, axis=0, keepdims=True) * ROUTED_SCALE
    gates_ref[...] = jnp.concatenate([gates, jnp.zeros_like(gates)], axis=0).T


def _router_call(h2, rwt, bias, tm):
    t = h2.shape[0]
    return pl.pallas_call(
        _router_kernel,
        out_shape=jax.ShapeDtypeStruct((t, 2 * N_EXPERTS), F32),
        grid=(t // tm,),
        in_specs=[pl.BlockSpec((tm, D_MODEL), lambda i: (i, 0)),
                  pl.BlockSpec(rwt.shape, lambda i: (0, 0)),
                  pl.BlockSpec(bias.shape, lambda i: (0, 0))],
        out_specs=pl.BlockSpec((tm, 2 * N_EXPERTS), lambda i: (i, 0)),
        compiler_params=_cparams(("parallel",)),
        name="router",
    )(h2, rwt, bias)


def _moe_kernel(h_ref, gates_ref, wg_ref, wu_ref, wd_ref, sg_ref, su_ref, sd_ref, g3_ref, b3_ref,
                o_ref, xb_ref, acc_ref):
    e = pl.program_id(1)

    def ffn(xb, wg, wu, wd):
        a = _dot(xb, wg)
        u = _dot(xb, wu)
        return _dot((jax.nn.silu(a) * u).astype(BF16), wd)

    @pl.when(e == 0)
    def _():
        xb_ref[...] = h_ref[...].astype(BF16)
        acc_ref[...] = ffn(xb_ref[...], sg_ref[...], su_ref[...], sd_ref[...])

    y = ffn(xb_ref[...], wg_ref[0], wu_ref[0], wd_ref[0])
    gates = gates_ref[...]
    lane = lax.broadcasted_iota(jnp.int32, gates.shape, 1)
    gcol = jnp.sum(jnp.where(lane == e, gates, 0.0), axis=1, keepdims=True)
    acc_ref[...] += gcol * y

    @pl.when(e == pl.num_programs(1) - 1)
    def _():
        o_ref[...] = _layer_norm(DEEPNORM_ALPHA * h_ref[...] + acc_ref[...], g3_ref[...], b3_ref[...])


def _moe_call(h2, gates, wg, wu, wd, sg, su, sd, g3, b3, tm):
    t = h2.shape[0]
    full = lambda a: pl.BlockSpec(a.shape, lambda i, e: (0, 0))
    return pl.pallas_call(
        _moe_kernel,
        out_shape=jax.ShapeDtypeStruct((t, D_MODEL), F32),
        grid=(t // tm, N_EXPERTS),
        in_specs=[pl.BlockSpec((tm, D_MODEL), lambda i, e: (i, 0)),
                  pl.BlockSpec((tm, gates.shape[1]), lambda i, e: (i, 0)),
                  pl.BlockSpec((1, D_MODEL, EXPERT_DIM), lambda i, e: (e, 0, 0)),
                  pl.BlockSpec((1, D_MODEL, EXPERT_DIM), lambda i, e: (e, 0, 0)),
                  pl.BlockSpec((1, EXPERT_DIM, D_MODEL), lambda i, e: (e, 0, 0)),
                  full(sg), full(su), full(sd), full(g3), full(b3)],
        out_specs=pl.BlockSpec((tm, D_MODEL), lambda i, e: (i, 0)),
        scratch_shapes=[pltpu.VMEM((tm, D_MODEL), BF16), pltpu.VMEM((tm, D_MODEL), F32)],
        compiler_params=_cparams(("parallel", "arbitrary")),
        name="moe_experts",
    )(h2, gates, wg, wu, wd, sg, su, sd, g3, b3)


def _prep_w_in_t(w_in):
    pad = jnp.zeros((D_MODEL, V7X_LANES - MLA_ROPE_DIM), w_in.dtype)
    return jnp.concatenate([w_in, pad], axis=1).T.astype(BF16)


def _prep_w_uq_t(w_uq):
    w = w_uq.reshape(MLA_Q_RANK, MLA_HEADS, MLA_NOPE_DIM + MLA_ROPE_DIM)
    pad = jnp.zeros((MLA_Q_RANK, MLA_HEADS, MLA_QK_PAD - MLA_NOPE_DIM - MLA_ROPE_DIM), w.dtype)
    return jnp.concatenate([w, pad], axis=2).reshape(MLA_Q_RANK, MLA_HEADS * MLA_QK_PAD).T.astype(BF16)


def _prep_w_ukv_t(w_ukv):
    w = w_ukv.reshape(MLA_KV_RANK, MLA_HEADS, MLA_NOPE_DIM + MLA_V_DIM)
    k = w[:, :, :MLA_NOPE_DIM].reshape(MLA_KV_RANK, MLA_HEADS * MLA_NOPE_DIM)
    v = w[:, :, MLA_NOPE_DIM:].reshape(MLA_KV_RANK, MLA_HEADS * MLA_V_DIM)
    return jnp.concatenate([k, v], axis=1).T.astype(BF16)


def _rope_inv_freq(dim):
    return (ROPE_THETA ** (-jnp.arange(0, dim, 2, dtype=F32) / dim)).reshape(-1, 1)


def _layer(x2, mem2, pos_row, b, s, l, w_in, lam_q1, lam_k1, lam_q2, lam_k2, diff_subln_g, mla_q_norm_g,
           mla_w_uq, mla_kv_norm_g, mla_w_ukv, mla_out_norm_g, w_o, ln1_g, ln1_b, mem_w_q, mem_w_k, mem_w_v,
           mem_w_o, ln2_g, ln2_b, router_w, router_bias, exp_w_gate, exp_w_up, exp_w_down, sh_w_gate,
           sh_w_up, sh_w_down, ln3_g, ln3_b):
    row = lambda a: a.reshape(1, -1)
    col = lambda a: a.reshape(-1, 1)
    dqt, dk, dvt, mqt, mk, mvt = _proj_call(
        x2, pos_row, _rope_inv_freq(DIFF_ROT_DIM), _rope_inv_freq(MLA_ROPE_DIM), _prep_w_in_t(w_in),
        col(mla_q_norm_g), _prep_w_uq_t(mla_w_uq), col(mla_kv_norm_g), _prep_w_ukv_t(mla_w_ukv), tm=min(512, s))

    lam_init = 0.8 - 0.6 * math.exp(-0.3 * l)
    lam_vecs = jnp.stack([lam_q1, lam_k1, lam_q2, lam_k2]).astype(F32)
    tq = min(512, s)
    diff_o = _attn_call(
        functools.partial(_diff_attn_kernel, tq=tq, lam_init=lam_init), "diff_attn",
        dqt, dk, dvt, [lam_vecs, col(diff_subln_g)], b, s, DIFF_HEADS, 2 * DIFF_QK_DIM, DIFF_V_DIM, tq, 2)
    mla_o = _attn_call(
        functools.partial(_mla_attn_kernel, tq=tq), "mla_attn",
        mqt, mk, mvt, [col(mla_out_norm_g)], b, s, MLA_HEADS, MLA_QK_PAD, MLA_V_DIM, tq, 1)

    kmem, vmem = _memkv_call(mem2, mem_w_k.astype(BF16), mem_w_v.astype(BF16))
    h2 = _post_call(diff_o, mla_o, x2, w_o.astype(BF16), row(ln1_g), row(ln1_b), mem_w_q.astype(BF16),
                    kmem, vmem, mem_w_o.astype(BF16), row(ln2_g), row(ln2_b), s, tm=min(512, s))

    gates = _router_call(h2, router_w.T, col(router_bias), tm=min(512, s))
    return _moe_call(h2, gates, exp_w_gate.astype(BF16), exp_w_up.astype(BF16), exp_w_down.astype(BF16),
                     sh_w_gate.astype(BF16), sh_w_up.astype(BF16), sh_w_down.astype(BF16),
                     row(ln3_g), row(ln3_b), tm=min(1024, s))


def kernel(x, mem, positions, w_in, lam_q1, lam_k1, lam_q2, lam_k2, diff_subln_g, mla_q_norm_g, mla_w_uq,
           mla_kv_norm_g, mla_w_ukv, mla_out_norm_g, w_o, ln1_g, ln1_b, mem_w_q, mem_w_k, mem_w_v, mem_w_o,
           ln2_g, ln2_b, router_w, router_bias, exp_w_gate, exp_w_up, exp_w_down, sh_w_gate, sh_w_up,
           sh_w_down, ln3_g, ln3_b):
    b, s, d = x.shape
    h = x.reshape(b * s, d)
    mem2 = mem.reshape(b * mem.shape[1], d)
    pos_row = positions.reshape(1, b * s)
    params = (w_in, lam_q1, lam_k1, lam_q2, lam_k2, diff_subln_g, mla_q_norm_g, mla_w_uq, mla_kv_norm_g,
              mla_w_ukv, mla_out_norm_g, w_o, ln1_g, ln1_b, mem_w_q, mem_w_k, mem_w_v, mem_w_o, ln2_g, ln2_b,
              router_w, router_bias, exp_w_gate, exp_w_up, exp_w_down, sh_w_gate, sh_w_up, sh_w_down,
              ln3_g, ln3_b)
    for l in range(w_in.shape[0]):
        h = _layer(h, mem2, pos_row, b, s, l, *[p[l] for p in params])
    return h.reshape(b, s, d)
```

```python
import functools
import math

import jax
import jax.numpy as jnp
from jax import lax
from jax.experimental import pallas as pl
from jax.experimental.pallas import tpu as pltpu

F32 = jnp.float32
BF16 = jnp.bfloat16

D_MODEL = 1024
N_MEM = 256
ROPE_THETA = 500000.0
NEG_INF = -1e30
LN_EPS = 1e-5
RMS_EPS = 1e-6
DIFF_HEADS = 4
DIFF_QK_DIM = 64
DIFF_V_DIM = 128
DIFF_ROT_DIM = DIFF_QK_DIM // 4
MLA_HEADS = 4
MLA_Q_RANK = 256
MLA_KV_RANK = 128
MLA_NOPE_DIM = 128
MLA_ROPE_DIM = 64
MLA_V_DIM = 128
MLA_QK_PAD = 256
MEM_HEADS = 4
MEM_HEAD_DIM = 128
N_EXPERTS = 64
TOP_K = 8
N_GROUPS = 8
TOPK_GROUPS = 4
GROUP_SIZE = N_EXPERTS // N_GROUPS
EXPERT_DIM = 256
SHARED_DIM = 256
ROUTED_SCALE = 2.5
DEPTH = 1
DEEPNORM_ALPHA = (2.0 * DEPTH) ** 0.25
LOG2E = math.log2(math.e)

V7X_LANES = 128
V7X_SUBLANES = 8
V7X_VMEM_LIMIT_BYTES = 56 * 1024 * 1024

N_DQ = DIFF_HEADS * 2 * DIFF_QK_DIM
N_DV = DIFF_HEADS * DIFF_V_DIM
O_DK = N_DQ
O_DV = 2 * N_DQ
O_CQ = O_DV + N_DV
O_CKV = O_CQ + MLA_Q_RANK
O_KR = O_CKV + MLA_KV_RANK
N_IN_PAD = O_KR + V7X_LANES


def _cparams(sem):
    return pltpu.CompilerParams(dimension_semantics=sem, vmem_limit_bytes=V7X_VMEM_LIMIT_BYTES)


def _dot(a, b):
    return jnp.dot(a, b, preferred_element_type=F32)


def _dot_nt(a, b):
    return lax.dot_general(a, b, (((1,), (1,)), ((), ())), preferred_element_type=F32)


def _rms(x, g):
    return x * lax.rsqrt(jnp.mean(jnp.square(x), axis=-1, keepdims=True) + RMS_EPS) * g


def _rms_rows(x, g):
    return x * lax.rsqrt(jnp.mean(jnp.square(x), axis=0, keepdims=True) + RMS_EPS) * g


def _layer_norm(x, g, b):
    mu = jnp.mean(x, axis=-1, keepdims=True)
    xc = x - mu
    var = jnp.mean(jnp.square(xc), axis=-1, keepdims=True)
    return xc * lax.rsqrt(var + LN_EPS) * g + b


def _rope_rows(x, cos, sin, period):
    h = cos.shape[0]
    pieces = []
    for base in range(0, x.shape[0], period):
        x1 = x[base:base + h]
        x2 = x[base + h:base + 2 * h]
        pieces.append(x1 * cos - x2 * sin)
        pieces.append(x2 * cos + x1 * sin)
        if period > 2 * h:
            pieces.append(x[base + 2 * h:base + period])
    return jnp.concatenate(pieces, axis=0)


def _proj_kernel(x_ref, pos_ref, fd_ref, fm_ref, w_in_ref, gq_ref, wuq_ref, gkv_ref, wukv_ref,
                 dq_ref, dk_ref, dv_ref, mq_ref, mk_ref, mv_ref):
    xb = x_ref[...].astype(BF16)
    pos = pos_ref[...].astype(F32)
    ang_d = fd_ref[...] * pos
    ang_m = fm_ref[...] * pos
    cos_d, sin_d = jnp.cos(ang_d), jnp.sin(ang_d)
    cos_m, sin_m = jnp.cos(ang_m), jnp.sin(ang_m)

    dq = _rope_rows(_dot_nt(w_in_ref[0:N_DQ, :], xb), cos_d, sin_d, DIFF_QK_DIM)
    dq_ref[...] = (dq * (DIFF_QK_DIM ** -0.5 * LOG2E)).astype(BF16)
    dk = _rope_rows(_dot_nt(w_in_ref[O_DK:O_DK + N_DQ, :], xb), cos_d, sin_d, DIFF_QK_DIM)
    dk_ref[...] = dk.T.astype(BF16)
    dv_ref[...] = _dot_nt(w_in_ref[O_DV:O_DV + N_DV, :], xb).astype(BF16)

    c_q = _dot_nt(w_in_ref[O_CQ:O_CQ + MLA_Q_RANK, :], xb)
    q = _dot(wuq_ref[...], _rms_rows(c_q, gq_ref[...]).astype(BF16))
    q = q * ((MLA_NOPE_DIM + MLA_ROPE_DIM) ** -0.5 * LOG2E)
    pieces = []
    for h in range(MLA_HEADS):
        o = h * MLA_QK_PAD
        pieces.append(q[o:o + MLA_NOPE_DIM])
        pieces.append(_rope_rows(q[o + MLA_NOPE_DIM:o + MLA_NOPE_DIM + MLA_ROPE_DIM], cos_m, sin_m, MLA_ROPE_DIM))
        pieces.append(q[o + MLA_NOPE_DIM + MLA_ROPE_DIM:o + MLA_QK_PAD])
    mq_ref[...] = jnp.concatenate(pieces, axis=0).astype(BF16)

    c_kv = _dot_nt(w_in_ref[O_CKV:O_CKV + MLA_KV_RANK, :], xb)
    kv = _dot(wukv_ref[...], _rms_rows(c_kv, gkv_ref[...]).astype(BF16))
    n_kn = MLA_HEADS * MLA_NOPE_DIM
    mv_ref[...] = kv[n_kn:].astype(BF16)
    k_rope = _dot_nt(w_in_ref[O_KR:O_KR + V7X_LANES, :], xb)
    k_pe = jnp.concatenate([_rope_rows(k_rope[:MLA_ROPE_DIM], cos_m, sin_m, MLA_ROPE_DIM),
                            k_rope[MLA_ROPE_DIM:]], axis=0)
    k_nope_t = kv[:n_kn].T.astype(BF16)
    k_pe_t = k_pe.T.astype(BF16)
    for h in range(MLA_HEADS):
        o = h * MLA_QK_PAD
        mk_ref[:, o:o + MLA_NOPE_DIM] = k_nope_t[:, h * MLA_NOPE_DIM:(h + 1) * MLA_NOPE_DIM]
        mk_ref[:, o + MLA_NOPE_DIM:o + MLA_QK_PAD] = k_pe_t


def _proj_call(x2, pos_row, fd, fm, w_in_t, gq, wuq_t, gkv, wukv_t, tm):
    t = x2.shape[0]
    full = lambda a: pl.BlockSpec(a.shape, lambda i: (0, 0))
    col_blk = lambda r: pl.BlockSpec((r, tm), lambda i: (0, i))
    row_blk = lambda w: pl.BlockSpec((tm, w), lambda i: (i, 0))
    n_mq = MLA_HEADS * MLA_QK_PAD
    return pl.pallas_call(
        _proj_kernel,
        out_shape=[jax.ShapeDtypeStruct((N_DQ, t), BF16), jax.ShapeDtypeStruct((t, N_DQ), BF16),
                   jax.ShapeDtypeStruct((N_DV, t), BF16), jax.ShapeDtypeStruct((n_mq, t), BF16),
                   jax.ShapeDtypeStruct((t, n_mq), BF16), jax.ShapeDtypeStruct((MLA_HEADS * MLA_V_DIM, t), BF16)],
        grid=(t // tm,),
        in_specs=[row_blk(D_MODEL), col_blk(1), full(fd), full(fm), full(w_in_t), full(gq), full(wuq_t),
                  full(gkv), full(wukv_t)],
        out_specs=[col_blk(N_DQ), row_blk(N_DQ), col_blk(N_DV), col_blk(n_mq), row_blk(n_mq),
                   col_blk(MLA_HEADS * MLA_V_DIM)],
        compiler_params=_cparams(("parallel",)),
        name="proj_rope",
    )(x2, pos_row, fd, fm, w_in_t, gq, wuq_t, gkv, wukv_t)


def _flash_t(qq, k_ref, vt_ref, s_refs, p_refs, acc_ref, i, tq):
    n = qq.shape[1]
    tk = s_refs[0].shape[0]
    assert tq == 2 * tk
    q_pos = i * tq + (lax.broadcasted_iota(jnp.int32, (tk, n), 1) & (tq - 1))
    key_row = lax.broadcasted_iota(jnp.int32, (tk, n), 0)

    def scores(c, s_ref):
        off = pl.multiple_of(c * tk, tk)
        s = _dot(k_ref[pl.ds(off, tk), :], qq)
        s_ref[...] = s
        return jnp.max(s, axis=0, keepdims=True)

    def softmax(c, s_ref, p_ref, cmax, m, l, masked):
        s = s_ref[...]
        if masked:
            s = jnp.where(key_row + c * tk <= q_pos, s, NEG_INF)
            cmax = jnp.max(s, axis=0, keepdims=True)
        m_new = jnp.maximum(m, cmax)
        alpha = jnp.exp2(m - m_new)
        p = jnp.exp2(s - m_new)
        p_ref[...] = p.astype(BF16)
        return m_new, alpha * l + jnp.sum(p, axis=0, keepdims=True), alpha

    def values(c, p_ref, alpha):
        off = pl.multiple_of(jnp.maximum(c, 0) * tk, tk)
        acc_ref[...] = alpha * acc_ref[...] + _dot(vt_ref[:, pl.ds(off, tk)], p_ref[...])

    def pair(u, carry, masked_second):
        cm1, alpha0, m, l = carry
        c = 2 * u
        cm0 = scores(c, s_refs[0])
        values(c - 2, p_refs[0], alpha0)
        m, l, alpha1 = softmax(c - 1, s_refs[1], p_refs[1], cm1, m, l, False)
        cm1 = scores(c + 1, s_refs[1])
        values(c - 1, p_refs[1], alpha1)
        m, l, alpha0 = softmax(c, s_refs[0], p_refs[0], cm0, m, l, masked_second)
        return cm1, alpha0, m, l

    s_refs[1][...] = jnp.full(s_refs[1].shape, 2 * NEG_INF, F32)
    p_refs[0][...] = jnp.zeros(p_refs[0].shape, BF16)
    acc_ref[...] = jnp.zeros(acc_ref.shape, F32)
    init = (jnp.full((1, n), 2 * NEG_INF, F32), jnp.ones((1, n), F32),
            jnp.full((1, n), NEG_INF, F32), jnp.zeros((1, n), F32))
    carry = lax.fori_loop(0, i, functools.partial(pair, masked_second=False), init)
    cm1, alpha0, m, l = pair(i, carry, True)
    values(2 * i, p_refs[0], alpha0)
    m, l, alpha1 = softmax(2 * i + 1, s_refs[1], p_refs[1], cm1, m, l, True)
    values(2 * i + 1, p_refs[1], alpha1)
    return acc_ref[...] / l


def _diff_attn_kernel(qt_ref, k_ref, vt_ref, lam_ref, g_ref, o_ref, s0, s1, p0, p1, acc_ref, *, tq, lam_init):
    i = pl.program_id(2)
    qt = qt_ref[...]
    row = lax.broadcasted_iota(jnp.int32, qt.shape, 0)
    zero = jnp.zeros_like(qt)
    qq = jnp.concatenate([jnp.where(row < DIFF_QK_DIM, qt, zero), jnp.where(row >= DIFF_QK_DIM, qt, zero)], axis=1)
    o = _flash_t(qq, k_ref, vt_ref, (s0, s1), (p0, p1), acc_ref, i, tq)
    lv = lam_ref[...]
    lam = (jnp.exp(jnp.sum(lv[0:1] * lv[1:2], axis=1, keepdims=True))
           - jnp.exp(jnp.sum(lv[2:3] * lv[3:4], axis=1, keepdims=True)) + lam_init)
    d = o[:, :tq] - lam * o[:, tq:]
    o_ref[...] = (_rms_rows(d, g_ref[...]) * (1.0 - lam_init)).T.astype(o_ref.dtype)


def _mla_attn_kernel(qt_ref, k_ref, vt_ref, g_ref, o_ref, s0, s1, p0, p1, acc_ref, *, tq):
    i = pl.program_id(2)
    o = _flash_t(qt_ref[...], k_ref, vt_ref, (s0, s1), (p0, p1), acc_ref, i, tq)
    o_ref[...] = _rms_rows(o, g_ref[...]).T.astype(o_ref.dtype)


def _attn_call(kernel, name, qt, k, vt, extras, b, s, heads, dk_w, dv_w, tq, n_maps):
    nq = s // tq
    t = b * s
    tk = tq // 2
    n = n_maps * tq
    small = [pl.BlockSpec(a.shape, lambda bi, h, i: (0, 0)) for a in extras]
    return pl.pallas_call(
        kernel,
        out_shape=jax.ShapeDtypeStruct((t, heads * dv_w), BF16),
        grid=(b, heads, nq),
        in_specs=[pl.BlockSpec((dk_w, tq), lambda bi, h, i: (h, bi * nq + i)),
                  pl.BlockSpec((s, dk_w), lambda bi, h, i: (bi, h)),
                  pl.BlockSpec((dv_w, s), lambda bi, h, i: (h, bi))] + small,
        out_specs=pl.BlockSpec((tq, dv_w), lambda bi, h, i: (bi * nq + i, h)),
        scratch_shapes=[pltpu.VMEM((tk, n), F32), pltpu.VMEM((tk, n), F32),
                        pltpu.VMEM((tk, n), BF16), pltpu.VMEM((tk, n), BF16), pltpu.VMEM((dv_w, n), F32)],
        compiler_params=_cparams(("parallel", "parallel", "arbitrary")),
        name=name,
    )(qt, k, vt, *extras)


def _memkv_kernel(mem_ref, wk_ref, wv_ref, k_ref, v_ref):
    mb = mem_ref[...].astype(BF16)
    k_ref[...] = _dot(mb, wk_ref[...]).astype(BF16)
    v_ref[...] = _dot(mb, wv_ref[...]).astype(BF16)


def _memkv_call(mem2, wk, wv):
    n = mem2.shape[0]
    w = wk.shape[1]
    tm = N_MEM
    return pl.pallas_call(
        _memkv_kernel,
        out_shape=[jax.ShapeDtypeStruct((n, w), BF16)] * 2,
        grid=(n // tm,),
        in_specs=[pl.BlockSpec((tm, D_MODEL), lambda i: (i, 0)),
                  pl.BlockSpec(wk.shape, lambda i: (0, 0)),
                  pl.BlockSpec(wv.shape, lambda i: (0, 0))],
        out_specs=[pl.BlockSpec((tm, w), lambda i: (i, 0))] * 2,
        compiler_params=_cparams(("parallel",)),
        name="mem_kv",
    )(mem2, wk, wv)


def _pack_bf16_pairs(x):
    half = x.shape[1] // 2
    r = pltpu.bitcast(x.astype(BF16).astype(F32), jnp.uint32)
    return (r[:, :half] >> 16) | (r[:, half:] & jnp.uint32(0xFFFF0000))


def _unpack_bf16_pairs(w):
    lo = pltpu.bitcast(w << 16, F32).astype(BF16)
    hi = pltpu.bitcast(w & jnp.uint32(0xFFFF0000), F32).astype(BF16)
    return lo, hi


def _post_kernel(da_ref, ma_ref, x_ref, wo_ref, g1_ref, b1_ref, wq_ref, km_ref, vm_ref, wmo_ref,
                 g2_ref, b2_ref, h_ref, hp_ref):
    n_d = DIFF_HEADS * DIFF_V_DIM
    mix = _dot(da_ref[...], wo_ref[0:n_d, :]) + _dot(ma_ref[...], wo_ref[n_d:, :])
    h1 = _layer_norm(DEEPNORM_ALPHA * x_ref[...] + mix, g1_ref[...], b1_ref[...])
    q = (_dot(h1.astype(BF16), wq_ref[...]) * (MEM_HEAD_DIM ** -0.5)).astype(BF16)
    outs = []
    for h in range(MEM_HEADS):
        sl = slice(h * MEM_HEAD_DIM, (h + 1) * MEM_HEAD_DIM)
        s = _dot_nt(q[:, sl], km_ref[:, sl])
        m = jnp.max(s, axis=1, keepdims=True)
        p = jnp.exp(s - m)
        p = p / jnp.sum(p, axis=1, keepdims=True)
        outs.append(_dot(p.astype(BF16), vm_ref[:, sl]))
    o = jnp.concatenate(outs, axis=1).astype(BF16)
    xat = _dot(o, wmo_ref[...])
    h2 = _layer_norm(DEEPNORM_ALPHA * h1 + xat, g2_ref[...], b2_ref[...])
    h_ref[...] = h2
    hp_ref[...] = _pack_bf16_pairs(h2)


def _post_call(diff_o, mla_o, x2, wo, g1, b1, wq, kmem, vmem, wmo, g2, b2, s, tm):
    t = x2.shape[0]
    per_b = s // tm
    row = lambda w: pl.BlockSpec((tm, w), lambda i: (i, 0))
    full = lambda a: pl.BlockSpec(a.shape, lambda i: (0, 0))
    memb = pl.BlockSpec((N_MEM, kmem.shape[1]), lambda i: (i // per_b, 0))
    return pl.pallas_call(
        _post_kernel,
        out_shape=[jax.ShapeDtypeStruct((t, D_MODEL), F32), jax.ShapeDtypeStruct((t, D_MODEL // 2), jnp.uint32)],
        grid=(t // tm,),
        in_specs=[row(diff_o.shape[1]), row(mla_o.shape[1]), row(D_MODEL), full(wo), full(g1), full(b1),
                  full(wq), memb, memb, full(wmo), full(g2), full(b2)],
        out_specs=[row(D_MODEL), row(D_MODEL // 2)],
        compiler_params=_cparams(("parallel",)),
        name="post_mix_xattn",
    )(diff_o, mla_o, x2, wo, g1, b1, wq, kmem, vmem, wmo, g2, b2)


def _router_kernel(h_ref, rwt_ref, bias_ref, tri_ref, eid_ref, rnk_ref, wt_ref, cnt_ref):
    tm = h_ref.shape[0]

    @pl.when(pl.program_id(0) == 0)
    def _():
        cnt_ref[...] = jnp.zeros(cnt_ref.shape, cnt_ref.dtype)

    logits = lax.dot_general(rwt_ref[...], h_ref[...], (((1,), (1,)), ((), ())),
                             precision=lax.Precision.HIGHEST, preferred_element_type=F32)
    scores = jax.nn.sigmoid(logits)
    choice = scores + bias_ref[...]
    neg = float("-inf")

    c3 = choice.reshape(N_GROUPS, GROUP_SIZE, tm)
    mi = lax.broadcasted_iota(jnp.int32, c3.shape, 1)
    top1 = jnp.max(c3, axis=1, keepdims=True)
    first = jnp.min(jnp.where(c3 == top1, mi, GROUP_SIZE), axis=1, keepdims=True)
    top2 = jnp.max(jnp.where(mi == first, neg, c3), axis=1, keepdims=True)
    gs = (top1 + top2).reshape(N_GROUPS, tm)

    gi = lax.broadcasted_iota(jnp.int32, gs.shape, 0)
    rank = jnp.zeros(gs.shape, jnp.int32)
    for g in range(N_GROUPS):
        rowg = gs[g:g + 1, :]
        beats = (rowg > gs) | ((rowg == gs) & (g < gi))
        rank = rank + beats.astype(jnp.int32)
    gmask = rank < TOPK_GROUPS
    emask = jnp.broadcast_to(gmask.reshape(N_GROUPS, 1, tm), c3.shape).reshape(N_EXPERTS, tm)

    work = jnp.where(emask, choice, NEG_INF)
    ei = lax.broadcasted_iota(jnp.int32, work.shape, 0)
    sel = jnp.zeros(work.shape, jnp.bool_)
    picks = []
    for _ in range(TOP_K):
        mx = jnp.max(work, axis=0, keepdims=True)
        fi = jnp.min(jnp.where(work == mx, ei, N_EXPERTS), axis=0, keepdims=True)
        pick = ei == fi
        picks.append((fi, pick))
        sel = sel | pick
        work = jnp.where(pick, neg, work)
    top_w = jnp.where(sel, scores, 0.0)
    gates = top_w / jnp.sum(top_w, axis=0, keepdims=True) * ROUTED_SCALE

    sel_b = jnp.where(sel, 1.0, 0.0).astype(BF16)
    before = _dot(sel_b, tri_ref[...]).astype(jnp.int32)
    run = cnt_ref[:, 0:1]
    pos = run + before
    cnt_ref[...] = jnp.broadcast_to(run + jnp.sum(sel.astype(jnp.int32), axis=1, keepdims=True), cnt_ref.shape)

    eid_ref[...] = jnp.concatenate([fi for fi, _ in picks], axis=0)
    rnk_ref[...] = jnp.concatenate(
        [jnp.sum(jnp.where(pick, pos, 0), axis=0, keepdims=True) for _, pick in picks], axis=0)
    w_rows = [jnp.sum(jnp.where(pick, gates, 0.0), axis=0, keepdims=True) for _, pick in picks]
    w_pad = jnp.concatenate(w_rows + [jnp.zeros((V7X_LANES - TOP_K, tm), F32)], axis=0)
    wt_ref[...] = w_pad.T


def _router_call(h2, rwt, bias, tm):
    t = h2.shape[0]
    tri = jnp.triu(jnp.ones((tm, tm), BF16), k=1)
    full = lambda a: pl.BlockSpec(a.shape, lambda i: (0, 0))
    return pl.pallas_call(
        _router_kernel,
        out_shape=[jax.ShapeDtypeStruct((TOP_K, t), jnp.int32), jax.ShapeDtypeStruct((TOP_K, t), jnp.int32),
                   jax.ShapeDtypeStruct((t, V7X_LANES), F32), jax.ShapeDtypeStruct((N_EXPERTS, V7X_LANES), jnp.int32)],
        grid=(t // tm,),
        in_specs=[pl.BlockSpec((tm, D_MODEL), lambda i: (i, 0)), full(rwt), full(bias), full(tri)],
        out_specs=[pl.BlockSpec((TOP_K, tm), lambda i: (0, i)), pl.BlockSpec((TOP_K, tm), lambda i: (0, i)),
                   pl.BlockSpec((tm, V7X_LANES), lambda i: (i, 0)),
                   pl.BlockSpec((N_EXPERTS, V7X_LANES), lambda i: (0, 0))],
        compiler_params=_cparams(("arbitrary",)),
        name="router",
    )(h2, rwt, bias, tri)


def _dispatch_kernel(fill_ref, dest_ref, x_ref, xs_ref, zero_ref, sem, *, tmx):
    tm = x_ref.shape[0]

    @pl.when(pl.program_id(0) == 0)
    def _():
        zero_ref[...] = jnp.zeros(zero_ref.shape, zero_ref.dtype)
        copies = [pltpu.make_async_copy(zero_ref, xs_ref.at[pl.ds(pl.multiple_of(fill_ref[e], tmx), tmx)], sem)
                  for e in range(N_EXPERTS)]
        for c in copies:
            c.start()
        for c in copies:
            c.wait()

    def row_copy(t, k):
        return pltpu.make_async_copy(x_ref.at[pl.ds(t, 1)], xs_ref.at[pl.ds(dest_ref[k, t], 1)], sem)

    def issue(t, carry):
        for k in range(TOP_K):
            row_copy(t, k).start(priority=k % 2)
        return carry

    def drain(t, carry):
        for k in range(TOP_K):
            row_copy(t, k).wait()
        return carry

    lax.fori_loop(0, tm, issue, 0)
    lax.fori_loop(0, tm, drain, 0)


def _dispatch_call(fill_rows, dest, hp, n_rows, tmx, tm):
    t, w = hp.shape
    return pl.pallas_call(
        functools.partial(_dispatch_kernel, tmx=tmx),
        out_shape=jax.ShapeDtypeStruct((n_rows, w), hp.dtype),
        grid_spec=pltpu.PrefetchScalarGridSpec(
            num_scalar_prefetch=1,
            grid=(t // tm,),
            in_specs=[pl.BlockSpec((TOP_K, tm), lambda i, f: (0, i), memory_space=pltpu.SMEM),
                      pl.BlockSpec((tm, w), lambda i, f: (i, 0))],
            out_specs=pl.BlockSpec(memory_space=pl.ANY),
            scratch_shapes=[pltpu.VMEM((tmx, w), hp.dtype), pltpu.SemaphoreType.DMA],
        ),
        compiler_params=_cparams(("arbitrary",)),
        name="moe_dispatch",
    )(fill_rows, dest, hp)


def _experts_kernel(te_ref, nu_ref, xs_ref, wg_ref, wu_ref, wd_ref, y_ref):
    @pl.when(pl.program_id(0) < nu_ref[0])
    def _():
        half = D_MODEL // 2
        lo, hi = _unpack_bf16_pairs(xs_ref[...])
        a = _dot(lo, wg_ref[0, :half, :]) + _dot(hi, wg_ref[0, half:, :])
        u = _dot(lo, wu_ref[0, :half, :]) + _dot(hi, wu_ref[0, half:, :])
        y_ref[...] = _dot((jax.nn.silu(a) * u).astype(BF16), wd_ref[0])


def _experts_call(tile_expert, n_used, xs, wg, wu, wd, tmx):
    n_rows, w = xs.shape
    tile = lambda i, te, nu: jnp.minimum(i, nu[0] - 1)
    wspec = lambda shp: pl.BlockSpec((1,) + shp, lambda i, te, nu: (te[tile(i, te, nu)], 0, 0))
    return pl.pallas_call(
        _experts_kernel,
        out_shape=jax.ShapeDtypeStruct((n_rows, D_MODEL), F32),
        grid_spec=pltpu.PrefetchScalarGridSpec(
            num_scalar_prefetch=2,
            grid=(n_rows // tmx,),
            in_specs=[pl.BlockSpec((tmx, w), lambda i, te, nu: (tile(i, te, nu), 0)),
                      wspec((D_MODEL, EXPERT_DIM)), wspec((D_MODEL, EXPERT_DIM)), wspec((EXPERT_DIM, D_MODEL))],
            out_specs=pl.BlockSpec((tmx, D_MODEL), lambda i, te, nu: (tile(i, te, nu), 0)),
        ),
        compiler_params=_cparams(("arbitrary",)),
        name="moe_experts",
    )(tile_expert, n_used, xs, wg, wu, wd)


def _combine_kernel(dest_ref, wt_ref, h_ref, y_ref, sg_ref, su_ref, sd_ref, g3_ref, b3_ref, o_ref, ybuf, sem):
    tm = h_ref.shape[0]

    def row_copy(t, k):
        return pltpu.make_async_copy(y_ref.at[pl.ds(dest_ref[k, t], 1)], ybuf.at[k, pl.ds(t, 1)], sem)

    def issue(t, carry):
        for k in range(TOP_K):
            row_copy(t, k).start(priority=k % 2)
        return carry

    def drain(t, carry):
        for k in range(TOP_K):
            row_copy(t, k).wait()
        return carry

    lax.fori_loop(0, tm, issue, 0)
    h = h_ref[...]
    hb = h.astype(BF16)
    acc = _dot((jax.nn.silu(_dot(hb, sg_ref[...])) * _dot(hb, su_ref[...])).astype(BF16), sd_ref[...])
    lax.fori_loop(0, tm, drain, 0)
    wt = wt_ref[...]
    for k in range(TOP_K):
        acc = acc + wt[:, k:k + 1] * ybuf[k]
    o_ref[...] = _layer_norm(DEEPNORM_ALPHA * h + acc, g3_ref[...], b3_ref[...])


def _combine_call(dest, wt, h2, y, sg, su, sd, g3, b3, tm):
    t = h2.shape[0]
    full = lambda a: pl.BlockSpec(a.shape, lambda i: (0, 0))
    return pl.pallas_call(
        _combine_kernel,
        out_shape=jax.ShapeDtypeStruct((t, D_MODEL), F32),
        grid=(t // tm,),
        in_specs=[pl.BlockSpec((TOP_K, tm), lambda i: (0, i), memory_space=pltpu.SMEM),
                  pl.BlockSpec((tm, V7X_LANES), lambda i: (i, 0)),
                  pl.BlockSpec((tm, D_MODEL), lambda i: (i, 0)),
                  pl.BlockSpec(memory_space=pl.ANY),
                  full(sg), full(su), full(sd), full(g3), full(b3)],
        out_specs=pl.BlockSpec((tm, D_MODEL), lambda i: (i, 0)),
        scratch_shapes=[pltpu.VMEM((TOP_K, tm, D_MODEL), F32), pltpu.SemaphoreType.DMA],
        compiler_params=_cparams(("arbitrary",)),
        name="moe_combine",
    )(dest, wt, h2, y, sg, su, sd, g3, b3)


def _moe(h2, hp, router_w, router_bias, wg, wu, wd, sg, su, sd, g3, b3, tm_route, tmx, tm_move):
    t = h2.shape[0]
    eid, rnk, wt, cnt = _router_call(h2, router_w.T, router_bias.reshape(-1, 1), tm_route)
    counts = cnt[:, 0]
    tiles_per_e = (counts + (tmx - 1)) // tmx
    tile_end = jnp.cumsum(tiles_per_e)
    row_start = (tile_end - tiles_per_e) * tmx
    n_tiles = (t * TOP_K) // tmx + N_EXPERTS
    n_used = tile_end[-1:].astype(jnp.int32)
    tile_expert = jnp.minimum(
        jnp.sum(tile_end[None, :] <= jnp.arange(n_tiles, dtype=jnp.int32)[:, None], axis=1), N_EXPERTS - 1
    ).astype(jnp.int32)
    fill_rows = (jnp.maximum(tile_end - 1, 0) * tmx).astype(jnp.int32)
    dest = (row_start.astype(jnp.int32)[eid] + rnk).astype(jnp.int32)
    xs = _dispatch_call(fill_rows, dest, hp, n_tiles * tmx, tmx, tm_move)
    y = _experts_call(tile_expert, n_used, xs, wg, wu, wd, tmx)
    return _combine_call(dest, wt, h2, y, sg, su, sd, g3, b3, tm_move)


def _prep_w_in_t(w_in):
    pad = jnp.zeros((D_MODEL, V7X_LANES - MLA_ROPE_DIM), w_in.dtype)
    return jnp.concatenate([w_in, pad], axis=1).T.astype(BF16)


def _prep_w_uq_t(w_uq):
    w = w_uq.reshape(MLA_Q_RANK, MLA_HEADS, MLA_NOPE_DIM + MLA_ROPE_DIM)
    pad = jnp.zeros((MLA_Q_RANK, MLA_HEADS, MLA_QK_PAD - MLA_NOPE_DIM - MLA_ROPE_DIM), w.dtype)
    return jnp.concatenate([w, pad], axis=2).reshape(MLA_Q_RANK, MLA_HEADS * MLA_QK_PAD).T.astype(BF16)


def _prep_w_ukv_t(w_ukv):
    w = w_ukv.reshape(MLA_KV_RANK, MLA_HEADS, MLA_NOPE_DIM + MLA_V_DIM)
    k = w[:, :, :MLA_NOPE_DIM].reshape(MLA_KV_RANK, MLA_HEADS * MLA_NOPE_DIM)
    v = w[:, :, MLA_NOPE_DIM:].reshape(MLA_KV_RANK, MLA_HEADS * MLA_V_DIM)
    return jnp.concatenate([k, v], axis=1).T.astype(BF16)


def _rope_inv_freq(dim):
    return (ROPE_THETA ** (-jnp.arange(0, dim, 2, dtype=F32) / dim)).reshape(-1, 1)


def _layer(x2, mem2, pos_row, b, s, l, w_in, lam_q1, lam_k1, lam_q2, lam_k2, diff_subln_g, mla_q_norm_g,
           mla_w_uq, mla_kv_norm_g, mla_w_ukv, mla_out_norm_g, w_o, ln1_g, ln1_b, mem_w_q, mem_w_k, mem_w_v,
           mem_w_o, ln2_g, ln2_b, router_w, router_bias, exp_w_gate, exp_w_up, exp_w_down, sh_w_gate,
           sh_w_up, sh_w_down, ln3_g, ln3_b):
    row = lambda a: a.reshape(1, -1)
    col = lambda a: a.reshape(-1, 1)
    dqt, dk, dvt, mqt, mk, mvt = _proj_call(
        x2, pos_row, _rope_inv_freq(DIFF_ROT_DIM), _rope_inv_freq(MLA_ROPE_DIM), _prep_w_in_t(w_in),
        col(mla_q_norm_g), _prep_w_uq_t(mla_w_uq), col(mla_kv_norm_g), _prep_w_ukv_t(mla_w_ukv), tm=min(512, s))

    lam_init = 0.8 - 0.6 * math.exp(-0.3 * l)
    lam_vecs = jnp.stack([lam_q1, lam_k1, lam_q2, lam_k2]).astype(F32)
    tq = min(512, s)
    diff_o = _attn_call(
        functools.partial(_diff_attn_kernel, tq=tq, lam_init=lam_init), "diff_attn",
        dqt, dk, dvt, [lam_vecs, col(diff_subln_g)], b, s, DIFF_HEADS, 2 * DIFF_QK_DIM, DIFF_V_DIM, tq, 2)
    mla_o = _attn_call(
        functools.partial(_mla_attn_kernel, tq=tq), "mla_attn",
        mqt, mk, mvt, [col(mla_out_norm_g)], b, s, MLA_HEADS, MLA_QK_PAD, MLA_V_DIM, tq, 1)

    kmem, vmem = _memkv_call(mem2, mem_w_k.astype(BF16), mem_w_v.astype(BF16))
    h2, hp = _post_call(diff_o, mla_o, x2, w_o.astype(BF16), row(ln1_g), row(ln1_b), mem_w_q.astype(BF16),
                        kmem, vmem, mem_w_o.astype(BF16), row(ln2_g), row(ln2_b), s, tm=min(512, s))

    return _moe(h2, hp, router_w, router_bias, exp_w_gate.astype(BF16), exp_w_up.astype(BF16),
                exp_w_down.astype(BF16), sh_w_gate.astype(BF16), sh_w_up.astype(BF16), sh_w_down.astype(BF16),
                row(ln3_g), row(ln3_b), tm_route=min(512, s), tmx=256, tm_move=min(256, s))


def kernel(x, mem, positions, w_in, lam_q1, lam_k1, lam_q2, lam_k2, diff_subln_g, mla_q_norm_g, mla_w_uq,
           mla_kv_norm_g, mla_w_ukv, mla_out_norm_g, w_o, ln1_g, ln1_b, mem_w_q, mem_w_k, mem_w_v, mem_w_o,
           ln2_g, ln2_b, router_w, router_bias, exp_w_gate, exp_w_up, exp_w_down, sh_w_gate, sh_w_up,
           sh_w_down, ln3_g, ln3_b):
    b, s, d = x.shape
    h = x.reshape(b * s, d)
    mem2 = mem.reshape(b * mem.shape[1], d)
    pos_row = positions.reshape(1, b * s)
    params = (w_in, lam_q1, lam_k1, lam_q2, lam_k2, diff_subln_g, mla_q_norm_g, mla_w_uq, mla_kv_norm_g,
              mla_w_ukv, mla_out_norm_g, w_o, ln1_g, ln1_b, mem_w_q, mem_w_k, mem_w_v, mem_w_o, ln2_g, ln2_b,
              router_w, router_bias, exp_w_gate, exp_w_up, exp_w_down, sh_w_gate, sh_w_up, sh_w_down,
              ln3_g, ln3_b)
    for l in range(w_in.shape[0]):
        h = _layer(h, mem2, pos_row, b, s, l, *[p[l] for p in params])
    return h.reshape(b, s, d)
```

```python
import functools
import math

import jax
import jax.numpy as jnp
from jax import lax
from jax.experimental import pallas as pl
from jax.experimental.pallas import tpu as pltpu
from jax.experimental.pallas import tpu_sc as plsc

F32 = jnp.float32
BF16 = jnp.bfloat16

D_MODEL = 1024
N_MEM = 256
ROPE_THETA = 500000.0
NEG_INF = -1e30
LN_EPS = 1e-5
RMS_EPS = 1e-6
DIFF_HEADS = 4
DIFF_QK_DIM = 64
DIFF_V_DIM = 128
DIFF_ROT_DIM = DIFF_QK_DIM // 4
MLA_HEADS = 4
MLA_Q_RANK = 256
MLA_KV_RANK = 128
MLA_NOPE_DIM = 128
MLA_ROPE_DIM = 64
MLA_V_DIM = 128
MLA_QK_PAD = 256
MEM_HEADS = 4
MEM_HEAD_DIM = 128
N_EXPERTS = 64
TOP_K = 8
N_GROUPS = 8
TOPK_GROUPS = 4
GROUP_SIZE = N_EXPERTS // N_GROUPS
EXPERT_DIM = 256
SHARED_DIM = 256
ROUTED_SCALE = 2.5
DEPTH = 1
DEEPNORM_ALPHA = (2.0 * DEPTH) ** 0.25
LOG2E = math.log2(math.e)

V7X_LANES = 128
V7X_SUBLANES = 8
V7X_VMEM_LIMIT_BYTES = 56 * 1024 * 1024
V7X_SC_CORES = 2
V7X_SC_SUBCORES = 16
SC_INDEX_WINDOW = 128

N_DQ = DIFF_HEADS * 2 * DIFF_QK_DIM
N_DV = DIFF_HEADS * DIFF_V_DIM
O_DK = N_DQ
O_DV = 2 * N_DQ
O_CQ = O_DV + N_DV
O_CKV = O_CQ + MLA_Q_RANK
O_KR = O_CKV + MLA_KV_RANK
N_IN_PAD = O_KR + V7X_LANES


def _cparams(sem):
    return pltpu.CompilerParams(dimension_semantics=sem, vmem_limit_bytes=V7X_VMEM_LIMIT_BYTES)


def _dot(a, b):
    return jnp.dot(a, b, preferred_element_type=F32)


def _dot_nt(a, b):
    return lax.dot_general(a, b, (((1,), (1,)), ((), ())), preferred_element_type=F32)


def _rms(x, g):
    return x * lax.rsqrt(jnp.mean(jnp.square(x), axis=-1, keepdims=True) + RMS_EPS) * g


def _rms_rows(x, g):
    return x * lax.rsqrt(jnp.mean(jnp.square(x), axis=0, keepdims=True) + RMS_EPS) * g


def _layer_norm(x, g, b):
    mu = jnp.mean(x, axis=-1, keepdims=True)
    xc = x - mu
    var = jnp.mean(jnp.square(xc), axis=-1, keepdims=True)
    return xc * lax.rsqrt(var + LN_EPS) * g + b


def _rope_rows(x, cos, sin, period):
    h = cos.shape[0]
    pieces = []
    for base in range(0, x.shape[0], period):
        x1 = x[base:base + h]
        x2 = x[base + h:base + 2 * h]
        pieces.append(x1 * cos - x2 * sin)
        pieces.append(x2 * cos + x1 * sin)
        if period > 2 * h:
            pieces.append(x[base + 2 * h:base + period])
    return jnp.concatenate(pieces, axis=0)


def _proj_kernel(x_ref, pos_ref, fd_ref, fm_ref, w_in_ref, gq_ref, wuq_ref, gkv_ref, wukv_ref,
                 dq_ref, dk_ref, dv_ref, mq_ref, mk_ref, mv_ref):
    xb = x_ref[...].astype(BF16)
    pos = pos_ref[...].astype(F32)
    ang_d = fd_ref[...] * pos
    ang_m = fm_ref[...] * pos
    cos_d, sin_d = jnp.cos(ang_d), jnp.sin(ang_d)
    cos_m, sin_m = jnp.cos(ang_m), jnp.sin(ang_m)

    dq = _rope_rows(_dot_nt(w_in_ref[0:N_DQ, :], xb), cos_d, sin_d, DIFF_QK_DIM)
    dq_ref[...] = (dq * (DIFF_QK_DIM ** -0.5 * LOG2E)).astype(BF16)
    dk = _rope_rows(_dot_nt(w_in_ref[O_DK:O_DK + N_DQ, :], xb), cos_d, sin_d, DIFF_QK_DIM)
    dk_ref[...] = dk.T.astype(BF16)
    dv_ref[...] = _dot_nt(w_in_ref[O_DV:O_DV + N_DV, :], xb).astype(BF16)

    c_q = _dot_nt(w_in_ref[O_CQ:O_CQ + MLA_Q_RANK, :], xb)
    q = _dot(wuq_ref[...], _rms_rows(c_q, gq_ref[...]).astype(BF16))
    q = q * ((MLA_NOPE_DIM + MLA_ROPE_DIM) ** -0.5 * LOG2E)
    pieces = []
    for h in range(MLA_HEADS):
        o = h * MLA_QK_PAD
        pieces.append(q[o:o + MLA_NOPE_DIM])
        pieces.append(_rope_rows(q[o + MLA_NOPE_DIM:o + MLA_NOPE_DIM + MLA_ROPE_DIM], cos_m, sin_m, MLA_ROPE_DIM))
        pieces.append(q[o + MLA_NOPE_DIM + MLA_ROPE_DIM:o + MLA_QK_PAD])
    mq_ref[...] = jnp.concatenate(pieces, axis=0).astype(BF16)

    c_kv = _dot_nt(w_in_ref[O_CKV:O_CKV + MLA_KV_RANK, :], xb)
    kv = _dot(wukv_ref[...], _rms_rows(c_kv, gkv_ref[...]).astype(BF16))
    n_kn = MLA_HEADS * MLA_NOPE_DIM
    mv_ref[...] = kv[n_kn:].astype(BF16)
    k_rope = _dot_nt(w_in_ref[O_KR:O_KR + V7X_LANES, :], xb)
    k_pe = jnp.concatenate([_rope_rows(k_rope[:MLA_ROPE_DIM], cos_m, sin_m, MLA_ROPE_DIM),
                            k_rope[MLA_ROPE_DIM:]], axis=0)
    k_nope_t = kv[:n_kn].T.astype(BF16)
    k_pe_t = k_pe.T.astype(BF16)
    for h in range(MLA_HEADS):
        o = h * MLA_QK_PAD
        mk_ref[:, o:o + MLA_NOPE_DIM] = k_nope_t[:, h * MLA_NOPE_DIM:(h + 1) * MLA_NOPE_DIM]
        mk_ref[:, o + MLA_NOPE_DIM:o + MLA_QK_PAD] = k_pe_t


def _proj_call(x2, pos_row, fd, fm, w_in_t, gq, wuq_t, gkv, wukv_t, tm):
    t = x2.shape[0]
    full = lambda a: pl.BlockSpec(a.shape, lambda i: (0, 0))
    col_blk = lambda r: pl.BlockSpec((r, tm), lambda i: (0, i))
    row_blk = lambda w: pl.BlockSpec((tm, w), lambda i: (i, 0))
    n_mq = MLA_HEADS * MLA_QK_PAD
    return pl.pallas_call(
        _proj_kernel,
        out_shape=[jax.ShapeDtypeStruct((N_DQ, t), BF16), jax.ShapeDtypeStruct((t, N_DQ), BF16),
                   jax.ShapeDtypeStruct((N_DV, t), BF16), jax.ShapeDtypeStruct((n_mq, t), BF16),
                   jax.ShapeDtypeStruct((t, n_mq), BF16), jax.ShapeDtypeStruct((MLA_HEADS * MLA_V_DIM, t), BF16)],
        grid=(t // tm,),
        in_specs=[row_blk(D_MODEL), col_blk(1), full(fd), full(fm), full(w_in_t), full(gq), full(wuq_t),
                  full(gkv), full(wukv_t)],
        out_specs=[col_blk(N_DQ), row_blk(N_DQ), col_blk(N_DV), col_blk(n_mq), row_blk(n_mq),
                   col_blk(MLA_HEADS * MLA_V_DIM)],
        compiler_params=_cparams(("parallel",)),
        name="proj_rope",
    )(x2, pos_row, fd, fm, w_in_t, gq, wuq_t, gkv, wukv_t)


def _flash_t(qq, k_ref, vt_ref, s_refs, p_refs, acc_ref, i, tq):
    n = qq.shape[1]
    tk = s_refs[0].shape[0]
    assert tq == 2 * tk
    q_pos = i * tq + (lax.broadcasted_iota(jnp.int32, (tk, n), 1) & (tq - 1))
    key_row = lax.broadcasted_iota(jnp.int32, (tk, n), 0)

    def scores(c, s_ref):
        off = pl.multiple_of(c * tk, tk)
        s = _dot(k_ref[pl.ds(off, tk), :], qq)
        s_ref[...] = s
        return jnp.max(s, axis=0, keepdims=True)

    def softmax(c, s_ref, p_ref, cmax, m, l, masked):
        s = s_ref[...]
        if masked:
            s = jnp.where(key_row + c * tk <= q_pos, s, NEG_INF)
            cmax = jnp.max(s, axis=0, keepdims=True)
        m_new = jnp.maximum(m, cmax)
        alpha = jnp.exp2(m - m_new)
        p = jnp.exp2(s - m_new)
        p_ref[...] = p.astype(BF16)
        return m_new, alpha * l + jnp.sum(p, axis=0, keepdims=True), alpha

    def values(c, p_ref, alpha):
        off = pl.multiple_of(jnp.maximum(c, 0) * tk, tk)
        acc_ref[...] = alpha * acc_ref[...] + _dot(vt_ref[:, pl.ds(off, tk)], p_ref[...])

    def pair(u, carry, masked_second):
        cm1, alpha0, m, l = carry
        c = 2 * u
        cm0 = scores(c, s_refs[0])
        values(c - 2, p_refs[0], alpha0)
        m, l, alpha1 = softmax(c - 1, s_refs[1], p_refs[1], cm1, m, l, False)
        cm1 = scores(c + 1, s_refs[1])
        values(c - 1, p_refs[1], alpha1)
        m, l, alpha0 = softmax(c, s_refs[0], p_refs[0], cm0, m, l, masked_second)
        return cm1, alpha0, m, l

    s_refs[1][...] = jnp.full(s_refs[1].shape, 2 * NEG_INF, F32)
    p_refs[0][...] = jnp.zeros(p_refs[0].shape, BF16)
    acc_ref[...] = jnp.zeros(acc_ref.shape, F32)
    init = (jnp.full((1, n), 2 * NEG_INF, F32), jnp.ones((1, n), F32),
            jnp.full((1, n), NEG_INF, F32), jnp.zeros((1, n), F32))
    carry = lax.fori_loop(0, i, functools.partial(pair, masked_second=False), init)
    cm1, alpha0, m, l = pair(i, carry, True)
    values(2 * i, p_refs[0], alpha0)
    m, l, alpha1 = softmax(2 * i + 1, s_refs[1], p_refs[1], cm1, m, l, True)
    values(2 * i + 1, p_refs[1], alpha1)
    return acc_ref[...] / l


def _diff_attn_kernel(qt_ref, k_ref, vt_ref, lam_ref, g_ref, o_ref, s0, s1, p0, p1, acc_ref, *, tq, lam_init):
    i = pl.program_id(2)
    qt = qt_ref[...]
    row = lax.broadcasted_iota(jnp.int32, qt.shape, 0)
    zero = jnp.zeros_like(qt)
    qq = jnp.concatenate([jnp.where(row < DIFF_QK_DIM, qt, zero), jnp.where(row >= DIFF_QK_DIM, qt, zero)], axis=1)
    o = _flash_t(qq, k_ref, vt_ref, (s0, s1), (p0, p1), acc_ref, i, tq)
    lv = lam_ref[...]
    lam = (jnp.exp(jnp.sum(lv[0:1] * lv[1:2], axis=1, keepdims=True))
           - jnp.exp(jnp.sum(lv[2:3] * lv[3:4], axis=1, keepdims=True)) + lam_init)
    d = o[:, :tq] - lam * o[:, tq:]
    o_ref[...] = (_rms_rows(d, g_ref[...]) * (1.0 - lam_init)).T.astype(o_ref.dtype)


def _mla_attn_kernel(qt_ref, k_ref, vt_ref, g_ref, o_ref, s0, s1, p0, p1, acc_ref, *, tq):
    i = pl.program_id(2)
    o = _flash_t(qt_ref[...], k_ref, vt_ref, (s0, s1), (p0, p1), acc_ref, i, tq)
    o_ref[...] = _rms_rows(o, g_ref[...]).T.astype(o_ref.dtype)


def _attn_call(kernel, name, qt, k, vt, extras, b, s, heads, dk_w, dv_w, tq, n_maps):
    nq = s // tq
    t = b * s
    tk = tq // 2
    n = n_maps * tq
    small = [pl.BlockSpec(a.shape, lambda bi, h, i: (0, 0)) for a in extras]
    return pl.pallas_call(
        kernel,
        out_shape=jax.ShapeDtypeStruct((t, heads * dv_w), BF16),
        grid=(b, heads, nq),
        in_specs=[pl.BlockSpec((dk_w, tq), lambda bi, h, i: (h, bi * nq + i)),
                  pl.BlockSpec((s, dk_w), lambda bi, h, i: (bi, h)),
                  pl.BlockSpec((dv_w, s), lambda bi, h, i: (h, bi))] + small,
        out_specs=pl.BlockSpec((tq, dv_w), lambda bi, h, i: (bi * nq + i, h)),
        scratch_shapes=[pltpu.VMEM((tk, n), F32), pltpu.VMEM((tk, n), F32),
                        pltpu.VMEM((tk, n), BF16), pltpu.VMEM((tk, n), BF16), pltpu.VMEM((dv_w, n), F32)],
        compiler_params=_cparams(("parallel", "parallel", "arbitrary")),
        name=name,
    )(qt, k, vt, *extras)


def _memkv_kernel(mem_ref, wk_ref, wv_ref, k_ref, v_ref):
    mb = mem_ref[...].astype(BF16)
    k_ref[...] = _dot(mb, wk_ref[...]).astype(BF16)
    v_ref[...] = _dot(mb, wv_ref[...]).astype(BF16)


def _memkv_call(mem2, wk, wv):
    n = mem2.shape[0]
    w = wk.shape[1]
    tm = N_MEM
    return pl.pallas_call(
        _memkv_kernel,
        out_shape=[jax.ShapeDtypeStruct((n, w), BF16)] * 2,
        grid=(n // tm,),
        in_specs=[pl.BlockSpec((tm, D_MODEL), lambda i: (i, 0)),
                  pl.BlockSpec(wk.shape, lambda i: (0, 0)),
                  pl.BlockSpec(wv.shape, lambda i: (0, 0))],
        out_specs=[pl.BlockSpec((tm, w), lambda i: (i, 0))] * 2,
        compiler_params=_cparams(("parallel",)),
        name="mem_kv",
    )(mem2, wk, wv)


def _pack_bf16_pairs(x):
    half = x.shape[1] // 2
    r = pltpu.bitcast(x.astype(BF16).astype(F32), jnp.uint32)
    return (r[:, :half] >> 16) | (r[:, half:] & jnp.uint32(0xFFFF0000))


def _unpack_bf16_pairs(w):
    lo = pltpu.bitcast(w << 16, F32).astype(BF16)
    hi = pltpu.bitcast(w & jnp.uint32(0xFFFF0000), F32).astype(BF16)
    return lo, hi


def _post_kernel(da_ref, ma_ref, x_ref, wo_ref, g1_ref, b1_ref, wq_ref, km_ref, vm_ref, wmo_ref,
                 g2_ref, b2_ref, h_ref, hp_ref):
    n_d = DIFF_HEADS * DIFF_V_DIM
    mix = _dot(da_ref[...], wo_ref[0:n_d, :]) + _dot(ma_ref[...], wo_ref[n_d:, :])
    h1 = _layer_norm(DEEPNORM_ALPHA * x_ref[...] + mix, g1_ref[...], b1_ref[...])
    q = (_dot(h1.astype(BF16), wq_ref[...]) * (MEM_HEAD_DIM ** -0.5)).astype(BF16)
    outs = []
    for h in range(MEM_HEADS):
        sl = slice(h * MEM_HEAD_DIM, (h + 1) * MEM_HEAD_DIM)
        s = _dot_nt(q[:, sl], km_ref[:, sl])
        m = jnp.max(s, axis=1, keepdims=True)
        p = jnp.exp(s - m)
        p = p / jnp.sum(p, axis=1, keepdims=True)
        outs.append(_dot(p.astype(BF16), vm_ref[:, sl]))
    o = jnp.concatenate(outs, axis=1).astype(BF16)
    xat = _dot(o, wmo_ref[...])
    h2 = _layer_norm(DEEPNORM_ALPHA * h1 + xat, g2_ref[...], b2_ref[...])
    h_ref[...] = h2
    hp_ref[...] = _pack_bf16_pairs(h2)


def _post_call(diff_o, mla_o, x2, wo, g1, b1, wq, kmem, vmem, wmo, g2, b2, s, tm):
    t = x2.shape[0]
    per_b = s // tm
    row = lambda w: pl.BlockSpec((tm, w), lambda i: (i, 0))
    full = lambda a: pl.BlockSpec(a.shape, lambda i: (0, 0))
    memb = pl.BlockSpec((N_MEM, kmem.shape[1]), lambda i: (i // per_b, 0))
    return pl.pallas_call(
        _post_kernel,
        out_shape=[jax.ShapeDtypeStruct((t, D_MODEL), F32), jax.ShapeDtypeStruct((t, D_MODEL // 2), jnp.uint32)],
        grid=(t // tm,),
        in_specs=[row(diff_o.shape[1]), row(mla_o.shape[1]), row(D_MODEL), full(wo), full(g1), full(b1),
                  full(wq), memb, memb, full(wmo), full(g2), full(b2)],
        out_specs=[row(D_MODEL), row(D_MODEL // 2)],
        compiler_params=_cparams(("parallel",)),
        name="post_mix_xattn",
    )(diff_o, mla_o, x2, wo, g1, b1, wq, kmem, vmem, wmo, g2, b2)


def _router_kernel(h_ref, rwt_ref, bias_ref, tri_ref, eid_ref, rnk_ref, wt_ref, cnt_ref):
    tm = h_ref.shape[0]

    @pl.when(pl.program_id(0) == 0)
    def _():
        cnt_ref[...] = jnp.zeros(cnt_ref.shape, cnt_ref.dtype)

    logits = lax.dot_general(rwt_ref[...], h_ref[...], (((1,), (1,)), ((), ())),
                             precision=lax.Precision.HIGHEST, preferred_element_type=F32)
    scores = jax.nn.sigmoid(logits)
    choice = scores + bias_ref[...]
    neg = float("-inf")

    c3 = choice.reshape(N_GROUPS, GROUP_SIZE, tm)
    mi = lax.broadcasted_iota(jnp.int32, c3.shape, 1)
    top1 = jnp.max(c3, axis=1, keepdims=True)
    first = jnp.min(jnp.where(c3 == top1, mi, GROUP_SIZE), axis=1, keepdims=True)
    top2 = jnp.max(jnp.where(mi == first, neg, c3), axis=1, keepdims=True)
    gs = (top1 + top2).reshape(N_GROUPS, tm)

    gi = lax.broadcasted_iota(jnp.int32, gs.shape, 0)
    rank = jnp.zeros(gs.shape, jnp.int32)
    for g in range(N_GROUPS):
        rowg = gs[g:g + 1, :]
        beats = (rowg > gs) | ((rowg == gs) & (g < gi))
        rank = rank + beats.astype(jnp.int32)
    gmask = rank < TOPK_GROUPS
    emask = jnp.broadcast_to(gmask.reshape(N_GROUPS, 1, tm), c3.shape).reshape(N_EXPERTS, tm)

    work = jnp.where(emask, choice, NEG_INF)
    ei = lax.broadcasted_iota(jnp.int32, work.shape, 0)
    sel = jnp.zeros(work.shape, jnp.bool_)
    picks = []
    for _ in range(TOP_K):
        mx = jnp.max(work, axis=0, keepdims=True)
        fi = jnp.min(jnp.where(work == mx, ei, N_EXPERTS), axis=0, keepdims=True)
        pick = ei == fi
        picks.append((fi, pick))
        sel = sel | pick
        work = jnp.where(pick, neg, work)
    top_w = jnp.where(sel, scores, 0.0)
    gates = top_w / jnp.sum(top_w, axis=0, keepdims=True) * ROUTED_SCALE

    sel_b = jnp.where(sel, 1.0, 0.0).astype(BF16)
    before = _dot(sel_b, tri_ref[...]).astype(jnp.int32)
    run = cnt_ref[:, 0:1]
    pos = run + before
    cnt_ref[...] = jnp.broadcast_to(run + jnp.sum(sel.astype(jnp.int32), axis=1, keepdims=True), cnt_ref.shape)

    eid_ref[...] = jnp.concatenate([fi for fi, _ in picks], axis=0)
    rnk_ref[...] = jnp.concatenate(
        [jnp.sum(jnp.where(pick, pos, 0), axis=0, keepdims=True) for _, pick in picks], axis=0)
    w_rows = [jnp.sum(jnp.where(pick, gates, 0.0), axis=0, keepdims=True) for _, pick in picks]
    w_pad = jnp.concatenate(w_rows + [jnp.zeros((V7X_LANES - TOP_K, tm), F32)], axis=0)
    wt_ref[...] = w_pad.T


def _router_call(h2, rwt, bias, tm):
    t = h2.shape[0]
    tri = jnp.triu(jnp.ones((tm, tm), BF16), k=1)
    full = lambda a: pl.BlockSpec(a.shape, lambda i: (0, 0))
    return pl.pallas_call(
        _router_kernel,
        out_shape=[jax.ShapeDtypeStruct((TOP_K, t), jnp.int32), jax.ShapeDtypeStruct((TOP_K, t), jnp.int32),
                   jax.ShapeDtypeStruct((t, V7X_LANES), F32), jax.ShapeDtypeStruct((N_EXPERTS, V7X_LANES), jnp.int32)],
        grid=(t // tm,),
        in_specs=[pl.BlockSpec((tm, D_MODEL), lambda i: (i, 0)), full(rwt), full(bias), full(tri)],
        out_specs=[pl.BlockSpec((TOP_K, tm), lambda i: (0, i)), pl.BlockSpec((TOP_K, tm), lambda i: (0, i)),
                   pl.BlockSpec((tm, V7X_LANES), lambda i: (i, 0)),
                   pl.BlockSpec((N_EXPERTS, V7X_LANES), lambda i: (0, 0))],
        compiler_params=_cparams(("arbitrary",)),
        name="router",
    )(h2, rwt, bias, tri)


_SC_MESH_AXES = ("core", "subcore")


def _sc_mesh():
    return plsc.VectorSubcoreMesh(core_axis_name=_SC_MESH_AXES[0], subcore_axis_name=_SC_MESH_AXES[1])


def _sc_worker_chunks(n_chunks):
    n_workers = V7X_SC_CORES * V7X_SC_SUBCORES
    assert n_chunks % n_workers == 0
    per_worker = n_chunks // n_workers
    first = (lax.axis_index(_SC_MESH_AXES[1]) * V7X_SC_CORES + lax.axis_index(_SC_MESH_AXES[0])) * per_worker
    return first, per_worker


def _sc_scatter_rows(x, idx, n_rows):
    t, w = x.shape
    n_chunks, n_k, window = idx.shape

    @functools.partial(pl.kernel, out_type=jax.ShapeDtypeStruct((n_rows, w), x.dtype), mesh=_sc_mesh(),
                       scratch_types=[pltpu.VMEM((n_k, window), jnp.int32), pltpu.VMEM((window, w), x.dtype)],
                       name="moe_dispatch_sc")
    def scatter(x_hbm, i_hbm, o_hbm, idx_v, rows_v):
        first, per_worker = _sc_worker_chunks(n_chunks)

        @pl.loop(0, per_worker)
        def _(c):
            chunk = first + c
            pltpu.sync_copy(i_hbm.at[chunk], idx_v)
            pltpu.sync_copy(x_hbm.at[pl.ds(chunk * window, window)], rows_v)
            for k in range(n_k):
                pltpu.sync_copy(rows_v, o_hbm.at[idx_v.at[k]])

    return scatter(x, idx)


def _sc_gather_rows(y, idx):
    n_chunks, n_k, window = idx.shape
    w = y.shape[1]

    @functools.partial(pl.kernel, out_type=jax.ShapeDtypeStruct((n_k, n_chunks * window, w), y.dtype),
                       mesh=_sc_mesh(),
                       scratch_types=[pltpu.VMEM((n_k, window), jnp.int32), pltpu.VMEM((window, w), y.dtype)],
                       name="moe_combine_sc")
    def gather(y_hbm, i_hbm, o_hbm, idx_v, rows_v):
        first, per_worker = _sc_worker_chunks(n_chunks)

        @pl.loop(0, per_worker)
        def _(c):
            chunk = first + c
            pltpu.sync_copy(i_hbm.at[chunk], idx_v)
            for k in range(n_k):
                pltpu.sync_copy(y_hbm.at[idx_v.at[k]], rows_v)
                pltpu.sync_copy(rows_v, o_hbm.at[k, pl.ds(chunk * window, window)])

    return gather(y, idx)


def _experts_kernel(te_ref, nu_ref, xs_ref, wg_ref, wu_ref, wd_ref, y_ref):
    @pl.when(pl.program_id(0) < nu_ref[0])
    def _():
        half = D_MODEL // 2
        lo, hi = _unpack_bf16_pairs(xs_ref[...])
        a = _dot(lo, wg_ref[0, :half, :]) + _dot(hi, wg_ref[0, half:, :])
        u = _dot(lo, wu_ref[0, :half, :]) + _dot(hi, wu_ref[0, half:, :])
        y_ref[...] = _pack_bf16_pairs(_dot((jax.nn.silu(a) * u).astype(BF16), wd_ref[0]))


def _experts_call(tile_expert, n_used, xs, wg, wu, wd, tmx):
    n_rows, w = xs.shape
    tile = lambda i, te, nu: jnp.minimum(i, nu[0] - 1)
    wspec = lambda shp: pl.BlockSpec((1,) + shp, lambda i, te, nu: (te[tile(i, te, nu)], 0, 0))
    return pl.pallas_call(
        _experts_kernel,
        out_shape=jax.ShapeDtypeStruct((n_rows, w), xs.dtype),
        grid_spec=pltpu.PrefetchScalarGridSpec(
            num_scalar_prefetch=2,
            grid=(n_rows // tmx,),
            in_specs=[pl.BlockSpec((tmx, w), lambda i, te, nu: (tile(i, te, nu), 0)),
                      wspec((D_MODEL, EXPERT_DIM)), wspec((D_MODEL, EXPERT_DIM)), wspec((EXPERT_DIM, D_MODEL))],
            out_specs=pl.BlockSpec((tmx, w), lambda i, te, nu: (tile(i, te, nu), 0)),
        ),
        compiler_params=_cparams(("arbitrary",)),
        name="moe_experts",
    )(tile_expert, n_used, xs, wg, wu, wd)


def _combine_kernel(wt_ref, h_ref, yg_ref, sg_ref, su_ref, sd_ref, g3_ref, b3_ref, o_ref):
    h = h_ref[...]
    hb = h.astype(BF16)
    acc = _dot((jax.nn.silu(_dot(hb, sg_ref[...])) * _dot(hb, su_ref[...])).astype(BF16), sd_ref[...])
    wt = wt_ref[...]
    lo_sum = None
    hi_sum = None
    for k in range(TOP_K):
        lo, hi = _unpack_bf16_pairs(yg_ref[k])
        wk = wt[:, k:k + 1]
        lo_sum = wk * lo if lo_sum is None else lo_sum + wk * lo
        hi_sum = wk * hi if hi_sum is None else hi_sum + wk * hi
    acc = acc + jnp.concatenate([lo_sum, hi_sum], axis=1)
    o_ref[...] = _layer_norm(DEEPNORM_ALPHA * h + acc, g3_ref[...], b3_ref[...])


def _combine_call(wt, h2, yg, sg, su, sd, g3, b3, tm):
    t = h2.shape[0]
    full = lambda a: pl.BlockSpec(a.shape, lambda i: (0, 0))
    return pl.pallas_call(
        _combine_kernel,
        out_shape=jax.ShapeDtypeStruct((t, D_MODEL), F32),
        grid=(t // tm,),
        in_specs=[pl.BlockSpec((tm, V7X_LANES), lambda i: (i, 0)),
                  pl.BlockSpec((tm, D_MODEL), lambda i: (i, 0)),
                  pl.BlockSpec((TOP_K, tm, yg.shape[2]), lambda i: (0, i, 0)),
                  full(sg), full(su), full(sd), full(g3), full(b3)],
        out_specs=pl.BlockSpec((tm, D_MODEL), lambda i: (i, 0)),
        compiler_params=_cparams(("parallel",)),
        name="moe_combine",
    )(wt, h2, yg, sg, su, sd, g3, b3)


def _moe(h2, hp, router_w, router_bias, wg, wu, wd, sg, su, sd, g3, b3, tm_route, tmx, tm_comb):
    t = h2.shape[0]
    eid, rnk, wt, cnt = _router_call(h2, router_w.T, router_bias.reshape(-1, 1), tm_route)
    counts = cnt[:, 0]
    tiles_per_e = (counts + (tmx - 1)) // tmx
    tile_end = jnp.cumsum(tiles_per_e)
    row_start = ((tile_end - tiles_per_e) * tmx).astype(jnp.int32)
    n_tiles = (t * TOP_K) // tmx + N_EXPERTS
    n_used = tile_end[-1:].astype(jnp.int32)
    tile_expert = jnp.minimum(
        jnp.sum(tile_end[None, :] <= jnp.arange(n_tiles, dtype=jnp.int32)[:, None], axis=1), N_EXPERTS - 1
    ).astype(jnp.int32)
    expert_ids = jnp.arange(N_EXPERTS, dtype=jnp.int32)[:, None, None]
    dest = rnk + jnp.sum(jnp.where(eid[None] == expert_ids, row_start[:, None, None], 0), axis=0)
    dest_chunks = dest.reshape(TOP_K, t // SC_INDEX_WINDOW, SC_INDEX_WINDOW).transpose(1, 0, 2)
    xs = _sc_scatter_rows(hp, dest_chunks, n_tiles * tmx)
    yp = _experts_call(tile_expert, n_used, xs, wg, wu, wd, tmx)
    yg = _sc_gather_rows(yp, dest_chunks)
    return _combine_call(wt, h2, yg, sg, su, sd, g3, b3, tm_comb)


def _prep_w_in_t(w_in):
    pad = jnp.zeros((D_MODEL, V7X_LANES - MLA_ROPE_DIM), w_in.dtype)
    return jnp.concatenate([w_in, pad], axis=1).T.astype(BF16)


def _prep_w_uq_t(w_uq):
    w = w_uq.reshape(MLA_Q_RANK, MLA_HEADS, MLA_NOPE_DIM + MLA_ROPE_DIM)
    pad = jnp.zeros((MLA_Q_RANK, MLA_HEADS, MLA_QK_PAD - MLA_NOPE_DIM - MLA_ROPE_DIM), w.dtype)
    return jnp.concatenate([w, pad], axis=2).reshape(MLA_Q_RANK, MLA_HEADS * MLA_QK_PAD).T.astype(BF16)


def _prep_w_ukv_t(w_ukv):
    w = w_ukv.reshape(MLA_KV_RANK, MLA_HEADS, MLA_NOPE_DIM + MLA_V_DIM)
    k = w[:, :, :MLA_NOPE_DIM].reshape(MLA_KV_RANK, MLA_HEADS * MLA_NOPE_DIM)
    v = w[:, :, MLA_NOPE_DIM:].reshape(MLA_KV_RANK, MLA_HEADS * MLA_V_DIM)
    return jnp.concatenate([k, v], axis=1).T.astype(BF16)


def _rope_inv_freq(dim):
    return (ROPE_THETA ** (-jnp.arange(0, dim, 2, dtype=F32) / dim)).reshape(-1, 1)


def _layer(x2, mem2, pos_row, b, s, l, w_in, lam_q1, lam_k1, lam_q2, lam_k2, diff_subln_g, mla_q_norm_g,
           mla_w_uq, mla_kv_norm_g, mla_w_ukv, mla_out_norm_g, w_o, ln1_g, ln1_b, mem_w_q, mem_w_k, mem_w_v,
           mem_w_o, ln2_g, ln2_b, router_w, router_bias, exp_w_gate, exp_w_up, exp_w_down, sh_w_gate,
           sh_w_up, sh_w_down, ln3_g, ln3_b):
    row = lambda a: a.reshape(1, -1)
    col = lambda a: a.reshape(-1, 1)
    dqt, dk, dvt, mqt, mk, mvt = _proj_call(
        x2, pos_row, _rope_inv_freq(DIFF_ROT_DIM), _rope_inv_freq(MLA_ROPE_DIM), _prep_w_in_t(w_in),
        col(mla_q_norm_g), _prep_w_uq_t(mla_w_uq), col(mla_kv_norm_g), _prep_w_ukv_t(mla_w_ukv), tm=min(512, s))

    lam_init = 0.8 - 0.6 * math.exp(-0.3 * l)
    lam_vecs = jnp.stack([lam_q1, lam_k1, lam_q2, lam_k2]).astype(F32)
    tq = min(512, s)
    diff_o = _attn_call(
        functools.partial(_diff_attn_kernel, tq=tq, lam_init=lam_init), "diff_attn",
        dqt, dk, dvt, [lam_vecs, col(diff_subln_g)], b, s, DIFF_HEADS, 2 * DIFF_QK_DIM, DIFF_V_DIM, tq, 2)
    mla_o = _attn_call(
        functools.partial(_mla_attn_kernel, tq=tq), "mla_attn",
        mqt, mk, mvt, [col(mla_out_norm_g)], b, s, MLA_HEADS, MLA_QK_PAD, MLA_V_DIM, tq, 1)

    kmem, vmem = _memkv_call(mem2, mem_w_k.astype(BF16), mem_w_v.astype(BF16))
    h2, hp = _post_call(diff_o, mla_o, x2, w_o.astype(BF16), row(ln1_g), row(ln1_b), mem_w_q.astype(BF16),
                        kmem, vmem, mem_w_o.astype(BF16), row(ln2_g), row(ln2_b), s, tm=min(512, s))

    return _moe(h2, hp, router_w, router_bias, exp_w_gate.astype(BF16), exp_w_up.astype(BF16),
                exp_w_down.astype(BF16), sh_w_gate.astype(BF16), sh_w_up.astype(BF16), sh_w_down.astype(BF16),
                row(ln3_g), row(ln3_b), tm_route=min(512, s), tmx=512, tm_comb=min(256, s))


def kernel(x, mem, positions, w_in, lam_q1, lam_k1, lam_q2, lam_k2, diff_subln_g, mla_q_norm_g, mla_w_uq,
           mla_kv_norm_g, mla_w_ukv, mla_out_norm_g, w_o, ln1_g, ln1_b, mem_w_q, mem_w_k, mem_w_v, mem_w_o,
           ln2_g, ln2_b, router_w, router_bias, exp_w_gate, exp_w_up, exp_w_down, sh_w_gate, sh_w_up,
           sh_w_down, ln3_g, ln3_b):
    b, s, d = x.shape
    h = x.reshape(b * s, d)
    mem2 = mem.reshape(b * mem.shape[1], d)
    pos_row = positions.reshape(1, b * s)
    params = (w_in, lam_q1, lam_k1, lam_q2, lam_k2, diff_subln_g, mla_q_norm_g, mla_w_uq, mla_kv_norm_g,
              mla_w_ukv, mla_out_norm_g, w_o, ln1_g, ln1_b, mem_w_q, mem_w_k, mem_w_v, mem_w_o, ln2_g, ln2_b,
              router_w, router_bias, exp_w_gate, exp_w_up, exp_w_down, sh_w_gate, sh_w_up, sh_w_down,
              ln3_g, ln3_b)
    for l in range(w_in.shape[0]):
        h = _layer(h, mem2, pos_row, b, s, l, *[p[l] for p in params])
    return h.reshape(b, s, d)
```

```python
import functools
import math

import jax
import jax.numpy as jnp
from jax import lax
from jax.experimental import pallas as pl
from jax.experimental.pallas import tpu as pltpu
from jax.experimental.pallas import tpu_sc as plsc

F32 = jnp.float32
BF16 = jnp.bfloat16

D_MODEL = 1024
N_MEM = 256
ROPE_THETA = 500000.0
NEG_INF = -1e30
LN_EPS = 1e-5
RMS_EPS = 1e-6
DIFF_HEADS = 4
DIFF_QK_DIM = 64
DIFF_V_DIM = 128
DIFF_ROT_DIM = DIFF_QK_DIM // 4
MLA_HEADS = 4
MLA_Q_RANK = 256
MLA_KV_RANK = 128
MLA_NOPE_DIM = 128
MLA_ROPE_DIM = 64
MLA_V_DIM = 128
MLA_QK_PAD = 256
V_AUG_DIM = 128 + 16
MEM_HEADS = 4
MEM_HEAD_DIM = 128
N_EXPERTS = 64
TOP_K = 8
N_GROUPS = 8
TOPK_GROUPS = 4
GROUP_SIZE = N_EXPERTS // N_GROUPS
EXPERT_DIM = 256
SHARED_DIM = 256
ROUTED_SCALE = 2.5
DEPTH = 1
DEEPNORM_ALPHA = (2.0 * DEPTH) ** 0.25
LOG2E = math.log2(math.e)

V7X_LANES = 128
V7X_SUBLANES = 8
V7X_VMEM_LIMIT_BYTES = 56 * 1024 * 1024
V7X_SC_CORES = 2
V7X_SC_SUBCORES = 16
SC_INDEX_WINDOW = 128
V7X_BF16_SUBLANES = 16

N_DQ = DIFF_HEADS * 2 * DIFF_QK_DIM
N_DV = DIFF_HEADS * DIFF_V_DIM
O_DK = N_DQ
O_DV = 2 * N_DQ
O_CQ = O_DV + N_DV
O_CKV = O_CQ + MLA_Q_RANK
O_KR = O_CKV + MLA_KV_RANK
N_IN_PAD = O_KR + V7X_LANES


def _cparams(sem):
    return pltpu.CompilerParams(dimension_semantics=sem, vmem_limit_bytes=V7X_VMEM_LIMIT_BYTES)


def _dot(a, b):
    return jnp.dot(a, b, preferred_element_type=F32)


def _dot_nt(a, b):
    return lax.dot_general(a, b, (((1,), (1,)), ((), ())), preferred_element_type=F32)


def _rms(x, g):
    return x * lax.rsqrt(jnp.mean(jnp.square(x), axis=-1, keepdims=True) + RMS_EPS) * g


def _rms_rows(x, g):
    return x * lax.rsqrt(jnp.mean(jnp.square(x), axis=0, keepdims=True) + RMS_EPS) * g


def _layer_norm(x, g, b):
    mu = jnp.mean(x, axis=-1, keepdims=True)
    xc = x - mu
    var = jnp.mean(jnp.square(xc), axis=-1, keepdims=True)
    return xc * lax.rsqrt(var + LN_EPS) * g + b


def _with_ones_rows(vt, heads):
    dv = vt.shape[0] // heads
    ones = jnp.ones((V_AUG_DIM - dv, vt.shape[1]), vt.dtype)
    return jnp.concatenate([p for h in range(heads) for p in (vt[h * dv:(h + 1) * dv], ones)], axis=0)


def _rope_rows(x, cos, sin, period):
    h = cos.shape[0]
    pieces = []
    for base in range(0, x.shape[0], period):
        x1 = x[base:base + h]
        x2 = x[base + h:base + 2 * h]
        pieces.append(x1 * cos - x2 * sin)
        pieces.append(x2 * cos + x1 * sin)
        if period > 2 * h:
            pieces.append(x[base + 2 * h:base + period])
    return jnp.concatenate(pieces, axis=0)


def _proj_kernel(x_ref, pos_ref, fd_ref, fm_ref, w_in_ref, gq_ref, wuq_ref, gkv_ref, wukv_ref,
                 dq_ref, dk_ref, dv_ref, mq_ref, mk_ref, mv_ref):
    xb = x_ref[...].astype(BF16)
    pos = pos_ref[...].astype(F32)
    ang_d = fd_ref[...] * pos
    ang_m = fm_ref[...] * pos
    cos_d, sin_d = jnp.cos(ang_d), jnp.sin(ang_d)
    cos_m, sin_m = jnp.cos(ang_m), jnp.sin(ang_m)

    dq = _rope_rows(_dot_nt(w_in_ref[0:N_DQ, :], xb), cos_d, sin_d, DIFF_QK_DIM)
    dq_ref[...] = (dq * (DIFF_QK_DIM ** -0.5 * LOG2E)).astype(BF16)
    dk = _rope_rows(_dot_nt(w_in_ref[O_DK:O_DK + N_DQ, :], xb), cos_d, sin_d, DIFF_QK_DIM)
    dk_ref[...] = dk.T.astype(BF16)
    dv_ref[...] = _with_ones_rows(_dot_nt(w_in_ref[O_DV:O_DV + N_DV, :], xb), DIFF_HEADS).astype(BF16)

    c_q = _dot_nt(w_in_ref[O_CQ:O_CQ + MLA_Q_RANK, :], xb)
    q = _dot(wuq_ref[...], _rms_rows(c_q, gq_ref[...]).astype(BF16))
    q = q * ((MLA_NOPE_DIM + MLA_ROPE_DIM) ** -0.5 * LOG2E)
    pieces = []
    for h in range(MLA_HEADS):
        o = h * MLA_QK_PAD
        pieces.append(q[o:o + MLA_NOPE_DIM])
        pieces.append(_rope_rows(q[o + MLA_NOPE_DIM:o + MLA_NOPE_DIM + MLA_ROPE_DIM], cos_m, sin_m, MLA_ROPE_DIM))
        pieces.append(q[o + MLA_NOPE_DIM + MLA_ROPE_DIM:o + MLA_QK_PAD])
    mq_ref[...] = jnp.concatenate(pieces, axis=0).astype(BF16)

    c_kv = _dot_nt(w_in_ref[O_CKV:O_CKV + MLA_KV_RANK, :], xb)
    kv = _dot(wukv_ref[...], _rms_rows(c_kv, gkv_ref[...]).astype(BF16))
    n_kn = MLA_HEADS * MLA_NOPE_DIM
    mv_ref[...] = _with_ones_rows(kv[n_kn:], MLA_HEADS).astype(BF16)
    k_rope = _dot_nt(w_in_ref[O_KR:O_KR + V7X_LANES, :], xb)
    k_pe = jnp.concatenate([_rope_rows(k_rope[:MLA_ROPE_DIM], cos_m, sin_m, MLA_ROPE_DIM),
                            k_rope[MLA_ROPE_DIM:]], axis=0)
    k_nope_t = kv[:n_kn].T.astype(BF16)
    k_pe_t = k_pe.T.astype(BF16)
    for h in range(MLA_HEADS):
        o = h * MLA_QK_PAD
        mk_ref[:, o:o + MLA_NOPE_DIM] = k_nope_t[:, h * MLA_NOPE_DIM:(h + 1) * MLA_NOPE_DIM]
        mk_ref[:, o + MLA_NOPE_DIM:o + MLA_QK_PAD] = k_pe_t


def _proj_call(x2, pos_row, fd, fm, w_in_t, gq, wuq_t, gkv, wukv_t, tm):
    t = x2.shape[0]
    full = lambda a: pl.BlockSpec(a.shape, lambda i: (0, 0))
    col_blk = lambda r: pl.BlockSpec((r, tm), lambda i: (0, i))
    row_blk = lambda w: pl.BlockSpec((tm, w), lambda i: (i, 0))
    n_mq = MLA_HEADS * MLA_QK_PAD
    return pl.pallas_call(
        _proj_kernel,
        out_shape=[jax.ShapeDtypeStruct((N_DQ, t), BF16), jax.ShapeDtypeStruct((t, N_DQ), BF16),
                   jax.ShapeDtypeStruct((DIFF_HEADS * V_AUG_DIM, t), BF16), jax.ShapeDtypeStruct((n_mq, t), BF16),
                   jax.ShapeDtypeStruct((t, n_mq), BF16), jax.ShapeDtypeStruct((MLA_HEADS * V_AUG_DIM, t), BF16)],
        grid=(t // tm,),
        in_specs=[row_blk(D_MODEL), col_blk(1), full(fd), full(fm), full(w_in_t), full(gq), full(wuq_t),
                  full(gkv), full(wukv_t)],
        out_specs=[col_blk(N_DQ), row_blk(N_DQ), col_blk(DIFF_HEADS * V_AUG_DIM), col_blk(n_mq), row_blk(n_mq),
                   col_blk(MLA_HEADS * V_AUG_DIM)],
        compiler_params=_cparams(("parallel",)),
        name="proj_rope",
    )(x2, pos_row, fd, fm, w_in_t, gq, wuq_t, gkv, wukv_t)


def _flash_t(qq, k_ref, vt_ref, s_refs, p_refs, acc_ref, i, tq):
    n = qq.shape[1]
    tk = s_refs[0].shape[0]
    assert tq == 2 * tk
    q_pos = i * tq + (lax.broadcasted_iota(jnp.int32, (tk, n), 1) & (tq - 1))
    key_row = lax.broadcasted_iota(jnp.int32, (tk, n), 0)

    def scores(c, s_ref):
        off = pl.multiple_of(c * tk, tk)
        s = _dot(k_ref[pl.ds(off, tk), :], qq)
        s_ref[...] = s
        return jnp.max(s, axis=0, keepdims=True)

    def softmax(c, s_ref, p_ref, cmax, m, masked):
        s = s_ref[...]
        if masked:
            s = jnp.where(key_row + c * tk <= q_pos, s, NEG_INF)
            cmax = jnp.max(s, axis=0, keepdims=True)
        m_new = jnp.maximum(m, cmax)
        alpha = jnp.exp2(m - m_new)
        p_ref[...] = jnp.exp2((s - m_new).astype(BF16))
        return m_new, alpha

    def values(c, p_ref, alpha):
        off = pl.multiple_of(jnp.maximum(c, 0) * tk, tk)
        acc_ref[...] = alpha * acc_ref[...] + _dot(vt_ref[:, pl.ds(off, tk)], p_ref[...])

    def pair(u, carry, masked_second):
        cm1, alpha0, m = carry
        c = 2 * u
        cm0 = scores(c, s_refs[0])
        values(c - 2, p_refs[0], alpha0)
        m, alpha1 = softmax(c - 1, s_refs[1], p_refs[1], cm1, m, False)
        cm1 = scores(c + 1, s_refs[1])
        values(c - 1, p_refs[1], alpha1)
        m, alpha0 = softmax(c, s_refs[0], p_refs[0], cm0, m, masked_second)
        return cm1, alpha0, m

    s_refs[1][...] = jnp.full(s_refs[1].shape, 2 * NEG_INF, F32)
    p_refs[0][...] = jnp.zeros(p_refs[0].shape, BF16)
    acc_ref[...] = jnp.zeros(acc_ref.shape, F32)
    init = (jnp.full((1, n), 2 * NEG_INF, F32), jnp.ones((1, n), F32), jnp.full((1, n), NEG_INF, F32))
    carry = lax.fori_loop(0, i, functools.partial(pair, masked_second=False), init)
    cm1, alpha0, m = pair(i, carry, True)
    values(2 * i, p_refs[0], alpha0)
    m, alpha1 = softmax(2 * i + 1, s_refs[1], p_refs[1], cm1, m, True)
    values(2 * i + 1, p_refs[1], alpha1)
    dv = vt_ref.shape[0] - V7X_BF16_SUBLANES
    return acc_ref[0:dv, :] / acc_ref[dv:dv + 1, :]


def _diff_attn_kernel(qt_ref, k_ref, vt_ref, lam_ref, g_ref, o_ref, s0, s1, p0, p1, acc_ref, *, tq, lam_init):
    i = pl.program_id(2)
    qt = qt_ref[...]
    row = lax.broadcasted_iota(jnp.int32, qt.shape, 0)
    zero = jnp.zeros_like(qt)
    qq = jnp.concatenate([jnp.where(row < DIFF_QK_DIM, qt, zero), jnp.where(row >= DIFF_QK_DIM, qt, zero)], axis=1)
    o = _flash_t(qq, k_ref, vt_ref, (s0, s1), (p0, p1), acc_ref, i, tq)
    lv = lam_ref[...]
    lam = (jnp.exp(jnp.sum(lv[0:1] * lv[1:2], axis=1, keepdims=True))
           - jnp.exp(jnp.sum(lv[2:3] * lv[3:4], axis=1, keepdims=True)) + lam_init)
    d = o[:, :tq] - lam * o[:, tq:]
    o_ref[...] = (_rms_rows(d, g_ref[...]) * (1.0 - lam_init)).T.astype(o_ref.dtype)


def _mla_attn_kernel(qt_ref, k_ref, vt_ref, g_ref, o_ref, s0, s1, p0, p1, acc_ref, *, tq):
    i = pl.program_id(2)
    o = _flash_t(qt_ref[...], k_ref, vt_ref, (s0, s1), (p0, p1), acc_ref, i, tq)
    o_ref[...] = _rms_rows(o, g_ref[...]).T.astype(o_ref.dtype)


def _attn_call(kernel, name, qt, k, vt, extras, b, s, heads, dk_w, dv_w, tq, n_maps):
    nq = s // tq
    t = b * s
    tk = tq // 2
    n = n_maps * tq
    small = [pl.BlockSpec(a.shape, lambda bi, h, i: (0, 0)) for a in extras]
    return pl.pallas_call(
        kernel,
        out_shape=jax.ShapeDtypeStruct((t, heads * dv_w), BF16),
        grid=(b, heads, nq),
        in_specs=[pl.BlockSpec((dk_w, tq), lambda bi, h, i: (h, bi * nq + i)),
                  pl.BlockSpec((s, dk_w), lambda bi, h, i: (bi, h)),
                  pl.BlockSpec((V_AUG_DIM, s), lambda bi, h, i: (h, bi))] + small,
        out_specs=pl.BlockSpec((tq, dv_w), lambda bi, h, i: (bi * nq + i, h)),
        scratch_shapes=[pltpu.VMEM((tk, n), F32), pltpu.VMEM((tk, n), F32),
                        pltpu.VMEM((tk, n), BF16), pltpu.VMEM((tk, n), BF16), pltpu.VMEM((V_AUG_DIM, n), F32)],
        compiler_params=_cparams(("parallel", "parallel", "arbitrary")),
        name=name,
    )(qt, k, vt, *extras)


def _memkv_kernel(mem_ref, wk_ref, wv_ref, k_ref, v_ref):
    mb = mem_ref[...].astype(BF16)
    k_ref[...] = _dot(mb, wk_ref[...]).astype(BF16)
    v_ref[...] = _dot(mb, wv_ref[...]).astype(BF16)


def _memkv_call(mem2, wk, wv):
    n = mem2.shape[0]
    w = wk.shape[1]
    tm = N_MEM
    return pl.pallas_call(
        _memkv_kernel,
        out_shape=[jax.ShapeDtypeStruct((n, w), BF16)] * 2,
        grid=(n // tm,),
        in_specs=[pl.BlockSpec((tm, D_MODEL), lambda i: (i, 0)),
                  pl.BlockSpec(wk.shape, lambda i: (0, 0)),
                  pl.BlockSpec(wv.shape, lambda i: (0, 0))],
        out_specs=[pl.BlockSpec((tm, w), lambda i: (i, 0))] * 2,
        compiler_params=_cparams(("parallel",)),
        name="mem_kv",
    )(mem2, wk, wv)


def _pack_bf16_pairs(x):
    half = x.shape[1] // 2
    r = pltpu.bitcast(x.astype(BF16).astype(F32), jnp.uint32)
    return (r[:, :half] >> 16) | (r[:, half:] & jnp.uint32(0xFFFF0000))


def _unpack_bf16_pairs(w):
    lo = pltpu.bitcast(w << 16, F32).astype(BF16)
    hi = pltpu.bitcast(w & jnp.uint32(0xFFFF0000), F32).astype(BF16)
    return lo, hi


def _post_kernel(da_ref, ma_ref, x_ref, wo_ref, g1_ref, b1_ref, wq_ref, km_ref, vm_ref, wmo_ref,
                 g2_ref, b2_ref, h_ref, hp_ref):
    n_d = DIFF_HEADS * DIFF_V_DIM
    mix = _dot(da_ref[...], wo_ref[0:n_d, :]) + _dot(ma_ref[...], wo_ref[n_d:, :])
    h1 = _layer_norm(DEEPNORM_ALPHA * x_ref[...] + mix, g1_ref[...], b1_ref[...])
    q = (_dot(h1.astype(BF16), wq_ref[...]) * (MEM_HEAD_DIM ** -0.5)).astype(BF16)
    outs = []
    for h in range(MEM_HEADS):
        sl = slice(h * MEM_HEAD_DIM, (h + 1) * MEM_HEAD_DIM)
        s = _dot_nt(q[:, sl], km_ref[:, sl])
        m = jnp.max(s, axis=1, keepdims=True)
        p = jnp.exp(s - m)
        p = p / jnp.sum(p, axis=1, keepdims=True)
        outs.append(_dot(p.astype(BF16), vm_ref[:, sl]))
    o = jnp.concatenate(outs, axis=1).astype(BF16)
    xat = _dot(o, wmo_ref[...])
    h2 = _layer_norm(DEEPNORM_ALPHA * h1 + xat, g2_ref[...], b2_ref[...])
    h_ref[...] = h2
    hp_ref[...] = _pack_bf16_pairs(h2)


def _post_call(diff_o, mla_o, x2, wo, g1, b1, wq, kmem, vmem, wmo, g2, b2, s, tm):
    t = x2.shape[0]
    per_b = s // tm
    row = lambda w: pl.BlockSpec((tm, w), lambda i: (i, 0))
    full = lambda a: pl.BlockSpec(a.shape, lambda i: (0, 0))
    memb = pl.BlockSpec((N_MEM, kmem.shape[1]), lambda i: (i // per_b, 0))
    return pl.pallas_call(
        _post_kernel,
        out_shape=[jax.ShapeDtypeStruct((t, D_MODEL), F32), jax.ShapeDtypeStruct((t, D_MODEL // 2), jnp.uint32)],
        grid=(t // tm,),
        in_specs=[row(diff_o.shape[1]), row(mla_o.shape[1]), row(D_MODEL), full(wo), full(g1), full(b1),
                  full(wq), memb, memb, full(wmo), full(g2), full(b2)],
        out_specs=[row(D_MODEL), row(D_MODEL // 2)],
        compiler_params=_cparams(("parallel",)),
        name="post_mix_xattn",
    )(diff_o, mla_o, x2, wo, g1, b1, wq, kmem, vmem, wmo, g2, b2)


def _router_kernel(h_ref, rwt_ref, bias_ref, tri_ref, eid_ref, rnk_ref, wt_ref, cnt_ref):
    tm = h_ref.shape[0]

    @pl.when(pl.program_id(0) == 0)
    def _():
        cnt_ref[...] = jnp.zeros(cnt_ref.shape, cnt_ref.dtype)

    logits = lax.dot_general(rwt_ref[...], h_ref[...], (((1,), (1,)), ((), ())),
                             precision=lax.Precision.HIGHEST, preferred_element_type=F32)
    scores = jax.nn.sigmoid(logits)
    choice = scores + bias_ref[...]
    neg = float("-inf")

    c3 = choice.reshape(N_GROUPS, GROUP_SIZE, tm)
    mi = lax.broadcasted_iota(jnp.int32, c3.shape, 1)
    top1 = jnp.max(c3, axis=1, keepdims=True)
    first = jnp.min(jnp.where(c3 == top1, mi, GROUP_SIZE), axis=1, keepdims=True)
    top2 = jnp.max(jnp.where(mi == first, neg, c3), axis=1, keepdims=True)
    gs = (top1 + top2).reshape(N_GROUPS, tm)

    gi = lax.broadcasted_iota(jnp.int32, gs.shape, 0)
    rank = jnp.zeros(gs.shape, jnp.int32)
    for g in range(N_GROUPS):
        rowg = gs[g:g + 1, :]
        beats = (rowg > gs) | ((rowg == gs) & (g < gi))
        rank = rank + beats.astype(jnp.int32)
    gmask = rank < TOPK_GROUPS
    emask = jnp.broadcast_to(gmask.reshape(N_GROUPS, 1, tm), c3.shape).reshape(N_EXPERTS, tm)

    work = jnp.where(emask, choice, NEG_INF)
    ei = lax.broadcasted_iota(jnp.int32, work.shape, 0)
    sel = jnp.zeros(work.shape, jnp.bool_)
    picks = []
    for _ in range(TOP_K):
        mx = jnp.max(work, axis=0, keepdims=True)
        fi = jnp.min(jnp.where(work == mx, ei, N_EXPERTS), axis=0, keepdims=True)
        pick = ei == fi
        picks.append((fi, pick))
        sel = sel | pick
        work = jnp.where(pick, neg, work)
    top_w = jnp.where(sel, scores, 0.0)
    gates = top_w / jnp.sum(top_w, axis=0, keepdims=True) * ROUTED_SCALE

    sel_b = jnp.where(sel, 1.0, 0.0).astype(BF16)
    before = _dot(sel_b, tri_ref[...]).astype(jnp.int32)
    run = cnt_ref[:, 0:1]
    pos = run + before
    cnt_ref[...] = jnp.broadcast_to(run + jnp.sum(sel.astype(jnp.int32), axis=1, keepdims=True), cnt_ref.shape)

    eid_ref[...] = jnp.concatenate([fi for fi, _ in picks], axis=0)
    rnk_ref[...] = jnp.concatenate(
        [jnp.sum(jnp.where(pick, pos, 0), axis=0, keepdims=True) for _, pick in picks], axis=0)
    w_rows = [jnp.sum(jnp.where(pick, gates, 0.0), axis=0, keepdims=True) for _, pick in picks]
    w_pad = jnp.concatenate(w_rows + [jnp.zeros((V7X_LANES - TOP_K, tm), F32)], axis=0)
    wt_ref[...] = w_pad.T


def _router_call(h2, rwt, bias, tm):
    t = h2.shape[0]
    tri = jnp.triu(jnp.ones((tm, tm), BF16), k=1)
    full = lambda a: pl.BlockSpec(a.shape, lambda i: (0, 0))
    return pl.pallas_call(
        _router_kernel,
        out_shape=[jax.ShapeDtypeStruct((TOP_K, t), jnp.int32), jax.ShapeDtypeStruct((TOP_K, t), jnp.int32),
                   jax.ShapeDtypeStruct((t, V7X_LANES), F32), jax.ShapeDtypeStruct((N_EXPERTS, V7X_LANES), jnp.int32)],
        grid=(t // tm,),
        in_specs=[pl.BlockSpec((tm, D_MODEL), lambda i: (i, 0)), full(rwt), full(bias), full(tri)],
        out_specs=[pl.BlockSpec((TOP_K, tm), lambda i: (0, i)), pl.BlockSpec((TOP_K, tm), lambda i: (0, i)),
                   pl.BlockSpec((tm, V7X_LANES), lambda i: (i, 0)),
                   pl.BlockSpec((N_EXPERTS, V7X_LANES), lambda i: (0, 0))],
        compiler_params=_cparams(("arbitrary",)),
        name="router",
    )(h2, rwt, bias, tri)


_SC_MESH_AXES = ("core", "subcore")


def _sc_mesh():
    return plsc.VectorSubcoreMesh(core_axis_name=_SC_MESH_AXES[0], subcore_axis_name=_SC_MESH_AXES[1])


def _sc_worker_chunks(n_chunks):
    n_workers = V7X_SC_CORES * V7X_SC_SUBCORES
    assert n_chunks % n_workers == 0
    per_worker = n_chunks // n_workers
    first = (lax.axis_index(_SC_MESH_AXES[1]) * V7X_SC_CORES + lax.axis_index(_SC_MESH_AXES[0])) * per_worker
    return first, per_worker


def _sc_scatter_rows(x, idx, n_rows):
    t, w = x.shape
    n_chunks, n_k, window = idx.shape

    @functools.partial(pl.kernel, out_type=jax.ShapeDtypeStruct((n_rows, w), x.dtype), mesh=_sc_mesh(),
                       scratch_types=[pltpu.VMEM((n_k, window), jnp.int32), pltpu.VMEM((window, w), x.dtype)],
                       name="moe_dispatch_sc")
    def scatter(x_hbm, i_hbm, o_hbm, idx_v, rows_v):
        first, per_worker = _sc_worker_chunks(n_chunks)

        @pl.loop(0, per_worker)
        def _(c):
            chunk = first + c
            pltpu.sync_copy(i_hbm.at[chunk], idx_v)
            pltpu.sync_copy(x_hbm.at[pl.ds(chunk * window, window)], rows_v)
            for k in range(n_k):
                pltpu.sync_copy(rows_v, o_hbm.at[idx_v.at[k]])

    return scatter(x, idx)


def _sc_gather_rows(y, idx):
    n_chunks, n_k, window = idx.shape
    w = y.shape[1]

    half = window // 2
    steps = [(k, h) for k in range(n_k) for h in range(2)]

    @functools.partial(pl.kernel, out_type=jax.ShapeDtypeStruct((n_k, n_chunks * window, w), y.dtype),
                       mesh=_sc_mesh(),
                       scratch_types=[pltpu.VMEM((n_k, window), jnp.int32), pltpu.VMEM((half, w), y.dtype),
                                      pltpu.VMEM((half, w), y.dtype), pltpu.SemaphoreType.DMA,
                                      pltpu.SemaphoreType.DMA],
                       name="moe_combine_sc")
    def gather(y_hbm, i_hbm, o_hbm, idx_v, rows_a, rows_b, sem_a, sem_b):
        first, per_worker = _sc_worker_chunks(n_chunks)
        bufs = ((rows_a, sem_a), (rows_b, sem_b))

        @pl.loop(0, per_worker)
        def _(c):
            chunk = first + c
            pltpu.sync_copy(i_hbm.at[chunk], idx_v)

            def start(j):
                k, h = steps[j]
                buf, sem = bufs[j % 2]
                return pltpu.async_copy(y_hbm.at[idx_v.at[k, pl.ds(h * half, half)]], buf, sem)

            pending = start(0)
            for j, (k, h) in enumerate(steps):
                following = start(j + 1) if j + 1 < len(steps) else None
                pending.wait()
                pltpu.sync_copy(bufs[j % 2][0], o_hbm.at[k, pl.ds(chunk * window + h * half, half)])
                pending = following

    return gather(y, idx)


def _experts_kernel(te_ref, nu_ref, xs_ref, wg_ref, wu_ref, wd_ref, y_ref, wg_b, wu_b, wd_b):
    i = pl.program_id(0)

    @pl.when(i < nu_ref[0])
    def _():
        @pl.when((i == 0) | (te_ref[i] != te_ref[jnp.maximum(i - 1, 0)]))
        def _():
            wg_b[...] = wg_ref[0].astype(BF16)
            wu_b[...] = wu_ref[0].astype(BF16)
            wd_b[...] = wd_ref[0].astype(BF16)

        half = D_MODEL // 2
        lo, hi = _unpack_bf16_pairs(xs_ref[...])
        a = _dot(lo, wg_b[:half, :]) + _dot(hi, wg_b[half:, :])
        u = _dot(lo, wu_b[:half, :]) + _dot(hi, wu_b[half:, :])
        y_ref[...] = _pack_bf16_pairs(_dot((jax.nn.silu(a) * u).astype(BF16), wd_b[...]))


def _experts_call(tile_expert, n_used, xs, wg, wu, wd, tmx):
    n_rows, w = xs.shape
    tile = lambda i, te, nu: jnp.minimum(i, nu[0] - 1)
    wspec = lambda shp: pl.BlockSpec((1,) + shp, lambda i, te, nu: (te[tile(i, te, nu)], 0, 0))
    return pl.pallas_call(
        _experts_kernel,
        out_shape=jax.ShapeDtypeStruct((n_rows, w), xs.dtype),
        grid_spec=pltpu.PrefetchScalarGridSpec(
            num_scalar_prefetch=2,
            grid=(n_rows // tmx,),
            in_specs=[pl.BlockSpec((tmx, w), lambda i, te, nu: (tile(i, te, nu), 0)),
                      wspec((D_MODEL, EXPERT_DIM)), wspec((D_MODEL, EXPERT_DIM)), wspec((EXPERT_DIM, D_MODEL))],
            out_specs=pl.BlockSpec((tmx, w), lambda i, te, nu: (tile(i, te, nu), 0)),
            scratch_shapes=[pltpu.VMEM((D_MODEL, EXPERT_DIM), BF16), pltpu.VMEM((D_MODEL, EXPERT_DIM), BF16),
                            pltpu.VMEM((EXPERT_DIM, D_MODEL), BF16)],
        ),
        compiler_params=_cparams(("arbitrary",)),
        name="moe_experts",
    )(tile_expert, n_used, xs, wg, wu, wd)


def _combine_kernel(wt_ref, h_ref, yg_ref, sg_ref, su_ref, sd_ref, g3_ref, b3_ref, o_ref):
    h = h_ref[...]
    hb = h.astype(BF16)
    acc = _dot((jax.nn.silu(_dot(hb, sg_ref[...])) * _dot(hb, su_ref[...])).astype(BF16), sd_ref[...])
    wt = wt_ref[...]
    lo_sum = None
    hi_sum = None
    for k in range(TOP_K):
        lo, hi = _unpack_bf16_pairs(yg_ref[k])
        wk = wt[:, k:k + 1]
        lo_sum = wk * lo if lo_sum is None else lo_sum + wk * lo
        hi_sum = wk * hi if hi_sum is None else hi_sum + wk * hi
    acc = acc + jnp.concatenate([lo_sum, hi_sum], axis=1)
    o_ref[...] = _layer_norm(DEEPNORM_ALPHA * h + acc, g3_ref[...], b3_ref[...])


def _combine_call(wt, h2, yg, sg, su, sd, g3, b3, tm):
    t = h2.shape[0]
    full = lambda a: pl.BlockSpec(a.shape, lambda i: (0, 0))
    return pl.pallas_call(
        _combine_kernel,
        out_shape=jax.ShapeDtypeStruct((t, D_MODEL), F32),
        grid=(t // tm,),
        in_specs=[pl.BlockSpec((tm, V7X_LANES), lambda i: (i, 0)),
                  pl.BlockSpec((tm, D_MODEL), lambda i: (i, 0)),
                  pl.BlockSpec((TOP_K, tm, yg.shape[2]), lambda i: (0, i, 0)),
                  full(sg), full(su), full(sd), full(g3), full(b3)],
        out_specs=pl.BlockSpec((tm, D_MODEL), lambda i: (i, 0)),
        compiler_params=_cparams(("parallel",)),
        name="moe_combine",
    )(wt, h2, yg, sg, su, sd, g3, b3)


def _moe(h2, hp, router_w, router_bias, wg, wu, wd, sg, su, sd, g3, b3, tm_route, tmx, tm_comb):
    t = h2.shape[0]
    eid, rnk, wt, cnt = _router_call(h2, router_w.T, router_bias.reshape(-1, 1), tm_route)
    counts = cnt[:, 0]
    tiles_per_e = (counts + (tmx - 1)) // tmx
    tile_end = jnp.cumsum(tiles_per_e)
    row_start = ((tile_end - tiles_per_e) * tmx).astype(jnp.int32)
    n_tiles = (t * TOP_K) // tmx + N_EXPERTS
    n_used = tile_end[-1:].astype(jnp.int32)
    tile_expert = jnp.minimum(
        jnp.sum(tile_end[None, :] <= jnp.arange(n_tiles, dtype=jnp.int32)[:, None], axis=1), N_EXPERTS - 1
    ).astype(jnp.int32)
    expert_ids = jnp.arange(N_EXPERTS, dtype=jnp.int32)[:, None, None]
    dest = rnk + jnp.sum(jnp.where(eid[None] == expert_ids, row_start[:, None, None], 0), axis=0)
    dest_chunks = dest.reshape(TOP_K, t // SC_INDEX_WINDOW, SC_INDEX_WINDOW).transpose(1, 0, 2)
    xs = _sc_scatter_rows(hp, dest_chunks, n_tiles * tmx)
    yp = _experts_call(tile_expert, n_used, xs, wg, wu, wd, tmx)
    yg = _sc_gather_rows(yp, dest_chunks)
    return _combine_call(wt, h2, yg, sg, su, sd, g3, b3, tm_comb)


def _prep_w_in_t(w_in):
    pad = jnp.zeros((D_MODEL, V7X_LANES - MLA_ROPE_DIM), w_in.dtype)
    return jnp.concatenate([w_in, pad], axis=1).T.astype(BF16)


def _prep_w_uq_t(w_uq):
    w = w_uq.reshape(MLA_Q_RANK, MLA_HEADS, MLA_NOPE_DIM + MLA_ROPE_DIM)
    pad = jnp.zeros((MLA_Q_RANK, MLA_HEADS, MLA_QK_PAD - MLA_NOPE_DIM - MLA_ROPE_DIM), w.dtype)
    return jnp.concatenate([w, pad], axis=2).reshape(MLA_Q_RANK, MLA_HEADS * MLA_QK_PAD).T.astype(BF16)


def _prep_w_ukv_t(w_ukv):
    w = w_ukv.reshape(MLA_KV_RANK, MLA_HEADS, MLA_NOPE_DIM + MLA_V_DIM)
    k = w[:, :, :MLA_NOPE_DIM].reshape(MLA_KV_RANK, MLA_HEADS * MLA_NOPE_DIM)
    v = w[:, :, MLA_NOPE_DIM:].reshape(MLA_KV_RANK, MLA_HEADS * MLA_V_DIM)
    return jnp.concatenate([k, v], axis=1).T.astype(BF16)


def _rope_inv_freq(dim):
    return (ROPE_THETA ** (-jnp.arange(0, dim, 2, dtype=F32) / dim)).reshape(-1, 1)


def _layer(x2, mem2, pos_row, b, s, l, w_in, lam_q1, lam_k1, lam_q2, lam_k2, diff_subln_g, mla_q_norm_g,
           mla_w_uq, mla_kv_norm_g, mla_w_ukv, mla_out_norm_g, w_o, ln1_g, ln1_b, mem_w_q, mem_w_k, mem_w_v,
           mem_w_o, ln2_g, ln2_b, router_w, router_bias, exp_w_gate, exp_w_up, exp_w_down, sh_w_gate,
           sh_w_up, sh_w_down, ln3_g, ln3_b):
    row = lambda a: a.reshape(1, -1)
    col = lambda a: a.reshape(-1, 1)
    dqt, dk, dvt, mqt, mk, mvt = _proj_call(
        x2, pos_row, _rope_inv_freq(DIFF_ROT_DIM), _rope_inv_freq(MLA_ROPE_DIM), _prep_w_in_t(w_in),
        col(mla_q_norm_g), _prep_w_uq_t(mla_w_uq), col(mla_kv_norm_g), _prep_w_ukv_t(mla_w_ukv), tm=min(512, s))

    lam_init = 0.8 - 0.6 * math.exp(-0.3 * l)
    lam_vecs = jnp.stack([lam_q1, lam_k1, lam_q2, lam_k2]).astype(F32)
    tq = min(512, s)
    diff_o = _attn_call(
        functools.partial(_diff_attn_kernel, tq=tq, lam_init=lam_init), "diff_attn",
        dqt, dk, dvt, [lam_vecs, col(diff_subln_g)], b, s, DIFF_HEADS, 2 * DIFF_QK_DIM, DIFF_V_DIM, tq, 2)
    mla_o = _attn_call(
        functools.partial(_mla_attn_kernel, tq=tq), "mla_attn",
        mqt, mk, mvt, [col(mla_out_norm_g)], b, s, MLA_HEADS, MLA_QK_PAD, MLA_V_DIM, tq, 1)

    kmem, vmem = _memkv_call(mem2, mem_w_k.astype(BF16), mem_w_v.astype(BF16))
    h2, hp = _post_call(diff_o, mla_o, x2, w_o.astype(BF16), row(ln1_g), row(ln1_b), mem_w_q.astype(BF16),
                        kmem, vmem, mem_w_o.astype(BF16), row(ln2_g), row(ln2_b), s, tm=min(512, s))

    return _moe(h2, hp, router_w, router_bias, exp_w_gate, exp_w_up, exp_w_down,
                sh_w_gate.astype(BF16), sh_w_up.astype(BF16), sh_w_down.astype(BF16),
                row(ln3_g), row(ln3_b), tm_route=min(512, s), tmx=512, tm_comb=min(256, s))


def kernel(x, mem, positions, w_in, lam_q1, lam_k1, lam_q2, lam_k2, diff_subln_g, mla_q_norm_g, mla_w_uq,
           mla_kv_norm_g, mla_w_ukv, mla_out_norm_g, w_o, ln1_g, ln1_b, mem_w_q, mem_w_k, mem_w_v, mem_w_o,
           ln2_g, ln2_b, router_w, router_bias, exp_w_gate, exp_w_up, exp_w_down, sh_w_gate, sh_w_up,
           sh_w_down, ln3_g, ln3_b):
    b, s, d = x.shape
    h = x.reshape(b * s, d)
    mem2 = mem.reshape(b * mem.shape[1], d)
    pos_row = positions.reshape(1, b * s)
    params = (w_in, lam_q1, lam_k1, lam_q2, lam_k2, diff_subln_g, mla_q_norm_g, mla_w_uq, mla_kv_norm_g,
              mla_w_ukv, mla_out_norm_g, w_o, ln1_g, ln1_b, mem_w_q, mem_w_k, mem_w_v, mem_w_o, ln2_g, ln2_b,
              router_w, router_bias, exp_w_gate, exp_w_up, exp_w_down, sh_w_gate, sh_w_up, sh_w_down,
              ln3_g, ln3_b)
    for l in range(w_in.shape[0]):
        h = _layer(h, mem2, pos_row, b, s, l, *[p[l] for p in params])
    return h.reshape(b, s, d)
```

```python
import functools
import math

import jax
import jax.numpy as jnp
from jax import lax
from jax.experimental import pallas as pl
from jax.experimental.pallas import tpu as pltpu
from jax.experimental.pallas import tpu_sc as plsc

F32 = jnp.float32
BF16 = jnp.bfloat16

D_MODEL = 1024
N_MEM = 256
ROPE_THETA = 500000.0
NEG_INF = -1e30
LN_EPS = 1e-5
RMS_EPS = 1e-6
DIFF_HEADS = 4
DIFF_QK_DIM = 64
DIFF_V_DIM = 128
DIFF_ROT_DIM = DIFF_QK_DIM // 4
MLA_HEADS = 4
MLA_Q_RANK = 256
MLA_KV_RANK = 128
MLA_NOPE_DIM = 128
MLA_ROPE_DIM = 64
MLA_V_DIM = 128
MLA_QK_PAD = 256
V_AUG_DIM = 128 + 16
MEM_HEADS = 4
MEM_HEAD_DIM = 128
N_EXPERTS = 64
TOP_K = 8
N_GROUPS = 8
TOPK_GROUPS = 4
GROUP_SIZE = N_EXPERTS // N_GROUPS
EXPERT_DIM = 256
SHARED_DIM = 256
ROUTED_SCALE = 2.5
DEPTH = 1
DEEPNORM_ALPHA = (2.0 * DEPTH) ** 0.25
LOG2E = math.log2(math.e)

V7X_LANES = 128
V7X_SUBLANES = 8
V7X_VMEM_LIMIT_BYTES = 56 * 1024 * 1024
V7X_SC_CORES = 2
V7X_SC_SUBCORES = 16
SC_INDEX_WINDOW = 128
V7X_BF16_SUBLANES = 16

N_DQ = DIFF_HEADS * 2 * DIFF_QK_DIM
N_DV = DIFF_HEADS * DIFF_V_DIM
O_DK = N_DQ
O_DV = 2 * N_DQ
O_CQ = O_DV + N_DV
O_CKV = O_CQ + MLA_Q_RANK
O_KR = O_CKV + MLA_KV_RANK
N_IN_PAD = O_KR + V7X_LANES


def _cparams(sem):
    return pltpu.CompilerParams(dimension_semantics=sem, vmem_limit_bytes=V7X_VMEM_LIMIT_BYTES)


def _dot(a, b):
    return jnp.dot(a, b, preferred_element_type=F32)


def _dot_nt(a, b):
    return lax.dot_general(a, b, (((1,), (1,)), ((), ())), preferred_element_type=F32)


def _rms(x, g):
    return x * lax.rsqrt(jnp.mean(jnp.square(x), axis=-1, keepdims=True) + RMS_EPS) * g


def _rms_rows(x, g):
    return x * lax.rsqrt(jnp.mean(jnp.square(x), axis=0, keepdims=True) + RMS_EPS) * g


def _layer_norm(x, g, b):
    mu = jnp.mean(x, axis=-1, keepdims=True)
    xc = x - mu
    var = jnp.mean(jnp.square(xc), axis=-1, keepdims=True)
    return xc * lax.rsqrt(var + LN_EPS) * g + b


def _with_ones_rows(vt, heads):
    dv = vt.shape[0] // heads
    ones = jnp.ones((V_AUG_DIM - dv, vt.shape[1]), vt.dtype)
    return jnp.concatenate([p for h in range(heads) for p in (vt[h * dv:(h + 1) * dv], ones)], axis=0)


def _rope_rows(x, cos, sin, period):
    h = cos.shape[0]
    pieces = []
    for base in range(0, x.shape[0], period):
        x1 = x[base:base + h]
        x2 = x[base + h:base + 2 * h]
        pieces.append(x1 * cos - x2 * sin)
        pieces.append(x2 * cos + x1 * sin)
        if period > 2 * h:
            pieces.append(x[base + 2 * h:base + period])
    return jnp.concatenate(pieces, axis=0)


def _proj_kernel(x_ref, pos_ref, fd_ref, fm_ref, w_in_ref, gq_ref, wuq_ref, gkv_ref, wukv_ref,
                 dq_ref, dk_ref, dv_ref, mq_ref, mk_ref, mv_ref):
    xb = x_ref[...].astype(BF16)
    pos = pos_ref[...].astype(F32)
    ang_d = fd_ref[...] * pos
    ang_m = fm_ref[...] * pos
    cos_d, sin_d = jnp.cos(ang_d), jnp.sin(ang_d)
    cos_m, sin_m = jnp.cos(ang_m), jnp.sin(ang_m)

    dq = _rope_rows(_dot_nt(w_in_ref[0:N_DQ, :], xb), cos_d, sin_d, DIFF_QK_DIM)
    dq_ref[...] = (dq * (DIFF_QK_DIM ** -0.5 * LOG2E)).astype(BF16)
    dk = _rope_rows(_dot_nt(w_in_ref[O_DK:O_DK + N_DQ, :], xb), cos_d, sin_d, DIFF_QK_DIM)
    dk_ref[...] = dk.T.astype(BF16)
    dv_ref[...] = _with_ones_rows(_dot_nt(w_in_ref[O_DV:O_DV + N_DV, :], xb), DIFF_HEADS).astype(BF16)

    c_q = _dot_nt(w_in_ref[O_CQ:O_CQ + MLA_Q_RANK, :], xb)
    q = _dot(wuq_ref[...], _rms_rows(c_q, gq_ref[...]).astype(BF16))
    q = q * ((MLA_NOPE_DIM + MLA_ROPE_DIM) ** -0.5 * LOG2E)
    pieces = []
    for h in range(MLA_HEADS):
        o = h * MLA_QK_PAD
        pieces.append(q[o:o + MLA_NOPE_DIM])
        pieces.append(_rope_rows(q[o + MLA_NOPE_DIM:o + MLA_NOPE_DIM + MLA_ROPE_DIM], cos_m, sin_m, MLA_ROPE_DIM))
        pieces.append(q[o + MLA_NOPE_DIM + MLA_ROPE_DIM:o + MLA_QK_PAD])
    mq_ref[...] = jnp.concatenate(pieces, axis=0).astype(BF16)

    c_kv = _dot_nt(w_in_ref[O_CKV:O_CKV + MLA_KV_RANK, :], xb)
    kv = _dot(wukv_ref[...], _rms_rows(c_kv, gkv_ref[...]).astype(BF16))
    n_kn = MLA_HEADS * MLA_NOPE_DIM
    mv_ref[...] = _with_ones_rows(kv[n_kn:], MLA_HEADS).astype(BF16)
    k_rope = _dot_nt(w_in_ref[O_KR:O_KR + V7X_LANES, :], xb)
    k_pe = jnp.concatenate([_rope_rows(k_rope[:MLA_ROPE_DIM], cos_m, sin_m, MLA_ROPE_DIM),
                            k_rope[MLA_ROPE_DIM:]], axis=0)
    k_nope_t = kv[:n_kn].T.astype(BF16)
    k_pe_t = k_pe.T.astype(BF16)
    for h in range(MLA_HEADS):
        o = h * MLA_QK_PAD
        mk_ref[:, o:o + MLA_NOPE_DIM] = k_nope_t[:, h * MLA_NOPE_DIM:(h + 1) * MLA_NOPE_DIM]
        mk_ref[:, o + MLA_NOPE_DIM:o + MLA_QK_PAD] = k_pe_t


def _proj_call(x2, pos_row, fd, fm, w_in_t, gq, wuq_t, gkv, wukv_t, tm):
    t = x2.shape[0]
    full = lambda a: pl.BlockSpec(a.shape, lambda i: (0, 0))
    col_blk = lambda r: pl.BlockSpec((r, tm), lambda i: (0, i))
    row_blk = lambda w: pl.BlockSpec((tm, w), lambda i: (i, 0))
    n_mq = MLA_HEADS * MLA_QK_PAD
    return pl.pallas_call(
        _proj_kernel,
        out_shape=[jax.ShapeDtypeStruct((N_DQ, t), BF16), jax.ShapeDtypeStruct((t, N_DQ), BF16),
                   jax.ShapeDtypeStruct((DIFF_HEADS * V_AUG_DIM, t), BF16), jax.ShapeDtypeStruct((n_mq, t), BF16),
                   jax.ShapeDtypeStruct((t, n_mq), BF16), jax.ShapeDtypeStruct((MLA_HEADS * V_AUG_DIM, t), BF16)],
        grid=(t // tm,),
        in_specs=[row_blk(D_MODEL), col_blk(1), full(fd), full(fm), full(w_in_t), full(gq), full(wuq_t),
                  full(gkv), full(wukv_t)],
        out_specs=[col_blk(N_DQ), row_blk(N_DQ), col_blk(DIFF_HEADS * V_AUG_DIM), col_blk(n_mq), row_blk(n_mq),
                   col_blk(MLA_HEADS * V_AUG_DIM)],
        compiler_params=_cparams(("parallel",)),
        name="proj_rope",
    )(x2, pos_row, fd, fm, w_in_t, gq, wuq_t, gkv, wukv_t)


def _flash_t(qq, k_ref, vt_ref, s_refs, p_refs, acc_ref, i, tq):
    n = qq.shape[1]
    tk = s_refs[0].shape[0]
    assert tq == 2 * tk
    q_pos = i * tq + (lax.broadcasted_iota(jnp.int32, (tk, n), 1) & (tq - 1))
    key_row = lax.broadcasted_iota(jnp.int32, (tk, n), 0)

    def scores(c, s_ref):
        off = pl.multiple_of(c * tk, tk)
        s = _dot(k_ref[pl.ds(off, tk), :], qq)
        s_ref[...] = s
        return jnp.max(s, axis=0, keepdims=True)

    def softmax(c, s_ref, p_ref, cmax, m, masked):
        s = s_ref[...]
        if masked:
            s = jnp.where(key_row + c * tk <= q_pos, s, NEG_INF)
            cmax = jnp.max(s, axis=0, keepdims=True)
        m_new = jnp.maximum(m, cmax)
        alpha = jnp.exp2(m - m_new)
        p_ref[...] = jnp.exp2((s - m_new).astype(BF16))
        return m_new, alpha

    def values(c, p_ref, alpha):
        off = pl.multiple_of(jnp.maximum(c, 0) * tk, tk)
        acc_ref[...] = alpha * acc_ref[...] + _dot(vt_ref[:, pl.ds(off, tk)], p_ref[...])

    def pair(u, carry, masked_second):
        cm1, alpha0, m = carry
        c = 2 * u
        cm0 = scores(c, s_refs[0])
        values(c - 2, p_refs[0], alpha0)
        m, alpha1 = softmax(c - 1, s_refs[1], p_refs[1], cm1, m, False)
        cm1 = scores(c + 1, s_refs[1])
        values(c - 1, p_refs[1], alpha1)
        m, alpha0 = softmax(c, s_refs[0], p_refs[0], cm0, m, masked_second)
        return cm1, alpha0, m

    s_refs[1][...] = jnp.full(s_refs[1].shape, 2 * NEG_INF, F32)
    p_refs[0][...] = jnp.zeros(p_refs[0].shape, BF16)
    acc_ref[...] = jnp.zeros(acc_ref.shape, F32)
    init = (jnp.full((1, n), 2 * NEG_INF, F32), jnp.ones((1, n), F32), jnp.full((1, n), NEG_INF, F32))
    carry = lax.fori_loop(0, i, functools.partial(pair, masked_second=False), init)
    cm1, alpha0, m = pair(i, carry, True)
    values(2 * i, p_refs[0], alpha0)
    m, alpha1 = softmax(2 * i + 1, s_refs[1], p_refs[1], cm1, m, True)
    values(2 * i + 1, p_refs[1], alpha1)
    dv = vt_ref.shape[0] - V7X_BF16_SUBLANES
    return acc_ref[0:dv, :] / acc_ref[dv:dv + 1, :]


def _diff_attn_kernel(qt_ref, k_ref, vt_ref, lam_ref, g_ref, o_ref, s0, s1, p0, p1, acc_ref, *, tq, lam_init):
    i = pl.program_id(2)
    qt = qt_ref[...]
    row = lax.broadcasted_iota(jnp.int32, qt.shape, 0)
    zero = jnp.zeros_like(qt)
    qq = jnp.concatenate([jnp.where(row < DIFF_QK_DIM, qt, zero), jnp.where(row >= DIFF_QK_DIM, qt, zero)], axis=1)
    o = _flash_t(qq, k_ref, vt_ref, (s0, s1), (p0, p1), acc_ref, i, tq)
    lv = lam_ref[...]
    lam = (jnp.exp(jnp.sum(lv[0:1] * lv[1:2], axis=1, keepdims=True))
           - jnp.exp(jnp.sum(lv[2:3] * lv[3:4], axis=1, keepdims=True)) + lam_init)
    d = o[:, :tq] - lam * o[:, tq:]
    o_ref[...] = (_rms_rows(d, g_ref[...]) * (1.0 - lam_init)).T.astype(o_ref.dtype)


def _mla_attn_kernel(qt_ref, k_ref, vt_ref, g_ref, o_ref, s0, s1, p0, p1, acc_ref, *, tq):
    i = pl.program_id(2)
    o = _flash_t(qt_ref[...], k_ref, vt_ref, (s0, s1), (p0, p1), acc_ref, i, tq)
    o_ref[...] = _rms_rows(o, g_ref[...]).T.astype(o_ref.dtype)


def _attn_call(kernel, name, qt, k, vt, extras, b, s, heads, dk_w, dv_w, tq, n_maps):
    nq = s // tq
    t = b * s
    tk = tq // 2
    n = n_maps * tq
    small = [pl.BlockSpec(a.shape, lambda bi, h, i: (0, 0)) for a in extras]
    return pl.pallas_call(
        kernel,
        out_shape=jax.ShapeDtypeStruct((t, heads * dv_w), BF16),
        grid=(b, heads, nq),
        in_specs=[pl.BlockSpec((dk_w, tq), lambda bi, h, i: (h, bi * nq + i)),
                  pl.BlockSpec((s, dk_w), lambda bi, h, i: (bi, h)),
                  pl.BlockSpec((V_AUG_DIM, s), lambda bi, h, i: (h, bi))] + small,
        out_specs=pl.BlockSpec((tq, dv_w), lambda bi, h, i: (bi * nq + i, h)),
        scratch_shapes=[pltpu.VMEM((tk, n), F32), pltpu.VMEM((tk, n), F32),
                        pltpu.VMEM((tk, n), BF16), pltpu.VMEM((tk, n), BF16), pltpu.VMEM((V_AUG_DIM, n), F32)],
        compiler_params=_cparams(("parallel", "parallel", "arbitrary")),
        name=name,
    )(qt, k, vt, *extras)


def _memkv_kernel(mem_ref, wk_ref, wv_ref, k_ref, v_ref):
    mb = mem_ref[...].astype(BF16)
    k_ref[...] = _dot(mb, wk_ref[...]).astype(BF16)
    v_ref[...] = _dot(mb, wv_ref[...]).astype(BF16)


def _memkv_call(mem2, wk, wv):
    n = mem2.shape[0]
    w = wk.shape[1]
    tm = N_MEM
    return pl.pallas_call(
        _memkv_kernel,
        out_shape=[jax.ShapeDtypeStruct((n, w), BF16)] * 2,
        grid=(n // tm,),
        in_specs=[pl.BlockSpec((tm, D_MODEL), lambda i: (i, 0)),
                  pl.BlockSpec(wk.shape, lambda i: (0, 0)),
                  pl.BlockSpec(wv.shape, lambda i: (0, 0))],
        out_specs=[pl.BlockSpec((tm, w), lambda i: (i, 0))] * 2,
        compiler_params=_cparams(("parallel",)),
        name="mem_kv",
    )(mem2, wk, wv)


def _pack_bf16_pairs(x):
    half = x.shape[1] // 2
    r = pltpu.bitcast(x.astype(BF16).astype(F32), jnp.uint32)
    return (r[:, :half] >> 16) | (r[:, half:] & jnp.uint32(0xFFFF0000))


def _unpack_bf16_pairs(w):
    lo = pltpu.bitcast(w << 16, F32).astype(BF16)
    hi = pltpu.bitcast(w & jnp.uint32(0xFFFF0000), F32).astype(BF16)
    return lo, hi


def _post_kernel(da_ref, ma_ref, x_ref, wo_ref, g1_ref, b1_ref, wq_ref, km_ref, vm_ref, wmo_ref,
                 g2_ref, b2_ref, h_ref, hp_ref):
    n_d = DIFF_HEADS * DIFF_V_DIM
    mix = _dot(da_ref[...], wo_ref[0:n_d, :]) + _dot(ma_ref[...], wo_ref[n_d:, :])
    h1 = _layer_norm(DEEPNORM_ALPHA * x_ref[...] + mix, g1_ref[...], b1_ref[...])
    q = (_dot(h1.astype(BF16), wq_ref[...]) * (MEM_HEAD_DIM ** -0.5)).astype(BF16)
    outs = []
    for h in range(MEM_HEADS):
        sl = slice(h * MEM_HEAD_DIM, (h + 1) * MEM_HEAD_DIM)
        s = _dot_nt(q[:, sl], km_ref[:, sl])
        m = jnp.max(s, axis=1, keepdims=True)
        p = jnp.exp(s - m)
        p = p / jnp.sum(p, axis=1, keepdims=True)
        outs.append(_dot(p.astype(BF16), vm_ref[:, sl]))
    o = jnp.concatenate(outs, axis=1).astype(BF16)
    xat = _dot(o, wmo_ref[...])
    h2 = _layer_norm(DEEPNORM_ALPHA * h1 + xat, g2_ref[...], b2_ref[...])
    h_ref[...] = h2
    hp_ref[...] = _pack_bf16_pairs(h2)


def _post_call(diff_o, mla_o, x2, wo, g1, b1, wq, kmem, vmem, wmo, g2, b2, s, tm):
    t = x2.shape[0]
    per_b = s // tm
    row = lambda w: pl.BlockSpec((tm, w), lambda i: (i, 0))
    full = lambda a: pl.BlockSpec(a.shape, lambda i: (0, 0))
    memb = pl.BlockSpec((N_MEM, kmem.shape[1]), lambda i: (i // per_b, 0))
    return pl.pallas_call(
        _post_kernel,
        out_shape=[jax.ShapeDtypeStruct((t, D_MODEL), F32), jax.ShapeDtypeStruct((t, D_MODEL // 2), jnp.uint32)],
        grid=(t // tm,),
        in_specs=[row(diff_o.shape[1]), row(mla_o.shape[1]), row(D_MODEL), full(wo), full(g1), full(b1),
                  full(wq), memb, memb, full(wmo), full(g2), full(b2)],
        out_specs=[row(D_MODEL), row(D_MODEL // 2)],
        compiler_params=_cparams(("parallel",)),
        name="post_mix_xattn",
    )(diff_o, mla_o, x2, wo, g1, b1, wq, kmem, vmem, wmo, g2, b2)


def _router_kernel(h_ref, rwt_ref, bias_ref, tri_ref, eid_ref, rnk_ref, wt_ref, cnt_ref):
    tm = h_ref.shape[0]

    @pl.when(pl.program_id(0) == 0)
    def _():
        cnt_ref[...] = jnp.zeros(cnt_ref.shape, cnt_ref.dtype)

    logits = lax.dot_general(rwt_ref[...], h_ref[...], (((1,), (1,)), ((), ())),
                             precision=lax.Precision.HIGHEST, preferred_element_type=F32)
    scores = jax.nn.sigmoid(logits)
    choice = scores + bias_ref[...]
    neg = float("-inf")

    c3 = choice.reshape(N_GROUPS, GROUP_SIZE, tm)
    mi = lax.broadcasted_iota(jnp.int32, c3.shape, 1)
    top1 = jnp.max(c3, axis=1, keepdims=True)
    first = jnp.min(jnp.where(c3 == top1, mi, GROUP_SIZE), axis=1, keepdims=True)
    top2 = jnp.max(jnp.where(mi == first, neg, c3), axis=1, keepdims=True)
    gs = (top1 + top2).reshape(N_GROUPS, tm)

    gi = lax.broadcasted_iota(jnp.int32, gs.shape, 0)
    rank = jnp.zeros(gs.shape, jnp.int32)
    for g in range(N_GROUPS):
        rowg = gs[g:g + 1, :]
        beats = (rowg > gs) | ((rowg == gs) & (g < gi))
        rank = rank + beats.astype(jnp.int32)
    gmask = rank < TOPK_GROUPS
    emask = jnp.broadcast_to(gmask.reshape(N_GROUPS, 1, tm), c3.shape).reshape(N_EXPERTS, tm)

    work = jnp.where(emask, choice, NEG_INF)
    ei = lax.broadcasted_iota(jnp.int32, work.shape, 0)
    sel = jnp.zeros(work.shape, jnp.bool_)
    picks = []
    for _ in range(TOP_K):
        mx = jnp.max(work, axis=0, keepdims=True)
        fi = jnp.min(jnp.where(work == mx, ei, N_EXPERTS), axis=0, keepdims=True)
        pick = ei == fi
        picks.append((fi, pick))
        sel = sel | pick
        work = jnp.where(pick, neg, work)
    top_w = jnp.where(sel, scores, 0.0)
    gates = top_w / jnp.sum(top_w, axis=0, keepdims=True) * ROUTED_SCALE

    sel_b = jnp.where(sel, 1.0, 0.0).astype(BF16)
    before = _dot(sel_b, tri_ref[...]).astype(jnp.int32)
    run = cnt_ref[:, 0:1]
    pos = run + before
    cnt_ref[...] = jnp.broadcast_to(run + jnp.sum(sel.astype(jnp.int32), axis=1, keepdims=True), cnt_ref.shape)

    eid_ref[...] = jnp.concatenate([fi for fi, _ in picks], axis=0)
    rnk_ref[...] = jnp.concatenate(
        [jnp.sum(jnp.where(pick, pos, 0), axis=0, keepdims=True) for _, pick in picks], axis=0)
    w_rows = [jnp.sum(jnp.where(pick, gates, 0.0), axis=0, keepdims=True) for _, pick in picks]
    w_pad = jnp.concatenate(w_rows + [jnp.zeros((V7X_LANES - TOP_K, tm), F32)], axis=0)
    wt_ref[...] = w_pad.T


def _router_call(h2, rwt, bias, tm):
    t = h2.shape[0]
    tri = jnp.triu(jnp.ones((tm, tm), BF16), k=1)
    full = lambda a: pl.BlockSpec(a.shape, lambda i: (0, 0))
    return pl.pallas_call(
        _router_kernel,
        out_shape=[jax.ShapeDtypeStruct((TOP_K, t), jnp.int32), jax.ShapeDtypeStruct((TOP_K, t), jnp.int32),
                   jax.ShapeDtypeStruct((t, V7X_LANES), F32), jax.ShapeDtypeStruct((N_EXPERTS, V7X_LANES), jnp.int32)],
        grid=(t // tm,),
        in_specs=[pl.BlockSpec((tm, D_MODEL), lambda i: (i, 0)), full(rwt), full(bias), full(tri)],
        out_specs=[pl.BlockSpec((TOP_K, tm), lambda i: (0, i)), pl.BlockSpec((TOP_K, tm), lambda i: (0, i)),
                   pl.BlockSpec((tm, V7X_LANES), lambda i: (i, 0)),
                   pl.BlockSpec((N_EXPERTS, V7X_LANES), lambda i: (0, 0))],
        compiler_params=_cparams(("arbitrary",)),
        name="router",
    )(h2, rwt, bias, tri)


_SC_MESH_AXES = ("core", "subcore")


def _sc_mesh():
    return plsc.VectorSubcoreMesh(core_axis_name=_SC_MESH_AXES[0], subcore_axis_name=_SC_MESH_AXES[1])


def _sc_worker_chunks(n_chunks):
    n_workers = V7X_SC_CORES * V7X_SC_SUBCORES
    assert n_chunks % n_workers == 0
    per_worker = n_chunks // n_workers
    first = (lax.axis_index(_SC_MESH_AXES[1]) * V7X_SC_CORES + lax.axis_index(_SC_MESH_AXES[0])) * per_worker
    return first, per_worker


def _sc_scatter_rows(x, idx, n_rows):
    t, w = x.shape
    n_chunks, n_k, window = idx.shape

    @functools.partial(pl.kernel, out_type=jax.ShapeDtypeStruct((n_rows, w), x.dtype), mesh=_sc_mesh(),
                       scratch_types=[pltpu.VMEM((n_k, window), jnp.int32), pltpu.VMEM((window, w), x.dtype)],
                       name="moe_dispatch_sc")
    def scatter(x_hbm, i_hbm, o_hbm, idx_v, rows_v):
        first, per_worker = _sc_worker_chunks(n_chunks)

        @pl.loop(0, per_worker)
        def _(c):
            chunk = first + c
            pltpu.sync_copy(i_hbm.at[chunk], idx_v)
            pltpu.sync_copy(x_hbm.at[pl.ds(chunk * window, window)], rows_v)
            for k in range(n_k):
                pltpu.sync_copy(rows_v, o_hbm.at[idx_v.at[k]])

    return scatter(x, idx)


def _sc_gather_rows(y, idx):
    n_chunks, n_k, window = idx.shape
    w = y.shape[1]

    half = window // 2
    steps = [(k, h) for k in range(n_k) for h in range(2)]

    @functools.partial(pl.kernel, out_type=jax.ShapeDtypeStruct((n_k, n_chunks * window, w), y.dtype),
                       mesh=_sc_mesh(),
                       scratch_types=[pltpu.VMEM((n_k, window), jnp.int32), pltpu.VMEM((half, w), y.dtype),
                                      pltpu.VMEM((half, w), y.dtype), pltpu.SemaphoreType.DMA,
                                      pltpu.SemaphoreType.DMA],
                       name="moe_combine_sc")
    def gather(y_hbm, i_hbm, o_hbm, idx_v, rows_a, rows_b, sem_a, sem_b):
        first, per_worker = _sc_worker_chunks(n_chunks)
        bufs = ((rows_a, sem_a), (rows_b, sem_b))

        @pl.loop(0, per_worker)
        def _(c):
            chunk = first + c
            pltpu.sync_copy(i_hbm.at[chunk], idx_v)

            def start(j):
                k, h = steps[j]
                buf, sem = bufs[j % 2]
                return pltpu.async_copy(y_hbm.at[idx_v.at[k, pl.ds(h * half, half)]], buf, sem)

            pending = start(0)
            for j, (k, h) in enumerate(steps):
                following = start(j + 1) if j + 1 < len(steps) else None
                pending.wait()
                pltpu.sync_copy(bufs[j % 2][0], o_hbm.at[k, pl.ds(chunk * window + h * half, half)])
                pending = following

    return gather(y, idx)


def _experts_kernel(te_ref, nu_ref, xs_ref, wg_ref, wu_ref, wd_ref, y_ref, wg_b, wu_b, wd_b):
    i = pl.program_id(0)

    @pl.when(i < nu_ref[0])
    def _():
        @pl.when((i == 0) | (te_ref[i] != te_ref[jnp.maximum(i - 1, 0)]))
        def _():
            wg_b[...] = wg_ref[0].astype(BF16)
            wu_b[...] = wu_ref[0].astype(BF16)
            wd_b[...] = wd_ref[0].astype(BF16)

        half = D_MODEL // 2
        lo, hi = _unpack_bf16_pairs(xs_ref[...])
        a = _dot(lo, wg_b[:half, :]) + _dot(hi, wg_b[half:, :])
        u = _dot(lo, wu_b[:half, :]) + _dot(hi, wu_b[half:, :])
        y_ref[...] = _pack_bf16_pairs(_dot((jax.nn.silu(a) * u).astype(BF16), wd_b[...]))


def _experts_call(tile_expert, n_used, xs, wg, wu, wd, tmx):
    n_rows, w = xs.shape
    tile = lambda i, te, nu: jnp.minimum(i, nu[0] - 1)
    wspec = lambda shp: pl.BlockSpec((1,) + shp, lambda i, te, nu: (te[tile(i, te, nu)], 0, 0))
    return pl.pallas_call(
        _experts_kernel,
        out_shape=jax.ShapeDtypeStruct((n_rows, w), xs.dtype),
        grid_spec=pltpu.PrefetchScalarGridSpec(
            num_scalar_prefetch=2,
            grid=(n_rows // tmx,),
            in_specs=[pl.BlockSpec((tmx, w), lambda i, te, nu: (tile(i, te, nu), 0)),
                      wspec((D_MODEL, EXPERT_DIM)), wspec((D_MODEL, EXPERT_DIM)), wspec((EXPERT_DIM, D_MODEL))],
            out_specs=pl.BlockSpec((tmx, w), lambda i, te, nu: (tile(i, te, nu), 0)),
            scratch_shapes=[pltpu.VMEM((D_MODEL, EXPERT_DIM), BF16), pltpu.VMEM((D_MODEL, EXPERT_DIM), BF16),
                            pltpu.VMEM((EXPERT_DIM, D_MODEL), BF16)],
        ),
        compiler_params=_cparams(("arbitrary",)),
        name="moe_experts",
    )(tile_expert, n_used, xs, wg, wu, wd)


def _combine_kernel(wt_ref, h_ref, yg_ref, sg_ref, su_ref, sd_ref, g3_ref, b3_ref, o_ref):
    h = h_ref[...]
    hb = h.astype(BF16)
    acc = _dot((jax.nn.silu(_dot(hb, sg_ref[...])) * _dot(hb, su_ref[...])).astype(BF16), sd_ref[...])
    wt = wt_ref[...]
    lo_sum = None
    hi_sum = None
    for k in range(TOP_K):
        lo, hi = _unpack_bf16_pairs(yg_ref[k])
        wk = wt[:, k:k + 1]
        lo_sum = wk * lo if lo_sum is None else lo_sum + wk * lo
        hi_sum = wk * hi if hi_sum is None else hi_sum + wk * hi
    acc = acc + jnp.concatenate([lo_sum, hi_sum], axis=1)
    o_ref[...] = _layer_norm(DEEPNORM_ALPHA * h + acc, g3_ref[...], b3_ref[...])


def _combine_call(wt, h2, yg, sg, su, sd, g3, b3, tm):
    t = h2.shape[0]
    full = lambda a: pl.BlockSpec(a.shape, lambda i: (0, 0))
    return pl.pallas_call(
        _combine_kernel,
        out_shape=jax.ShapeDtypeStruct((t, D_MODEL), F32),
        grid=(t // tm,),
        in_specs=[pl.BlockSpec((tm, V7X_LANES), lambda i: (i, 0)),
                  pl.BlockSpec((tm, D_MODEL), lambda i: (i, 0)),
                  pl.BlockSpec((TOP_K, tm, yg.shape[2]), lambda i: (0, i, 0)),
                  full(sg), full(su), full(sd), full(g3), full(b3)],
        out_specs=pl.BlockSpec((tm, D_MODEL), lambda i: (i, 0)),
        compiler_params=_cparams(("parallel",)),
        name="moe_combine",
    )(wt, h2, yg, sg, su, sd, g3, b3)


def _moe(h2, hp, router_w, router_bias, wg, wu, wd, sg, su, sd, g3, b3, tm_route, tmx, tm_comb):
    t = h2.shape[0]
    eid, rnk, wt, cnt = _router_call(h2, router_w.T, router_bias.reshape(-1, 1), tm_route)
    counts = cnt[:, 0]
    tiles_per_e = (counts + (tmx - 1)) // tmx
    tile_end = jnp.cumsum(tiles_per_e)
    row_start = ((tile_end - tiles_per_e) * tmx).astype(jnp.int32)
    n_tiles = (t * TOP_K) // tmx + N_EXPERTS
    n_used = tile_end[-1:].astype(jnp.int32)
    tile_expert = jnp.minimum(
        jnp.sum(tile_end[None, :] <= jnp.arange(n_tiles, dtype=jnp.int32)[:, None], axis=1), N_EXPERTS - 1
    ).astype(jnp.int32)
    expert_ids = jnp.arange(N_EXPERTS, dtype=jnp.int32)[:, None, None]
    dest = rnk + jnp.sum(jnp.where(eid[None] == expert_ids, row_start[:, None, None], 0), axis=0)
    dest_chunks = dest.reshape(TOP_K, t // SC_INDEX_WINDOW, SC_INDEX_WINDOW).transpose(1, 0, 2)
    xs = _sc_scatter_rows(hp, dest_chunks, n_tiles * tmx)
    yp = _experts_call(tile_expert, n_used, xs, wg, wu, wd, tmx)
    yg = _sc_gather_rows(yp, dest_chunks)
    return _combine_call(wt, h2, yg, sg, su, sd, g3, b3, tm_comb)


def _prep_w_in_t(w_in):
    pad = jnp.zeros((D_MODEL, V7X_LANES - MLA_ROPE_DIM), w_in.dtype)
    return jnp.concatenate([w_in, pad], axis=1).T.astype(BF16)


def _prep_w_uq_t(w_uq):
    w = w_uq.reshape(MLA_Q_RANK, MLA_HEADS, MLA_NOPE_DIM + MLA_ROPE_DIM)
    pad = jnp.zeros((MLA_Q_RANK, MLA_HEADS, MLA_QK_PAD - MLA_NOPE_DIM - MLA_ROPE_DIM), w.dtype)
    return jnp.concatenate([w, pad], axis=2).reshape(MLA_Q_RANK, MLA_HEADS * MLA_QK_PAD).T.astype(BF16)


def _prep_w_ukv_t(w_ukv):
    w = w_ukv.reshape(MLA_KV_RANK, MLA_HEADS, MLA_NOPE_DIM + MLA_V_DIM)
    k = w[:, :, :MLA_NOPE_DIM].reshape(MLA_KV_RANK, MLA_HEADS * MLA_NOPE_DIM)
    v = w[:, :, MLA_NOPE_DIM:].reshape(MLA_KV_RANK, MLA_HEADS * MLA_V_DIM)
    return jnp.concatenate([k, v], axis=1).T.astype(BF16)


def _rope_inv_freq(dim):
    return (ROPE_THETA ** (-jnp.arange(0, dim, 2, dtype=F32) / dim)).reshape(-1, 1)


def _layer(x2, mem2, pos_row, b, s, l, w_in, lam_q1, lam_k1, lam_q2, lam_k2, diff_subln_g, mla_q_norm_g,
           mla_w_uq, mla_kv_norm_g, mla_w_ukv, mla_out_norm_g, w_o, ln1_g, ln1_b, mem_w_q, mem_w_k, mem_w_v,
           mem_w_o, ln2_g, ln2_b, router_w, router_bias, exp_w_gate, exp_w_up, exp_w_down, sh_w_gate,
           sh_w_up, sh_w_down, ln3_g, ln3_b):
    row = lambda a: a.reshape(1, -1)
    col = lambda a: a.reshape(-1, 1)
    dqt, dk, dvt, mqt, mk, mvt = _proj_call(
        x2, pos_row, _rope_inv_freq(DIFF_ROT_DIM), _rope_inv_freq(MLA_ROPE_DIM), _prep_w_in_t(w_in),
        col(mla_q_norm_g), _prep_w_uq_t(mla_w_uq), col(mla_kv_norm_g), _prep_w_ukv_t(mla_w_ukv), tm=min(512, s))

    lam_init = 0.8 - 0.6 * math.exp(-0.3 * l)
    lam_vecs = jnp.stack([lam_q1, lam_k1, lam_q2, lam_k2]).astype(F32)
    tq = min(512, s)
    diff_o = _attn_call(
        functools.partial(_diff_attn_kernel, tq=tq, lam_init=lam_init), "diff_attn",
        dqt, dk, dvt, [lam_vecs, col(diff_subln_g)], b, s, DIFF_HEADS, 2 * DIFF_QK_DIM, DIFF_V_DIM, tq, 2)
    mla_o = _attn_call(
        functools.partial(_mla_attn_kernel, tq=tq), "mla_attn",
        mqt, mk, mvt, [col(mla_out_norm_g)], b, s, MLA_HEADS, MLA_QK_PAD, MLA_V_DIM, tq, 1)

    kmem, vmem = _memkv_call(mem2, mem_w_k.astype(BF16), mem_w_v.astype(BF16))
    h2, hp = _post_call(diff_o, mla_o, x2, w_o.astype(BF16), row(ln1_g), row(ln1_b), mem_w_q.astype(BF16),
                        kmem, vmem, mem_w_o.astype(BF16), row(ln2_g), row(ln2_b), s, tm=min(512, s))

    return _moe(h2, hp, router_w, router_bias, exp_w_gate, exp_w_up, exp_w_down,
                sh_w_gate.astype(BF16), sh_w_up.astype(BF16), sh_w_down.astype(BF16),
                row(ln3_g), row(ln3_b), tm_route=min(512, s), tmx=1024, tm_comb=min(256, s))


def kernel(x, mem, positions, w_in, lam_q1, lam_k1, lam_q2, lam_k2, diff_subln_g, mla_q_norm_g, mla_w_uq,
           mla_kv_norm_g, mla_w_ukv, mla_out_norm_g, w_o, ln1_g, ln1_b, mem_w_q, mem_w_k, mem_w_v, mem_w_o,
           ln2_g, ln2_b, router_w, router_bias, exp_w_gate, exp_w_up, exp_w_down, sh_w_gate, sh_w_up,
           sh_w_down, ln3_g, ln3_b):
    b, s, d = x.shape
    h = x.reshape(b * s, d)
    mem2 = mem.reshape(b * mem.shape[1], d)
    pos_row = positions.reshape(1, b * s)
    params = (w_in, lam_q1, lam_k1, lam_q2, lam_k2, diff_subln_g, mla_q_norm_g, mla_w_uq, mla_kv_norm_g,
              mla_w_ukv, mla_out_norm_g, w_o, ln1_g, ln1_b, mem_w_q, mem_w_k, mem_w_v, mem_w_o, ln2_g, ln2_b,
              router_w, router_bias, exp_w_gate, exp_w_up, exp_w_down, sh_w_gate, sh_w_up, sh_w_down,
              ln3_g, ln3_b)
    for l in range(w_in.shape[0]):
        h = _layer(h, mem2, pos_row, b, s, l, *[p[l] for p in params])
    return h.reshape(b, s, d)
```

```python
import functools
import math

import jax
import jax.numpy as jnp
from jax import lax
from jax.experimental import pallas as pl
from jax.experimental.pallas import tpu as pltpu
from jax.experimental.pallas import tpu_sc as plsc

F32 = jnp.float32
BF16 = jnp.bfloat16

D_MODEL = 1024
N_MEM = 256
ROPE_THETA = 500000.0
NEG_INF = -1e30
LN_EPS = 1e-5
RMS_EPS = 1e-6
DIFF_HEADS = 4
DIFF_QK_DIM = 64
DIFF_V_DIM = 128
DIFF_ROT_DIM = DIFF_QK_DIM // 4
MLA_HEADS = 4
MLA_Q_RANK = 256
MLA_KV_RANK = 128
MLA_NOPE_DIM = 128
MLA_ROPE_DIM = 64
MLA_V_DIM = 128
MLA_QK_PAD = 256
V_AUG_DIM = 128 + 16
MEM_HEADS = 4
MEM_HEAD_DIM = 128
N_EXPERTS = 64
TOP_K = 8
N_GROUPS = 8
TOPK_GROUPS = 4
GROUP_SIZE = N_EXPERTS // N_GROUPS
EXPERT_DIM = 256
SHARED_DIM = 256
ROUTED_SCALE = 2.5
DEPTH = 1
DEEPNORM_ALPHA = (2.0 * DEPTH) ** 0.25
LOG2E = math.log2(math.e)

V7X_LANES = 128
V7X_SUBLANES = 8
V7X_VMEM_LIMIT_BYTES = 56 * 1024 * 1024
V7X_SC_CORES = 2
V7X_SC_SUBCORES = 16
SC_INDEX_WINDOW = 128
V7X_BF16_SUBLANES = 16

N_DQ = DIFF_HEADS * 2 * DIFF_QK_DIM
N_DV = DIFF_HEADS * DIFF_V_DIM
O_DK = N_DQ
O_DV = 2 * N_DQ
O_CQ = O_DV + N_DV
O_CKV = O_CQ + MLA_Q_RANK
O_KR = O_CKV + MLA_KV_RANK
N_IN_PAD = O_KR + V7X_LANES


def _cparams(sem):
    return pltpu.CompilerParams(dimension_semantics=sem, vmem_limit_bytes=V7X_VMEM_LIMIT_BYTES)


def _dot(a, b):
    return jnp.dot(a, b, preferred_element_type=F32)


def _dot_nt(a, b):
    return lax.dot_general(a, b, (((1,), (1,)), ((), ())), preferred_element_type=F32)


def _rms(x, g):
    return x * lax.rsqrt(jnp.mean(jnp.square(x), axis=-1, keepdims=True) + RMS_EPS) * g


def _rms_rows(x, g):
    return x * lax.rsqrt(jnp.mean(jnp.square(x), axis=0, keepdims=True) + RMS_EPS) * g


def _layer_norm(x, g, b):
    mu = jnp.mean(x, axis=-1, keepdims=True)
    xc = x - mu
    var = jnp.mean(jnp.square(xc), axis=-1, keepdims=True)
    return xc * lax.rsqrt(var + LN_EPS) * g + b


def _with_ones_rows(vt, heads):
    dv = vt.shape[0] // heads
    ones = jnp.ones((V_AUG_DIM - dv, vt.shape[1]), vt.dtype)
    return jnp.concatenate([p for h in range(heads) for p in (vt[h * dv:(h + 1) * dv], ones)], axis=0)


def _rope_rows(x, cos, sin, period):
    h = cos.shape[0]
    pieces = []
    for base in range(0, x.shape[0], period):
        x1 = x[base:base + h]
        x2 = x[base + h:base + 2 * h]
        pieces.append(x1 * cos - x2 * sin)
        pieces.append(x2 * cos + x1 * sin)
        if period > 2 * h:
            pieces.append(x[base + 2 * h:base + period])
    return jnp.concatenate(pieces, axis=0)


def _proj_kernel(x_ref, pos_ref, fd_ref, fm_ref, w_in_ref, gq_ref, wuq_ref, gkv_ref, wukv_ref,
                 dq_ref, dk_ref, dv_ref, mq_ref, mk_ref, mv_ref):
    xb = x_ref[...].astype(BF16)
    pos = pos_ref[...].astype(F32)
    ang_d = fd_ref[...] * pos
    ang_m = fm_ref[...] * pos
    cos_d, sin_d = jnp.cos(ang_d), jnp.sin(ang_d)
    cos_m, sin_m = jnp.cos(ang_m), jnp.sin(ang_m)

    dq = _rope_rows(_dot_nt(w_in_ref[0:N_DQ, :], xb), cos_d, sin_d, DIFF_QK_DIM)
    dq_ref[...] = (dq * (DIFF_QK_DIM ** -0.5 * LOG2E)).astype(BF16)
    dk = _rope_rows(_dot_nt(w_in_ref[O_DK:O_DK + N_DQ, :], xb), cos_d, sin_d, DIFF_QK_DIM)
    dk_ref[...] = dk.T.astype(BF16)
    dv_ref[...] = _with_ones_rows(_dot_nt(w_in_ref[O_DV:O_DV + N_DV, :], xb), DIFF_HEADS).astype(BF16)

    c_q = _dot_nt(w_in_ref[O_CQ:O_CQ + MLA_Q_RANK, :], xb)
    q = _dot(wuq_ref[...], _rms_rows(c_q, gq_ref[...]).astype(BF16))
    q = q * ((MLA_NOPE_DIM + MLA_ROPE_DIM) ** -0.5 * LOG2E)
    pieces = []
    for h in range(MLA_HEADS):
        o = h * MLA_QK_PAD
        pieces.append(q[o:o + MLA_NOPE_DIM])
        pieces.append(_rope_rows(q[o + MLA_NOPE_DIM:o + MLA_NOPE_DIM + MLA_ROPE_DIM], cos_m, sin_m, MLA_ROPE_DIM))
        pieces.append(q[o + MLA_NOPE_DIM + MLA_ROPE_DIM:o + MLA_QK_PAD])
    mq_ref[...] = jnp.concatenate(pieces, axis=0).astype(BF16)

    c_kv = _dot_nt(w_in_ref[O_CKV:O_CKV + MLA_KV_RANK, :], xb)
    kv = _dot(wukv_ref[...], _rms_rows(c_kv, gkv_ref[...]).astype(BF16))
    n_kn = MLA_HEADS * MLA_NOPE_DIM
    mv_ref[...] = _with_ones_rows(kv[n_kn:], MLA_HEADS).astype(BF16)
    k_rope = _dot_nt(w_in_ref[O_KR:O_KR + V7X_LANES, :], xb)
    k_pe = jnp.concatenate([_rope_rows(k_rope[:MLA_ROPE_DIM], cos_m, sin_m, MLA_ROPE_DIM),
                            k_rope[MLA_ROPE_DIM:]], axis=0)
    k_nope_t = kv[:n_kn].T.astype(BF16)
    k_pe_t = k_pe.T.astype(BF16)
    for h in range(MLA_HEADS):
        o = h * MLA_QK_PAD
        mk_ref[:, o:o + MLA_NOPE_DIM] = k_nope_t[:, h * MLA_NOPE_DIM:(h + 1) * MLA_NOPE_DIM]
        mk_ref[:, o + MLA_NOPE_DIM:o + MLA_QK_PAD] = k_pe_t


def _proj_call(x2, pos_row, fd, fm, w_in_t, gq, wuq_t, gkv, wukv_t, tm):
    t = x2.shape[0]
    full = lambda a: pl.BlockSpec(a.shape, lambda i: (0, 0))
    col_blk = lambda r: pl.BlockSpec((r, tm), lambda i: (0, i))
    row_blk = lambda w: pl.BlockSpec((tm, w), lambda i: (i, 0))
    n_mq = MLA_HEADS * MLA_QK_PAD
    return pl.pallas_call(
        _proj_kernel,
        out_shape=[jax.ShapeDtypeStruct((N_DQ, t), BF16), jax.ShapeDtypeStruct((t, N_DQ), BF16),
                   jax.ShapeDtypeStruct((DIFF_HEADS * V_AUG_DIM, t), BF16), jax.ShapeDtypeStruct((n_mq, t), BF16),
                   jax.ShapeDtypeStruct((t, n_mq), BF16), jax.ShapeDtypeStruct((MLA_HEADS * V_AUG_DIM, t), BF16)],
        grid=(t // tm,),
        in_specs=[row_blk(D_MODEL), col_blk(1), full(fd), full(fm), full(w_in_t), full(gq), full(wuq_t),
                  full(gkv), full(wukv_t)],
        out_specs=[col_blk(N_DQ), row_blk(N_DQ), col_blk(DIFF_HEADS * V_AUG_DIM), col_blk(n_mq), row_blk(n_mq),
                   col_blk(MLA_HEADS * V_AUG_DIM)],
        compiler_params=_cparams(("parallel",)),
        name="proj_rope",
    )(x2, pos_row, fd, fm, w_in_t, gq, wuq_t, gkv, wukv_t)


def _flash_t(qq, k_ref, vt_ref, s_refs, p_refs, acc_ref, i, tq):
    n = qq.shape[1]
    tk = s_refs[0].shape[0]
    assert tq == 2 * tk
    q_pos = i * tq + (lax.broadcasted_iota(jnp.int32, (tk, n), 1) & (tq - 1))
    key_row = lax.broadcasted_iota(jnp.int32, (tk, n), 0)

    def scores(c, s_ref):
        off = pl.multiple_of(c * tk, tk)
        s = _dot(k_ref[pl.ds(off, tk), :], qq)
        s_ref[...] = s
        return jnp.max(s, axis=0, keepdims=True)

    def softmax(c, s_ref, p_ref, cmax, m, masked):
        s = s_ref[...]
        if masked:
            s = jnp.where(key_row + c * tk <= q_pos, s, NEG_INF)
            cmax = jnp.max(s, axis=0, keepdims=True)
        m_new = jnp.maximum(m, cmax)
        alpha = jnp.exp2(m - m_new)
        p_ref[...] = jnp.exp2((s - m_new).astype(BF16))
        return m_new, alpha

    def values(c, p_ref, alpha):
        off = pl.multiple_of(jnp.maximum(c, 0) * tk, tk)
        acc_ref[...] = alpha * acc_ref[...] + _dot(vt_ref[:, pl.ds(off, tk)], p_ref[...])

    def pair(u, carry, masked_second):
        cm1, alpha0, m = carry
        c = 2 * u
        cm0 = scores(c, s_refs[0])
        values(c - 2, p_refs[0], alpha0)
        m, alpha1 = softmax(c - 1, s_refs[1], p_refs[1], cm1, m, False)
        cm1 = scores(c + 1, s_refs[1])
        values(c - 1, p_refs[1], alpha1)
        m, alpha0 = softmax(c, s_refs[0], p_refs[0], cm0, m, masked_second)
        return cm1, alpha0, m

    s_refs[1][...] = jnp.full(s_refs[1].shape, 2 * NEG_INF, F32)
    p_refs[0][...] = jnp.zeros(p_refs[0].shape, BF16)
    acc_ref[...] = jnp.zeros(acc_ref.shape, F32)
    init = (jnp.full((1, n), 2 * NEG_INF, F32), jnp.ones((1, n), F32), jnp.full((1, n), NEG_INF, F32))
    carry = lax.fori_loop(0, i, functools.partial(pair, masked_second=False), init)
    cm1, alpha0, m = pair(i, carry, True)
    values(2 * i, p_refs[0], alpha0)
    m, alpha1 = softmax(2 * i + 1, s_refs[1], p_refs[1], cm1, m, True)
    values(2 * i + 1, p_refs[1], alpha1)
    dv = vt_ref.shape[0] - V7X_BF16_SUBLANES
    return acc_ref[0:dv, :] / acc_ref[dv:dv + 1, :]


def _diff_attn_kernel(qt_ref, k_ref, vt_ref, lam_ref, g_ref, o_ref, s0, s1, p0, p1, acc_ref, *, tq, lam_init):
    i = pl.program_id(2)
    qt = qt_ref[...]
    row = lax.broadcasted_iota(jnp.int32, qt.shape, 0)
    zero = jnp.zeros_like(qt)
    qq = jnp.concatenate([jnp.where(row < DIFF_QK_DIM, qt, zero), jnp.where(row >= DIFF_QK_DIM, qt, zero)], axis=1)
    o = _flash_t(qq, k_ref, vt_ref, (s0, s1), (p0, p1), acc_ref, i, tq)
    lv = lam_ref[...]
    lam = (jnp.exp(jnp.sum(lv[0:1] * lv[1:2], axis=1, keepdims=True))
           - jnp.exp(jnp.sum(lv[2:3] * lv[3:4], axis=1, keepdims=True)) + lam_init)
    d = o[:, :tq] - lam * o[:, tq:]
    o_ref[...] = (_rms_rows(d, g_ref[...]) * (1.0 - lam_init)).T.astype(o_ref.dtype)


def _mla_attn_kernel(qt_ref, k_ref, vt_ref, g_ref, o_ref, s0, s1, p0, p1, acc_ref, *, tq):
    i = pl.program_id(2)
    o = _flash_t(qt_ref[...], k_ref, vt_ref, (s0, s1), (p0, p1), acc_ref, i, tq)
    o_ref[...] = _rms_rows(o, g_ref[...]).T.astype(o_ref.dtype)


def _attn_call(kernel, name, qt, k, vt, extras, b, s, heads, dk_w, dv_w, tq, n_maps):
    nq = s // tq
    t = b * s
    tk = tq // 2
    n = n_maps * tq
    small = [pl.BlockSpec(a.shape, lambda bi, h, i: (0, 0)) for a in extras]
    return pl.pallas_call(
        kernel,
        out_shape=jax.ShapeDtypeStruct((t, heads * dv_w), BF16),
        grid=(b, heads, nq),
        in_specs=[pl.BlockSpec((dk_w, tq), lambda bi, h, i: (h, bi * nq + i)),
                  pl.BlockSpec((s, dk_w), lambda bi, h, i: (bi, h)),
                  pl.BlockSpec((V_AUG_DIM, s), lambda bi, h, i: (h, bi))] + small,
        out_specs=pl.BlockSpec((tq, dv_w), lambda bi, h, i: (bi * nq + i, h)),
        scratch_shapes=[pltpu.VMEM((tk, n), F32), pltpu.VMEM((tk, n), F32),
                        pltpu.VMEM((tk, n), BF16), pltpu.VMEM((tk, n), BF16), pltpu.VMEM((V_AUG_DIM, n), F32)],
        compiler_params=_cparams(("parallel", "parallel", "arbitrary")),
        name=name,
    )(qt, k, vt, *extras)


def _memkv_kernel(mem_ref, wk_ref, wv_ref, k_ref, v_ref):
    mb = mem_ref[...].astype(BF16)
    k_ref[...] = _dot(mb, wk_ref[...]).astype(BF16)
    v_ref[...] = _dot(mb, wv_ref[...]).astype(BF16)


def _memkv_call(mem2, wk, wv):
    n = mem2.shape[0]
    w = wk.shape[1]
    tm = N_MEM
    return pl.pallas_call(
        _memkv_kernel,
        out_shape=[jax.ShapeDtypeStruct((n, w), BF16)] * 2,
        grid=(n // tm,),
        in_specs=[pl.BlockSpec((tm, D_MODEL), lambda i: (i, 0)),
                  pl.BlockSpec(wk.shape, lambda i: (0, 0)),
                  pl.BlockSpec(wv.shape, lambda i: (0, 0))],
        out_specs=[pl.BlockSpec((tm, w), lambda i: (i, 0))] * 2,
        compiler_params=_cparams(("parallel",)),
        name="mem_kv",
    )(mem2, wk, wv)


def _pack_bf16_pairs(x):
    half = x.shape[1] // 2
    r = pltpu.bitcast(x.astype(BF16).astype(F32), jnp.uint32)
    return (r[:, :half] >> 16) | (r[:, half:] & jnp.uint32(0xFFFF0000))


def _unpack_bf16_pairs(w):
    lo = pltpu.bitcast(w << 16, F32).astype(BF16)
    hi = pltpu.bitcast(w & jnp.uint32(0xFFFF0000), F32).astype(BF16)
    return lo, hi


def _post_kernel(da_ref, ma_ref, x_ref, wo_ref, g1_ref, b1_ref, wq_ref, km_ref, vm_ref, wmo_ref,
                 g2_ref, b2_ref, h_ref, hp_ref):
    n_d = DIFF_HEADS * DIFF_V_DIM
    mix = _dot(da_ref[...], wo_ref[0:n_d, :]) + _dot(ma_ref[...], wo_ref[n_d:, :])
    h1 = _layer_norm(DEEPNORM_ALPHA * x_ref[...] + mix, g1_ref[...], b1_ref[...])
    q = (_dot(h1.astype(BF16), wq_ref[...]) * (MEM_HEAD_DIM ** -0.5)).astype(BF16)
    outs = []
    for h in range(MEM_HEADS):
        sl = slice(h * MEM_HEAD_DIM, (h + 1) * MEM_HEAD_DIM)
        s = _dot_nt(q[:, sl], km_ref[:, sl])
        m = jnp.max(s, axis=1, keepdims=True)
        p = jnp.exp(s - m)
        p = p / jnp.sum(p, axis=1, keepdims=True)
        outs.append(_dot(p.astype(BF16), vm_ref[:, sl]))
    o = jnp.concatenate(outs, axis=1).astype(BF16)
    xat = _dot(o, wmo_ref[...])
    h2 = _layer_norm(DEEPNORM_ALPHA * h1 + xat, g2_ref[...], b2_ref[...])
    h_ref[...] = h2
    hp_ref[...] = _pack_bf16_pairs(h2)


def _post_call(diff_o, mla_o, x2, wo, g1, b1, wq, kmem, vmem, wmo, g2, b2, s, tm):
    t = x2.shape[0]
    per_b = s // tm
    row = lambda w: pl.BlockSpec((tm, w), lambda i: (i, 0))
    full = lambda a: pl.BlockSpec(a.shape, lambda i: (0, 0))
    memb = pl.BlockSpec((N_MEM, kmem.shape[1]), lambda i: (i // per_b, 0))
    return pl.pallas_call(
        _post_kernel,
        out_shape=[jax.ShapeDtypeStruct((t, D_MODEL), F32), jax.ShapeDtypeStruct((t, D_MODEL // 2), jnp.uint32)],
        grid=(t // tm,),
        in_specs=[row(diff_o.shape[1]), row(mla_o.shape[1]), row(D_MODEL), full(wo), full(g1), full(b1),
                  full(wq), memb, memb, full(wmo), full(g2), full(b2)],
        out_specs=[row(D_MODEL), row(D_MODEL // 2)],
        compiler_params=_cparams(("parallel",)),
        name="post_mix_xattn",
    )(diff_o, mla_o, x2, wo, g1, b1, wq, kmem, vmem, wmo, g2, b2)


def _router_kernel(h_ref, rwt_ref, bias_ref, tri_ref, eid_ref, rnk_ref, wt_ref, cnt_ref):
    tm = h_ref.shape[0]

    @pl.when(pl.program_id(0) == 0)
    def _():
        cnt_ref[...] = jnp.zeros(cnt_ref.shape, cnt_ref.dtype)

    logits = lax.dot_general(rwt_ref[...], h_ref[...], (((1,), (1,)), ((), ())),
                             precision=lax.Precision.HIGHEST, preferred_element_type=F32)
    scores = jax.nn.sigmoid(logits)
    choice = scores + bias_ref[...]
    neg = float("-inf")

    c3 = choice.reshape(N_GROUPS, GROUP_SIZE, tm)
    mi = lax.broadcasted_iota(jnp.int32, c3.shape, 1)
    top1 = jnp.max(c3, axis=1, keepdims=True)
    first = jnp.min(jnp.where(c3 == top1, mi, GROUP_SIZE), axis=1, keepdims=True)
    top2 = jnp.max(jnp.where(mi == first, neg, c3), axis=1, keepdims=True)
    gs = (top1 + top2).reshape(N_GROUPS, tm)

    gi = lax.broadcasted_iota(jnp.int32, gs.shape, 0)
    rank = jnp.zeros(gs.shape, jnp.int32)
    for g in range(N_GROUPS):
        rowg = gs[g:g + 1, :]
        beats = (rowg > gs) | ((rowg == gs) & (g < gi))
        rank = rank + beats.astype(jnp.int32)
    gmask = rank < TOPK_GROUPS
    emask = jnp.broadcast_to(gmask.reshape(N_GROUPS, 1, tm), c3.shape).reshape(N_EXPERTS, tm)

    work = jnp.where(emask, choice, NEG_INF)
    ei = lax.broadcasted_iota(jnp.int32, work.shape, 0)
    sel = jnp.zeros(work.shape, jnp.bool_)
    picks = []
    for _ in range(TOP_K):
        mx = jnp.max(work, axis=0, keepdims=True)
        fi = jnp.min(jnp.where(work == mx, ei, N_EXPERTS), axis=0, keepdims=True)
        pick = ei == fi
        picks.append((fi, pick))
        sel = sel | pick
        work = jnp.where(pick, neg, work)
    top_w = jnp.where(sel, scores, 0.0)
    gates = top_w / jnp.sum(top_w, axis=0, keepdims=True) * ROUTED_SCALE

    sel_b = jnp.where(sel, 1.0, 0.0).astype(BF16)
    before = _dot(sel_b, tri_ref[...]).astype(jnp.int32)
    run = cnt_ref[:, 0:1]
    pos = run + before
    cnt_ref[...] = jnp.broadcast_to(run + jnp.sum(sel.astype(jnp.int32), axis=1, keepdims=True), cnt_ref.shape)

    eid_ref[...] = jnp.concatenate([fi for fi, _ in picks], axis=0)
    rnk_ref[...] = jnp.concatenate(
        [jnp.sum(jnp.where(pick, pos, 0), axis=0, keepdims=True) for _, pick in picks], axis=0)
    w_rows = [jnp.sum(jnp.where(pick, gates, 0.0), axis=0, keepdims=True) for _, pick in picks]
    w_pad = jnp.concatenate(w_rows + [jnp.zeros((V7X_LANES - TOP_K, tm), F32)], axis=0)
    wt_ref[...] = w_pad.T


def _router_call(h2, rwt, bias, tm):
    t = h2.shape[0]
    tri = jnp.triu(jnp.ones((tm, tm), BF16), k=1)
    full = lambda a: pl.BlockSpec(a.shape, lambda i: (0, 0))
    return pl.pallas_call(
        _router_kernel,
        out_shape=[jax.ShapeDtypeStruct((TOP_K, t), jnp.int32), jax.ShapeDtypeStruct((TOP_K, t), jnp.int32),
                   jax.ShapeDtypeStruct((t, V7X_LANES), F32), jax.ShapeDtypeStruct((N_EXPERTS, V7X_LANES), jnp.int32)],
        grid=(t // tm,),
        in_specs=[pl.BlockSpec((tm, D_MODEL), lambda i: (i, 0)), full(rwt), full(bias), full(tri)],
        out_specs=[pl.BlockSpec((TOP_K, tm), lambda i: (0, i)), pl.BlockSpec((TOP_K, tm), lambda i: (0, i)),
                   pl.BlockSpec((tm, V7X_LANES), lambda i: (i, 0)),
                   pl.BlockSpec((N_EXPERTS, V7X_LANES), lambda i: (0, 0))],
        compiler_params=_cparams(("arbitrary",)),
        name="router",
    )(h2, rwt, bias, tri)


_SC_MESH_AXES = ("core", "subcore")


def _sc_mesh():
    return plsc.VectorSubcoreMesh(core_axis_name=_SC_MESH_AXES[0], subcore_axis_name=_SC_MESH_AXES[1])


def _sc_worker_chunks(n_chunks):
    n_workers = V7X_SC_CORES * V7X_SC_SUBCORES
    assert n_chunks % n_workers == 0
    per_worker = n_chunks // n_workers
    first = (lax.axis_index(_SC_MESH_AXES[1]) * V7X_SC_CORES + lax.axis_index(_SC_MESH_AXES[0])) * per_worker
    return first, per_worker


def _sc_scatter_rows(x, idx, n_rows):
    t, w = x.shape
    n_chunks, n_k, window = idx.shape

    @functools.partial(pl.kernel, out_type=jax.ShapeDtypeStruct((n_rows, w), x.dtype), mesh=_sc_mesh(),
                       scratch_types=[pltpu.VMEM((n_k, window), jnp.int32), pltpu.VMEM((window, w), x.dtype)],
                       name="moe_dispatch_sc")
    def scatter(x_hbm, i_hbm, o_hbm, idx_v, rows_v):
        first, per_worker = _sc_worker_chunks(n_chunks)

        @pl.loop(0, per_worker)
        def _(c):
            chunk = first + c
            pltpu.sync_copy(i_hbm.at[chunk], idx_v)
            pltpu.sync_copy(x_hbm.at[pl.ds(chunk * window, window)], rows_v)
            for k in range(n_k):
                pltpu.sync_copy(rows_v, o_hbm.at[idx_v.at[k]])

    return scatter(x, idx)


def _sc_gather_rows(y, idx):
    n_chunks, n_k, window = idx.shape
    w = y.shape[1]

    half = window // 2
    steps = [(k, h) for k in range(n_k) for h in range(2)]

    @functools.partial(pl.kernel, out_type=jax.ShapeDtypeStruct((n_k, n_chunks * window, w), y.dtype),
                       mesh=_sc_mesh(),
                       scratch_types=[pltpu.VMEM((n_k, window), jnp.int32), pltpu.VMEM((half, w), y.dtype),
                                      pltpu.VMEM((half, w), y.dtype), pltpu.SemaphoreType.DMA,
                                      pltpu.SemaphoreType.DMA],
                       name="moe_combine_sc")
    def gather(y_hbm, i_hbm, o_hbm, idx_v, rows_a, rows_b, sem_a, sem_b):
        first, per_worker = _sc_worker_chunks(n_chunks)
        bufs = ((rows_a, sem_a), (rows_b, sem_b))

        @pl.loop(0, per_worker)
        def _(c):
            chunk = first + c
            pltpu.sync_copy(i_hbm.at[chunk], idx_v)

            def start(j):
                k, h = steps[j]
                buf, sem = bufs[j % 2]
                return pltpu.async_copy(y_hbm.at[idx_v.at[k, pl.ds(h * half, half)]], buf, sem)

            pending = start(0)
            for j, (k, h) in enumerate(steps):
                following = start(j + 1) if j + 1 < len(steps) else None
                pending.wait()
                pltpu.sync_copy(bufs[j % 2][0], o_hbm.at[k, pl.ds(chunk * window + h * half, half)])
                pending = following

    return gather(y, idx)


def _experts_kernel(te_ref, nu_ref, xs_ref, wg_ref, wu_ref, wd_ref, y_ref, wg_b, wu_b, wd_b):
    i = pl.program_id(0)

    @pl.when(i < nu_ref[0])
    def _():
        @pl.when((i == 0) | (te_ref[i] != te_ref[jnp.maximum(i - 1, 0)]))
        def _():
            wg_b[...] = wg_ref[0].astype(BF16)
            wu_b[...] = wu_ref[0].astype(BF16)
            wd_b[...] = wd_ref[0].astype(BF16)

        half = D_MODEL // 2
        lo, hi = _unpack_bf16_pairs(xs_ref[...])
        a = _dot(lo, wg_b[:half, :]) + _dot(hi, wg_b[half:, :])
        u = _dot(lo, wu_b[:half, :]) + _dot(hi, wu_b[half:, :])
        y_ref[...] = _pack_bf16_pairs(_dot((jax.nn.silu(a) * u).astype(BF16), wd_b[...]))


def _experts_call(tile_expert, n_used, xs, wg, wu, wd, tmx):
    n_rows, w = xs.shape
    tile = lambda i, te, nu: jnp.minimum(i, nu[0] - 1)
    wspec = lambda shp: pl.BlockSpec((1,) + shp, lambda i, te, nu: (te[tile(i, te, nu)], 0, 0))
    return pl.pallas_call(
        _experts_kernel,
        out_shape=jax.ShapeDtypeStruct((n_rows, w), xs.dtype),
        grid_spec=pltpu.PrefetchScalarGridSpec(
            num_scalar_prefetch=2,
            grid=(n_rows // tmx,),
            in_specs=[pl.BlockSpec((tmx, w), lambda i, te, nu: (tile(i, te, nu), 0)),
                      wspec((D_MODEL, EXPERT_DIM)), wspec((D_MODEL, EXPERT_DIM)), wspec((EXPERT_DIM, D_MODEL))],
            out_specs=pl.BlockSpec((tmx, w), lambda i, te, nu: (tile(i, te, nu), 0)),
            scratch_shapes=[pltpu.VMEM((D_MODEL, EXPERT_DIM), BF16), pltpu.VMEM((D_MODEL, EXPERT_DIM), BF16),
                            pltpu.VMEM((EXPERT_DIM, D_MODEL), BF16)],
        ),
        compiler_params=_cparams(("arbitrary",)),
        name="moe_experts",
    )(tile_expert, n_used, xs, wg, wu, wd)


def _combine_kernel(wt_ref, h_ref, yg_ref, sg_ref, su_ref, sd_ref, g3_ref, b3_ref, o_ref):
    h = h_ref[...]
    hb = h.astype(BF16)
    acc = _dot((jax.nn.silu(_dot(hb, sg_ref[...])) * _dot(hb, su_ref[...])).astype(BF16), sd_ref[...])
    wt = wt_ref[...]
    lo_sum = None
    hi_sum = None
    for k in range(TOP_K):
        lo, hi = _unpack_bf16_pairs(yg_ref[k])
        wk = wt[:, k:k + 1]
        lo_sum = wk * lo if lo_sum is None else lo_sum + wk * lo
        hi_sum = wk * hi if hi_sum is None else hi_sum + wk * hi
    acc = acc + jnp.concatenate([lo_sum, hi_sum], axis=1)
    o_ref[...] = _layer_norm(DEEPNORM_ALPHA * h + acc, g3_ref[...], b3_ref[...])


def _combine_call(wt, h2, yg, sg, su, sd, g3, b3, tm):
    t = h2.shape[0]
    full = lambda a: pl.BlockSpec(a.shape, lambda i: (0, 0))
    return pl.pallas_call(
        _combine_kernel,
        out_shape=jax.ShapeDtypeStruct((t, D_MODEL), F32),
        grid=(t // tm,),
        in_specs=[pl.BlockSpec((tm, V7X_LANES), lambda i: (i, 0)),
                  pl.BlockSpec((tm, D_MODEL), lambda i: (i, 0)),
                  pl.BlockSpec((TOP_K, tm, yg.shape[2]), lambda i: (0, i, 0)),
                  full(sg), full(su), full(sd), full(g3), full(b3)],
        out_specs=pl.BlockSpec((tm, D_MODEL), lambda i: (i, 0)),
        compiler_params=_cparams(("parallel",)),
        name="moe_combine",
    )(wt, h2, yg, sg, su, sd, g3, b3)


def _moe(h2, hp, router_w, router_bias, wg, wu, wd, sg, su, sd, g3, b3, tm_route, tmx, tm_comb):
    t = h2.shape[0]
    eid, rnk, wt, cnt = _router_call(h2, router_w.T, router_bias.reshape(-1, 1), tm_route)
    counts = cnt[:, 0]
    tiles_per_e = (counts + (tmx - 1)) // tmx
    tile_end = jnp.cumsum(tiles_per_e)
    row_start = ((tile_end - tiles_per_e) * tmx).astype(jnp.int32)
    n_tiles = (t * TOP_K) // tmx + N_EXPERTS
    n_used = tile_end[-1:].astype(jnp.int32)
    tile_expert = jnp.minimum(
        jnp.sum(tile_end[None, :] <= jnp.arange(n_tiles, dtype=jnp.int32)[:, None], axis=1), N_EXPERTS - 1
    ).astype(jnp.int32)
    expert_ids = jnp.arange(N_EXPERTS, dtype=jnp.int32)[:, None, None]
    dest = rnk + jnp.sum(jnp.where(eid[None] == expert_ids, row_start[:, None, None], 0), axis=0)
    dest_chunks = dest.reshape(TOP_K, t // SC_INDEX_WINDOW, SC_INDEX_WINDOW).transpose(1, 0, 2)
    xs = _sc_scatter_rows(hp, dest_chunks, n_tiles * tmx)
    yp = _experts_call(tile_expert, n_used, xs, wg, wu, wd, tmx)
    yg = _sc_gather_rows(yp, dest_chunks)
    return _combine_call(wt, h2, yg, sg, su, sd, g3, b3, tm_comb)


def _prep_w_in_t(w_in):
    pad = jnp.zeros((D_MODEL, V7X_LANES - MLA_ROPE_DIM), w_in.dtype)
    return jnp.concatenate([w_in, pad], axis=1).T.astype(BF16)


def _prep_w_uq_t(w_uq):
    w = w_uq.reshape(MLA_Q_RANK, MLA_HEADS, MLA_NOPE_DIM + MLA_ROPE_DIM)
    pad = jnp.zeros((MLA_Q_RANK, MLA_HEADS, MLA_QK_PAD - MLA_NOPE_DIM - MLA_ROPE_DIM), w.dtype)
    return jnp.concatenate([w, pad], axis=2).reshape(MLA_Q_RANK, MLA_HEADS * MLA_QK_PAD).T.astype(BF16)


def _prep_w_ukv_t(w_ukv):
    w = w_ukv.reshape(MLA_KV_RANK, MLA_HEADS, MLA_NOPE_DIM + MLA_V_DIM)
    k = w[:, :, :MLA_NOPE_DIM].reshape(MLA_KV_RANK, MLA_HEADS * MLA_NOPE_DIM)
    v = w[:, :, MLA_NOPE_DIM:].reshape(MLA_KV_RANK, MLA_HEADS * MLA_V_DIM)
    return jnp.concatenate([k, v], axis=1).T.astype(BF16)


def _rope_inv_freq(dim):
    return (ROPE_THETA ** (-jnp.arange(0, dim, 2, dtype=F32) / dim)).reshape(-1, 1)


def _layer(x2, mem2, pos_row, b, s, l, w_in, lam_q1, lam_k1, lam_q2, lam_k2, diff_subln_g, mla_q_norm_g,
           mla_w_uq, mla_kv_norm_g, mla_w_ukv, mla_out_norm_g, w_o, ln1_g, ln1_b, mem_w_q, mem_w_k, mem_w_v,
           mem_w_o, ln2_g, ln2_b, router_w, router_bias, exp_w_gate, exp_w_up, exp_w_down, sh_w_gate,
           sh_w_up, sh_w_down, ln3_g, ln3_b):
    row = lambda a: a.reshape(1, -1)
    col = lambda a: a.reshape(-1, 1)
    dqt, dk, dvt, mqt, mk, mvt = _proj_call(
        x2, pos_row, _rope_inv_freq(DIFF_ROT_DIM), _rope_inv_freq(MLA_ROPE_DIM), _prep_w_in_t(w_in),
        col(mla_q_norm_g), _prep_w_uq_t(mla_w_uq), col(mla_kv_norm_g), _prep_w_ukv_t(mla_w_ukv), tm=min(1024, s))

    lam_init = 0.8 - 0.6 * math.exp(-0.3 * l)
    lam_vecs = jnp.stack([lam_q1, lam_k1, lam_q2, lam_k2]).astype(F32)
    tq = min(512, s)
    diff_o = _attn_call(
        functools.partial(_diff_attn_kernel, tq=tq, lam_init=lam_init), "diff_attn",
        dqt, dk, dvt, [lam_vecs, col(diff_subln_g)], b, s, DIFF_HEADS, 2 * DIFF_QK_DIM, DIFF_V_DIM, tq, 2)
    tq_m = min(1024, s)
    mla_o = _attn_call(
        functools.partial(_mla_attn_kernel, tq=tq_m), "mla_attn",
        mqt, mk, mvt, [col(mla_out_norm_g)], b, s, MLA_HEADS, MLA_QK_PAD, MLA_V_DIM, tq_m, 1)

    kmem, vmem = _memkv_call(mem2, mem_w_k.astype(BF16), mem_w_v.astype(BF16))
    h2, hp = _post_call(diff_o, mla_o, x2, w_o.astype(BF16), row(ln1_g), row(ln1_b), mem_w_q.astype(BF16),
                        kmem, vmem, mem_w_o.astype(BF16), row(ln2_g), row(ln2_b), s, tm=min(512, s))

    return _moe(h2, hp, router_w, router_bias, exp_w_gate, exp_w_up, exp_w_down,
                sh_w_gate.astype(BF16), sh_w_up.astype(BF16), sh_w_down.astype(BF16),
                row(ln3_g), row(ln3_b), tm_route=min(1024, s), tmx=1024, tm_comb=min(512, s))


def kernel(x, mem, positions, w_in, lam_q1, lam_k1, lam_q2, lam_k2, diff_subln_g, mla_q_norm_g, mla_w_uq,
           mla_kv_norm_g, mla_w_ukv, mla_out_norm_g, w_o, ln1_g, ln1_b, mem_w_q, mem_w_k, mem_w_v, mem_w_o,
           ln2_g, ln2_b, router_w, router_bias, exp_w_gate, exp_w_up, exp_w_down, sh_w_gate, sh_w_up,
           sh_w_down, ln3_g, ln3_b):
    b, s, d = x.shape
    h = x.reshape(b * s, d)
    mem2 = mem.reshape(b * mem.shape[1], d)
    pos_row = positions.reshape(1, b * s)
    params = (w_in, lam_q1, lam_k1, lam_q2, lam_k2, diff_subln_g, mla_q_norm_g, mla_w_uq, mla_kv_norm_g,
              mla_w_ukv, mla_out_norm_g, w_o, ln1_g, ln1_b, mem_w_q, mem_w_k, mem_w_v, mem_w_o, ln2_g, ln2_b,
              router_w, router_bias, exp_w_gate, exp_w_up, exp_w_down, sh_w_gate, sh_w_up, sh_w_down,
              ln3_g, ln3_b)
    for l in range(w_in.shape[0]):
        h = _layer(h, mem2, pos_row, b, s, l, *[p[l] for p in params])
    return h.reshape(b, s, d)
```

```python
import functools
import math

import jax
import jax.numpy as jnp
from jax import lax
from jax.experimental import pallas as pl
from jax.experimental.pallas import tpu as pltpu
from jax.experimental.pallas import tpu_sc as plsc

F32 = jnp.float32
BF16 = jnp.bfloat16

D_MODEL = 1024
N_MEM = 256
ROPE_THETA = 500000.0
NEG_INF = -1e30
LN_EPS = 1e-5
RMS_EPS = 1e-6
DIFF_HEADS = 4
DIFF_QK_DIM = 64
DIFF_V_DIM = 128
DIFF_ROT_DIM = DIFF_QK_DIM // 4
MLA_HEADS = 4
MLA_Q_RANK = 256
MLA_KV_RANK = 128
MLA_NOPE_DIM = 128
MLA_ROPE_DIM = 64
MLA_V_DIM = 128
MLA_QK_PAD = 256
V_AUG_DIM = 128 + 16
MEM_HEADS = 4
MEM_HEAD_DIM = 128
N_EXPERTS = 64
TOP_K = 8
N_GROUPS = 8
TOPK_GROUPS = 4
GROUP_SIZE = N_EXPERTS // N_GROUPS
EXPERT_DIM = 256
SHARED_DIM = 256
ROUTED_SCALE = 2.5
DEPTH = 1
DEEPNORM_ALPHA = (2.0 * DEPTH) ** 0.25
LOG2E = math.log2(math.e)

V7X_LANES = 128
V7X_SUBLANES = 8
V7X_VMEM_LIMIT_BYTES = 56 * 1024 * 1024
V7X_SC_CORES = 2
V7X_SC_SUBCORES = 16
SC_INDEX_WINDOW = 128
V7X_BF16_SUBLANES = 16

N_DQ = DIFF_HEADS * 2 * DIFF_QK_DIM
N_DV = DIFF_HEADS * DIFF_V_DIM
O_DK = N_DQ
O_DV = 2 * N_DQ
O_CQ = O_DV + N_DV
O_CKV = O_CQ + MLA_Q_RANK
O_KR = O_CKV + MLA_KV_RANK
N_IN_PAD = O_KR + V7X_LANES


def _cparams(sem):
    return pltpu.CompilerParams(dimension_semantics=sem, vmem_limit_bytes=V7X_VMEM_LIMIT_BYTES)


def _dot(a, b):
    return jnp.dot(a, b, preferred_element_type=F32)


def _dot_nt(a, b):
    return lax.dot_general(a, b, (((1,), (1,)), ((), ())), preferred_element_type=F32)


def _rms(x, g):
    return x * lax.rsqrt(jnp.mean(jnp.square(x), axis=-1, keepdims=True) + RMS_EPS) * g


def _rms_rows(x, g):
    return x * lax.rsqrt(jnp.mean(jnp.square(x), axis=0, keepdims=True) + RMS_EPS) * g


def _layer_norm(x, g, b):
    mu = jnp.mean(x, axis=-1, keepdims=True)
    xc = x - mu
    var = jnp.mean(jnp.square(xc), axis=-1, keepdims=True)
    return xc * lax.rsqrt(var + LN_EPS) * g + b


def _with_ones_rows(vt, heads):
    dv = vt.shape[0] // heads
    ones = jnp.ones((V_AUG_DIM - dv, vt.shape[1]), vt.dtype)
    return jnp.concatenate([p for h in range(heads) for p in (vt[h * dv:(h + 1) * dv], ones)], axis=0)


def _rope_rows(x, cos, sin, period):
    h = cos.shape[0]
    pieces = []
    for base in range(0, x.shape[0], period):
        x1 = x[base:base + h]
        x2 = x[base + h:base + 2 * h]
        pieces.append(x1 * cos - x2 * sin)
        pieces.append(x2 * cos + x1 * sin)
        if period > 2 * h:
            pieces.append(x[base + 2 * h:base + period])
    return jnp.concatenate(pieces, axis=0)


def _proj_kernel(x_ref, pos_ref, fd_ref, fm_ref, w_in_ref, gq_ref, wuq_ref, gkv_ref, wukv_ref,
                 dq_ref, dk_ref, dv_ref, mq_ref, mk_ref, mv_ref):
    xb = x_ref[...].astype(BF16)
    pos = pos_ref[...].astype(F32)
    ang_d = fd_ref[...] * pos
    ang_m = fm_ref[...] * pos
    cos_d, sin_d = jnp.cos(ang_d), jnp.sin(ang_d)
    cos_m, sin_m = jnp.cos(ang_m), jnp.sin(ang_m)

    dq = _rope_rows(_dot_nt(w_in_ref[0:N_DQ, :], xb), cos_d, sin_d, DIFF_QK_DIM)
    dq_ref[...] = (dq * (DIFF_QK_DIM ** -0.5 * LOG2E)).astype(BF16)
    dk = _rope_rows(_dot_nt(w_in_ref[O_DK:O_DK + N_DQ, :], xb), cos_d, sin_d, DIFF_QK_DIM)
    dk_ref[...] = dk.T.astype(BF16)
    dv_ref[...] = _with_ones_rows(_dot_nt(w_in_ref[O_DV:O_DV + N_DV, :], xb), DIFF_HEADS).astype(BF16)

    c_q = _dot_nt(w_in_ref[O_CQ:O_CQ + MLA_Q_RANK, :], xb)
    q = _dot(wuq_ref[...], _rms_rows(c_q, gq_ref[...]).astype(BF16))
    q = q * ((MLA_NOPE_DIM + MLA_ROPE_DIM) ** -0.5 * LOG2E)
    pieces = []
    for h in range(MLA_HEADS):
        o = h * MLA_QK_PAD
        pieces.append(q[o:o + MLA_NOPE_DIM])
        pieces.append(_rope_rows(q[o + MLA_NOPE_DIM:o + MLA_NOPE_DIM + MLA_ROPE_DIM], cos_m, sin_m, MLA_ROPE_DIM))
        pieces.append(q[o + MLA_NOPE_DIM + MLA_ROPE_DIM:o + MLA_QK_PAD])
    mq_ref[...] = jnp.concatenate(pieces, axis=0).astype(BF16)

    c_kv = _dot_nt(w_in_ref[O_CKV:O_CKV + MLA_KV_RANK, :], xb)
    kv = _dot(wukv_ref[...], _rms_rows(c_kv, gkv_ref[...]).astype(BF16))
    n_kn = MLA_HEADS * MLA_NOPE_DIM
    mv_ref[...] = _with_ones_rows(kv[n_kn:], MLA_HEADS).astype(BF16)
    k_rope = _dot_nt(w_in_ref[O_KR:O_KR + V7X_LANES, :], xb)
    k_pe = jnp.concatenate([_rope_rows(k_rope[:MLA_ROPE_DIM], cos_m, sin_m, MLA_ROPE_DIM),
                            k_rope[MLA_ROPE_DIM:]], axis=0)
    k_nope_t = kv[:n_kn].T.astype(BF16)
    k_pe_t = k_pe.T.astype(BF16)
    for h in range(MLA_HEADS):
        o = h * MLA_QK_PAD
        mk_ref[:, o:o + MLA_NOPE_DIM] = k_nope_t[:, h * MLA_NOPE_DIM:(h + 1) * MLA_NOPE_DIM]
        mk_ref[:, o + MLA_NOPE_DIM:o + MLA_QK_PAD] = k_pe_t


def _proj_call(x2, pos_row, fd, fm, w_in_t, gq, wuq_t, gkv, wukv_t, tm):
    t = x2.shape[0]
    full = lambda a: pl.BlockSpec(a.shape, lambda i: (0, 0))
    col_blk = lambda r: pl.BlockSpec((r, tm), lambda i: (0, i))
    row_blk = lambda w: pl.BlockSpec((tm, w), lambda i: (i, 0))
    n_mq = MLA_HEADS * MLA_QK_PAD
    return pl.pallas_call(
        _proj_kernel,
        out_shape=[jax.ShapeDtypeStruct((N_DQ, t), BF16), jax.ShapeDtypeStruct((t, N_DQ), BF16),
                   jax.ShapeDtypeStruct((DIFF_HEADS * V_AUG_DIM, t), BF16), jax.ShapeDtypeStruct((n_mq, t), BF16),
                   jax.ShapeDtypeStruct((t, n_mq), BF16), jax.ShapeDtypeStruct((MLA_HEADS * V_AUG_DIM, t), BF16)],
        grid=(t // tm,),
        in_specs=[row_blk(D_MODEL), col_blk(1), full(fd), full(fm), full(w_in_t), full(gq), full(wuq_t),
                  full(gkv), full(wukv_t)],
        out_specs=[col_blk(N_DQ), row_blk(N_DQ), col_blk(DIFF_HEADS * V_AUG_DIM), col_blk(n_mq), row_blk(n_mq),
                   col_blk(MLA_HEADS * V_AUG_DIM)],
        compiler_params=_cparams(("parallel",)),
        name="proj_rope",
    )(x2, pos_row, fd, fm, w_in_t, gq, wuq_t, gkv, wukv_t)


def _flash_t(qq, k_ref, vt_ref, s_refs, p_refs, acc_ref, m_ref, i, tq):
    n = qq.shape[1]
    tk = s_refs[0].shape[0]
    assert tq == 2 * tk
    q_pos = i * tq + (lax.broadcasted_iota(jnp.int32, (tk, n), 1) & (tq - 1))
    key_row = lax.broadcasted_iota(jnp.int32, (tk, n), 0)

    def scores(c, s_ref):
        off = pl.multiple_of(c * tk, tk)
        s = _dot(k_ref[pl.ds(off, tk), :], qq)
        s_ref[...] = s
        return jnp.max(s, axis=0, keepdims=True)

    def softmax(c, s_ref, p_ref, cmax, m, masked):
        s = s_ref[...]
        if masked:
            s = jnp.where(key_row + c * tk <= q_pos, s, NEG_INF)
            cmax = jnp.max(s, axis=0, keepdims=True)
        m_new = jnp.maximum(m, cmax)
        alpha = jnp.exp2(m - m_new)
        p_ref[...] = jnp.exp2((s - m_new).astype(BF16))
        return m_new, alpha

    def values(c, p_ref, alpha):
        off = pl.multiple_of(jnp.maximum(c, 0) * tk, tk)
        acc_ref[...] = alpha * acc_ref[...] + _dot(vt_ref[:, pl.ds(off, tk)], p_ref[...])

    late = [slice(a + tq // 2, a + tq) for a in range(0, n, tq)]

    def scores_late(c, s_ref):
        off = pl.multiple_of(c * tk, tk)
        kc = k_ref[pl.ds(off, tk), :]
        for sl in late:
            s_ref[:, sl] = _dot(kc, qq[:, sl])

    def softmax_late(c, s_ref, p_ref, m):
        m_ref[...] = m
        alphas = []
        for sl in late:
            shape = (tk, sl.stop - sl.start)
            key_pos = lax.broadcasted_iota(jnp.int32, shape, 0) + c * tk
            query_pos = i * tq + ((lax.broadcasted_iota(jnp.int32, shape, 1) + sl.start) & (tq - 1))
            s = jnp.where(key_pos <= query_pos, s_ref[:, sl], NEG_INF)
            m_old = m_ref[:, sl]
            m_new = jnp.maximum(m_old, jnp.max(s, axis=0, keepdims=True))
            alphas.append(jnp.exp2(m_old - m_new))
            p_ref[:, sl] = jnp.exp2((s - m_new).astype(BF16))
        return alphas

    def values_late(c, p_ref, alphas):
        off = pl.multiple_of(c * tk, tk)
        vc = vt_ref[:, pl.ds(off, tk)]
        for sl, alpha in zip(late, alphas):
            acc_ref[:, sl] = alpha * acc_ref[:, sl] + _dot(vc, p_ref[:, sl])

    def pair(u, carry, masked_second):
        cm1, alpha0, m = carry
        c = 2 * u
        cm0 = scores(c, s_refs[0])
        values(c - 2, p_refs[0], alpha0)
        m, alpha1 = softmax(c - 1, s_refs[1], p_refs[1], cm1, m, False)
        if masked_second:
            scores_late(c + 1, s_refs[1])
        else:
            cm1 = scores(c + 1, s_refs[1])
        values(c - 1, p_refs[1], alpha1)
        m, alpha0 = softmax(c, s_refs[0], p_refs[0], cm0, m, masked_second)
        return cm1, alpha0, m

    acc_ref[...] = jnp.zeros(acc_ref.shape, F32)
    scores(0, s_refs[0])
    cm1 = scores(1, s_refs[1])
    m, alpha0 = softmax(0, s_refs[0], p_refs[0], None, jnp.full((1, n), NEG_INF, F32), True)
    carry = lax.fori_loop(1, i, functools.partial(pair, masked_second=False), (cm1, alpha0, m))
    cm1, alpha0, m = lax.fori_loop(jnp.maximum(i, 1), i + 1, functools.partial(pair, masked_second=True), carry)
    values(2 * i, p_refs[0], alpha0)
    values_late(2 * i + 1, p_refs[1], softmax_late(2 * i + 1, s_refs[1], p_refs[1], m))
    dv = vt_ref.shape[0] - V7X_BF16_SUBLANES
    return acc_ref[0:dv, :] / acc_ref[dv:dv + 1, :]


def _diff_attn_kernel(qt_ref, k_ref, vt_ref, lam_ref, g_ref, o_ref, s0, s1, p0, p1, acc_ref, m_ref, *, tq,
                      lam_init):
    i = pl.program_id(2)
    qt = qt_ref[...]
    row = lax.broadcasted_iota(jnp.int32, qt.shape, 0)
    zero = jnp.zeros_like(qt)
    qq = jnp.concatenate([jnp.where(row < DIFF_QK_DIM, qt, zero), jnp.where(row >= DIFF_QK_DIM, qt, zero)], axis=1)
    o = _flash_t(qq, k_ref, vt_ref, (s0, s1), (p0, p1), acc_ref, m_ref, i, tq)
    lv = lam_ref[...]
    lam = (jnp.exp(jnp.sum(lv[0:1] * lv[1:2], axis=1, keepdims=True))
           - jnp.exp(jnp.sum(lv[2:3] * lv[3:4], axis=1, keepdims=True)) + lam_init)
    d = o[:, :tq] - lam * o[:, tq:]
    o_ref[...] = (_rms_rows(d, g_ref[...]) * (1.0 - lam_init)).T.astype(o_ref.dtype)


def _mla_attn_kernel(qt_ref, k_ref, vt_ref, g_ref, o_ref, s0, s1, p0, p1, acc_ref, m_ref, *, tq):
    i = pl.program_id(2)
    o = _flash_t(qt_ref[...], k_ref, vt_ref, (s0, s1), (p0, p1), acc_ref, m_ref, i, tq)
    o_ref[...] = _rms_rows(o, g_ref[...]).T.astype(o_ref.dtype)


def _attn_call(kernel, name, qt, k, vt, extras, b, s, heads, dk_w, dv_w, tq, n_maps):
    nq = s // tq
    t = b * s
    tk = tq // 2
    n = n_maps * tq
    small = [pl.BlockSpec(a.shape, lambda bi, h, i: (0, 0)) for a in extras]
    return pl.pallas_call(
        kernel,
        out_shape=jax.ShapeDtypeStruct((t, heads * dv_w), BF16),
        grid=(b, heads, nq),
        in_specs=[pl.BlockSpec((dk_w, tq), lambda bi, h, i: (h, bi * nq + i)),
                  pl.BlockSpec((s, dk_w), lambda bi, h, i: (bi, h)),
                  pl.BlockSpec((V_AUG_DIM, s), lambda bi, h, i: (h, bi))] + small,
        out_specs=pl.BlockSpec((tq, dv_w), lambda bi, h, i: (bi * nq + i, h)),
        scratch_shapes=[pltpu.VMEM((tk, n), F32), pltpu.VMEM((tk, n), F32),
                        pltpu.VMEM((tk, n), BF16), pltpu.VMEM((tk, n), BF16), pltpu.VMEM((V_AUG_DIM, n), F32),
                        pltpu.VMEM((1, n), F32)],
        compiler_params=_cparams(("parallel", "parallel", "arbitrary")),
        name=name,
    )(qt, k, vt, *extras)


def _memkv_kernel(mem_ref, wk_ref, wv_ref, k_ref, v_ref):
    mb = mem_ref[...].astype(BF16)
    k_ref[...] = _dot(mb, wk_ref[...]).astype(BF16)
    v_ref[...] = _dot(mb, wv_ref[...]).astype(BF16)


def _memkv_call(mem2, wk, wv):
    n = mem2.shape[0]
    w = wk.shape[1]
    tm = N_MEM
    return pl.pallas_call(
        _memkv_kernel,
        out_shape=[jax.ShapeDtypeStruct((n, w), BF16)] * 2,
        grid=(n // tm,),
        in_specs=[pl.BlockSpec((tm, D_MODEL), lambda i: (i, 0)),
                  pl.BlockSpec(wk.shape, lambda i: (0, 0)),
                  pl.BlockSpec(wv.shape, lambda i: (0, 0))],
        out_specs=[pl.BlockSpec((tm, w), lambda i: (i, 0))] * 2,
        compiler_params=_cparams(("parallel",)),
        name="mem_kv",
    )(mem2, wk, wv)


def _pack_bf16_pairs(x):
    half = x.shape[1] // 2
    r = pltpu.bitcast(x.astype(BF16).astype(F32), jnp.uint32)
    return (r[:, :half] >> 16) | (r[:, half:] & jnp.uint32(0xFFFF0000))


def _unpack_bf16_pairs(w):
    lo = pltpu.bitcast(w << 16, F32).astype(BF16)
    hi = pltpu.bitcast(w & jnp.uint32(0xFFFF0000), F32).astype(BF16)
    return lo, hi


def _post_kernel(da_ref, ma_ref, x_ref, wo_ref, g1_ref, b1_ref, wq_ref, km_ref, vm_ref, wmo_ref,
                 g2_ref, b2_ref, h_ref, hp_ref):
    n_d = DIFF_HEADS * DIFF_V_DIM
    mix = _dot(da_ref[...], wo_ref[0:n_d, :]) + _dot(ma_ref[...], wo_ref[n_d:, :])
    h1 = _layer_norm(DEEPNORM_ALPHA * x_ref[...] + mix, g1_ref[...], b1_ref[...])
    q = (_dot(h1.astype(BF16), wq_ref[...]) * (MEM_HEAD_DIM ** -0.5)).astype(BF16)
    outs = []
    for h in range(MEM_HEADS):
        sl = slice(h * MEM_HEAD_DIM, (h + 1) * MEM_HEAD_DIM)
        s = _dot_nt(q[:, sl], km_ref[:, sl])
        m = jnp.max(s, axis=1, keepdims=True)
        p = jnp.exp(s - m)
        p = p / jnp.sum(p, axis=1, keepdims=True)
        outs.append(_dot(p.astype(BF16), vm_ref[:, sl]))
    o = jnp.concatenate(outs, axis=1).astype(BF16)
    xat = _dot(o, wmo_ref[...])
    h2 = _layer_norm(DEEPNORM_ALPHA * h1 + xat, g2_ref[...], b2_ref[...])
    h_ref[...] = h2
    hp_ref[...] = _pack_bf16_pairs(h2)


def _post_call(diff_o, mla_o, x2, wo, g1, b1, wq, kmem, vmem, wmo, g2, b2, s, tm):
    t = x2.shape[0]
    per_b = s // tm
    row = lambda w: pl.BlockSpec((tm, w), lambda i: (i, 0))
    full = lambda a: pl.BlockSpec(a.shape, lambda i: (0, 0))
    memb = pl.BlockSpec((N_MEM, kmem.shape[1]), lambda i: (i // per_b, 0))
    return pl.pallas_call(
        _post_kernel,
        out_shape=[jax.ShapeDtypeStruct((t, D_MODEL), F32), jax.ShapeDtypeStruct((t, D_MODEL // 2), jnp.uint32)],
        grid=(t // tm,),
        in_specs=[row(diff_o.shape[1]), row(mla_o.shape[1]), row(D_MODEL), full(wo), full(g1), full(b1),
                  full(wq), memb, memb, full(wmo), full(g2), full(b2)],
        out_specs=[row(D_MODEL), row(D_MODEL // 2)],
        compiler_params=_cparams(("parallel",)),
        name="post_mix_xattn",
    )(diff_o, mla_o, x2, wo, g1, b1, wq, kmem, vmem, wmo, g2, b2)


def _router_kernel(h_ref, rwt_ref, bias_ref, tri_ref, eid_ref, rnk_ref, wt_ref, cnt_ref):
    tm = h_ref.shape[0]

    @pl.when(pl.program_id(0) == 0)
    def _():
        cnt_ref[...] = jnp.zeros(cnt_ref.shape, cnt_ref.dtype)

    logits = lax.dot_general(rwt_ref[...], h_ref[...], (((1,), (1,)), ((), ())),
                             precision=lax.Precision.HIGHEST, preferred_element_type=F32)
    scores = jax.nn.sigmoid(logits)
    choice = scores + bias_ref[...]
    neg = float("-inf")

    c3 = choice.reshape(N_GROUPS, GROUP_SIZE, tm)
    mi = lax.broadcasted_iota(jnp.int32, c3.shape, 1)
    top1 = jnp.max(c3, axis=1, keepdims=True)
    first = jnp.min(jnp.where(c3 == top1, mi, GROUP_SIZE), axis=1, keepdims=True)
    top2 = jnp.max(jnp.where(mi == first, neg, c3), axis=1, keepdims=True)
    gs = (top1 + top2).reshape(N_GROUPS, tm)

    gi = lax.broadcasted_iota(jnp.int32, gs.shape, 0)
    rank = jnp.zeros(gs.shape, jnp.int32)
    for g in range(N_GROUPS):
        rowg = gs[g:g + 1, :]
        beats = (rowg > gs) | ((rowg == gs) & (g < gi))
        rank = rank + beats.astype(jnp.int32)
    gmask = rank < TOPK_GROUPS
    emask = jnp.broadcast_to(gmask.reshape(N_GROUPS, 1, tm), c3.shape).reshape(N_EXPERTS, tm)

    work = jnp.where(emask, choice, NEG_INF)
    ei = lax.broadcasted_iota(jnp.int32, work.shape, 0)
    sel = jnp.zeros(work.shape, jnp.bool_)
    picks = []
    for _ in range(TOP_K):
        mx = jnp.max(work, axis=0, keepdims=True)
        fi = jnp.min(jnp.where(work == mx, ei, N_EXPERTS), axis=0, keepdims=True)
        pick = ei == fi
        picks.append((fi, pick))
        sel = sel | pick
        work = jnp.where(pick, neg, work)
    top_w = jnp.where(sel, scores, 0.0)
    gates = top_w / jnp.sum(top_w, axis=0, keepdims=True) * ROUTED_SCALE

    sel_b = jnp.where(sel, 1.0, 0.0).astype(BF16)
    before = _dot(sel_b, tri_ref[...]).astype(jnp.int32)
    run = cnt_ref[:, 0:1]
    pos = run + before
    cnt_ref[...] = jnp.broadcast_to(run + jnp.sum(sel.astype(jnp.int32), axis=1, keepdims=True), cnt_ref.shape)

    eid_ref[...] = jnp.concatenate([fi for fi, _ in picks], axis=0)
    rnk_ref[...] = jnp.concatenate(
        [jnp.sum(jnp.where(pick, pos, 0), axis=0, keepdims=True) for _, pick in picks], axis=0)
    w_rows = [jnp.sum(jnp.where(pick, gates, 0.0), axis=0, keepdims=True) for _, pick in picks]
    w_pad = jnp.concatenate(w_rows + [jnp.zeros((V7X_LANES - TOP_K, tm), F32)], axis=0)
    wt_ref[...] = w_pad.T


def _router_call(h2, rwt, bias, tm):
    t = h2.shape[0]
    tri = jnp.triu(jnp.ones((tm, tm), BF16), k=1)
    full = lambda a: pl.BlockSpec(a.shape, lambda i: (0, 0))
    return pl.pallas_call(
        _router_kernel,
        out_shape=[jax.ShapeDtypeStruct((TOP_K, t), jnp.int32), jax.ShapeDtypeStruct((TOP_K, t), jnp.int32),
                   jax.ShapeDtypeStruct((t, V7X_LANES), F32), jax.ShapeDtypeStruct((N_EXPERTS, V7X_LANES), jnp.int32)],
        grid=(t // tm,),
        in_specs=[pl.BlockSpec((tm, D_MODEL), lambda i: (i, 0)), full(rwt), full(bias), full(tri)],
        out_specs=[pl.BlockSpec((TOP_K, tm), lambda i: (0, i)), pl.BlockSpec((TOP_K, tm), lambda i: (0, i)),
                   pl.BlockSpec((tm, V7X_LANES), lambda i: (i, 0)),
                   pl.BlockSpec((N_EXPERTS, V7X_LANES), lambda i: (0, 0))],
        compiler_params=_cparams(("arbitrary",)),
        name="router",
    )(h2, rwt, bias, tri)


_SC_MESH_AXES = ("core", "subcore")


def _sc_mesh():
    return plsc.VectorSubcoreMesh(core_axis_name=_SC_MESH_AXES[0], subcore_axis_name=_SC_MESH_AXES[1])


def _sc_worker_chunks(n_chunks):
    n_workers = V7X_SC_CORES * V7X_SC_SUBCORES
    assert n_chunks % n_workers == 0
    per_worker = n_chunks // n_workers
    first = (lax.axis_index(_SC_MESH_AXES[1]) * V7X_SC_CORES + lax.axis_index(_SC_MESH_AXES[0])) * per_worker
    return first, per_worker


def _sc_scatter_rows(x, idx, n_rows):
    t, w = x.shape
    n_chunks, n_k, window = idx.shape

    @functools.partial(pl.kernel, out_type=jax.ShapeDtypeStruct((n_rows, w), x.dtype), mesh=_sc_mesh(),
                       scratch_types=[pltpu.VMEM((n_k, window), jnp.int32), pltpu.VMEM((window, w), x.dtype)],
                       name="moe_dispatch_sc")
    def scatter(x_hbm, i_hbm, o_hbm, idx_v, rows_v):
        first, per_worker = _sc_worker_chunks(n_chunks)

        @pl.loop(0, per_worker)
        def _(c):
            chunk = first + c
            pltpu.sync_copy(i_hbm.at[chunk], idx_v)
            pltpu.sync_copy(x_hbm.at[pl.ds(chunk * window, window)], rows_v)
            for k in range(n_k):
                pltpu.sync_copy(rows_v, o_hbm.at[idx_v.at[k]])

    return scatter(x, idx)


def _sc_gather_rows(y, idx):
    n_chunks, n_k, window = idx.shape
    w = y.shape[1]

    half = window // 2
    steps = [(k, h) for k in range(n_k) for h in range(2)]

    @functools.partial(pl.kernel, out_type=jax.ShapeDtypeStruct((n_k, n_chunks * window, w), y.dtype),
                       mesh=_sc_mesh(),
                       scratch_types=[pltpu.VMEM((n_k, window), jnp.int32), pltpu.VMEM((half, w), y.dtype),
                                      pltpu.VMEM((half, w), y.dtype), pltpu.SemaphoreType.DMA,
                                      pltpu.SemaphoreType.DMA],
                       name="moe_combine_sc")
    def gather(y_hbm, i_hbm, o_hbm, idx_v, rows_a, rows_b, sem_a, sem_b):
        first, per_worker = _sc_worker_chunks(n_chunks)
        bufs = ((rows_a, sem_a), (rows_b, sem_b))

        @pl.loop(0, per_worker)
        def _(c):
            chunk = first + c
            pltpu.sync_copy(i_hbm.at[chunk], idx_v)

            def start(j):
                k, h = steps[j]
                buf, sem = bufs[j % 2]
                return pltpu.async_copy(y_hbm.at[idx_v.at[k, pl.ds(h * half, half)]], buf, sem)

            pending = start(0)
            for j, (k, h) in enumerate(steps):
                following = start(j + 1) if j + 1 < len(steps) else None
                pending.wait()
                pltpu.sync_copy(bufs[j % 2][0], o_hbm.at[k, pl.ds(chunk * window + h * half, half)])
                pending = following

    return gather(y, idx)


def _experts_kernel(te_ref, nu_ref, xs_ref, wg_ref, wu_ref, wd_ref, y_ref, wg_b, wu_b, wd_b):
    i = pl.program_id(0)

    @pl.when(i < nu_ref[0])
    def _():
        @pl.when((i == 0) | (te_ref[i] != te_ref[jnp.maximum(i - 1, 0)]))
        def _():
            wg_b[...] = wg_ref[0].astype(BF16)
            wu_b[...] = wu_ref[0].astype(BF16)
            wd_b[...] = wd_ref[0].astype(BF16)

        half = D_MODEL // 2
        lo, hi = _unpack_bf16_pairs(xs_ref[...])
        a = _dot(lo, wg_b[:half, :]) + _dot(hi, wg_b[half:, :])
        u = _dot(lo, wu_b[:half, :]) + _dot(hi, wu_b[half:, :])
        y_ref[...] = _pack_bf16_pairs(_dot((jax.nn.silu(a) * u).astype(BF16), wd_b[...]))


def _experts_call(tile_expert, n_used, xs, wg, wu, wd, tmx):
    n_rows, w = xs.shape
    tile = lambda i, te, nu: jnp.minimum(i, nu[0] - 1)
    wspec = lambda shp: pl.BlockSpec((1,) + shp, lambda i, te, nu: (te[tile(i, te, nu)], 0, 0))
    return pl.pallas_call(
        _experts_kernel,
        out_shape=jax.ShapeDtypeStruct((n_rows, w), xs.dtype),
        grid_spec=pltpu.PrefetchScalarGridSpec(
            num_scalar_prefetch=2,
            grid=(n_rows // tmx,),
            in_specs=[pl.BlockSpec((tmx, w), lambda i, te, nu: (tile(i, te, nu), 0)),
                      wspec((D_MODEL, EXPERT_DIM)), wspec((D_MODEL, EXPERT_DIM)), wspec((EXPERT_DIM, D_MODEL))],
            out_specs=pl.BlockSpec((tmx, w), lambda i, te, nu: (tile(i, te, nu), 0)),
            scratch_shapes=[pltpu.VMEM((D_MODEL, EXPERT_DIM), BF16), pltpu.VMEM((D_MODEL, EXPERT_DIM), BF16),
                            pltpu.VMEM((EXPERT_DIM, D_MODEL), BF16)],
        ),
        compiler_params=_cparams(("arbitrary",)),
        name="moe_experts",
    )(tile_expert, n_used, xs, wg, wu, wd)


def _combine_kernel(wt_ref, h_ref, yg_ref, sg_ref, su_ref, sd_ref, g3_ref, b3_ref, o_ref):
    h = h_ref[...]
    hb = h.astype(BF16)
    acc = _dot((jax.nn.silu(_dot(hb, sg_ref[...])) * _dot(hb, su_ref[...])).astype(BF16), sd_ref[...])
    wt = wt_ref[...]
    lo_sum = None
    hi_sum = None
    for k in range(TOP_K):
        lo, hi = _unpack_bf16_pairs(yg_ref[k])
        wk = wt[:, k:k + 1]
        lo_sum = wk * lo if lo_sum is None else lo_sum + wk * lo
        hi_sum = wk * hi if hi_sum is None else hi_sum + wk * hi
    acc = acc + jnp.concatenate([lo_sum, hi_sum], axis=1)
    o_ref[...] = _layer_norm(DEEPNORM_ALPHA * h + acc, g3_ref[...], b3_ref[...])


def _combine_call(wt, h2, yg, sg, su, sd, g3, b3, tm):
    t = h2.shape[0]
    full = lambda a: pl.BlockSpec(a.shape, lambda i: (0, 0))
    return pl.pallas_call(
        _combine_kernel,
        out_shape=jax.ShapeDtypeStruct((t, D_MODEL), F32),
        grid=(t // tm,),
        in_specs=[pl.BlockSpec((tm, V7X_LANES), lambda i: (i, 0)),
                  pl.BlockSpec((tm, D_MODEL), lambda i: (i, 0)),
                  pl.BlockSpec((TOP_K, tm, yg.shape[2]), lambda i: (0, i, 0)),
                  full(sg), full(su), full(sd), full(g3), full(b3)],
        out_specs=pl.BlockSpec((tm, D_MODEL), lambda i: (i, 0)),
        compiler_params=_cparams(("parallel",)),
        name="moe_combine",
    )(wt, h2, yg, sg, su, sd, g3, b3)


def _moe(h2, hp, router_w, router_bias, wg, wu, wd, sg, su, sd, g3, b3, tm_route, tmx, tm_comb):
    t = h2.shape[0]
    eid, rnk, wt, cnt = _router_call(h2, router_w.T, router_bias.reshape(-1, 1), tm_route)
    counts = cnt[:, 0]
    tiles_per_e = (counts + (tmx - 1)) // tmx
    tile_end = jnp.cumsum(tiles_per_e)
    row_start = ((tile_end - tiles_per_e) * tmx).astype(jnp.int32)
    n_tiles = (t * TOP_K) // tmx + N_EXPERTS
    n_used = tile_end[-1:].astype(jnp.int32)
    tile_expert = jnp.minimum(
        jnp.sum(tile_end[None, :] <= jnp.arange(n_tiles, dtype=jnp.int32)[:, None], axis=1), N_EXPERTS - 1
    ).astype(jnp.int32)
    expert_ids = jnp.arange(N_EXPERTS, dtype=jnp.int32)[:, None, None]
    dest = rnk + jnp.sum(jnp.where(eid[None] == expert_ids, row_start[:, None, None], 0), axis=0)
    dest_chunks = dest.reshape(TOP_K, t // SC_INDEX_WINDOW, SC_INDEX_WINDOW).transpose(1, 0, 2)
    xs = _sc_scatter_rows(hp, dest_chunks, n_tiles * tmx)
    yp = _experts_call(tile_expert, n_used, xs, wg, wu, wd, tmx)
    yg = _sc_gather_rows(yp, dest_chunks)
    return _combine_call(wt, h2, yg, sg, su, sd, g3, b3, tm_comb)


def _prep_w_in_t(w_in):
    pad = jnp.zeros((D_MODEL, V7X_LANES - MLA_ROPE_DIM), w_in.dtype)
    return jnp.concatenate([w_in, pad], axis=1).T.astype(BF16)


def _prep_w_uq_t(w_uq):
    w = w_uq.reshape(MLA_Q_RANK, MLA_HEADS, MLA_NOPE_DIM + MLA_ROPE_DIM)
    pad = jnp.zeros((MLA_Q_RANK, MLA_HEADS, MLA_QK_PAD - MLA_NOPE_DIM - MLA_ROPE_DIM), w.dtype)
    return jnp.concatenate([w, pad], axis=2).reshape(MLA_Q_RANK, MLA_HEADS * MLA_QK_PAD).T.astype(BF16)


def _prep_w_ukv_t(w_ukv):
    w = w_ukv.reshape(MLA_KV_RANK, MLA_HEADS, MLA_NOPE_DIM + MLA_V_DIM)
    k = w[:, :, :MLA_NOPE_DIM].reshape(MLA_KV_RANK, MLA_HEADS * MLA_NOPE_DIM)
    v = w[:, :, MLA_NOPE_DIM:].reshape(MLA_KV_RANK, MLA_HEADS * MLA_V_DIM)
    return jnp.concatenate([k, v], axis=1).T.astype(BF16)


def _rope_inv_freq(dim):
    return (ROPE_THETA ** (-jnp.arange(0, dim, 2, dtype=F32) / dim)).reshape(-1, 1)


def _layer(x2, mem2, pos_row, b, s, l, w_in, lam_q1, lam_k1, lam_q2, lam_k2, diff_subln_g, mla_q_norm_g,
           mla_w_uq, mla_kv_norm_g, mla_w_ukv, mla_out_norm_g, w_o, ln1_g, ln1_b, mem_w_q, mem_w_k, mem_w_v,
           mem_w_o, ln2_g, ln2_b, router_w, router_bias, exp_w_gate, exp_w_up, exp_w_down, sh_w_gate,
           sh_w_up, sh_w_down, ln3_g, ln3_b):
    row = lambda a: a.reshape(1, -1)
    col = lambda a: a.reshape(-1, 1)
    dqt, dk, dvt, mqt, mk, mvt = _proj_call(
        x2, pos_row, _rope_inv_freq(DIFF_ROT_DIM), _rope_inv_freq(MLA_ROPE_DIM), _prep_w_in_t(w_in),
        col(mla_q_norm_g), _prep_w_uq_t(mla_w_uq), col(mla_kv_norm_g), _prep_w_ukv_t(mla_w_ukv), tm=min(1024, s))

    lam_init = 0.8 - 0.6 * math.exp(-0.3 * l)
    lam_vecs = jnp.stack([lam_q1, lam_k1, lam_q2, lam_k2]).astype(F32)
    tq = min(512, s)
    diff_o = _attn_call(
        functools.partial(_diff_attn_kernel, tq=tq, lam_init=lam_init), "diff_attn",
        dqt, dk, dvt, [lam_vecs, col(diff_subln_g)], b, s, DIFF_HEADS, 2 * DIFF_QK_DIM, DIFF_V_DIM, tq, 2)
    tq_m = min(1024, s)
    mla_o = _attn_call(
        functools.partial(_mla_attn_kernel, tq=tq_m), "mla_attn",
        mqt, mk, mvt, [col(mla_out_norm_g)], b, s, MLA_HEADS, MLA_QK_PAD, MLA_V_DIM, tq_m, 1)

    kmem, vmem = _memkv_call(mem2, mem_w_k.astype(BF16), mem_w_v.astype(BF16))
    h2, hp = _post_call(diff_o, mla_o, x2, w_o.astype(BF16), row(ln1_g), row(ln1_b), mem_w_q.astype(BF16),
                        kmem, vmem, mem_w_o.astype(BF16), row(ln2_g), row(ln2_b), s, tm=min(512, s))

    return _moe(h2, hp, router_w, router_bias, exp_w_gate, exp_w_up, exp_w_down,
                sh_w_gate.astype(BF16), sh_w_up.astype(BF16), sh_w_down.astype(BF16),
                row(ln3_g), row(ln3_b), tm_route=min(1024, s), tmx=1024, tm_comb=min(512, s))


def kernel(x, mem, positions, w_in, lam_q1, lam_k1, lam_q2, lam_k2, diff_subln_g, mla_q_norm_g, mla_w_uq,
           mla_kv_norm_g, mla_w_ukv, mla_out_norm_g, w_o, ln1_g, ln1_b, mem_w_q, mem_w_k, mem_w_v, mem_w_o,
           ln2_g, ln2_b, router_w, router_bias, exp_w_gate, exp_w_up, exp_w_down, sh_w_gate, sh_w_up,
           sh_w_down, ln3_g, ln3_b):
    b, s, d = x.shape
    h = x.reshape(b * s, d)
    mem2 = mem.reshape(b * mem.shape[1], d)
    pos_row = positions.reshape(1, b * s)
    params = (w_in, lam_q1, lam_k1, lam_q2, lam_k2, diff_subln_g, mla_q_norm_g, mla_w_uq, mla_kv_norm_g,
              mla_w_ukv, mla_out_norm_g, w_o, ln1_g, ln1_b, mem_w_q, mem_w_k, mem_w_v, mem_w_o, ln2_g, ln2_b,
              router_w, router_bias, exp_w_gate, exp_w_up, exp_w_down, sh_w_gate, sh_w_up, sh_w_down,
              ln3_g, ln3_b)
    for l in range(w_in.shape[0]):
        h = _layer(h, mem2, pos_row, b, s, l, *[p[l] for p in params])
    return h.reshape(b, s, d)
```

```python
import functools
import math

import jax
import jax.numpy as jnp
from jax import lax
from jax.experimental import pallas as pl
from jax.experimental.pallas import tpu as pltpu
from jax.experimental.pallas import tpu_sc as plsc

F32 = jnp.float32
BF16 = jnp.bfloat16

D_MODEL = 1024
N_MEM = 256
ROPE_THETA = 500000.0
NEG_INF = -1e30
LN_EPS = 1e-5
RMS_EPS = 1e-6
DIFF_HEADS = 4
DIFF_QK_DIM = 64
DIFF_V_DIM = 128
DIFF_ROT_DIM = DIFF_QK_DIM // 4
MLA_HEADS = 4
MLA_Q_RANK = 256
MLA_KV_RANK = 128
MLA_NOPE_DIM = 128
MLA_ROPE_DIM = 64
MLA_V_DIM = 128
MLA_QK_PAD = 256
V_AUG_DIM = 128 + 16
MEM_HEADS = 4
MEM_HEAD_DIM = 128
N_EXPERTS = 64
TOP_K = 8
N_GROUPS = 8
TOPK_GROUPS = 4
GROUP_SIZE = N_EXPERTS // N_GROUPS
EXPERT_DIM = 256
SHARED_DIM = 256
ROUTED_SCALE = 2.5
DEPTH = 1
DEEPNORM_ALPHA = (2.0 * DEPTH) ** 0.25
LOG2E = math.log2(math.e)

V7X_LANES = 128
V7X_SUBLANES = 8
V7X_VMEM_LIMIT_BYTES = 56 * 1024 * 1024
V7X_SC_CORES = 2
V7X_SC_SUBCORES = 16
SC_INDEX_WINDOW = 128
V7X_BF16_SUBLANES = 16

N_DQ = DIFF_HEADS * 2 * DIFF_QK_DIM
N_DV = DIFF_HEADS * DIFF_V_DIM
O_DK = N_DQ
O_DV = 2 * N_DQ
O_CQ = O_DV + N_DV
O_CKV = O_CQ + MLA_Q_RANK
O_KR = O_CKV + MLA_KV_RANK
N_IN_PAD = O_KR + V7X_LANES


def _cparams(sem):
    return pltpu.CompilerParams(dimension_semantics=sem, vmem_limit_bytes=V7X_VMEM_LIMIT_BYTES)


def _dot(a, b):
    return jnp.dot(a, b, preferred_element_type=F32)


def _dot_nt(a, b):
    return lax.dot_general(a, b, (((1,), (1,)), ((), ())), preferred_element_type=F32)


def _rms(x, g):
    return x * lax.rsqrt(jnp.mean(jnp.square(x), axis=-1, keepdims=True) + RMS_EPS) * g


def _rms_rows(x, g):
    return x * lax.rsqrt(jnp.mean(jnp.square(x), axis=0, keepdims=True) + RMS_EPS) * g


def _layer_norm(x, g, b):
    mu = jnp.mean(x, axis=-1, keepdims=True)
    xc = x - mu
    var = jnp.mean(jnp.square(xc), axis=-1, keepdims=True)
    return xc * lax.rsqrt(var + LN_EPS) * g + b


def _with_ones_rows(vt, heads):
    dv = vt.shape[0] // heads
    ones = jnp.ones((V_AUG_DIM - dv, vt.shape[1]), vt.dtype)
    return jnp.concatenate([p for h in range(heads) for p in (vt[h * dv:(h + 1) * dv], ones)], axis=0)


def _rope_rows(x, cos, sin, period):
    h = cos.shape[0]
    pieces = []
    for base in range(0, x.shape[0], period):
        x1 = x[base:base + h]
        x2 = x[base + h:base + 2 * h]
        pieces.append(x1 * cos - x2 * sin)
        pieces.append(x2 * cos + x1 * sin)
        if period > 2 * h:
            pieces.append(x[base + 2 * h:base + period])
    return jnp.concatenate(pieces, axis=0)


def _proj_kernel(x_ref, pos_ref, fd_ref, fm_ref, w_in_ref, gq_ref, wuq_ref, gkv_ref, wukv_ref,
                 dq_ref, dk_ref, dv_ref, mq_ref, mk_ref, mv_ref):
    xb = x_ref[...].astype(BF16)
    pos = pos_ref[...].astype(F32)
    ang_d = fd_ref[...] * pos
    ang_m = fm_ref[...] * pos
    cos_d, sin_d = jnp.cos(ang_d), jnp.sin(ang_d)
    cos_m, sin_m = jnp.cos(ang_m), jnp.sin(ang_m)

    dq = _rope_rows(_dot_nt(w_in_ref[0:N_DQ, :], xb), cos_d, sin_d, DIFF_QK_DIM)
    dq_ref[...] = (dq * (DIFF_QK_DIM ** -0.5 * LOG2E)).astype(BF16)
    dk = _rope_rows(_dot_nt(w_in_ref[O_DK:O_DK + N_DQ, :], xb), cos_d, sin_d, DIFF_QK_DIM)
    dk_ref[...] = dk.T.astype(BF16)
    dv_ref[...] = _with_ones_rows(_dot_nt(w_in_ref[O_DV:O_DV + N_DV, :], xb), DIFF_HEADS).astype(BF16)

    c_q = _dot_nt(w_in_ref[O_CQ:O_CQ + MLA_Q_RANK, :], xb)
    q = _dot(wuq_ref[...], _rms_rows(c_q, gq_ref[...]).astype(BF16))
    q = q * ((MLA_NOPE_DIM + MLA_ROPE_DIM) ** -0.5 * LOG2E)
    pieces = []
    for h in range(MLA_HEADS):
        o = h * MLA_QK_PAD
        pieces.append(q[o:o + MLA_NOPE_DIM])
        pieces.append(_rope_rows(q[o + MLA_NOPE_DIM:o + MLA_NOPE_DIM + MLA_ROPE_DIM], cos_m, sin_m, MLA_ROPE_DIM))
        pieces.append(q[o + MLA_NOPE_DIM + MLA_ROPE_DIM:o + MLA_QK_PAD])
    mq_ref[...] = jnp.concatenate(pieces, axis=0).astype(BF16)

    c_kv = _dot_nt(w_in_ref[O_CKV:O_CKV + MLA_KV_RANK, :], xb)
    kv = _dot(wukv_ref[...], _rms_rows(c_kv, gkv_ref[...]).astype(BF16))
    n_kn = MLA_HEADS * MLA_NOPE_DIM
    mv_ref[...] = _with_ones_rows(kv[n_kn:], MLA_HEADS).astype(BF16)
    k_rope = _dot_nt(w_in_ref[O_KR:O_KR + V7X_LANES, :], xb)
    k_pe = jnp.concatenate([_rope_rows(k_rope[:MLA_ROPE_DIM], cos_m, sin_m, MLA_ROPE_DIM),
                            k_rope[MLA_ROPE_DIM:]], axis=0)
    k_nope_t = kv[:n_kn].T.astype(BF16)
    k_pe_t = k_pe.T.astype(BF16)
    for h in range(MLA_HEADS):
        o = h * MLA_QK_PAD
        mk_ref[:, o:o + MLA_NOPE_DIM] = k_nope_t[:, h * MLA_NOPE_DIM:(h + 1) * MLA_NOPE_DIM]
        mk_ref[:, o + MLA_NOPE_DIM:o + MLA_QK_PAD] = k_pe_t


def _proj_call(x2, pos_row, fd, fm, w_in_t, gq, wuq_t, gkv, wukv_t, tm):
    t = x2.shape[0]
    full = lambda a: pl.BlockSpec(a.shape, lambda i: (0, 0))
    col_blk = lambda r: pl.BlockSpec((r, tm), lambda i: (0, i))
    row_blk = lambda w: pl.BlockSpec((tm, w), lambda i: (i, 0))
    n_mq = MLA_HEADS * MLA_QK_PAD
    return pl.pallas_call(
        _proj_kernel,
        out_shape=[jax.ShapeDtypeStruct((N_DQ, t), BF16), jax.ShapeDtypeStruct((t, N_DQ), BF16),
                   jax.ShapeDtypeStruct((DIFF_HEADS * V_AUG_DIM, t), BF16), jax.ShapeDtypeStruct((n_mq, t), BF16),
                   jax.ShapeDtypeStruct((t, n_mq), BF16), jax.ShapeDtypeStruct((MLA_HEADS * V_AUG_DIM, t), BF16)],
        grid=(t // tm,),
        in_specs=[row_blk(D_MODEL), col_blk(1), full(fd), full(fm), full(w_in_t), full(gq), full(wuq_t),
                  full(gkv), full(wukv_t)],
        out_specs=[col_blk(N_DQ), row_blk(N_DQ), col_blk(DIFF_HEADS * V_AUG_DIM), col_blk(n_mq), row_blk(n_mq),
                   col_blk(MLA_HEADS * V_AUG_DIM)],
        compiler_params=_cparams(("parallel",)),
        name="proj_rope",
    )(x2, pos_row, fd, fm, w_in_t, gq, wuq_t, gkv, wukv_t)


def _flash_t(qq, k_ref, vt_ref, s_refs, p_refs, acc_ref, m_ref, i, tq):
    n = qq.shape[1]
    tk = s_refs[0].shape[0]
    assert tq == 2 * tk
    q_pos = i * tq + (lax.broadcasted_iota(jnp.int32, (tk, n), 1) & (tq - 1))
    key_row = lax.broadcasted_iota(jnp.int32, (tk, n), 0)

    def scores(c, s_ref):
        off = pl.multiple_of(c * tk, tk)
        s = _dot(k_ref[pl.ds(off, tk), :], qq)
        s_ref[...] = s
        return jnp.max(s, axis=0, keepdims=True)

    def softmax(c, s_ref, p_ref, cmax, m, masked):
        s = s_ref[...]
        if masked:
            s = jnp.where(key_row + c * tk <= q_pos, s, NEG_INF)
            cmax = jnp.max(s, axis=0, keepdims=True)
        m_new = jnp.maximum(m, cmax)
        alpha = jnp.exp2(m - m_new)
        p_ref[...] = jnp.exp2((s - m_new).astype(BF16))
        return m_new, alpha

    def values(c, p_ref, alpha):
        off = pl.multiple_of(jnp.maximum(c, 0) * tk, tk)
        acc_ref[...] = alpha * acc_ref[...] + _dot(vt_ref[:, pl.ds(off, tk)], p_ref[...])

    late = [slice(a + tq // 2, a + tq) for a in range(0, n, tq)]

    def scores_late(c, s_ref):
        off = pl.multiple_of(c * tk, tk)
        kc = k_ref[pl.ds(off, tk), :]
        for sl in late:
            s_ref[:, sl] = _dot(kc, qq[:, sl])

    def softmax_late(c, s_ref, p_ref, m):
        m_ref[...] = m
        alphas = []
        for sl in late:
            shape = (tk, sl.stop - sl.start)
            key_pos = lax.broadcasted_iota(jnp.int32, shape, 0) + c * tk
            query_pos = i * tq + ((lax.broadcasted_iota(jnp.int32, shape, 1) + sl.start) & (tq - 1))
            s = jnp.where(key_pos <= query_pos, s_ref[:, sl], NEG_INF)
            m_old = m_ref[:, sl]
            m_new = jnp.maximum(m_old, jnp.max(s, axis=0, keepdims=True))
            alphas.append(jnp.exp2(m_old - m_new))
            p_ref[:, sl] = jnp.exp2((s - m_new).astype(BF16))
        return alphas

    def values_late(c, p_ref, alphas):
        off = pl.multiple_of(c * tk, tk)
        vc = vt_ref[:, pl.ds(off, tk)]
        for sl, alpha in zip(late, alphas):
            acc_ref[:, sl] = alpha * acc_ref[:, sl] + _dot(vc, p_ref[:, sl])

    def pair(u, carry, masked_second):
        cm1, alpha0, m = carry
        c = 2 * u
        cm0 = scores(c, s_refs[0])
        values(c - 2, p_refs[0], alpha0)
        m, alpha1 = softmax(c - 1, s_refs[1], p_refs[1], cm1, m, False)
        if masked_second:
            scores_late(c + 1, s_refs[1])
        else:
            cm1 = scores(c + 1, s_refs[1])
        values(c - 1, p_refs[1], alpha1)
        m, alpha0 = softmax(c, s_refs[0], p_refs[0], cm0, m, masked_second)
        return cm1, alpha0, m

    acc_ref[...] = jnp.zeros(acc_ref.shape, F32)
    scores(0, s_refs[0])
    cm1 = scores(1, s_refs[1])
    m, alpha0 = softmax(0, s_refs[0], p_refs[0], None, jnp.full((1, n), NEG_INF, F32), True)
    carry = lax.fori_loop(1, i, functools.partial(pair, masked_second=False), (cm1, alpha0, m))
    cm1, alpha0, m = lax.fori_loop(jnp.maximum(i, 1), i + 1, functools.partial(pair, masked_second=True), carry)
    values(2 * i, p_refs[0], alpha0)
    values_late(2 * i + 1, p_refs[1], softmax_late(2 * i + 1, s_refs[1], p_refs[1], m))
    dv = vt_ref.shape[0] - V7X_BF16_SUBLANES
    return acc_ref[0:dv, :] / acc_ref[dv:dv + 1, :]


def _diff_attn_kernel(qt_ref, k_ref, vt_ref, lam_ref, g_ref, o_ref, s0, s1, p0, p1, acc_ref, m_ref, *, tq,
                      lam_init):
    i = pl.program_id(2)
    qt = qt_ref[...]
    row = lax.broadcasted_iota(jnp.int32, qt.shape, 0)
    zero = jnp.zeros_like(qt)
    qq = jnp.concatenate([jnp.where(row < DIFF_QK_DIM, qt, zero), jnp.where(row >= DIFF_QK_DIM, qt, zero)], axis=1)
    o = _flash_t(qq, k_ref, vt_ref, (s0, s1), (p0, p1), acc_ref, m_ref, i, tq)
    lv = lam_ref[...]
    lam = (jnp.exp(jnp.sum(lv[0:1] * lv[1:2], axis=1, keepdims=True))
           - jnp.exp(jnp.sum(lv[2:3] * lv[3:4], axis=1, keepdims=True)) + lam_init)
    d = o[:, :tq] - lam * o[:, tq:]
    o_ref[...] = (_rms_rows(d, g_ref[...]) * (1.0 - lam_init)).T.astype(o_ref.dtype)


def _mla_attn_kernel(qt_ref, k_ref, vt_ref, g_ref, o_ref, s0, s1, p0, p1, acc_ref, m_ref, *, tq):
    i = pl.program_id(2)
    o = _flash_t(qt_ref[...], k_ref, vt_ref, (s0, s1), (p0, p1), acc_ref, m_ref, i, tq)
    o_ref[...] = _rms_rows(o, g_ref[...]).T.astype(o_ref.dtype)


def _attn_call(kernel, name, qt, k, vt, extras, b, s, heads, dk_w, dv_w, tq, n_maps):
    nq = s // tq
    t = b * s
    tk = tq // 2
    n = n_maps * tq
    small = [pl.BlockSpec(a.shape, lambda bi, h, i: (0, 0)) for a in extras]
    return pl.pallas_call(
        kernel,
        out_shape=jax.ShapeDtypeStruct((t, heads * dv_w), BF16),
        grid=(b, heads, nq),
        in_specs=[pl.BlockSpec((dk_w, tq), lambda bi, h, i: (h, bi * nq + i)),
                  pl.BlockSpec((s, dk_w), lambda bi, h, i: (bi, h)),
                  pl.BlockSpec((V_AUG_DIM, s), lambda bi, h, i: (h, bi))] + small,
        out_specs=pl.BlockSpec((tq, dv_w), lambda bi, h, i: (bi * nq + i, h)),
        scratch_shapes=[pltpu.VMEM((tk, n), F32), pltpu.VMEM((tk, n), F32),
                        pltpu.VMEM((tk, n), BF16), pltpu.VMEM((tk, n), BF16), pltpu.VMEM((V_AUG_DIM, n), F32),
                        pltpu.VMEM((1, n), F32)],
        compiler_params=_cparams(("parallel", "parallel", "arbitrary")),
        name=name,
    )(qt, k, vt, *extras)


def _memkv_kernel(mem_ref, wk_ref, wv_ref, k_ref, v_ref):
    mb = mem_ref[...].astype(BF16)
    k_ref[...] = _dot(mb, wk_ref[...]).astype(BF16)
    v_ref[...] = _dot(mb, wv_ref[...]).astype(BF16)


def _memkv_call(mem2, wk, wv):
    n = mem2.shape[0]
    w = wk.shape[1]
    tm = N_MEM
    return pl.pallas_call(
        _memkv_kernel,
        out_shape=[jax.ShapeDtypeStruct((n, w), BF16)] * 2,
        grid=(n // tm,),
        in_specs=[pl.BlockSpec((tm, D_MODEL), lambda i: (i, 0)),
                  pl.BlockSpec(wk.shape, lambda i: (0, 0)),
                  pl.BlockSpec(wv.shape, lambda i: (0, 0))],
        out_specs=[pl.BlockSpec((tm, w), lambda i: (i, 0))] * 2,
        compiler_params=_cparams(("parallel",)),
        name="mem_kv",
    )(mem2, wk, wv)


def _pack_bf16_pairs(x):
    half = x.shape[1] // 2
    r = pltpu.bitcast(x.astype(BF16).astype(F32), jnp.uint32)
    return (r[:, :half] >> 16) | (r[:, half:] & jnp.uint32(0xFFFF0000))


def _unpack_bf16_pairs(w):
    lo = pltpu.bitcast(w << 16, F32).astype(BF16)
    hi = pltpu.bitcast(w & jnp.uint32(0xFFFF0000), F32).astype(BF16)
    return lo, hi


def _post_kernel(da_ref, ma_ref, x_ref, wo_ref, g1_ref, b1_ref, wq_ref, km_ref, vm_ref, wmo_ref,
                 g2_ref, b2_ref, h_ref, hp_ref):
    n_d = DIFF_HEADS * DIFF_V_DIM
    mix = _dot(da_ref[...], wo_ref[0:n_d, :]) + _dot(ma_ref[...], wo_ref[n_d:, :])
    h1 = _layer_norm(DEEPNORM_ALPHA * x_ref[...] + mix, g1_ref[...], b1_ref[...])
    q = (_dot(h1.astype(BF16), wq_ref[...]) * (MEM_HEAD_DIM ** -0.5)).astype(BF16)
    outs = []
    for h in range(MEM_HEADS):
        sl = slice(h * MEM_HEAD_DIM, (h + 1) * MEM_HEAD_DIM)
        s = _dot_nt(q[:, sl], km_ref[:, sl])
        m = jnp.max(s, axis=1, keepdims=True)
        p = jnp.exp(s - m)
        p = p / jnp.sum(p, axis=1, keepdims=True)
        outs.append(_dot(p.astype(BF16), vm_ref[:, sl]))
    o = jnp.concatenate(outs, axis=1).astype(BF16)
    xat = _dot(o, wmo_ref[...])
    h2 = _layer_norm(DEEPNORM_ALPHA * h1 + xat, g2_ref[...], b2_ref[...])
    h_ref[...] = h2
    hp_ref[...] = _pack_bf16_pairs(h2)


def _post_call(diff_o, mla_o, x2, wo, g1, b1, wq, kmem, vmem, wmo, g2, b2, s, tm):
    t = x2.shape[0]
    per_b = s // tm
    row = lambda w: pl.BlockSpec((tm, w), lambda i: (i, 0))
    full = lambda a: pl.BlockSpec(a.shape, lambda i: (0, 0))
    memb = pl.BlockSpec((N_MEM, kmem.shape[1]), lambda i: (i // per_b, 0))
    return pl.pallas_call(
        _post_kernel,
        out_shape=[jax.ShapeDtypeStruct((t, D_MODEL), F32), jax.ShapeDtypeStruct((t, D_MODEL // 2), jnp.uint32)],
        grid=(t // tm,),
        in_specs=[row(diff_o.shape[1]), row(mla_o.shape[1]), row(D_MODEL), full(wo), full(g1), full(b1),
                  full(wq), memb, memb, full(wmo), full(g2), full(b2)],
        out_specs=[row(D_MODEL), row(D_MODEL // 2)],
        compiler_params=_cparams(("parallel",)),
        name="post_mix_xattn",
    )(diff_o, mla_o, x2, wo, g1, b1, wq, kmem, vmem, wmo, g2, b2)


def _router_kernel(h_ref, rwt_ref, bias_ref, tri_ref, eid_ref, rnk_ref, wt_ref, cnt_ref):
    tm = h_ref.shape[0]

    @pl.when(pl.program_id(0) == 0)
    def _():
        cnt_ref[...] = jnp.zeros(cnt_ref.shape, cnt_ref.dtype)

    h = h_ref[...]
    h_hi = h.astype(BF16)
    h_lo = (h - h_hi.astype(F32)).astype(BF16)
    w = rwt_ref[...]
    w_hi = w.astype(BF16)
    w_lo = (w - w_hi.astype(F32)).astype(BF16)
    both = _dot_nt(jnp.concatenate([w_hi, w_lo], axis=0), h_hi)
    logits = both[:N_EXPERTS] + (both[N_EXPERTS:] + _dot_nt(w_hi, h_lo))
    scores = jax.nn.sigmoid(logits)
    choice = scores + bias_ref[...]
    neg = float("-inf")

    c3 = choice.reshape(N_GROUPS, GROUP_SIZE, tm)
    mi = lax.broadcasted_iota(jnp.int32, c3.shape, 1)
    top1 = jnp.max(c3, axis=1, keepdims=True)
    first = jnp.min(jnp.where(c3 == top1, mi, GROUP_SIZE), axis=1, keepdims=True)
    top2 = jnp.max(jnp.where(mi == first, neg, c3), axis=1, keepdims=True)
    gs = (top1 + top2).reshape(N_GROUPS, tm)

    gi = lax.broadcasted_iota(jnp.int32, gs.shape, 0)
    rank = jnp.zeros(gs.shape, jnp.int32)
    for g in range(N_GROUPS):
        rowg = gs[g:g + 1, :]
        beats = (rowg > gs) | ((rowg == gs) & (g < gi))
        rank = rank + beats.astype(jnp.int32)
    gmask = rank < TOPK_GROUPS
    emask = jnp.broadcast_to(gmask.reshape(N_GROUPS, 1, tm), c3.shape).reshape(N_EXPERTS, tm)

    work = jnp.where(emask, choice, NEG_INF)
    ei = lax.broadcasted_iota(jnp.int32, work.shape, 0)
    sel = jnp.zeros(work.shape, jnp.bool_)
    picks = []
    for _ in range(TOP_K):
        mx = jnp.max(work, axis=0, keepdims=True)
        fi = jnp.min(jnp.where(work == mx, ei, N_EXPERTS), axis=0, keepdims=True)
        pick = ei == fi
        picks.append((fi, pick))
        sel = sel | pick
        work = jnp.where(pick, neg, work)
    top_w = jnp.where(sel, scores, 0.0)
    gates = top_w / jnp.sum(top_w, axis=0, keepdims=True) * ROUTED_SCALE

    sel_b = jnp.where(sel, 1.0, 0.0).astype(BF16)
    before = _dot(sel_b, tri_ref[...]).astype(jnp.int32)
    run = cnt_ref[:, 0:1]
    pos = run + before
    cnt_ref[...] = jnp.broadcast_to(run + jnp.sum(sel.astype(jnp.int32), axis=1, keepdims=True), cnt_ref.shape)

    eid_ref[...] = jnp.concatenate([fi for fi, _ in picks], axis=0)
    rnk_ref[...] = jnp.concatenate(
        [jnp.sum(jnp.where(pick, pos, 0), axis=0, keepdims=True) for _, pick in picks], axis=0)
    w_rows = [jnp.sum(jnp.where(pick, gates, 0.0), axis=0, keepdims=True) for _, pick in picks]
    w_pad = jnp.concatenate(w_rows + [jnp.zeros((V7X_LANES - TOP_K, tm), F32)], axis=0)
    wt_ref[...] = w_pad.T


def _router_call(h2, rwt, bias, tm):
    t = h2.shape[0]
    tri = jnp.triu(jnp.ones((tm, tm), BF16), k=1)
    full = lambda a: pl.BlockSpec(a.shape, lambda i: (0, 0))
    return pl.pallas_call(
        _router_kernel,
        out_shape=[jax.ShapeDtypeStruct((TOP_K, t), jnp.int32), jax.ShapeDtypeStruct((TOP_K, t), jnp.int32),
                   jax.ShapeDtypeStruct((t, V7X_LANES), F32), jax.ShapeDtypeStruct((N_EXPERTS, V7X_LANES), jnp.int32)],
        grid=(t // tm,),
        in_specs=[pl.BlockSpec((tm, D_MODEL), lambda i: (i, 0)), full(rwt), full(bias), full(tri)],
        out_specs=[pl.BlockSpec((TOP_K, tm), lambda i: (0, i)), pl.BlockSpec((TOP_K, tm), lambda i: (0, i)),
                   pl.BlockSpec((tm, V7X_LANES), lambda i: (i, 0)),
                   pl.BlockSpec((N_EXPERTS, V7X_LANES), lambda i: (0, 0))],
        compiler_params=_cparams(("arbitrary",)),
        name="router",
    )(h2, rwt, bias, tri)


_SC_MESH_AXES = ("core", "subcore")


def _sc_mesh():
    return plsc.VectorSubcoreMesh(core_axis_name=_SC_MESH_AXES[0], subcore_axis_name=_SC_MESH_AXES[1])


def _sc_worker_chunks(n_chunks):
    n_workers = V7X_SC_CORES * V7X_SC_SUBCORES
    assert n_chunks % n_workers == 0
    per_worker = n_chunks // n_workers
    first = (lax.axis_index(_SC_MESH_AXES[1]) * V7X_SC_CORES + lax.axis_index(_SC_MESH_AXES[0])) * per_worker
    return first, per_worker


def _sc_scatter_rows(x, idx, n_rows):
    t, w = x.shape
    n_chunks, n_k, window = idx.shape

    @functools.partial(pl.kernel, out_type=jax.ShapeDtypeStruct((n_rows, w), x.dtype), mesh=_sc_mesh(),
                       scratch_types=[pltpu.VMEM((n_k, window), jnp.int32), pltpu.VMEM((window, w), x.dtype)],
                       name="moe_dispatch_sc")
    def scatter(x_hbm, i_hbm, o_hbm, idx_v, rows_v):
        first, per_worker = _sc_worker_chunks(n_chunks)

        @pl.loop(0, per_worker)
        def _(c):
            chunk = first + c
            pltpu.sync_copy(i_hbm.at[chunk], idx_v)
            pltpu.sync_copy(x_hbm.at[pl.ds(chunk * window, window)], rows_v)
            for k in range(n_k):
                pltpu.sync_copy(rows_v, o_hbm.at[idx_v.at[k]])

    return scatter(x, idx)


def _sc_gather_rows(y, idx):
    n_chunks, n_k, window = idx.shape
    w = y.shape[1]

    half = window // 2
    steps = [(k, h) for k in range(n_k) for h in range(2)]

    @functools.partial(pl.kernel, out_type=jax.ShapeDtypeStruct((n_k, n_chunks * window, w), y.dtype),
                       mesh=_sc_mesh(),
                       scratch_types=[pltpu.VMEM((n_k, window), jnp.int32), pltpu.VMEM((half, w), y.dtype),
                                      pltpu.VMEM((half, w), y.dtype), pltpu.SemaphoreType.DMA,
                                      pltpu.SemaphoreType.DMA],
                       name="moe_combine_sc")
    def gather(y_hbm, i_hbm, o_hbm, idx_v, rows_a, rows_b, sem_a, sem_b):
        first, per_worker = _sc_worker_chunks(n_chunks)
        bufs = ((rows_a, sem_a), (rows_b, sem_b))

        @pl.loop(0, per_worker)
        def _(c):
            chunk = first + c
            pltpu.sync_copy(i_hbm.at[chunk], idx_v)

            def start(j):
                k, h = steps[j]
                buf, sem = bufs[j % 2]
                return pltpu.async_copy(y_hbm.at[idx_v.at[k, pl.ds(h * half, half)]], buf, sem)

            pending = start(0)
            for j, (k, h) in enumerate(steps):
                following = start(j + 1) if j + 1 < len(steps) else None
                pending.wait()
                pltpu.sync_copy(bufs[j % 2][0], o_hbm.at[k, pl.ds(chunk * window + h * half, half)])
                pending = following

    return gather(y, idx)


def _experts_kernel(te_ref, nu_ref, xs_ref, wg_ref, wu_ref, wd_ref, y_ref, wg_b, wu_b, wd_b):
    i = pl.program_id(0)

    @pl.when(i < nu_ref[0])
    def _():
        @pl.when((i == 0) | (te_ref[i] != te_ref[jnp.maximum(i - 1, 0)]))
        def _():
            wg_b[...] = wg_ref[0].astype(BF16)
            wu_b[...] = wu_ref[0].astype(BF16)
            wd_b[...] = wd_ref[0].astype(BF16)

        half = D_MODEL // 2
        lo, hi = _unpack_bf16_pairs(xs_ref[...])
        a = _dot(lo, wg_b[:half, :]) + _dot(hi, wg_b[half:, :])
        u = _dot(lo, wu_b[:half, :]) + _dot(hi, wu_b[half:, :])
        y_ref[...] = _pack_bf16_pairs(_dot((jax.nn.silu(a) * u).astype(BF16), wd_b[...]))


def _experts_call(tile_expert, n_used, xs, wg, wu, wd, tmx):
    n_rows, w = xs.shape
    tile = lambda i, te, nu: jnp.minimum(i, nu[0] - 1)
    wspec = lambda shp: pl.BlockSpec((1,) + shp, lambda i, te, nu: (te[tile(i, te, nu)], 0, 0))
    return pl.pallas_call(
        _experts_kernel,
        out_shape=jax.ShapeDtypeStruct((n_rows, w), xs.dtype),
        grid_spec=pltpu.PrefetchScalarGridSpec(
            num_scalar_prefetch=2,
            grid=(n_rows // tmx,),
            in_specs=[pl.BlockSpec((tmx, w), lambda i, te, nu: (tile(i, te, nu), 0)),
                      wspec((D_MODEL, EXPERT_DIM)), wspec((D_MODEL, EXPERT_DIM)), wspec((EXPERT_DIM, D_MODEL))],
            out_specs=pl.BlockSpec((tmx, w), lambda i, te, nu: (tile(i, te, nu), 0)),
            scratch_shapes=[pltpu.VMEM((D_MODEL, EXPERT_DIM), BF16), pltpu.VMEM((D_MODEL, EXPERT_DIM), BF16),
                            pltpu.VMEM((EXPERT_DIM, D_MODEL), BF16)],
        ),
        compiler_params=_cparams(("arbitrary",)),
        name="moe_experts",
    )(tile_expert, n_used, xs, wg, wu, wd)


def _combine_kernel(wt_ref, h_ref, yg_ref, sg_ref, su_ref, sd_ref, g3_ref, b3_ref, o_ref):
    h = h_ref[...]
    hb = h.astype(BF16)
    acc = _dot((jax.nn.silu(_dot(hb, sg_ref[...])) * _dot(hb, su_ref[...])).astype(BF16), sd_ref[...])
    wt = wt_ref[...]
    lo_sum = None
    hi_sum = None
    for k in range(TOP_K):
        lo, hi = _unpack_bf16_pairs(yg_ref[k])
        wk = wt[:, k:k + 1]
        lo_sum = wk * lo if lo_sum is None else lo_sum + wk * lo
        hi_sum = wk * hi if hi_sum is None else hi_sum + wk * hi
    acc = acc + jnp.concatenate([lo_sum, hi_sum], axis=1)
    o_ref[...] = _layer_norm(DEEPNORM_ALPHA * h + acc, g3_ref[...], b3_ref[...])


def _combine_call(wt, h2, yg, sg, su, sd, g3, b3, tm):
    t = h2.shape[0]
    full = lambda a: pl.BlockSpec(a.shape, lambda i: (0, 0))
    return pl.pallas_call(
        _combine_kernel,
        out_shape=jax.ShapeDtypeStruct((t, D_MODEL), F32),
        grid=(t // tm,),
        in_specs=[pl.BlockSpec((tm, V7X_LANES), lambda i: (i, 0)),
                  pl.BlockSpec((tm, D_MODEL), lambda i: (i, 0)),
                  pl.BlockSpec((TOP_K, tm, yg.shape[2]), lambda i: (0, i, 0)),
                  full(sg), full(su), full(sd), full(g3), full(b3)],
        out_specs=pl.BlockSpec((tm, D_MODEL), lambda i: (i, 0)),
        compiler_params=_cparams(("parallel",)),
        name="moe_combine",
    )(wt, h2, yg, sg, su, sd, g3, b3)


def _moe(h2, hp, router_w, router_bias, wg, wu, wd, sg, su, sd, g3, b3, tm_route, tmx, tm_comb):
    t = h2.shape[0]
    eid, rnk, wt, cnt = _router_call(h2, router_w.T, router_bias.reshape(-1, 1), tm_route)
    counts = cnt[:, 0]
    tiles_per_e = (counts + (tmx - 1)) // tmx
    tile_end = jnp.cumsum(tiles_per_e)
    row_start = ((tile_end - tiles_per_e) * tmx).astype(jnp.int32)
    n_tiles = (t * TOP_K) // tmx + N_EXPERTS
    n_used = tile_end[-1:].astype(jnp.int32)
    tile_expert = jnp.minimum(
        jnp.sum(tile_end[None, :] <= jnp.arange(n_tiles, dtype=jnp.int32)[:, None], axis=1), N_EXPERTS - 1
    ).astype(jnp.int32)
    expert_ids = jnp.arange(N_EXPERTS, dtype=jnp.int32)[:, None, None]
    dest = rnk + jnp.sum(jnp.where(eid[None] == expert_ids, row_start[:, None, None], 0), axis=0)
    dest_chunks = dest.reshape(TOP_K, t // SC_INDEX_WINDOW, SC_INDEX_WINDOW).transpose(1, 0, 2)
    xs = _sc_scatter_rows(hp, dest_chunks, n_tiles * tmx)
    yp = _experts_call(tile_expert, n_used, xs, wg, wu, wd, tmx)
    yg = _sc_gather_rows(yp, dest_chunks)
    return _combine_call(wt, h2, yg, sg, su, sd, g3, b3, tm_comb)


def _prep_w_in_t(w_in):
    pad = jnp.zeros((D_MODEL, V7X_LANES - MLA_ROPE_DIM), w_in.dtype)
    return jnp.concatenate([w_in, pad], axis=1).T.astype(BF16)


def _prep_w_uq_t(w_uq):
    w = w_uq.reshape(MLA_Q_RANK, MLA_HEADS, MLA_NOPE_DIM + MLA_ROPE_DIM)
    pad = jnp.zeros((MLA_Q_RANK, MLA_HEADS, MLA_QK_PAD - MLA_NOPE_DIM - MLA_ROPE_DIM), w.dtype)
    return jnp.concatenate([w, pad], axis=2).reshape(MLA_Q_RANK, MLA_HEADS * MLA_QK_PAD).T.astype(BF16)


def _prep_w_ukv_t(w_ukv):
    w = w_ukv.reshape(MLA_KV_RANK, MLA_HEADS, MLA_NOPE_DIM + MLA_V_DIM)
    k = w[:, :, :MLA_NOPE_DIM].reshape(MLA_KV_RANK, MLA_HEADS * MLA_NOPE_DIM)
    v = w[:, :, MLA_NOPE_DIM:].reshape(MLA_KV_RANK, MLA_HEADS * MLA_V_DIM)
    return jnp.concatenate([k, v], axis=1).T.astype(BF16)


def _rope_inv_freq(dim):
    return (ROPE_THETA ** (-jnp.arange(0, dim, 2, dtype=F32) / dim)).reshape(-1, 1)


def _layer(x2, mem2, pos_row, b, s, l, w_in, lam_q1, lam_k1, lam_q2, lam_k2, diff_subln_g, mla_q_norm_g,
           mla_w_uq, mla_kv_norm_g, mla_w_ukv, mla_out_norm_g, w_o, ln1_g, ln1_b, mem_w_q, mem_w_k, mem_w_v,
           mem_w_o, ln2_g, ln2_b, router_w, router_bias, exp_w_gate, exp_w_up, exp_w_down, sh_w_gate,
           sh_w_up, sh_w_down, ln3_g, ln3_b):
    row = lambda a: a.reshape(1, -1)
    col = lambda a: a.reshape(-1, 1)
    dqt, dk, dvt, mqt, mk, mvt = _proj_call(
        x2, pos_row, _rope_inv_freq(DIFF_ROT_DIM), _rope_inv_freq(MLA_ROPE_DIM), _prep_w_in_t(w_in),
        col(mla_q_norm_g), _prep_w_uq_t(mla_w_uq), col(mla_kv_norm_g), _prep_w_ukv_t(mla_w_ukv), tm=min(1024, s))

    lam_init = 0.8 - 0.6 * math.exp(-0.3 * l)
    lam_vecs = jnp.stack([lam_q1, lam_k1, lam_q2, lam_k2]).astype(F32)
    tq = min(1024, s)
    diff_o = _attn_call(
        functools.partial(_diff_attn_kernel, tq=tq, lam_init=lam_init), "diff_attn",
        dqt, dk, dvt, [lam_vecs, col(diff_subln_g)], b, s, DIFF_HEADS, 2 * DIFF_QK_DIM, DIFF_V_DIM, tq, 2)
    tq_m = min(1024, s)
    mla_o = _attn_call(
        functools.partial(_mla_attn_kernel, tq=tq_m), "mla_attn",
        mqt, mk, mvt, [col(mla_out_norm_g)], b, s, MLA_HEADS, MLA_QK_PAD, MLA_V_DIM, tq_m, 1)

    kmem, vmem = _memkv_call(mem2, mem_w_k.astype(BF16), mem_w_v.astype(BF16))
    h2, hp = _post_call(diff_o, mla_o, x2, w_o.astype(BF16), row(ln1_g), row(ln1_b), mem_w_q.astype(BF16),
                        kmem, vmem, mem_w_o.astype(BF16), row(ln2_g), row(ln2_b), s, tm=min(1024, s))

    return _moe(h2, hp, router_w, router_bias, exp_w_gate, exp_w_up, exp_w_down,
                sh_w_gate.astype(BF16), sh_w_up.astype(BF16), sh_w_down.astype(BF16),
                row(ln3_g), row(ln3_b), tm_route=min(1024, s), tmx=1024, tm_comb=min(512, s))


def kernel(x, mem, positions, w_in, lam_q1, lam_k1, lam_q2, lam_k2, diff_subln_g, mla_q_norm_g, mla_w_uq,
           mla_kv_norm_g, mla_w_ukv, mla_out_norm_g, w_o, ln1_g, ln1_b, mem_w_q, mem_w_k, mem_w_v, mem_w_o,
           ln2_g, ln2_b, router_w, router_bias, exp_w_gate, exp_w_up, exp_w_down, sh_w_gate, sh_w_up,
           sh_w_down, ln3_g, ln3_b):
    b, s, d = x.shape
    h = x.reshape(b * s, d)
    mem2 = mem.reshape(b * mem.shape[1], d)
    pos_row = positions.reshape(1, b * s)
    params = (w_in, lam_q1, lam_k1, lam_q2, lam_k2, diff_subln_g, mla_q_norm_g, mla_w_uq, mla_kv_norm_g,
              mla_w_ukv, mla_out_norm_g, w_o, ln1_g, ln1_b, mem_w_q, mem_w_k, mem_w_v, mem_w_o, ln2_g, ln2_b,
              router_w, router_bias, exp_w_gate, exp_w_up, exp_w_down, sh_w_gate, sh_w_up, sh_w_down,
              ln3_g, ln3_b)
    for l in range(w_in.shape[0]):
        h = _layer(h, mem2, pos_row, b, s, l, *[p[l] for p in params])
    return h.reshape(b, s, d)
```

```python
import functools
import math

import jax
import jax.numpy as jnp
from jax import lax
from jax.experimental import pallas as pl
from jax.experimental.pallas import tpu as pltpu
from jax.experimental.pallas import tpu_sc as plsc

F32 = jnp.float32
BF16 = jnp.bfloat16

D_MODEL = 1024
N_MEM = 256
ROPE_THETA = 500000.0
NEG_INF = -1e30
LN_EPS = 1e-5
RMS_EPS = 1e-6
DIFF_HEADS = 4
DIFF_QK_DIM = 64
DIFF_V_DIM = 128
DIFF_ROT_DIM = DIFF_QK_DIM // 4
MLA_HEADS = 4
MLA_Q_RANK = 256
MLA_KV_RANK = 128
MLA_NOPE_DIM = 128
MLA_ROPE_DIM = 64
MLA_V_DIM = 128
MLA_QK_PAD = 256
V_AUG_DIM = 128 + 16
MEM_HEADS = 4
MEM_HEAD_DIM = 128
N_EXPERTS = 64
TOP_K = 8
N_GROUPS = 8
TOPK_GROUPS = 4
GROUP_SIZE = N_EXPERTS // N_GROUPS
EXPERT_DIM = 256
SHARED_DIM = 256
ROUTED_SCALE = 2.5
MOE_TOKEN_GROUPS = 2
DEPTH = 1
DEEPNORM_ALPHA = (2.0 * DEPTH) ** 0.25
LOG2E = math.log2(math.e)

V7X_LANES = 128
V7X_SUBLANES = 8
V7X_VMEM_LIMIT_BYTES = 56 * 1024 * 1024
V7X_SC_CORES = 2
V7X_SC_SUBCORES = 16
SC_INDEX_WINDOW = 128
V7X_BF16_SUBLANES = 16

N_DQ = DIFF_HEADS * 2 * DIFF_QK_DIM
N_DV = DIFF_HEADS * DIFF_V_DIM
O_DK = N_DQ
O_DV = 2 * N_DQ
O_CQ = O_DV + N_DV
O_CKV = O_CQ + MLA_Q_RANK
O_KR = O_CKV + MLA_KV_RANK
N_IN_PAD = O_KR + V7X_LANES


def _cparams(sem):
    return pltpu.CompilerParams(dimension_semantics=sem, vmem_limit_bytes=V7X_VMEM_LIMIT_BYTES)


def _dot(a, b):
    return jnp.dot(a, b, preferred_element_type=F32)


def _dot_nt(a, b):
    return lax.dot_general(a, b, (((1,), (1,)), ((), ())), preferred_element_type=F32)


def _rms(x, g):
    return x * lax.rsqrt(jnp.mean(jnp.square(x), axis=-1, keepdims=True) + RMS_EPS) * g


def _rms_rows(x, g):
    return x * lax.rsqrt(jnp.mean(jnp.square(x), axis=0, keepdims=True) + RMS_EPS) * g


def _layer_norm(x, g, b):
    mu = jnp.mean(x, axis=-1, keepdims=True)
    xc = x - mu
    var = jnp.mean(jnp.square(xc), axis=-1, keepdims=True)
    return xc * lax.rsqrt(var + LN_EPS) * g + b


def _with_ones_rows(vt, heads):
    dv = vt.shape[0] // heads
    ones = jnp.ones((V_AUG_DIM - dv, vt.shape[1]), vt.dtype)
    return jnp.concatenate([p for h in range(heads) for p in (vt[h * dv:(h + 1) * dv], ones)], axis=0)


def _rope_rows(x, cos, sin, period):
    h = cos.shape[0]
    pieces = []
    for base in range(0, x.shape[0], period):
        x1 = x[base:base + h]
        x2 = x[base + h:base + 2 * h]
        pieces.append(x1 * cos - x2 * sin)
        pieces.append(x2 * cos + x1 * sin)
        if period > 2 * h:
            pieces.append(x[base + 2 * h:base + period])
    return jnp.concatenate(pieces, axis=0)


def _proj_kernel(x_ref, pos_ref, fd_ref, fm_ref, w_in_ref, gq_ref, wuq_ref, gkv_ref, wukv_ref,
                 dq_ref, dk_ref, dv_ref, mq_ref, mk_ref, mv_ref):
    xb = x_ref[...].astype(BF16)
    pos = pos_ref[...].astype(F32)
    ang_d = fd_ref[...] * pos
    ang_m = fm_ref[...] * pos
    cos_d, sin_d = jnp.cos(ang_d), jnp.sin(ang_d)
    cos_m, sin_m = jnp.cos(ang_m), jnp.sin(ang_m)

    dq = _rope_rows(_dot_nt(w_in_ref[0:N_DQ, :], xb), cos_d, sin_d, DIFF_QK_DIM)
    dq_ref[...] = (dq * (DIFF_QK_DIM ** -0.5 * LOG2E)).astype(BF16)
    dk = _rope_rows(_dot_nt(w_in_ref[O_DK:O_DK + N_DQ, :], xb), cos_d, sin_d, DIFF_QK_DIM)
    dk_ref[...] = dk.T.astype(BF16)
    dv_ref[...] = _with_ones_rows(_dot_nt(w_in_ref[O_DV:O_DV + N_DV, :], xb), DIFF_HEADS).astype(BF16)

    c_q = _dot_nt(w_in_ref[O_CQ:O_CQ + MLA_Q_RANK, :], xb)
    q = _dot(wuq_ref[...], _rms_rows(c_q, gq_ref[...]).astype(BF16))
    q = q * ((MLA_NOPE_DIM + MLA_ROPE_DIM) ** -0.5 * LOG2E)
    pieces = []
    for h in range(MLA_HEADS):
        o = h * MLA_QK_PAD
        pieces.append(q[o:o + MLA_NOPE_DIM])
        pieces.append(_rope_rows(q[o + MLA_NOPE_DIM:o + MLA_NOPE_DIM + MLA_ROPE_DIM], cos_m, sin_m, MLA_ROPE_DIM))
        pieces.append(q[o + MLA_NOPE_DIM + MLA_ROPE_DIM:o + MLA_QK_PAD])
    mq_ref[...] = jnp.concatenate(pieces, axis=0).astype(BF16)

    c_kv = _dot_nt(w_in_ref[O_CKV:O_CKV + MLA_KV_RANK, :], xb)
    kv = _dot(wukv_ref[...], _rms_rows(c_kv, gkv_ref[...]).astype(BF16))
    n_kn = MLA_HEADS * MLA_NOPE_DIM
    mv_ref[...] = _with_ones_rows(kv[n_kn:], MLA_HEADS).astype(BF16)
    k_rope = _dot_nt(w_in_ref[O_KR:O_KR + V7X_LANES, :], xb)
    k_pe = jnp.concatenate([_rope_rows(k_rope[:MLA_ROPE_DIM], cos_m, sin_m, MLA_ROPE_DIM),
                            k_rope[MLA_ROPE_DIM:]], axis=0)
    k_nope_t = kv[:n_kn].T.astype(BF16)
    k_pe_t = k_pe.T.astype(BF16)
    for h in range(MLA_HEADS):
        o = h * MLA_QK_PAD
        mk_ref[:, o:o + MLA_NOPE_DIM] = k_nope_t[:, h * MLA_NOPE_DIM:(h + 1) * MLA_NOPE_DIM]
        mk_ref[:, o + MLA_NOPE_DIM:o + MLA_QK_PAD] = k_pe_t


def _proj_call(x2, pos_row, fd, fm, w_in_t, gq, wuq_t, gkv, wukv_t, tm):
    t = x2.shape[0]
    full = lambda a: pl.BlockSpec(a.shape, lambda i: (0, 0))
    col_blk = lambda r: pl.BlockSpec((r, tm), lambda i: (0, i))
    row_blk = lambda w: pl.BlockSpec((tm, w), lambda i: (i, 0))
    n_mq = MLA_HEADS * MLA_QK_PAD
    return pl.pallas_call(
        _proj_kernel,
        out_shape=[jax.ShapeDtypeStruct((N_DQ, t), BF16), jax.ShapeDtypeStruct((t, N_DQ), BF16),
                   jax.ShapeDtypeStruct((DIFF_HEADS * V_AUG_DIM, t), BF16), jax.ShapeDtypeStruct((n_mq, t), BF16),
                   jax.ShapeDtypeStruct((t, n_mq), BF16), jax.ShapeDtypeStruct((MLA_HEADS * V_AUG_DIM, t), BF16)],
        grid=(t // tm,),
        in_specs=[row_blk(D_MODEL), col_blk(1), full(fd), full(fm), full(w_in_t), full(gq), full(wuq_t),
                  full(gkv), full(wukv_t)],
        out_specs=[col_blk(N_DQ), row_blk(N_DQ), col_blk(DIFF_HEADS * V_AUG_DIM), col_blk(n_mq), row_blk(n_mq),
                   col_blk(MLA_HEADS * V_AUG_DIM)],
        compiler_params=_cparams(("parallel",)),
        name="proj_rope",
    )(x2, pos_row, fd, fm, w_in_t, gq, wuq_t, gkv, wukv_t)


def _flash_t(qq, k_ref, vt_ref, s_refs, p_refs, acc_ref, m_ref, i, tq):
    n = qq.shape[1]
    tk = s_refs[0].shape[0]
    assert tq == 2 * tk
    q_pos = i * tq + (lax.broadcasted_iota(jnp.int32, (tk, n), 1) & (tq - 1))
    key_row = lax.broadcasted_iota(jnp.int32, (tk, n), 0)

    def scores(c, s_ref):
        off = pl.multiple_of(c * tk, tk)
        s = _dot(k_ref[pl.ds(off, tk), :], qq)
        s_ref[...] = s
        return jnp.max(s, axis=0, keepdims=True)

    def softmax(c, s_ref, p_ref, cmax, m, masked):
        s = s_ref[...]
        if masked:
            s = jnp.where(key_row + c * tk <= q_pos, s, NEG_INF)
            cmax = jnp.max(s, axis=0, keepdims=True)
        m_new = jnp.maximum(m, cmax)
        alpha = jnp.exp2(m - m_new)
        p_ref[...] = jnp.exp2((s - m_new).astype(BF16))
        return m_new, alpha

    def values(c, p_ref, alpha):
        off = pl.multiple_of(jnp.maximum(c, 0) * tk, tk)
        acc_ref[...] = alpha * acc_ref[...] + _dot(vt_ref[:, pl.ds(off, tk)], p_ref[...])

    late = [slice(a + tq // 2, a + tq) for a in range(0, n, tq)]

    def scores_late(c, s_ref):
        off = pl.multiple_of(c * tk, tk)
        kc = k_ref[pl.ds(off, tk), :]
        for sl in late:
            s_ref[:, sl] = _dot(kc, qq[:, sl])

    def softmax_late(c, s_ref, p_ref, m):
        m_ref[...] = m
        alphas = []
        for sl in late:
            shape = (tk, sl.stop - sl.start)
            key_pos = lax.broadcasted_iota(jnp.int32, shape, 0) + c * tk
            query_pos = i * tq + ((lax.broadcasted_iota(jnp.int32, shape, 1) + sl.start) & (tq - 1))
            s = jnp.where(key_pos <= query_pos, s_ref[:, sl], NEG_INF)
            m_old = m_ref[:, sl]
            m_new = jnp.maximum(m_old, jnp.max(s, axis=0, keepdims=True))
            alphas.append(jnp.exp2(m_old - m_new))
            p_ref[:, sl] = jnp.exp2((s - m_new).astype(BF16))
        return alphas

    def values_late(c, p_ref, alphas):
        off = pl.multiple_of(c * tk, tk)
        vc = vt_ref[:, pl.ds(off, tk)]
        for sl, alpha in zip(late, alphas):
            acc_ref[:, sl] = alpha * acc_ref[:, sl] + _dot(vc, p_ref[:, sl])

    def pair(u, carry, masked_second):
        cm1, alpha0, m = carry
        c = 2 * u
        cm0 = scores(c, s_refs[0])
        values(c - 2, p_refs[0], alpha0)
        m, alpha1 = softmax(c - 1, s_refs[1], p_refs[1], cm1, m, False)
        if masked_second:
            scores_late(c + 1, s_refs[1])
        else:
            cm1 = scores(c + 1, s_refs[1])
        values(c - 1, p_refs[1], alpha1)
        m, alpha0 = softmax(c, s_refs[0], p_refs[0], cm0, m, masked_second)
        return cm1, alpha0, m

    acc_ref[...] = jnp.zeros(acc_ref.shape, F32)
    scores(0, s_refs[0])
    cm1 = scores(1, s_refs[1])
    m, alpha0 = softmax(0, s_refs[0], p_refs[0], None, jnp.full((1, n), NEG_INF, F32), True)
    carry = lax.fori_loop(1, i, functools.partial(pair, masked_second=False), (cm1, alpha0, m))
    cm1, alpha0, m = lax.fori_loop(jnp.maximum(i, 1), i + 1, functools.partial(pair, masked_second=True), carry)
    values(2 * i, p_refs[0], alpha0)
    values_late(2 * i + 1, p_refs[1], softmax_late(2 * i + 1, s_refs[1], p_refs[1], m))
    dv = vt_ref.shape[0] - V7X_BF16_SUBLANES
    return acc_ref[0:dv, :] / acc_ref[dv:dv + 1, :]


def _diff_attn_kernel(qt_ref, k_ref, vt_ref, lam_ref, g_ref, o_ref, s0, s1, p0, p1, acc_ref, m_ref, *, tq,
                      lam_init):
    i = pl.program_id(2)
    qt = qt_ref[...]
    row = lax.broadcasted_iota(jnp.int32, qt.shape, 0)
    zero = jnp.zeros_like(qt)
    qq = jnp.concatenate([jnp.where(row < DIFF_QK_DIM, qt, zero), jnp.where(row >= DIFF_QK_DIM, qt, zero)], axis=1)
    o = _flash_t(qq, k_ref, vt_ref, (s0, s1), (p0, p1), acc_ref, m_ref, i, tq)
    lv = lam_ref[...]
    lam = (jnp.exp(jnp.sum(lv[0:1] * lv[1:2], axis=1, keepdims=True))
           - jnp.exp(jnp.sum(lv[2:3] * lv[3:4], axis=1, keepdims=True)) + lam_init)
    d = o[:, :tq] - lam * o[:, tq:]
    o_ref[...] = (_rms_rows(d, g_ref[...]) * (1.0 - lam_init)).T.astype(o_ref.dtype)


def _mla_attn_kernel(qt_ref, k_ref, vt_ref, g_ref, o_ref, s0, s1, p0, p1, acc_ref, m_ref, *, tq):
    i = pl.program_id(2)
    o = _flash_t(qt_ref[...], k_ref, vt_ref, (s0, s1), (p0, p1), acc_ref, m_ref, i, tq)
    o_ref[...] = _rms_rows(o, g_ref[...]).T.astype(o_ref.dtype)


def _attn_call(kernel, name, qt, k, vt, extras, b, s, heads, dk_w, dv_w, tq, n_maps):
    nq = s // tq
    t = b * s
    tk = tq // 2
    n = n_maps * tq
    small = [pl.BlockSpec(a.shape, lambda bi, h, i: (0, 0)) for a in extras]
    return pl.pallas_call(
        kernel,
        out_shape=jax.ShapeDtypeStruct((t, heads * dv_w), BF16),
        grid=(b, heads, nq),
        in_specs=[pl.BlockSpec((dk_w, tq), lambda bi, h, i: (h, bi * nq + i)),
                  pl.BlockSpec((s, dk_w), lambda bi, h, i: (bi, h)),
                  pl.BlockSpec((V_AUG_DIM, s), lambda bi, h, i: (h, bi))] + small,
        out_specs=pl.BlockSpec((tq, dv_w), lambda bi, h, i: (bi * nq + i, h)),
        scratch_shapes=[pltpu.VMEM((tk, n), F32), pltpu.VMEM((tk, n), F32),
                        pltpu.VMEM((tk, n), BF16), pltpu.VMEM((tk, n), BF16), pltpu.VMEM((V_AUG_DIM, n), F32),
                        pltpu.VMEM((1, n), F32)],
        compiler_params=_cparams(("parallel", "parallel", "arbitrary")),
        name=name,
    )(qt, k, vt, *extras)


def _memkv_kernel(mem_ref, wk_ref, wv_ref, k_ref, v_ref):
    mb = mem_ref[...].astype(BF16)
    k_ref[...] = _dot(mb, wk_ref[...]).astype(BF16)
    v_ref[...] = _dot(mb, wv_ref[...]).astype(BF16)


def _memkv_call(mem2, wk, wv):
    n = mem2.shape[0]
    w = wk.shape[1]
    tm = N_MEM
    return pl.pallas_call(
        _memkv_kernel,
        out_shape=[jax.ShapeDtypeStruct((n, w), BF16)] * 2,
        grid=(n // tm,),
        in_specs=[pl.BlockSpec((tm, D_MODEL), lambda i: (i, 0)),
                  pl.BlockSpec(wk.shape, lambda i: (0, 0)),
                  pl.BlockSpec(wv.shape, lambda i: (0, 0))],
        out_specs=[pl.BlockSpec((tm, w), lambda i: (i, 0))] * 2,
        compiler_params=_cparams(("parallel",)),
        name="mem_kv",
    )(mem2, wk, wv)


def _pack_bf16_pairs(x):
    half = x.shape[1] // 2
    r = pltpu.bitcast(x.astype(BF16).astype(F32), jnp.uint32)
    return (r[:, :half] >> 16) | (r[:, half:] & jnp.uint32(0xFFFF0000))


def _unpack_bf16_pairs(w):
    lo = pltpu.bitcast(w << 16, F32).astype(BF16)
    hi = pltpu.bitcast(w & jnp.uint32(0xFFFF0000), F32).astype(BF16)
    return lo, hi


def _post_kernel(da_ref, ma_ref, x_ref, wo_ref, g1_ref, b1_ref, wq_ref, km_ref, vm_ref, wmo_ref,
                 g2_ref, b2_ref, h_ref, hp_ref):
    n_d = DIFF_HEADS * DIFF_V_DIM
    mix = _dot(da_ref[...], wo_ref[0:n_d, :]) + _dot(ma_ref[...], wo_ref[n_d:, :])
    h1 = _layer_norm(DEEPNORM_ALPHA * x_ref[...] + mix, g1_ref[...], b1_ref[...])
    q = (_dot(h1.astype(BF16), wq_ref[...]) * (MEM_HEAD_DIM ** -0.5)).astype(BF16)
    outs = []
    for h in range(MEM_HEADS):
        sl = slice(h * MEM_HEAD_DIM, (h + 1) * MEM_HEAD_DIM)
        s = _dot_nt(q[:, sl], km_ref[:, sl])
        m = jnp.max(s, axis=1, keepdims=True)
        p = jnp.exp(s - m)
        p = p / jnp.sum(p, axis=1, keepdims=True)
        outs.append(_dot(p.astype(BF16), vm_ref[:, sl]))
    o = jnp.concatenate(outs, axis=1).astype(BF16)
    xat = _dot(o, wmo_ref[...])
    h2 = _layer_norm(DEEPNORM_ALPHA * h1 + xat, g2_ref[...], b2_ref[...])
    h_ref[...] = h2
    hp_ref[...] = _pack_bf16_pairs(h2)


def _post_call(diff_o, mla_o, x2, wo, g1, b1, wq, kmem, vmem, wmo, g2, b2, s, tm):
    t = x2.shape[0]
    per_b = s // tm
    row = lambda w: pl.BlockSpec((tm, w), lambda i: (i, 0))
    full = lambda a: pl.BlockSpec(a.shape, lambda i: (0, 0))
    memb = pl.BlockSpec((N_MEM, kmem.shape[1]), lambda i: (i // per_b, 0))
    return pl.pallas_call(
        _post_kernel,
        out_shape=[jax.ShapeDtypeStruct((t, D_MODEL), F32), jax.ShapeDtypeStruct((t, D_MODEL // 2), jnp.uint32)],
        grid=(t // tm,),
        in_specs=[row(diff_o.shape[1]), row(mla_o.shape[1]), row(D_MODEL), full(wo), full(g1), full(b1),
                  full(wq), memb, memb, full(wmo), full(g2), full(b2)],
        out_specs=[row(D_MODEL), row(D_MODEL // 2)],
        compiler_params=_cparams(("parallel",)),
        name="post_mix_xattn",
    )(diff_o, mla_o, x2, wo, g1, b1, wq, kmem, vmem, wmo, g2, b2)


def _router_kernel(h_ref, rwt_ref, bias_ref, tri_ref, eid_ref, rnk_ref, wt_ref, cnt_ref):
    tm = h_ref.shape[0]

    @pl.when(pl.program_id(0) == 0)
    def _():
        cnt_ref[...] = jnp.zeros(cnt_ref.shape, cnt_ref.dtype)

    h = h_ref[...]
    h_hi = h.astype(BF16)
    h_lo = (h - h_hi.astype(F32)).astype(BF16)
    w = rwt_ref[...]
    w_hi = w.astype(BF16)
    w_lo = (w - w_hi.astype(F32)).astype(BF16)
    both = _dot_nt(jnp.concatenate([w_hi, w_lo], axis=0), h_hi)
    logits = both[:N_EXPERTS] + (both[N_EXPERTS:] + _dot_nt(w_hi, h_lo))
    scores = jax.nn.sigmoid(logits)
    choice = scores + bias_ref[...]
    neg = float("-inf")

    c3 = choice.reshape(N_GROUPS, GROUP_SIZE, tm)
    mi = lax.broadcasted_iota(jnp.int32, c3.shape, 1)
    top1 = jnp.max(c3, axis=1, keepdims=True)
    first = jnp.min(jnp.where(c3 == top1, mi, GROUP_SIZE), axis=1, keepdims=True)
    top2 = jnp.max(jnp.where(mi == first, neg, c3), axis=1, keepdims=True)
    gs = (top1 + top2).reshape(N_GROUPS, tm)

    gi = lax.broadcasted_iota(jnp.int32, gs.shape, 0)
    rank = jnp.zeros(gs.shape, jnp.int32)
    for g in range(N_GROUPS):
        rowg = gs[g:g + 1, :]
        beats = (rowg > gs) | ((rowg == gs) & (g < gi))
        rank = rank + beats.astype(jnp.int32)
    gmask = rank < TOPK_GROUPS
    emask = jnp.broadcast_to(gmask.reshape(N_GROUPS, 1, tm), c3.shape).reshape(N_EXPERTS, tm)

    work = jnp.where(emask, choice, NEG_INF)
    ei = lax.broadcasted_iota(jnp.int32, work.shape, 0)
    sel = jnp.zeros(work.shape, jnp.bool_)
    picks = []
    for _ in range(TOP_K):
        mx = jnp.max(work, axis=0, keepdims=True)
        fi = jnp.min(jnp.where(work == mx, ei, N_EXPERTS), axis=0, keepdims=True)
        pick = ei == fi
        picks.append((fi, pick))
        sel = sel | pick
        work = jnp.where(pick, neg, work)
    top_w = jnp.where(sel, scores, 0.0)
    gates = top_w / jnp.sum(top_w, axis=0, keepdims=True) * ROUTED_SCALE

    sel_b = jnp.where(sel, 1.0, 0.0).astype(BF16)
    before = _dot(sel_b, tri_ref[...]).astype(jnp.int32)
    run = cnt_ref[:, 0:1]
    pos = run + before
    cnt_ref[...] = jnp.broadcast_to(run + jnp.sum(sel.astype(jnp.int32), axis=1, keepdims=True), cnt_ref.shape)

    eid_ref[...] = jnp.concatenate([fi for fi, _ in picks], axis=0)
    rnk_ref[...] = jnp.concatenate(
        [jnp.sum(jnp.where(pick, pos, 0), axis=0, keepdims=True) for _, pick in picks], axis=0)
    w_rows = [jnp.sum(jnp.where(pick, gates, 0.0), axis=0, keepdims=True) for _, pick in picks]
    w_pad = jnp.concatenate(w_rows + [jnp.zeros((V7X_LANES - TOP_K, tm), F32)], axis=0)
    wt_ref[...] = w_pad.T


def _router_call(h2, rwt, bias, tm, t, row0):
    tile0 = row0 // tm
    tri = jnp.triu(jnp.ones((tm, tm), BF16), k=1)
    full = lambda a: pl.BlockSpec(a.shape, lambda i: (0, 0))
    return pl.pallas_call(
        _router_kernel,
        out_shape=[jax.ShapeDtypeStruct((TOP_K, t), jnp.int32), jax.ShapeDtypeStruct((TOP_K, t), jnp.int32),
                   jax.ShapeDtypeStruct((t, V7X_LANES), F32), jax.ShapeDtypeStruct((N_EXPERTS, V7X_LANES), jnp.int32)],
        grid=(t // tm,),
        in_specs=[pl.BlockSpec((tm, D_MODEL), lambda i: (i + tile0, 0)), full(rwt), full(bias), full(tri)],
        out_specs=[pl.BlockSpec((TOP_K, tm), lambda i: (0, i)), pl.BlockSpec((TOP_K, tm), lambda i: (0, i)),
                   pl.BlockSpec((tm, V7X_LANES), lambda i: (i, 0)),
                   pl.BlockSpec((N_EXPERTS, V7X_LANES), lambda i: (0, 0))],
        compiler_params=_cparams(("arbitrary",)),
        name="router",
    )(h2, rwt, bias, tri)


_SC_MESH_AXES = ("core", "subcore")


def _sc_mesh():
    return plsc.VectorSubcoreMesh(core_axis_name=_SC_MESH_AXES[0], subcore_axis_name=_SC_MESH_AXES[1])


def _sc_worker_chunks(n_chunks):
    n_workers = V7X_SC_CORES * V7X_SC_SUBCORES
    assert n_chunks % n_workers == 0
    per_worker = n_chunks // n_workers
    first = (lax.axis_index(_SC_MESH_AXES[1]) * V7X_SC_CORES + lax.axis_index(_SC_MESH_AXES[0])) * per_worker
    return first, per_worker


def _sc_scatter_rows(x, idx, n_rows, row0):
    t, w = x.shape
    n_chunks, n_k, window = idx.shape

    @functools.partial(pl.kernel, out_type=jax.ShapeDtypeStruct((n_rows, w), x.dtype), mesh=_sc_mesh(),
                       scratch_types=[pltpu.VMEM((n_k, window), jnp.int32), pltpu.VMEM((window, w), x.dtype)],
                       name="moe_dispatch_sc")
    def scatter(x_hbm, i_hbm, o_hbm, idx_v, rows_v):
        first, per_worker = _sc_worker_chunks(n_chunks)

        @pl.loop(0, per_worker)
        def _(c):
            chunk = first + c
            pltpu.sync_copy(i_hbm.at[chunk], idx_v)
            pltpu.sync_copy(x_hbm.at[pl.ds(row0 + chunk * window, window)], rows_v)
            for k in range(n_k):
                pltpu.sync_copy(rows_v, o_hbm.at[idx_v.at[k]])

    return scatter(x, idx)


def _sc_gather_rows(y, idx):
    n_chunks, n_k, window = idx.shape
    w = y.shape[1]

    half = window // 2
    steps = [(k, h) for k in range(n_k) for h in range(2)]

    @functools.partial(pl.kernel, out_type=jax.ShapeDtypeStruct((n_k, n_chunks * window, w), y.dtype),
                       mesh=_sc_mesh(),
                       scratch_types=[pltpu.VMEM((n_k, window), jnp.int32), pltpu.VMEM((half, w), y.dtype),
                                      pltpu.VMEM((half, w), y.dtype), pltpu.SemaphoreType.DMA,
                                      pltpu.SemaphoreType.DMA],
                       name="moe_combine_sc")
    def gather(y_hbm, i_hbm, o_hbm, idx_v, rows_a, rows_b, sem_a, sem_b):
        first, per_worker = _sc_worker_chunks(n_chunks)
        bufs = ((rows_a, sem_a), (rows_b, sem_b))

        @pl.loop(0, per_worker)
        def _(c):
            chunk = first + c
            pltpu.sync_copy(i_hbm.at[chunk], idx_v)

            def start(j):
                k, h = steps[j]
                buf, sem = bufs[j % 2]
                return pltpu.async_copy(y_hbm.at[idx_v.at[k, pl.ds(h * half, half)]], buf, sem)

            pending = start(0)
            for j, (k, h) in enumerate(steps):
                following = start(j + 1) if j + 1 < len(steps) else None
                pending.wait()
                pltpu.sync_copy(bufs[j % 2][0], o_hbm.at[k, pl.ds(chunk * window + h * half, half)])
                pending = following

    return gather(y, idx)


def _experts_kernel(te_ref, nu_ref, xs_ref, wg_ref, wu_ref, wd_ref, y_ref, wg_b, wu_b, wd_b):
    i = pl.program_id(0)

    @pl.when(i < nu_ref[0])
    def _():
        @pl.when((i == 0) | (te_ref[i] != te_ref[jnp.maximum(i - 1, 0)]))
        def _():
            wg_b[...] = wg_ref[0].astype(BF16)
            wu_b[...] = wu_ref[0].astype(BF16)
            wd_b[...] = wd_ref[0].astype(BF16)

        half = D_MODEL // 2
        lo, hi = _unpack_bf16_pairs(xs_ref[...])
        a = _dot(lo, wg_b[:half, :]) + _dot(hi, wg_b[half:, :])
        u = _dot(lo, wu_b[:half, :]) + _dot(hi, wu_b[half:, :])
        y_ref[...] = _pack_bf16_pairs(_dot((jax.nn.silu(a) * u).astype(BF16), wd_b[...]))


def _experts_call(tile_expert, n_used, xs, wg, wu, wd, tmx):
    n_rows, w = xs.shape
    tile = lambda i, te, nu: jnp.minimum(i, nu[0] - 1)
    wspec = lambda shp: pl.BlockSpec((1,) + shp, lambda i, te, nu: (te[tile(i, te, nu)], 0, 0))
    return pl.pallas_call(
        _experts_kernel,
        out_shape=jax.ShapeDtypeStruct((n_rows, w), xs.dtype),
        grid_spec=pltpu.PrefetchScalarGridSpec(
            num_scalar_prefetch=2,
            grid=(n_rows // tmx,),
            in_specs=[pl.BlockSpec((tmx, w), lambda i, te, nu: (tile(i, te, nu), 0)),
                      wspec((D_MODEL, EXPERT_DIM)), wspec((D_MODEL, EXPERT_DIM)), wspec((EXPERT_DIM, D_MODEL))],
            out_specs=pl.BlockSpec((tmx, w), lambda i, te, nu: (tile(i, te, nu), 0)),
            scratch_shapes=[pltpu.VMEM((D_MODEL, EXPERT_DIM), BF16), pltpu.VMEM((D_MODEL, EXPERT_DIM), BF16),
                            pltpu.VMEM((EXPERT_DIM, D_MODEL), BF16)],
        ),
        compiler_params=_cparams(("arbitrary",)),
        name="moe_experts",
    )(tile_expert, n_used, xs, wg, wu, wd)


def _combine_kernel(wt_ref, h_ref, yg_ref, sg_ref, su_ref, sd_ref, g3_ref, b3_ref, o_ref):
    h = h_ref[...]
    hb = h.astype(BF16)
    acc = _dot((jax.nn.silu(_dot(hb, sg_ref[...])) * _dot(hb, su_ref[...])).astype(BF16), sd_ref[...])
    wt = wt_ref[...]
    lo_sum = None
    hi_sum = None
    for k in range(TOP_K):
        lo, hi = _unpack_bf16_pairs(yg_ref[k])
        wk = wt[:, k:k + 1]
        lo_sum = wk * lo if lo_sum is None else lo_sum + wk * lo
        hi_sum = wk * hi if hi_sum is None else hi_sum + wk * hi
    acc = acc + jnp.concatenate([lo_sum, hi_sum], axis=1)
    o_ref[...] = _layer_norm(DEEPNORM_ALPHA * h + acc, g3_ref[...], b3_ref[...])


def _combine_kernel_inplace(wt_ref, h_ref, yg_ref, sg_ref, su_ref, sd_ref, g3_ref, b3_ref, prev_ref, o_ref):
    del prev_ref
    _combine_kernel(wt_ref, h_ref, yg_ref, sg_ref, su_ref, sd_ref, g3_ref, b3_ref, o_ref)


def _combine_call(wt, h2, yg, sg, su, sd, g3, b3, tm, row0, out_prev):
    t = wt.shape[0]
    tile0 = row0 // tm
    full = lambda a: pl.BlockSpec(a.shape, lambda i, *_: (0, 0))
    in_specs = [pl.BlockSpec((tm, V7X_LANES), lambda i: (i, 0)),
                pl.BlockSpec((tm, D_MODEL), lambda i: (i + tile0, 0)),
                pl.BlockSpec((TOP_K, tm, yg.shape[2]), lambda i: (0, i, 0)),
                full(sg), full(su), full(sd), full(g3), full(b3)]
    args = [wt, h2, yg, sg, su, sd, g3, b3]
    kernel, aliases = _combine_kernel, {}
    if out_prev is not None:
        in_specs.append(pl.BlockSpec(memory_space=pl.ANY))
        args.append(out_prev)
        aliases = {len(args) - 1: 0}
        kernel = _combine_kernel_inplace
    return pl.pallas_call(
        kernel,
        out_shape=jax.ShapeDtypeStruct(h2.shape, F32),
        grid=(t // tm,),
        in_specs=in_specs,
        out_specs=pl.BlockSpec((tm, D_MODEL), lambda i: (i + tile0, 0)),
        input_output_aliases=aliases,
        compiler_params=_cparams(("parallel",)),
        name="moe_combine",
    )(*args)


def _moe(h2, hp, router_w, router_bias, wg, wu, wd, sg, su, sd, g3, b3, tm_route, tmx, tm_comb, row0, t, out_prev):
    eid, rnk, wt, cnt = _router_call(h2, router_w.T, router_bias.reshape(-1, 1), tm_route, t, row0)
    counts = cnt[:, 0]
    tiles_per_e = (counts + (tmx - 1)) // tmx
    tile_end = jnp.cumsum(tiles_per_e)
    row_start = ((tile_end - tiles_per_e) * tmx).astype(jnp.int32)
    n_tiles = (t * TOP_K) // tmx + N_EXPERTS
    n_used = tile_end[-1:].astype(jnp.int32)
    tile_expert = jnp.minimum(
        jnp.sum(tile_end[None, :] <= jnp.arange(n_tiles, dtype=jnp.int32)[:, None], axis=1), N_EXPERTS - 1
    ).astype(jnp.int32)
    expert_ids = jnp.arange(N_EXPERTS, dtype=jnp.int32)[:, None, None]
    dest = rnk + jnp.sum(jnp.where(eid[None] == expert_ids, row_start[:, None, None], 0), axis=0)
    dest_chunks = dest.reshape(TOP_K, t // SC_INDEX_WINDOW, SC_INDEX_WINDOW).transpose(1, 0, 2)
    xs = _sc_scatter_rows(hp, dest_chunks, n_tiles * tmx, row0)
    yp = _experts_call(tile_expert, n_used, xs, wg, wu, wd, tmx)
    yg = _sc_gather_rows(yp, dest_chunks)
    return _combine_call(wt, h2, yg, sg, su, sd, g3, b3, tm_comb, row0, out_prev)


def _prep_w_in_t(w_in):
    pad = jnp.zeros((D_MODEL, V7X_LANES - MLA_ROPE_DIM), w_in.dtype)
    return jnp.concatenate([w_in, pad], axis=1).T.astype(BF16)


def _prep_w_uq_t(w_uq):
    w = w_uq.reshape(MLA_Q_RANK, MLA_HEADS, MLA_NOPE_DIM + MLA_ROPE_DIM)
    pad = jnp.zeros((MLA_Q_RANK, MLA_HEADS, MLA_QK_PAD - MLA_NOPE_DIM - MLA_ROPE_DIM), w.dtype)
    return jnp.concatenate([w, pad], axis=2).reshape(MLA_Q_RANK, MLA_HEADS * MLA_QK_PAD).T.astype(BF16)


def _prep_w_ukv_t(w_ukv):
    w = w_ukv.reshape(MLA_KV_RANK, MLA_HEADS, MLA_NOPE_DIM + MLA_V_DIM)
    k = w[:, :, :MLA_NOPE_DIM].reshape(MLA_KV_RANK, MLA_HEADS * MLA_NOPE_DIM)
    v = w[:, :, MLA_NOPE_DIM:].reshape(MLA_KV_RANK, MLA_HEADS * MLA_V_DIM)
    return jnp.concatenate([k, v], axis=1).T.astype(BF16)


def _rope_inv_freq(dim):
    return (ROPE_THETA ** (-jnp.arange(0, dim, 2, dtype=F32) / dim)).reshape(-1, 1)


def _layer(x2, mem2, pos_row, b, s, l, w_in, lam_q1, lam_k1, lam_q2, lam_k2, diff_subln_g, mla_q_norm_g,
           mla_w_uq, mla_kv_norm_g, mla_w_ukv, mla_out_norm_g, w_o, ln1_g, ln1_b, mem_w_q, mem_w_k, mem_w_v,
           mem_w_o, ln2_g, ln2_b, router_w, router_bias, exp_w_gate, exp_w_up, exp_w_down, sh_w_gate,
           sh_w_up, sh_w_down, ln3_g, ln3_b):
    row = lambda a: a.reshape(1, -1)
    col = lambda a: a.reshape(-1, 1)
    dqt, dk, dvt, mqt, mk, mvt = _proj_call(
        x2, pos_row, _rope_inv_freq(DIFF_ROT_DIM), _rope_inv_freq(MLA_ROPE_DIM), _prep_w_in_t(w_in),
        col(mla_q_norm_g), _prep_w_uq_t(mla_w_uq), col(mla_kv_norm_g), _prep_w_ukv_t(mla_w_ukv), tm=min(1024, s))

    lam_init = 0.8 - 0.6 * math.exp(-0.3 * l)
    lam_vecs = jnp.stack([lam_q1, lam_k1, lam_q2, lam_k2]).astype(F32)
    tq = min(1024, s)
    diff_o = _attn_call(
        functools.partial(_diff_attn_kernel, tq=tq, lam_init=lam_init), "diff_attn",
        dqt, dk, dvt, [lam_vecs, col(diff_subln_g)], b, s, DIFF_HEADS, 2 * DIFF_QK_DIM, DIFF_V_DIM, tq, 2)
    tq_m = min(1024, s)
    mla_o = _attn_call(
        functools.partial(_mla_attn_kernel, tq=tq_m), "mla_attn",
        mqt, mk, mvt, [col(mla_out_norm_g)], b, s, MLA_HEADS, MLA_QK_PAD, MLA_V_DIM, tq_m, 1)

    kmem, vmem = _memkv_call(mem2, mem_w_k.astype(BF16), mem_w_v.astype(BF16))
    h2, hp = _post_call(diff_o, mla_o, x2, w_o.astype(BF16), row(ln1_g), row(ln1_b), mem_w_q.astype(BF16),
                        kmem, vmem, mem_w_o.astype(BF16), row(ln2_g), row(ln2_b), s, tm=min(1024, s))

    t = x2.shape[0]
    group = t // MOE_TOKEN_GROUPS
    shared = (sh_w_gate.astype(BF16), sh_w_up.astype(BF16), sh_w_down.astype(BF16))
    out = None
    for g in range(MOE_TOKEN_GROUPS):
        out = _moe(h2, hp, router_w, router_bias, exp_w_gate, exp_w_up, exp_w_down, *shared, row(ln3_g), row(ln3_b),
                   tm_route=min(1024, s), tmx=1024, tm_comb=min(512, s), row0=g * group, t=group, out_prev=out)
    return out


def kernel(x, mem, positions, w_in, lam_q1, lam_k1, lam_q2, lam_k2, diff_subln_g, mla_q_norm_g, mla_w_uq,
           mla_kv_norm_g, mla_w_ukv, mla_out_norm_g, w_o, ln1_g, ln1_b, mem_w_q, mem_w_k, mem_w_v, mem_w_o,
           ln2_g, ln2_b, router_w, router_bias, exp_w_gate, exp_w_up, exp_w_down, sh_w_gate, sh_w_up,
           sh_w_down, ln3_g, ln3_b):
    b, s, d = x.shape
    h = x.reshape(b * s, d)
    mem2 = mem.reshape(b * mem.shape[1], d)
    pos_row = positions.reshape(1, b * s)
    params = (w_in, lam_q1, lam_k1, lam_q2, lam_k2, diff_subln_g, mla_q_norm_g, mla_w_uq, mla_kv_norm_g,
              mla_w_ukv, mla_out_norm_g, w_o, ln1_g, ln1_b, mem_w_q, mem_w_k, mem_w_v, mem_w_o, ln2_g, ln2_b,
              router_w, router_bias, exp_w_gate, exp_w_up, exp_w_down, sh_w_gate, sh_w_up, sh_w_down,
              ln3_g, ln3_b)
    for l in range(w_in.shape[0]):
        h = _layer(h, mem2, pos_row, b, s, l, *[p[l] for p in params])
    return h.reshape(b, s, d)
```

```python
import functools
import math

import jax
import jax.numpy as jnp
from jax import lax
from jax.experimental import pallas as pl
from jax.experimental.pallas import tpu as pltpu
from jax.experimental.pallas import tpu_sc as plsc

F32 = jnp.float32
BF16 = jnp.bfloat16

D_MODEL = 1024
N_MEM = 256
ROPE_THETA = 500000.0
NEG_INF = -1e30
LN_EPS = 1e-5
RMS_EPS = 1e-6
DIFF_HEADS = 4
DIFF_QK_DIM = 64
DIFF_V_DIM = 128
DIFF_ROT_DIM = DIFF_QK_DIM // 4
MLA_HEADS = 4
MLA_Q_RANK = 256
MLA_KV_RANK = 128
MLA_NOPE_DIM = 128
MLA_ROPE_DIM = 64
MLA_V_DIM = 128
MLA_QK_PAD = 256
V_AUG_DIM = 128 + 16
MEM_HEADS = 4
MEM_HEAD_DIM = 128
N_EXPERTS = 64
TOP_K = 8
N_GROUPS = 8
TOPK_GROUPS = 4
GROUP_SIZE = N_EXPERTS // N_GROUPS
EXPERT_DIM = 256
ROUTED_SCALE = 2.5
DEPTH = 1
DEEPNORM_ALPHA = (2.0 * DEPTH) ** 0.25
LOG2E = math.log2(math.e)

V7X_LANES = 128
V7X_VMEM_LIMIT_BYTES = 56 * 1024 * 1024
V7X_SC_CORES = 2
V7X_SC_SUBCORES = 16
SC_INDEX_WINDOW = 128
V7X_BF16_SUBLANES = 16

TOKEN_TILE = 1024
ATTN_QUERY_TILE = 1024
EXPERT_ROW_TILE = 1024
COMBINE_TOKEN_TILE = 512

N_DQ = DIFF_HEADS * 2 * DIFF_QK_DIM
N_DV = DIFF_HEADS * DIFF_V_DIM
O_DK = N_DQ
O_DV = 2 * N_DQ
O_CQ = O_DV + N_DV
O_CKV = O_CQ + MLA_Q_RANK
O_KR = O_CKV + MLA_KV_RANK


def _cparams(sem):
    return pltpu.CompilerParams(dimension_semantics=sem, vmem_limit_bytes=V7X_VMEM_LIMIT_BYTES)


def _dot(a, b):
    return jnp.dot(a, b, preferred_element_type=F32)


def _dot_nt(a, b):
    return lax.dot_general(a, b, (((1,), (1,)), ((), ())), preferred_element_type=F32)


def _rms_rows(x, g):
    return x * lax.rsqrt(jnp.mean(jnp.square(x), axis=0, keepdims=True) + RMS_EPS) * g


def _layer_norm(x, g, b):
    mu = jnp.mean(x, axis=-1, keepdims=True)
    xc = x - mu
    var = jnp.mean(jnp.square(xc), axis=-1, keepdims=True)
    return xc * lax.rsqrt(var + LN_EPS) * g + b


def _with_ones_rows(vt, heads):
    dv = vt.shape[0] // heads
    ones = jnp.ones((V_AUG_DIM - dv, vt.shape[1]), vt.dtype)
    return jnp.concatenate([p for h in range(heads) for p in (vt[h * dv:(h + 1) * dv], ones)], axis=0)


def _rope_rows(x, cos, sin, period):
    h = cos.shape[0]
    pieces = []
    for base in range(0, x.shape[0], period):
        x1 = x[base:base + h]
        x2 = x[base + h:base + 2 * h]
        pieces.append(x1 * cos - x2 * sin)
        pieces.append(x2 * cos + x1 * sin)
        if period > 2 * h:
            pieces.append(x[base + 2 * h:base + period])
    return jnp.concatenate(pieces, axis=0)


def _proj_kernel(x_ref, pos_ref, fd_ref, fm_ref, w_in_ref, gq_ref, wuq_ref, gkv_ref, wukv_ref,
                 dq_ref, dk_ref, dv_ref, mq_ref, mk_ref, mv_ref):
    xb = x_ref[...].astype(BF16)
    pos = pos_ref[...].astype(F32)
    ang_d = fd_ref[...] * pos
    ang_m = fm_ref[...] * pos
    cos_d, sin_d = jnp.cos(ang_d), jnp.sin(ang_d)
    cos_m, sin_m = jnp.cos(ang_m), jnp.sin(ang_m)

    dq = _rope_rows(_dot_nt(w_in_ref[0:N_DQ, :], xb), cos_d, sin_d, DIFF_QK_DIM)
    dq_ref[...] = (dq * (DIFF_QK_DIM ** -0.5 * LOG2E)).astype(BF16)
    dk = _rope_rows(_dot_nt(w_in_ref[O_DK:O_DK + N_DQ, :], xb), cos_d, sin_d, DIFF_QK_DIM)
    dk_ref[...] = dk.T.astype(BF16)
    dv_ref[...] = _with_ones_rows(_dot_nt(w_in_ref[O_DV:O_DV + N_DV, :], xb), DIFF_HEADS).astype(BF16)

    c_q = _dot_nt(w_in_ref[O_CQ:O_CQ + MLA_Q_RANK, :], xb)
    q = _dot(wuq_ref[...], _rms_rows(c_q, gq_ref[...]).astype(BF16))
    q = q * ((MLA_NOPE_DIM + MLA_ROPE_DIM) ** -0.5 * LOG2E)
    pieces = []
    for h in range(MLA_HEADS):
        o = h * MLA_QK_PAD
        pieces.append(q[o:o + MLA_NOPE_DIM])
        pieces.append(_rope_rows(q[o + MLA_NOPE_DIM:o + MLA_NOPE_DIM + MLA_ROPE_DIM], cos_m, sin_m, MLA_ROPE_DIM))
        pieces.append(q[o + MLA_NOPE_DIM + MLA_ROPE_DIM:o + MLA_QK_PAD])
    mq_ref[...] = jnp.concatenate(pieces, axis=0).astype(BF16)

    c_kv = _dot_nt(w_in_ref[O_CKV:O_CKV + MLA_KV_RANK, :], xb)
    kv = _dot(wukv_ref[...], _rms_rows(c_kv, gkv_ref[...]).astype(BF16))
    n_kn = MLA_HEADS * MLA_NOPE_DIM
    mv_ref[...] = _with_ones_rows(kv[n_kn:], MLA_HEADS).astype(BF16)
    k_rope = _dot_nt(w_in_ref[O_KR:O_KR + V7X_LANES, :], xb)
    k_pe = jnp.concatenate([_rope_rows(k_rope[:MLA_ROPE_DIM], cos_m, sin_m, MLA_ROPE_DIM),
                            k_rope[MLA_ROPE_DIM:]], axis=0)
    k_nope_t = kv[:n_kn].T.astype(BF16)
    k_pe_t = k_pe.T.astype(BF16)
    for h in range(MLA_HEADS):
        o = h * MLA_QK_PAD
        mk_ref[:, o:o + MLA_NOPE_DIM] = k_nope_t[:, h * MLA_NOPE_DIM:(h + 1) * MLA_NOPE_DIM]
        mk_ref[:, o + MLA_NOPE_DIM:o + MLA_QK_PAD] = k_pe_t


def _proj_call(x2, pos_row, fd, fm, w_in_t, gq, wuq_t, gkv, wukv_t, tm):
    t = x2.shape[0]
    full = lambda a: pl.BlockSpec(a.shape, lambda i: (0, 0))
    col_blk = lambda r: pl.BlockSpec((r, tm), lambda i: (0, i))
    row_blk = lambda w: pl.BlockSpec((tm, w), lambda i: (i, 0))
    n_mq = MLA_HEADS * MLA_QK_PAD
    return pl.pallas_call(
        _proj_kernel,
        out_shape=[jax.ShapeDtypeStruct((N_DQ, t), BF16), jax.ShapeDtypeStruct((t, N_DQ), BF16),
                   jax.ShapeDtypeStruct((DIFF_HEADS * V_AUG_DIM, t), BF16), jax.ShapeDtypeStruct((n_mq, t), BF16),
                   jax.ShapeDtypeStruct((t, n_mq), BF16), jax.ShapeDtypeStruct((MLA_HEADS * V_AUG_DIM, t), BF16)],
        grid=(t // tm,),
        in_specs=[row_blk(D_MODEL), col_blk(1), full(fd), full(fm), full(w_in_t), full(gq), full(wuq_t),
                  full(gkv), full(wukv_t)],
        out_specs=[col_blk(N_DQ), row_blk(N_DQ), col_blk(DIFF_HEADS * V_AUG_DIM), col_blk(n_mq), row_blk(n_mq),
                   col_blk(MLA_HEADS * V_AUG_DIM)],
        compiler_params=_cparams(("parallel",)),
        name="proj_rope",
    )(x2, pos_row, fd, fm, w_in_t, gq, wuq_t, gkv, wukv_t)


def _flash_t(qq, k_ref, vt_ref, s_refs, p_refs, acc_ref, m_ref, i, tq):
    n = qq.shape[1]
    tk = s_refs[0].shape[0]
    assert tq == 2 * tk
    q_pos = i * tq + (lax.broadcasted_iota(jnp.int32, (tk, n), 1) & (tq - 1))
    key_row = lax.broadcasted_iota(jnp.int32, (tk, n), 0)

    def scores(c, s_ref):
        off = pl.multiple_of(c * tk, tk)
        s = _dot(k_ref[pl.ds(off, tk), :], qq)
        s_ref[...] = s
        return jnp.max(s, axis=0, keepdims=True)

    def softmax(c, s_ref, p_ref, cmax, m, masked):
        s = s_ref[...]
        if masked:
            s = jnp.where(key_row + c * tk <= q_pos, s, NEG_INF)
            cmax = jnp.max(s, axis=0, keepdims=True)
        m_new = jnp.maximum(m, cmax)
        alpha = jnp.exp2(m - m_new)
        p_ref[...] = jnp.exp2((s - m_new).astype(BF16))
        return m_new, alpha

    def values(c, p_ref, alpha):
        off = pl.multiple_of(jnp.maximum(c, 0) * tk, tk)
        acc_ref[...] = alpha * acc_ref[...] + _dot(vt_ref[:, pl.ds(off, tk)], p_ref[...])

    late = [slice(a + tq // 2, a + tq) for a in range(0, n, tq)]

    def scores_late(c, s_ref):
        off = pl.multiple_of(c * tk, tk)
        kc = k_ref[pl.ds(off, tk), :]
        for sl in late:
            s_ref[:, sl] = _dot(kc, qq[:, sl])

    def softmax_late(c, s_ref, p_ref, m):
        m_ref[...] = m
        alphas = []
        for sl in late:
            shape = (tk, sl.stop - sl.start)
            key_pos = lax.broadcasted_iota(jnp.int32, shape, 0) + c * tk
            query_pos = i * tq + ((lax.broadcasted_iota(jnp.int32, shape, 1) + sl.start) & (tq - 1))
            s = jnp.where(key_pos <= query_pos, s_ref[:, sl], NEG_INF)
            m_old = m_ref[:, sl]
            m_new = jnp.maximum(m_old, jnp.max(s, axis=0, keepdims=True))
            alphas.append(jnp.exp2(m_old - m_new))
            p_ref[:, sl] = jnp.exp2((s - m_new).astype(BF16))
        return alphas

    def values_late(c, p_ref, alphas):
        off = pl.multiple_of(c * tk, tk)
        vc = vt_ref[:, pl.ds(off, tk)]
        for sl, alpha in zip(late, alphas):
            acc_ref[:, sl] = alpha * acc_ref[:, sl] + _dot(vc, p_ref[:, sl])

    def pair(u, carry, masked_second):
        cm1, alpha0, m = carry
        c = 2 * u
        cm0 = scores(c, s_refs[0])
        values(c - 2, p_refs[0], alpha0)
        m, alpha1 = softmax(c - 1, s_refs[1], p_refs[1], cm1, m, False)
        if masked_second:
            scores_late(c + 1, s_refs[1])
        else:
            cm1 = scores(c + 1, s_refs[1])
        values(c - 1, p_refs[1], alpha1)
        m, alpha0 = softmax(c, s_refs[0], p_refs[0], cm0, m, masked_second)
        return cm1, alpha0, m

    def drain(alpha0, m):
        values(2 * i, p_refs[0], alpha0)
        values_late(2 * i + 1, p_refs[1], softmax_late(2 * i + 1, s_refs[1], p_refs[1], m))

    m_init = jnp.full((1, n), NEG_INF, F32)
    acc_ref[...] = jnp.zeros(acc_ref.shape, F32)

    @pl.when(i == 0)
    def _():
        scores(0, s_refs[0])
        scores_late(1, s_refs[1])
        m, alpha0 = softmax(0, s_refs[0], p_refs[0], None, m_init, True)
        drain(alpha0, m)

    @pl.when(i > 0)
    def _():
        cm0 = scores(0, s_refs[0])
        cm1 = scores(1, s_refs[1])
        m, alpha0 = softmax(0, s_refs[0], p_refs[0], cm0, m_init, False)
        carry = lax.fori_loop(1, i, functools.partial(pair, masked_second=False), (cm1, alpha0, m))
        _, alpha0, m = pair(i, carry, True)
        drain(alpha0, m)

    dv = vt_ref.shape[0] - V7X_BF16_SUBLANES
    return acc_ref[0:dv, :] / acc_ref[dv:dv + 1, :]


def _diff_attn_kernel(qt_ref, k_ref, vt_ref, lam_ref, g_ref, o_ref, s0, s1, p0, p1, acc_ref, m_ref, *, tq,
                      lam_init):
    i = pl.program_id(2)
    qt = qt_ref[...]
    row = lax.broadcasted_iota(jnp.int32, qt.shape, 0)
    zero = jnp.zeros_like(qt)
    qq = jnp.concatenate([jnp.where(row < DIFF_QK_DIM, qt, zero), jnp.where(row >= DIFF_QK_DIM, qt, zero)], axis=1)
    o = _flash_t(qq, k_ref, vt_ref, (s0, s1), (p0, p1), acc_ref, m_ref, i, tq)
    lv = lam_ref[...]
    lam = (jnp.exp(jnp.sum(lv[0:1] * lv[1:2], axis=1, keepdims=True))
           - jnp.exp(jnp.sum(lv[2:3] * lv[3:4], axis=1, keepdims=True)) + lam_init)
    d = o[:, :tq] - lam * o[:, tq:]
    o_ref[...] = (_rms_rows(d, g_ref[...]) * (1.0 - lam_init)).T.astype(o_ref.dtype)


def _mla_attn_kernel(qt_ref, k_ref, vt_ref, g_ref, o_ref, s0, s1, p0, p1, acc_ref, m_ref, *, tq):
    i = pl.program_id(2)
    o = _flash_t(qt_ref[...], k_ref, vt_ref, (s0, s1), (p0, p1), acc_ref, m_ref, i, tq)
    o_ref[...] = _rms_rows(o, g_ref[...]).T.astype(o_ref.dtype)


def _attn_call(kernel, name, qt, k, vt, extras, b, s, heads, dk_w, dv_w, tq, n_maps):
    nq = s // tq
    t = b * s
    tk = tq // 2
    n = n_maps * tq
    small = [pl.BlockSpec(a.shape, lambda bi, h, i: (0, 0)) for a in extras]
    return pl.pallas_call(
        kernel,
        out_shape=jax.ShapeDtypeStruct((t, heads * dv_w), BF16),
        grid=(b, heads, nq),
        in_specs=[pl.BlockSpec((dk_w, tq), lambda bi, h, i: (h, bi * nq + i)),
                  pl.BlockSpec((s, dk_w), lambda bi, h, i: (bi, h)),
                  pl.BlockSpec((V_AUG_DIM, s), lambda bi, h, i: (h, bi))] + small,
        out_specs=pl.BlockSpec((tq, dv_w), lambda bi, h, i: (bi * nq + i, h)),
        scratch_shapes=[pltpu.VMEM((tk, n), F32), pltpu.VMEM((tk, n), F32),
                        pltpu.VMEM((tk, n), BF16), pltpu.VMEM((tk, n), BF16), pltpu.VMEM((V_AUG_DIM, n), F32),
                        pltpu.VMEM((1, n), F32)],
        compiler_params=_cparams(("parallel", "parallel", "arbitrary")),
        name=name,
    )(qt, k, vt, *extras)


def _memkv_kernel(mem_ref, wk_ref, wv_ref, k_ref, v_ref):
    mb = mem_ref[...].astype(BF16)
    k_ref[...] = _dot(mb, wk_ref[...]).astype(BF16)
    v_ref[...] = _dot(mb, wv_ref[...]).astype(BF16)


def _memkv_call(mem2, wk, wv):
    n = mem2.shape[0]
    w = wk.shape[1]
    tm = N_MEM
    return pl.pallas_call(
        _memkv_kernel,
        out_shape=[jax.ShapeDtypeStruct((n, w), BF16)] * 2,
        grid=(n // tm,),
        in_specs=[pl.BlockSpec((tm, D_MODEL), lambda i: (i, 0)),
                  pl.BlockSpec(wk.shape, lambda i: (0, 0)),
                  pl.BlockSpec(wv.shape, lambda i: (0, 0))],
        out_specs=[pl.BlockSpec((tm, w), lambda i: (i, 0))] * 2,
        compiler_params=_cparams(("parallel",)),
        name="mem_kv",
    )(mem2, wk, wv)


def _pack_bf16_pairs(x):
    half = x.shape[1] // 2
    r = pltpu.bitcast(x.astype(BF16).astype(F32), jnp.uint32)
    return (r[:, :half] >> 16) | (r[:, half:] & jnp.uint32(0xFFFF0000))


def _unpack_bf16_pairs(w, dtype=BF16):
    lo = pltpu.bitcast(w << 16, F32)
    hi = pltpu.bitcast(w & jnp.uint32(0xFFFF0000), F32)
    return lo.astype(dtype), hi.astype(dtype)


def _post_kernel(da_ref, ma_ref, x_ref, wo_ref, g1_ref, b1_ref, wq_ref, km_ref, vm_ref, wmo_ref,
                 g2_ref, b2_ref, h_ref, hp_ref):
    n_d = DIFF_HEADS * DIFF_V_DIM
    mix = _dot(da_ref[...], wo_ref[0:n_d, :]) + _dot(ma_ref[...], wo_ref[n_d:, :])
    h1 = _layer_norm(DEEPNORM_ALPHA * x_ref[...] + mix, g1_ref[...], b1_ref[...])
    q = (_dot(h1.astype(BF16), wq_ref[...]) * (MEM_HEAD_DIM ** -0.5)).astype(BF16)
    outs = []
    for h in range(MEM_HEADS):
        sl = slice(h * MEM_HEAD_DIM, (h + 1) * MEM_HEAD_DIM)
        s = _dot_nt(q[:, sl], km_ref[:, sl])
        m = jnp.max(s, axis=1, keepdims=True)
        p = jnp.exp(s - m)
        p = p / jnp.sum(p, axis=1, keepdims=True)
        outs.append(_dot(p.astype(BF16), vm_ref[:, sl]))
    o = jnp.concatenate(outs, axis=1).astype(BF16)
    xat = _dot(o, wmo_ref[...])
    h2 = _layer_norm(DEEPNORM_ALPHA * h1 + xat, g2_ref[...], b2_ref[...])
    h_ref[...] = h2
    hp_ref[...] = _pack_bf16_pairs(h2)


def _post_call(diff_o, mla_o, x2, wo, g1, b1, wq, kmem, vmem, wmo, g2, b2, s, tm):
    t = x2.shape[0]
    per_b = s // tm
    row = lambda w: pl.BlockSpec((tm, w), lambda i: (i, 0))
    full = lambda a: pl.BlockSpec(a.shape, lambda i: (0, 0))
    memb = pl.BlockSpec((N_MEM, kmem.shape[1]), lambda i: (i // per_b, 0))
    return pl.pallas_call(
        _post_kernel,
        out_shape=[jax.ShapeDtypeStruct((t, D_MODEL), F32), jax.ShapeDtypeStruct((t, D_MODEL // 2), jnp.uint32)],
        grid=(t // tm,),
        in_specs=[row(diff_o.shape[1]), row(mla_o.shape[1]), row(D_MODEL), full(wo), full(g1), full(b1),
                  full(wq), memb, memb, full(wmo), full(g2), full(b2)],
        out_specs=[row(D_MODEL), row(D_MODEL // 2)],
        compiler_params=_cparams(("parallel",)),
        name="post_mix_xattn",
    )(diff_o, mla_o, x2, wo, g1, b1, wq, kmem, vmem, wmo, g2, b2)


def _router_kernel(h_ref, rwt_ref, bias_ref, tri_ref, eid_ref, rnk_ref, wt_ref, cnt_ref):
    tm = h_ref.shape[0]

    @pl.when(pl.program_id(0) == 0)
    def _():
        cnt_ref[...] = jnp.zeros(cnt_ref.shape, cnt_ref.dtype)

    h = h_ref[...]
    h_hi = h.astype(BF16)
    h_lo = (h - h_hi.astype(F32)).astype(BF16)
    w = rwt_ref[...]
    w_hi = w.astype(BF16)
    w_lo = (w - w_hi.astype(F32)).astype(BF16)
    both = _dot_nt(jnp.concatenate([w_hi, w_lo], axis=0), h_hi)
    logits = both[:N_EXPERTS] + (both[N_EXPERTS:] + _dot_nt(w_hi, h_lo))
    scores = jax.nn.sigmoid(logits)
    choice = scores + bias_ref[...]
    neg = float("-inf")

    c3 = choice.reshape(N_GROUPS, GROUP_SIZE, tm)
    mi = lax.broadcasted_iota(jnp.int32, c3.shape, 1)
    top1 = jnp.max(c3, axis=1, keepdims=True)
    first = jnp.min(jnp.where(c3 == top1, mi, GROUP_SIZE), axis=1, keepdims=True)
    top2 = jnp.max(jnp.where(mi == first, neg, c3), axis=1, keepdims=True)
    gs = (top1 + top2).reshape(N_GROUPS, tm)

    gi = lax.broadcasted_iota(jnp.int32, gs.shape, 0)
    rank = jnp.zeros(gs.shape, jnp.int32)
    for g in range(N_GROUPS):
        rowg = gs[g:g + 1, :]
        beats = (rowg > gs) | ((rowg == gs) & (g < gi))
        rank = rank + beats.astype(jnp.int32)
    gmask = rank < TOPK_GROUPS
    emask = jnp.broadcast_to(gmask.reshape(N_GROUPS, 1, tm), c3.shape).reshape(N_EXPERTS, tm)

    work = jnp.where(emask, choice, NEG_INF)
    ei = lax.broadcasted_iota(jnp.int32, work.shape, 0)
    sel = jnp.zeros(work.shape, jnp.bool_)
    picks = []
    for _ in range(TOP_K):
        mx = jnp.max(work, axis=0, keepdims=True)
        fi = jnp.min(jnp.where(work == mx, ei, N_EXPERTS), axis=0, keepdims=True)
        pick = ei == fi
        picks.append((fi, pick))
        sel = sel | pick
        work = jnp.where(pick, neg, work)
    top_w = jnp.where(sel, scores, 0.0)
    gates = top_w / jnp.sum(top_w, axis=0, keepdims=True) * ROUTED_SCALE

    sel_b = jnp.where(sel, 1.0, 0.0).astype(BF16)
    before = _dot(sel_b, tri_ref[...]).astype(jnp.int32)
    run = cnt_ref[:, 0:1]
    pos = run + before
    cnt_ref[...] = jnp.broadcast_to(run + jnp.sum(sel.astype(jnp.int32), axis=1, keepdims=True), cnt_ref.shape)

    eid_ref[...] = jnp.concatenate([fi for fi, _ in picks], axis=0)
    rnk_ref[...] = jnp.concatenate(
        [jnp.sum(jnp.where(pick, pos, 0), axis=0, keepdims=True) for _, pick in picks], axis=0)
    w_rows = [jnp.sum(jnp.where(pick, gates, 0.0), axis=0, keepdims=True) for _, pick in picks]
    w_pad = jnp.concatenate(w_rows + [jnp.zeros((V7X_LANES - TOP_K, tm), F32)], axis=0)
    wt_ref[...] = w_pad.T


def _router_call(h2, rwt, bias, tm):
    t = h2.shape[0]
    tri = jnp.triu(jnp.ones((tm, tm), BF16), k=1)
    full = lambda a: pl.BlockSpec(a.shape, lambda i: (0, 0))
    return pl.pallas_call(
        _router_kernel,
        out_shape=[jax.ShapeDtypeStruct((TOP_K, t), jnp.int32), jax.ShapeDtypeStruct((TOP_K, t), jnp.int32),
                   jax.ShapeDtypeStruct((t, V7X_LANES), F32), jax.ShapeDtypeStruct((N_EXPERTS, V7X_LANES), jnp.int32)],
        grid=(t // tm,),
        in_specs=[pl.BlockSpec((tm, D_MODEL), lambda i: (i, 0)), full(rwt), full(bias), full(tri)],
        out_specs=[pl.BlockSpec((TOP_K, tm), lambda i: (0, i)), pl.BlockSpec((TOP_K, tm), lambda i: (0, i)),
                   pl.BlockSpec((tm, V7X_LANES), lambda i: (i, 0)),
                   pl.BlockSpec((N_EXPERTS, V7X_LANES), lambda i: (0, 0))],
        compiler_params=_cparams(("arbitrary",)),
        name="router",
    )(h2, rwt, bias, tri)


_SC_MESH_AXES = ("core", "subcore")


def _sc_mesh():
    return plsc.VectorSubcoreMesh(core_axis_name=_SC_MESH_AXES[0], subcore_axis_name=_SC_MESH_AXES[1])


def _sc_worker_chunks(n_chunks):
    n_workers = V7X_SC_CORES * V7X_SC_SUBCORES
    assert n_chunks % n_workers == 0
    per_worker = n_chunks // n_workers
    first = (lax.axis_index(_SC_MESH_AXES[1]) * V7X_SC_CORES + lax.axis_index(_SC_MESH_AXES[0])) * per_worker
    return first, per_worker


def _sc_scatter_rows(x, idx, n_rows):
    t, w = x.shape
    n_chunks, n_k, window = idx.shape

    @functools.partial(pl.kernel, out_type=jax.ShapeDtypeStruct((n_rows, w), x.dtype), mesh=_sc_mesh(),
                       scratch_types=[pltpu.VMEM((n_k, window), jnp.int32), pltpu.VMEM((window, w), x.dtype)],
                       name="moe_dispatch_sc")
    def scatter(x_hbm, i_hbm, o_hbm, idx_v, rows_v):
        first, per_worker = _sc_worker_chunks(n_chunks)

        @pl.loop(0, per_worker)
        def _(c):
            chunk = first + c
            pltpu.sync_copy(i_hbm.at[chunk], idx_v)
            pltpu.sync_copy(x_hbm.at[pl.ds(chunk * window, window)], rows_v)
            for k in range(n_k):
                pltpu.sync_copy(rows_v, o_hbm.at[idx_v.at[k]])

    return scatter(x, idx)


def _sc_gather_rows(y, idx):
    n_chunks, n_k, window = idx.shape
    w = y.shape[1]

    half = window // 2
    steps = [(k, h) for k in range(n_k) for h in range(2)]

    @functools.partial(pl.kernel, out_type=jax.ShapeDtypeStruct((n_k, n_chunks * window, w), y.dtype),
                       mesh=_sc_mesh(),
                       scratch_types=[pltpu.VMEM((n_k, window), jnp.int32), pltpu.VMEM((half, w), y.dtype),
                                      pltpu.VMEM((half, w), y.dtype), pltpu.SemaphoreType.DMA,
                                      pltpu.SemaphoreType.DMA],
                       name="moe_combine_sc")
    def gather(y_hbm, i_hbm, o_hbm, idx_v, rows_a, rows_b, sem_a, sem_b):
        first, per_worker = _sc_worker_chunks(n_chunks)
        bufs = ((rows_a, sem_a), (rows_b, sem_b))

        @pl.loop(0, per_worker)
        def _(c):
            chunk = first + c
            pltpu.sync_copy(i_hbm.at[chunk], idx_v)

            def start(j):
                k, h = steps[j]
                buf, sem = bufs[j % 2]
                return pltpu.async_copy(y_hbm.at[idx_v.at[k, pl.ds(h * half, half)]], buf, sem)

            pending = start(0)
            for j, (k, h) in enumerate(steps):
                following = start(j + 1) if j + 1 < len(steps) else None
                pending.wait()
                pltpu.sync_copy(bufs[j % 2][0], o_hbm.at[k, pl.ds(chunk * window + h * half, half)])
                pending = following

    return gather(y, idx)


def _experts_kernel(te_ref, nu_ref, xs_ref, wg_ref, wu_ref, wd_ref, y_ref, wg_b, wu_b, wd_b):
    i = pl.program_id(0)

    @pl.when(i < nu_ref[0])
    def _():
        @pl.when((i == 0) | (te_ref[i] != te_ref[jnp.maximum(i - 1, 0)]))
        def _():
            wg_b[...] = wg_ref[0].astype(BF16)
            wu_b[...] = wu_ref[0].astype(BF16)
            wd_b[...] = wd_ref[0].astype(BF16)

        half = D_MODEL // 2
        lo, hi = _unpack_bf16_pairs(xs_ref[...])
        a = _dot(lo, wg_b[:half, :]) + _dot(hi, wg_b[half:, :])
        u = _dot(lo, wu_b[:half, :]) + _dot(hi, wu_b[half:, :])
        y_ref[...] = _pack_bf16_pairs(_dot((jax.nn.silu(a) * u).astype(BF16), wd_b[...]))


def _experts_call(tile_expert, n_used, xs, wg, wu, wd, tmx):
    n_rows, w = xs.shape
    tile = lambda i, te, nu: jnp.minimum(i, nu[0] - 1)
    wspec = lambda shp: pl.BlockSpec((1,) + shp, lambda i, te, nu: (te[tile(i, te, nu)], 0, 0))
    return pl.pallas_call(
        _experts_kernel,
        out_shape=jax.ShapeDtypeStruct((n_rows, w), xs.dtype),
        grid_spec=pltpu.PrefetchScalarGridSpec(
            num_scalar_prefetch=2,
            grid=(n_rows // tmx,),
            in_specs=[pl.BlockSpec((tmx, w), lambda i, te, nu: (tile(i, te, nu), 0)),
                      wspec((D_MODEL, EXPERT_DIM)), wspec((D_MODEL, EXPERT_DIM)), wspec((EXPERT_DIM, D_MODEL))],
            out_specs=pl.BlockSpec((tmx, w), lambda i, te, nu: (tile(i, te, nu), 0)),
            scratch_shapes=[pltpu.VMEM((D_MODEL, EXPERT_DIM), BF16), pltpu.VMEM((D_MODEL, EXPERT_DIM), BF16),
                            pltpu.VMEM((EXPERT_DIM, D_MODEL), BF16)],
        ),
        compiler_params=_cparams(("arbitrary",)),
        name="moe_experts",
    )(tile_expert, n_used, xs, wg, wu, wd)


def _combine_kernel(wt_ref, h_ref, yg_ref, sg_ref, su_ref, sd_ref, g3_ref, b3_ref, o_ref):
    h = h_ref[...]
    hb = h.astype(BF16)
    acc = _dot((jax.nn.silu(_dot(hb, sg_ref[...])) * _dot(hb, su_ref[...])).astype(BF16), sd_ref[...])
    wt = wt_ref[...]
    lo_sum = None
    hi_sum = None
    for k in range(TOP_K):
        lo, hi = _unpack_bf16_pairs(yg_ref[k], F32)
        wk = wt[:, k:k + 1]
        lo_sum = wk * lo if lo_sum is None else lo_sum + wk * lo
        hi_sum = wk * hi if hi_sum is None else hi_sum + wk * hi
    acc = acc + jnp.concatenate([lo_sum, hi_sum], axis=1)
    o_ref[...] = _layer_norm(DEEPNORM_ALPHA * h + acc, g3_ref[...], b3_ref[...])


def _combine_call(wt, h2, yg, sg, su, sd, g3, b3, tm):
    t = h2.shape[0]
    full = lambda a: pl.BlockSpec(a.shape, lambda i: (0, 0))
    return pl.pallas_call(
        _combine_kernel,
        out_shape=jax.ShapeDtypeStruct((t, D_MODEL), F32),
        grid=(t // tm,),
        in_specs=[pl.BlockSpec((tm, V7X_LANES), lambda i: (i, 0)),
                  pl.BlockSpec((tm, D_MODEL), lambda i: (i, 0)),
                  pl.BlockSpec((TOP_K, tm, yg.shape[2]), lambda i: (0, i, 0)),
                  full(sg), full(su), full(sd), full(g3), full(b3)],
        out_specs=pl.BlockSpec((tm, D_MODEL), lambda i: (i, 0)),
        compiler_params=_cparams(("parallel",)),
        name="moe_combine",
    )(wt, h2, yg, sg, su, sd, g3, b3)


def _moe(h2, hp, router_w, router_bias, wg, wu, wd, sg, su, sd, g3, b3, tm_route, tmx, tm_comb):
    t = h2.shape[0]
    eid, rnk, wt, cnt = _router_call(h2, router_w.T, router_bias.reshape(-1, 1), tm_route)
    counts = cnt[:, 0]
    tiles_per_e = (counts + (tmx - 1)) // tmx
    tile_end = jnp.cumsum(tiles_per_e)
    row_start = ((tile_end - tiles_per_e) * tmx).astype(jnp.int32)
    n_tiles = (t * TOP_K) // tmx + N_EXPERTS
    n_used = tile_end[-1:].astype(jnp.int32)
    tile_expert = jnp.minimum(
        jnp.sum(tile_end[None, :] <= jnp.arange(n_tiles, dtype=jnp.int32)[:, None], axis=1), N_EXPERTS - 1
    ).astype(jnp.int32)
    expert_ids = jnp.arange(N_EXPERTS, dtype=jnp.int32)[:, None, None]
    dest = rnk + jnp.sum(jnp.where(eid[None] == expert_ids, row_start[:, None, None], 0), axis=0)
    dest_chunks = dest.reshape(TOP_K, t // SC_INDEX_WINDOW, SC_INDEX_WINDOW).transpose(1, 0, 2)
    xs = _sc_scatter_rows(hp, dest_chunks, n_tiles * tmx)
    yp = _experts_call(tile_expert, n_used, xs, wg, wu, wd, tmx)
    yg = _sc_gather_rows(yp, dest_chunks)
    return _combine_call(wt, h2, yg, sg, su, sd, g3, b3, tm_comb)


def _prep_w_in_t(w_in):
    pad = jnp.zeros((D_MODEL, V7X_LANES - MLA_ROPE_DIM), w_in.dtype)
    return jnp.concatenate([w_in, pad], axis=1).T.astype(BF16)


def _prep_w_uq_t(w_uq):
    w = w_uq.reshape(MLA_Q_RANK, MLA_HEADS, MLA_NOPE_DIM + MLA_ROPE_DIM)
    pad = jnp.zeros((MLA_Q_RANK, MLA_HEADS, MLA_QK_PAD - MLA_NOPE_DIM - MLA_ROPE_DIM), w.dtype)
    return jnp.concatenate([w, pad], axis=2).reshape(MLA_Q_RANK, MLA_HEADS * MLA_QK_PAD).T.astype(BF16)


def _prep_w_ukv_t(w_ukv):
    w = w_ukv.reshape(MLA_KV_RANK, MLA_HEADS, MLA_NOPE_DIM + MLA_V_DIM)
    k = w[:, :, :MLA_NOPE_DIM].reshape(MLA_KV_RANK, MLA_HEADS * MLA_NOPE_DIM)
    v = w[:, :, MLA_NOPE_DIM:].reshape(MLA_KV_RANK, MLA_HEADS * MLA_V_DIM)
    return jnp.concatenate([k, v], axis=1).T.astype(BF16)


def _rope_inv_freq(dim):
    return (ROPE_THETA ** (-jnp.arange(0, dim, 2, dtype=F32) / dim)).reshape(-1, 1)


def _layer(x2, mem2, pos_row, b, s, l, w_in, lam_q1, lam_k1, lam_q2, lam_k2, diff_subln_g, mla_q_norm_g,
           mla_w_uq, mla_kv_norm_g, mla_w_ukv, mla_out_norm_g, w_o, ln1_g, ln1_b, mem_w_q, mem_w_k, mem_w_v,
           mem_w_o, ln2_g, ln2_b, router_w, router_bias, exp_w_gate, exp_w_up, exp_w_down, sh_w_gate,
           sh_w_up, sh_w_down, ln3_g, ln3_b):
    row = lambda a: a.reshape(1, -1)
    col = lambda a: a.reshape(-1, 1)
    dqt, dk, dvt, mqt, mk, mvt = _proj_call(
        x2, pos_row, _rope_inv_freq(DIFF_ROT_DIM), _rope_inv_freq(MLA_ROPE_DIM), _prep_w_in_t(w_in),
        col(mla_q_norm_g), _prep_w_uq_t(mla_w_uq), col(mla_kv_norm_g), _prep_w_ukv_t(mla_w_ukv),
        tm=min(TOKEN_TILE, s))

    lam_init = 0.8 - 0.6 * math.exp(-0.3 * l)
    lam_vecs = jnp.stack([lam_q1, lam_k1, lam_q2, lam_k2]).astype(F32)
    tq = min(ATTN_QUERY_TILE, s)
    diff_o = _attn_call(
        functools.partial(_diff_attn_kernel, tq=tq, lam_init=lam_init), "diff_attn",
        dqt, dk, dvt, [lam_vecs, col(diff_subln_g)], b, s, DIFF_HEADS, 2 * DIFF_QK_DIM, DIFF_V_DIM, tq, 2)
    mla_o = _attn_call(
        functools.partial(_mla_attn_kernel, tq=tq), "mla_attn",
        mqt, mk, mvt, [col(mla_out_norm_g)], b, s, MLA_HEADS, MLA_QK_PAD, MLA_V_DIM, tq, 1)

    kmem, vmem = _memkv_call(mem2, mem_w_k.astype(BF16), mem_w_v.astype(BF16))
    h2, hp = _post_call(diff_o, mla_o, x2, w_o.astype(BF16), row(ln1_g), row(ln1_b), mem_w_q.astype(BF16),
                        kmem, vmem, mem_w_o.astype(BF16), row(ln2_g), row(ln2_b), s, tm=min(TOKEN_TILE, s))

    return _moe(h2, hp, router_w, router_bias, exp_w_gate, exp_w_up, exp_w_down,
                sh_w_gate.astype(BF16), sh_w_up.astype(BF16), sh_w_down.astype(BF16), row(ln3_g), row(ln3_b),
                tm_route=min(TOKEN_TILE, s), tmx=EXPERT_ROW_TILE, tm_comb=min(COMBINE_TOKEN_TILE, s))


def kernel(x, mem, positions, w_in, lam_q1, lam_k1, lam_q2, lam_k2, diff_subln_g, mla_q_norm_g, mla_w_uq,
           mla_kv_norm_g, mla_w_ukv, mla_out_norm_g, w_o, ln1_g, ln1_b, mem_w_q, mem_w_k, mem_w_v, mem_w_o,
           ln2_g, ln2_b, router_w, router_bias, exp_w_gate, exp_w_up, exp_w_down, sh_w_gate, sh_w_up,
           sh_w_down, ln3_g, ln3_b):
    b, s, d = x.shape
    h = x.reshape(b * s, d)
    mem2 = mem.reshape(b * mem.shape[1], d)
    pos_row = positions.reshape(1, b * s)
    params = (w_in, lam_q1, lam_k1, lam_q2, lam_k2, diff_subln_g, mla_q_norm_g, mla_w_uq, mla_kv_norm_g,
              mla_w_ukv, mla_out_norm_g, w_o, ln1_g, ln1_b, mem_w_q, mem_w_k, mem_w_v, mem_w_o, ln2_g, ln2_b,
              router_w, router_bias, exp_w_gate, exp_w_up, exp_w_down, sh_w_gate, sh_w_up, sh_w_down,
              ln3_g, ln3_b)
    for l in range(w_in.shape[0]):
        h = _layer(h, mem2, pos_row, b, s, l, *[p[l] for p in params])
    return h.reshape(b, s, d)
```

```python
import functools
import math

import jax
import jax.numpy as jnp
from jax import lax
from jax.experimental import pallas as pl
from jax.experimental.pallas import tpu as pltpu
from jax.experimental.pallas import tpu_sc as plsc

F32 = jnp.float32
BF16 = jnp.bfloat16

D_MODEL = 1024
N_MEM = 256
ROPE_THETA = 500000.0
NEG_INF = -1e30
LN_EPS = 1e-5
RMS_EPS = 1e-6
DIFF_HEADS = 4
DIFF_QK_DIM = 64
DIFF_V_DIM = 128
DIFF_ROT_DIM = DIFF_QK_DIM // 4
MLA_HEADS = 4
MLA_Q_RANK = 256
MLA_KV_RANK = 128
MLA_NOPE_DIM = 128
MLA_ROPE_DIM = 64
MLA_V_DIM = 128
MLA_QK_PAD = 256
V_AUG_DIM = 128 + 16
MEM_HEADS = 4
MEM_HEAD_DIM = 128
N_EXPERTS = 64
TOP_K = 8
N_GROUPS = 8
TOPK_GROUPS = 4
GROUP_SIZE = N_EXPERTS // N_GROUPS
EXPERT_DIM = 256
ROUTED_SCALE = 2.5
DEPTH = 1
DEEPNORM_ALPHA = (2.0 * DEPTH) ** 0.25
LOG2E = math.log2(math.e)

V7X_LANES = 128
V7X_VMEM_LIMIT_BYTES = 56 * 1024 * 1024
V7X_SC_CORES = 2
V7X_SC_SUBCORES = 16
SC_INDEX_WINDOW = 128
V7X_BF16_SUBLANES = 16

TOKEN_TILE = 1024
ATTN_QUERY_TILE = 1024
EXPERT_ROW_TILE = 1024
COMBINE_TOKEN_TILE = 512

N_DQ = DIFF_HEADS * 2 * DIFF_QK_DIM
N_DV = DIFF_HEADS * DIFF_V_DIM
O_DK = N_DQ
O_DV = 2 * N_DQ
O_CQ = O_DV + N_DV
O_CKV = O_CQ + MLA_Q_RANK
O_KR = O_CKV + MLA_KV_RANK


def _cparams(sem):
    return pltpu.CompilerParams(dimension_semantics=sem, vmem_limit_bytes=V7X_VMEM_LIMIT_BYTES)


def _dot(a, b):
    return jnp.dot(a, b, preferred_element_type=F32)


def _dot_nt(a, b):
    return lax.dot_general(a, b, (((1,), (1,)), ((), ())), preferred_element_type=F32)


def _rms_rows(x, g):
    return x * lax.rsqrt(jnp.mean(jnp.square(x), axis=0, keepdims=True) + RMS_EPS) * g


def _layer_norm(x, g, b):
    mu = jnp.mean(x, axis=-1, keepdims=True)
    xc = x - mu
    var = jnp.mean(jnp.square(xc), axis=-1, keepdims=True)
    return xc * lax.rsqrt(var + LN_EPS) * g + b


def _with_ones_rows(vt, heads):
    dv = vt.shape[0] // heads
    ones = jnp.ones((V_AUG_DIM - dv, vt.shape[1]), vt.dtype)
    return jnp.concatenate([p for h in range(heads) for p in (vt[h * dv:(h + 1) * dv], ones)], axis=0)


def _rope_rows(x, cos, sin, period):
    h = cos.shape[0]
    pieces = []
    for base in range(0, x.shape[0], period):
        x1 = x[base:base + h]
        x2 = x[base + h:base + 2 * h]
        pieces.append(x1 * cos - x2 * sin)
        pieces.append(x2 * cos + x1 * sin)
        if period > 2 * h:
            pieces.append(x[base + 2 * h:base + period])
    return jnp.concatenate(pieces, axis=0)


def _proj_kernel(x_ref, pos_ref, fd_ref, fm_ref, w_in_ref, gq_ref, wuq_ref, gkv_ref, wukv_ref,
                 dq_ref, dk_ref, dv_ref, mq_ref, mk_ref, mv_ref):
    xb = x_ref[...].astype(BF16)
    pos = pos_ref[...].astype(F32)
    ang_d = fd_ref[...] * pos
    ang_m = fm_ref[...] * pos
    cos_d, sin_d = jnp.cos(ang_d), jnp.sin(ang_d)
    cos_m, sin_m = jnp.cos(ang_m), jnp.sin(ang_m)

    dq = _rope_rows(_dot_nt(w_in_ref[0:N_DQ, :], xb), cos_d, sin_d, DIFF_QK_DIM)
    dq_ref[...] = (dq * (DIFF_QK_DIM ** -0.5 * LOG2E)).astype(BF16)
    dk = _rope_rows(_dot_nt(w_in_ref[O_DK:O_DK + N_DQ, :], xb), cos_d, sin_d, DIFF_QK_DIM)
    dk_ref[...] = dk.T.astype(BF16)
    dv_ref[...] = _with_ones_rows(_dot_nt(w_in_ref[O_DV:O_DV + N_DV, :], xb), DIFF_HEADS).astype(BF16)

    c_q = _dot_nt(w_in_ref[O_CQ:O_CQ + MLA_Q_RANK, :], xb)
    q = _dot(wuq_ref[...], _rms_rows(c_q, gq_ref[...]).astype(BF16))
    q = q * ((MLA_NOPE_DIM + MLA_ROPE_DIM) ** -0.5 * LOG2E)
    pieces = []
    for h in range(MLA_HEADS):
        o = h * MLA_QK_PAD
        pieces.append(q[o:o + MLA_NOPE_DIM])
        pieces.append(_rope_rows(q[o + MLA_NOPE_DIM:o + MLA_NOPE_DIM + MLA_ROPE_DIM], cos_m, sin_m, MLA_ROPE_DIM))
        pieces.append(q[o + MLA_NOPE_DIM + MLA_ROPE_DIM:o + MLA_QK_PAD])
    mq_ref[...] = jnp.concatenate(pieces, axis=0).astype(BF16)

    c_kv = _dot_nt(w_in_ref[O_CKV:O_CKV + MLA_KV_RANK, :], xb)
    kv = _dot(wukv_ref[...], _rms_rows(c_kv, gkv_ref[...]).astype(BF16))
    n_kn = MLA_HEADS * MLA_NOPE_DIM
    mv_ref[...] = _with_ones_rows(kv[n_kn:], MLA_HEADS).astype(BF16)
    k_rope = _dot_nt(w_in_ref[O_KR:O_KR + V7X_LANES, :], xb)
    k_pe = jnp.concatenate([_rope_rows(k_rope[:MLA_ROPE_DIM], cos_m, sin_m, MLA_ROPE_DIM),
                            k_rope[MLA_ROPE_DIM:]], axis=0)
    k_nope_t = kv[:n_kn].T.astype(BF16)
    k_pe_t = k_pe.T.astype(BF16)
    for h in range(MLA_HEADS):
        o = h * MLA_QK_PAD
        mk_ref[:, o:o + MLA_NOPE_DIM] = k_nope_t[:, h * MLA_NOPE_DIM:(h + 1) * MLA_NOPE_DIM]
        mk_ref[:, o + MLA_NOPE_DIM:o + MLA_QK_PAD] = k_pe_t


def _proj_call(x2, pos_row, fd, fm, w_in_t, gq, wuq_t, gkv, wukv_t, tm):
    t = x2.shape[0]
    full = lambda a: pl.BlockSpec(a.shape, lambda i: (0, 0))
    col_blk = lambda r: pl.BlockSpec((r, tm), lambda i: (0, i))
    row_blk = lambda w: pl.BlockSpec((tm, w), lambda i: (i, 0))
    n_mq = MLA_HEADS * MLA_QK_PAD
    return pl.pallas_call(
        _proj_kernel,
        out_shape=[jax.ShapeDtypeStruct((N_DQ, t), BF16), jax.ShapeDtypeStruct((t, N_DQ), BF16),
                   jax.ShapeDtypeStruct((DIFF_HEADS * V_AUG_DIM, t), BF16), jax.ShapeDtypeStruct((n_mq, t), BF16),
                   jax.ShapeDtypeStruct((t, n_mq), BF16), jax.ShapeDtypeStruct((MLA_HEADS * V_AUG_DIM, t), BF16)],
        grid=(t // tm,),
        in_specs=[row_blk(D_MODEL), col_blk(1), full(fd), full(fm), full(w_in_t), full(gq), full(wuq_t),
                  full(gkv), full(wukv_t)],
        out_specs=[col_blk(N_DQ), row_blk(N_DQ), col_blk(DIFF_HEADS * V_AUG_DIM), col_blk(n_mq), row_blk(n_mq),
                   col_blk(MLA_HEADS * V_AUG_DIM)],
        compiler_params=_cparams(("parallel",)),
        name="proj_rope",
    )(x2, pos_row, fd, fm, w_in_t, gq, wuq_t, gkv, wukv_t)


class _AttnStream:
    def __init__(self, qq, k_ref, vt_ref, scratch, i, tq):
        self.qq, self.k_ref, self.vt_ref, self.i, self.tq = qq, k_ref, vt_ref, i, tq
        self.s_refs, self.p_refs, self.acc_ref, self.m_ref = scratch[0:2], scratch[2:4], scratch[4], scratch[5]
        self.n = qq.shape[1]
        self.tk = self.s_refs[0].shape[0]
        assert tq == 2 * self.tk
        self.q_pos = i * tq + (lax.broadcasted_iota(jnp.int32, (self.tk, self.n), 1) & (tq - 1))
        self.key_row = lax.broadcasted_iota(jnp.int32, (self.tk, self.n), 0)
        self.late = [slice(a + tq // 2, a + tq) for a in range(0, self.n, tq)]

    def _chunk(self, c):
        return pl.ds(pl.multiple_of(c * self.tk, self.tk), self.tk)

    def scores(self, c, slot):
        s = _dot(self.k_ref[self._chunk(c), :], self.qq)
        self.s_refs[slot][...] = s
        return jnp.max(s, axis=0, keepdims=True)

    def softmax(self, c, slot, cmax, m, masked):
        s = self.s_refs[slot][...]
        if masked:
            s = jnp.where(self.key_row + c * self.tk <= self.q_pos, s, NEG_INF)
            cmax = jnp.max(s, axis=0, keepdims=True)
        m_new = jnp.maximum(m, cmax)
        alpha = jnp.exp2(m - m_new)
        self.p_refs[slot][...] = jnp.exp2((s - m_new).astype(BF16))
        return m_new, alpha

    def values(self, c, slot, alpha):
        self.acc_ref[...] = alpha * self.acc_ref[...] + _dot(self.vt_ref[:, self._chunk(c)], self.p_refs[slot][...])

    def scores_late(self, c, slot):
        kc = self.k_ref[self._chunk(c), :]
        for sl in self.late:
            self.s_refs[slot][:, sl] = _dot(kc, self.qq[:, sl])

    def softmax_late(self, c, slot, m):
        self.m_ref[...] = m
        alphas = []
        for sl in self.late:
            shape = (self.tk, sl.stop - sl.start)
            key_pos = lax.broadcasted_iota(jnp.int32, shape, 0) + c * self.tk
            query_pos = self.i * self.tq + ((lax.broadcasted_iota(jnp.int32, shape, 1) + sl.start) & (self.tq - 1))
            s = jnp.where(key_pos <= query_pos, self.s_refs[slot][:, sl], NEG_INF)
            m_old = self.m_ref[:, sl]
            m_new = jnp.maximum(m_old, jnp.max(s, axis=0, keepdims=True))
            alphas.append(jnp.exp2(m_old - m_new))
            self.p_refs[slot][:, sl] = jnp.exp2((s - m_new).astype(BF16))
        return alphas

    def values_late(self, c, slot, alphas):
        vc = self.vt_ref[:, self._chunk(c)]
        for sl, alpha in zip(self.late, alphas):
            self.acc_ref[:, sl] = alpha * self.acc_ref[:, sl] + _dot(vc, self.p_refs[slot][:, sl])

    def result(self):
        dv = self.vt_ref.shape[0] - V7X_BF16_SUBLANES
        return self.acc_ref[0:dv, :] / self.acc_ref[dv:dv + 1, :]


def _flash_streams(streams, i):
    def pair(u, carry, masked_second):
        c = 2 * u
        cm0 = [st.scores(c, 0) for st in streams]
        for st, (_, alpha0, _) in zip(streams, carry):
            st.values(c - 2, 0, alpha0)
        mid = [st.softmax(c - 1, 1, cm1, m, False) for st, (cm1, _, m) in zip(streams, carry)]
        if masked_second:
            for st in streams:
                st.scores_late(c + 1, 1)
            cm1s = [cm1 for cm1, _, _ in carry]
        else:
            cm1s = [st.scores(c + 1, 1) for st in streams]
        for st, (_, alpha1) in zip(streams, mid):
            st.values(c - 1, 1, alpha1)
        out = [st.softmax(c, 0, cm, m, masked_second) for st, cm, (m, _) in zip(streams, cm0, mid)]
        return tuple((cm1, alpha0, m) for cm1, (m, alpha0) in zip(cm1s, out))

    def drain(state):
        for st, (alpha0, _) in zip(streams, state):
            st.values(2 * i, 0, alpha0)
        lates = [st.softmax_late(2 * i + 1, 1, m) for st, (_, m) in zip(streams, state)]
        for st, alphas in zip(streams, lates):
            st.values_late(2 * i + 1, 1, alphas)

    for st in streams:
        st.acc_ref[...] = jnp.zeros(st.acc_ref.shape, F32)
    m_init = [jnp.full((1, st.n), NEG_INF, F32) for st in streams]

    @pl.when(i == 0)
    def _():
        for st in streams:
            st.scores(0, 0)
            st.scores_late(1, 1)
        state = [st.softmax(0, 0, None, m0, True) for st, m0 in zip(streams, m_init)]
        drain([(alpha0, m) for m, alpha0 in state])

    @pl.when(i > 0)
    def _():
        cm0 = [st.scores(0, 0) for st in streams]
        cm1 = [st.scores(1, 1) for st in streams]
        first = [st.softmax(0, 0, cm, m0, False) for st, cm, m0 in zip(streams, cm0, m_init)]
        carry = tuple((c1, alpha0, m) for c1, (m, alpha0) in zip(cm1, first))
        carry = lax.fori_loop(1, i, functools.partial(pair, masked_second=False), carry)
        carry = pair(i, carry, True)
        drain([(alpha0, m) for _, alpha0, m in carry])

    return [st.result() for st in streams]


def _mixer_attn_kernel(dq_ref, dk_ref, dvt_ref, mq_ref, mk_ref, mvt_ref, lam_ref, gd_ref, gm_ref, od_ref, om_ref,
                       *scratch, tq, lam_init):
    i = pl.program_id(2)
    qt = dq_ref[...]
    row = lax.broadcasted_iota(jnp.int32, qt.shape, 0)
    zero = jnp.zeros_like(qt)
    qq = jnp.concatenate([jnp.where(row < DIFF_QK_DIM, qt, zero), jnp.where(row >= DIFF_QK_DIM, qt, zero)], axis=1)
    streams = [_AttnStream(qq, dk_ref, dvt_ref, scratch[0:6], i, tq),
               _AttnStream(mq_ref[...], mk_ref, mvt_ref, scratch[6:12], i, tq)]
    o_diff, o_mla = _flash_streams(streams, i)
    lv = lam_ref[...]
    lam = (jnp.exp(jnp.sum(lv[0:1] * lv[1:2], axis=1, keepdims=True))
           - jnp.exp(jnp.sum(lv[2:3] * lv[3:4], axis=1, keepdims=True)) + lam_init)
    d = o_diff[:, :tq] - lam * o_diff[:, tq:]
    od_ref[...] = (_rms_rows(d, gd_ref[...]) * (1.0 - lam_init)).T.astype(od_ref.dtype)
    om_ref[...] = _rms_rows(o_mla, gm_ref[...]).T.astype(om_ref.dtype)


def _attn_scratch(tq, n):
    tk = tq // 2
    return [pltpu.VMEM((tk, n), F32), pltpu.VMEM((tk, n), F32), pltpu.VMEM((tk, n), BF16), pltpu.VMEM((tk, n), BF16),
            pltpu.VMEM((V_AUG_DIM, n), F32), pltpu.VMEM((1, n), F32)]


def _mixer_attn_call(dqt, dk, dvt, mqt, mk, mvt, lam_vecs, g_diff, g_mla, b, s, tq, lam_init):
    assert DIFF_HEADS == MLA_HEADS
    nq = s // tq
    t = b * s
    q_blk = lambda w: pl.BlockSpec((w, tq), lambda bi, h, i: (h, bi * nq + i))
    k_blk = lambda w: pl.BlockSpec((s, w), lambda bi, h, i: (bi, h))
    v_blk = pl.BlockSpec((V_AUG_DIM, s), lambda bi, h, i: (h, bi))
    small = lambda a: pl.BlockSpec(a.shape, lambda bi, h, i: (0, 0))
    o_blk = pl.BlockSpec((tq, DIFF_V_DIM), lambda bi, h, i: (bi * nq + i, h))
    return pl.pallas_call(
        functools.partial(_mixer_attn_kernel, tq=tq, lam_init=lam_init),
        out_shape=[jax.ShapeDtypeStruct((t, DIFF_HEADS * DIFF_V_DIM), BF16),
                   jax.ShapeDtypeStruct((t, MLA_HEADS * MLA_V_DIM), BF16)],
        grid=(b, DIFF_HEADS, nq),
        in_specs=[q_blk(2 * DIFF_QK_DIM), k_blk(2 * DIFF_QK_DIM), v_blk, q_blk(MLA_QK_PAD), k_blk(MLA_QK_PAD), v_blk,
                  small(lam_vecs), small(g_diff), small(g_mla)],
        out_specs=[o_blk, o_blk],
        scratch_shapes=_attn_scratch(tq, 2 * tq) + _attn_scratch(tq, tq),
        compiler_params=_cparams(("parallel", "parallel", "arbitrary")),
        name="mixer_attn",
    )(dqt, dk, dvt, mqt, mk, mvt, lam_vecs, g_diff, g_mla)


def _memkv_kernel(mem_ref, wk_ref, wv_ref, k_ref, v_ref):
    mb = mem_ref[...].astype(BF16)
    k_ref[...] = _dot(mb, wk_ref[...]).astype(BF16)
    v_ref[...] = _dot(mb, wv_ref[...]).astype(BF16)


def _memkv_call(mem2, wk, wv):
    n = mem2.shape[0]
    w = wk.shape[1]
    tm = N_MEM
    return pl.pallas_call(
        _memkv_kernel,
        out_shape=[jax.ShapeDtypeStruct((n, w), BF16)] * 2,
        grid=(n // tm,),
        in_specs=[pl.BlockSpec((tm, D_MODEL), lambda i: (i, 0)),
                  pl.BlockSpec(wk.shape, lambda i: (0, 0)),
                  pl.BlockSpec(wv.shape, lambda i: (0, 0))],
        out_specs=[pl.BlockSpec((tm, w), lambda i: (i, 0))] * 2,
        compiler_params=_cparams(("parallel",)),
        name="mem_kv",
    )(mem2, wk, wv)


def _pack_bf16_pairs(x):
    half = x.shape[1] // 2
    r = pltpu.bitcast(x.astype(BF16).astype(F32), jnp.uint32)
    return (r[:, :half] >> 16) | (r[:, half:] & jnp.uint32(0xFFFF0000))


def _unpack_bf16_pairs(w, dtype=BF16):
    lo = pltpu.bitcast(w << 16, F32)
    hi = pltpu.bitcast(w & jnp.uint32(0xFFFF0000), F32)
    return lo.astype(dtype), hi.astype(dtype)


def _post_kernel(da_ref, ma_ref, x_ref, wo_ref, g1_ref, b1_ref, wq_ref, km_ref, vm_ref, wmo_ref,
                 g2_ref, b2_ref, h_ref, hp_ref):
    n_d = DIFF_HEADS * DIFF_V_DIM
    mix = _dot(da_ref[...], wo_ref[0:n_d, :]) + _dot(ma_ref[...], wo_ref[n_d:, :])
    h1 = _layer_norm(DEEPNORM_ALPHA * x_ref[...] + mix, g1_ref[...], b1_ref[...])
    q = (_dot(h1.astype(BF16), wq_ref[...]) * (MEM_HEAD_DIM ** -0.5)).astype(BF16)
    outs = []
    for h in range(MEM_HEADS):
        sl = slice(h * MEM_HEAD_DIM, (h + 1) * MEM_HEAD_DIM)
        s = _dot_nt(q[:, sl], km_ref[:, sl])
        m = jnp.max(s, axis=1, keepdims=True)
        p = jnp.exp(s - m)
        p = p / jnp.sum(p, axis=1, keepdims=True)
        outs.append(_dot(p.astype(BF16), vm_ref[:, sl]))
    o = jnp.concatenate(outs, axis=1).astype(BF16)
    xat = _dot(o, wmo_ref[...])
    h2 = _layer_norm(DEEPNORM_ALPHA * h1 + xat, g2_ref[...], b2_ref[...])
    h_ref[...] = h2
    hp_ref[...] = _pack_bf16_pairs(h2)


def _post_call(diff_o, mla_o, x2, wo, g1, b1, wq, kmem, vmem, wmo, g2, b2, s, tm):
    t = x2.shape[0]
    per_b = s // tm
    row = lambda w: pl.BlockSpec((tm, w), lambda i: (i, 0))
    full = lambda a: pl.BlockSpec(a.shape, lambda i: (0, 0))
    memb = pl.BlockSpec((N_MEM, kmem.shape[1]), lambda i: (i // per_b, 0))
    return pl.pallas_call(
        _post_kernel,
        out_shape=[jax.ShapeDtypeStruct((t, D_MODEL), F32), jax.ShapeDtypeStruct((t, D_MODEL // 2), jnp.uint32)],
        grid=(t // tm,),
        in_specs=[row(diff_o.shape[1]), row(mla_o.shape[1]), row(D_MODEL), full(wo), full(g1), full(b1),
                  full(wq), memb, memb, full(wmo), full(g2), full(b2)],
        out_specs=[row(D_MODEL), row(D_MODEL // 2)],
        compiler_params=_cparams(("parallel",)),
        name="post_mix_xattn",
    )(diff_o, mla_o, x2, wo, g1, b1, wq, kmem, vmem, wmo, g2, b2)


def _router_kernel(h_ref, rwt_ref, bias_ref, tri_ref, eid_ref, rnk_ref, wt_ref, cnt_ref):
    tm = h_ref.shape[0]

    @pl.when(pl.program_id(0) == 0)
    def _():
        cnt_ref[...] = jnp.zeros(cnt_ref.shape, cnt_ref.dtype)

    h = h_ref[...]
    h_hi = h.astype(BF16)
    h_lo = (h - h_hi.astype(F32)).astype(BF16)
    w = rwt_ref[...]
    w_hi = w.astype(BF16)
    w_lo = (w - w_hi.astype(F32)).astype(BF16)
    both = _dot_nt(jnp.concatenate([w_hi, w_lo], axis=0), h_hi)
    logits = both[:N_EXPERTS] + (both[N_EXPERTS:] + _dot_nt(w_hi, h_lo))
    scores = jax.nn.sigmoid(logits)
    choice = scores + bias_ref[...]
    neg = float("-inf")

    c3 = choice.reshape(N_GROUPS, GROUP_SIZE, tm)
    mi = lax.broadcasted_iota(jnp.int32, c3.shape, 1)
    top1 = jnp.max(c3, axis=1, keepdims=True)
    first = jnp.min(jnp.where(c3 == top1, mi, GROUP_SIZE), axis=1, keepdims=True)
    top2 = jnp.max(jnp.where(mi == first, neg, c3), axis=1, keepdims=True)
    gs = (top1 + top2).reshape(N_GROUPS, tm)

    gi = lax.broadcasted_iota(jnp.int32, gs.shape, 0)
    rank = jnp.zeros(gs.shape, jnp.int32)
    for g in range(N_GROUPS):
        rowg = gs[g:g + 1, :]
        beats = (rowg > gs) | ((rowg == gs) & (g < gi))
        rank = rank + beats.astype(jnp.int32)
    gmask = rank < TOPK_GROUPS
    emask = jnp.broadcast_to(gmask.reshape(N_GROUPS, 1, tm), c3.shape).reshape(N_EXPERTS, tm)

    work = jnp.where(emask, choice, NEG_INF)
    ei = lax.broadcasted_iota(jnp.int32, work.shape, 0)
    sel = jnp.zeros(work.shape, jnp.bool_)
    picks = []
    for _ in range(TOP_K):
        mx = jnp.max(work, axis=0, keepdims=True)
        fi = jnp.min(jnp.where(work == mx, ei, N_EXPERTS), axis=0, keepdims=True)
        pick = ei == fi
        picks.append((fi, pick))
        sel = sel | pick
        work = jnp.where(pick, neg, work)
    top_w = jnp.where(sel, scores, 0.0)
    gates = top_w / jnp.sum(top_w, axis=0, keepdims=True) * ROUTED_SCALE

    sel_b = jnp.where(sel, 1.0, 0.0).astype(BF16)
    before = _dot(sel_b, tri_ref[...]).astype(jnp.int32)
    run = cnt_ref[:, 0:1]
    pos = run + before
    cnt_ref[...] = jnp.broadcast_to(run + jnp.sum(sel.astype(jnp.int32), axis=1, keepdims=True), cnt_ref.shape)

    eid_ref[...] = jnp.concatenate([fi for fi, _ in picks], axis=0)
    rnk_ref[...] = jnp.concatenate(
        [jnp.sum(jnp.where(pick, pos, 0), axis=0, keepdims=True) for _, pick in picks], axis=0)
    w_rows = [jnp.sum(jnp.where(pick, gates, 0.0), axis=0, keepdims=True) for _, pick in picks]
    w_pad = jnp.concatenate(w_rows + [jnp.zeros((V7X_LANES - TOP_K, tm), F32)], axis=0)
    wt_ref[...] = w_pad.T


def _router_call(h2, rwt, bias, tm):
    t = h2.shape[0]
    tri = jnp.triu(jnp.ones((tm, tm), BF16), k=1)
    full = lambda a: pl.BlockSpec(a.shape, lambda i: (0, 0))
    return pl.pallas_call(
        _router_kernel,
        out_shape=[jax.ShapeDtypeStruct((TOP_K, t), jnp.int32), jax.ShapeDtypeStruct((TOP_K, t), jnp.int32),
                   jax.ShapeDtypeStruct((t, V7X_LANES), F32), jax.ShapeDtypeStruct((N_EXPERTS, V7X_LANES), jnp.int32)],
        grid=(t // tm,),
        in_specs=[pl.BlockSpec((tm, D_MODEL), lambda i: (i, 0)), full(rwt), full(bias), full(tri)],
        out_specs=[pl.BlockSpec((TOP_K, tm), lambda i: (0, i)), pl.BlockSpec((TOP_K, tm), lambda i: (0, i)),
                   pl.BlockSpec((tm, V7X_LANES), lambda i: (i, 0)),
                   pl.BlockSpec((N_EXPERTS, V7X_LANES), lambda i: (0, 0))],
        compiler_params=_cparams(("arbitrary",)),
        name="router",
    )(h2, rwt, bias, tri)


_SC_MESH_AXES = ("core", "subcore")


def _sc_mesh():
    return plsc.VectorSubcoreMesh(core_axis_name=_SC_MESH_AXES[0], subcore_axis_name=_SC_MESH_AXES[1])


def _sc_worker_chunks(n_chunks):
    n_workers = V7X_SC_CORES * V7X_SC_SUBCORES
    assert n_chunks % n_workers == 0
    per_worker = n_chunks // n_workers
    first = (lax.axis_index(_SC_MESH_AXES[1]) * V7X_SC_CORES + lax.axis_index(_SC_MESH_AXES[0])) * per_worker
    return first, per_worker


def _sc_scatter_rows(x, idx, n_rows):
    t, w = x.shape
    n_chunks, n_k, window = idx.shape

    @functools.partial(pl.kernel, out_type=jax.ShapeDtypeStruct((n_rows, w), x.dtype), mesh=_sc_mesh(),
                       scratch_types=[pltpu.VMEM((n_k, window), jnp.int32), pltpu.VMEM((window, w), x.dtype)],
                       name="moe_dispatch_sc")
    def scatter(x_hbm, i_hbm, o_hbm, idx_v, rows_v):
        first, per_worker = _sc_worker_chunks(n_chunks)

        @pl.loop(0, per_worker)
        def _(c):
            chunk = first + c
            pltpu.sync_copy(i_hbm.at[chunk], idx_v)
            pltpu.sync_copy(x_hbm.at[pl.ds(chunk * window, window)], rows_v)
            for k in range(n_k):
                pltpu.sync_copy(rows_v, o_hbm.at[idx_v.at[k]])

    return scatter(x, idx)


def _sc_gather_rows(y, idx):
    n_chunks, n_k, window = idx.shape
    w = y.shape[1]

    half = window // 2
    steps = [(k, h) for k in range(n_k) for h in range(2)]

    @functools.partial(pl.kernel, out_type=jax.ShapeDtypeStruct((n_k, n_chunks * window, w), y.dtype),
                       mesh=_sc_mesh(),
                       scratch_types=[pltpu.VMEM((n_k, window), jnp.int32), pltpu.VMEM((half, w), y.dtype),
                                      pltpu.VMEM((half, w), y.dtype), pltpu.SemaphoreType.DMA,
                                      pltpu.SemaphoreType.DMA],
                       name="moe_combine_sc")
    def gather(y_hbm, i_hbm, o_hbm, idx_v, rows_a, rows_b, sem_a, sem_b):
        first, per_worker = _sc_worker_chunks(n_chunks)
        bufs = ((rows_a, sem_a), (rows_b, sem_b))

        @pl.loop(0, per_worker)
        def _(c):
            chunk = first + c
            pltpu.sync_copy(i_hbm.at[chunk], idx_v)

            def start(j):
                k, h = steps[j]
                buf, sem = bufs[j % 2]
                return pltpu.async_copy(y_hbm.at[idx_v.at[k, pl.ds(h * half, half)]], buf, sem)

            pending = start(0)
            for j, (k, h) in enumerate(steps):
                following = start(j + 1) if j + 1 < len(steps) else None
                pending.wait()
                pltpu.sync_copy(bufs[j % 2][0], o_hbm.at[k, pl.ds(chunk * window + h * half, half)])
                pending = following

    return gather(y, idx)


def _experts_kernel(te_ref, nu_ref, xs_ref, wg_ref, wu_ref, wd_ref, y_ref, wg_b, wu_b, wd_b):
    i = pl.program_id(0)

    @pl.when(i < nu_ref[0])
    def _():
        @pl.when((i == 0) | (te_ref[i] != te_ref[jnp.maximum(i - 1, 0)]))
        def _():
            wg_b[...] = wg_ref[0].astype(BF16)
            wu_b[...] = wu_ref[0].astype(BF16)
            wd_b[...] = wd_ref[0].astype(BF16)

        half = D_MODEL // 2
        lo, hi = _unpack_bf16_pairs(xs_ref[...])
        a = _dot(lo, wg_b[:half, :]) + _dot(hi, wg_b[half:, :])
        u = _dot(lo, wu_b[:half, :]) + _dot(hi, wu_b[half:, :])
        y_ref[...] = _pack_bf16_pairs(_dot((jax.nn.silu(a) * u).astype(BF16), wd_b[...]))


def _experts_call(tile_expert, n_used, xs, wg, wu, wd, tmx):
    n_rows, w = xs.shape
    tile = lambda i, te, nu: jnp.minimum(i, nu[0] - 1)
    wspec = lambda shp: pl.BlockSpec((1,) + shp, lambda i, te, nu: (te[tile(i, te, nu)], 0, 0))
    return pl.pallas_call(
        _experts_kernel,
        out_shape=jax.ShapeDtypeStruct((n_rows, w), xs.dtype),
        grid_spec=pltpu.PrefetchScalarGridSpec(
            num_scalar_prefetch=2,
            grid=(n_rows // tmx,),
            in_specs=[pl.BlockSpec((tmx, w), lambda i, te, nu: (tile(i, te, nu), 0)),
                      wspec((D_MODEL, EXPERT_DIM)), wspec((D_MODEL, EXPERT_DIM)), wspec((EXPERT_DIM, D_MODEL))],
            out_specs=pl.BlockSpec((tmx, w), lambda i, te, nu: (tile(i, te, nu), 0)),
            scratch_shapes=[pltpu.VMEM((D_MODEL, EXPERT_DIM), BF16), pltpu.VMEM((D_MODEL, EXPERT_DIM), BF16),
                            pltpu.VMEM((EXPERT_DIM, D_MODEL), BF16)],
        ),
        compiler_params=_cparams(("arbitrary",)),
        name="moe_experts",
    )(tile_expert, n_used, xs, wg, wu, wd)


def _combine_kernel(wt_ref, h_ref, yg_ref, sg_ref, su_ref, sd_ref, g3_ref, b3_ref, o_ref):
    h = h_ref[...]
    hb = h.astype(BF16)
    acc = _dot((jax.nn.silu(_dot(hb, sg_ref[...])) * _dot(hb, su_ref[...])).astype(BF16), sd_ref[...])
    wt = wt_ref[...]
    lo_sum = None
    hi_sum = None
    for k in range(TOP_K):
        lo, hi = _unpack_bf16_pairs(yg_ref[k], F32)
        wk = wt[:, k:k + 1]
        lo_sum = wk * lo if lo_sum is None else lo_sum + wk * lo
        hi_sum = wk * hi if hi_sum is None else hi_sum + wk * hi
    acc = acc + jnp.concatenate([lo_sum, hi_sum], axis=1)
    o_ref[...] = _layer_norm(DEEPNORM_ALPHA * h + acc, g3_ref[...], b3_ref[...])


def _combine_call(wt, h2, yg, sg, su, sd, g3, b3, tm):
    t = h2.shape[0]
    full = lambda a: pl.BlockSpec(a.shape, lambda i: (0, 0))
    return pl.pallas_call(
        _combine_kernel,
        out_shape=jax.ShapeDtypeStruct((t, D_MODEL), F32),
        grid=(t // tm,),
        in_specs=[pl.BlockSpec((tm, V7X_LANES), lambda i: (i, 0)),
                  pl.BlockSpec((tm, D_MODEL), lambda i: (i, 0)),
                  pl.BlockSpec((TOP_K, tm, yg.shape[2]), lambda i: (0, i, 0)),
                  full(sg), full(su), full(sd), full(g3), full(b3)],
        out_specs=pl.BlockSpec((tm, D_MODEL), lambda i: (i, 0)),
        compiler_params=_cparams(("parallel",)),
        name="moe_combine",
    )(wt, h2, yg, sg, su, sd, g3, b3)


def _moe(h2, hp, router_w, router_bias, wg, wu, wd, sg, su, sd, g3, b3, tm_route, tmx, tm_comb):
    t = h2.shape[0]
    eid, rnk, wt, cnt = _router_call(h2, router_w.T, router_bias.reshape(-1, 1), tm_route)
    counts = cnt[:, 0]
    tiles_per_e = (counts + (tmx - 1)) // tmx
    tile_end = jnp.cumsum(tiles_per_e)
    row_start = ((tile_end - tiles_per_e) * tmx).astype(jnp.int32)
    n_tiles = (t * TOP_K) // tmx + N_EXPERTS
    n_used = tile_end[-1:].astype(jnp.int32)
    tile_expert = jnp.minimum(
        jnp.sum(tile_end[None, :] <= jnp.arange(n_tiles, dtype=jnp.int32)[:, None], axis=1), N_EXPERTS - 1
    ).astype(jnp.int32)
    expert_ids = jnp.arange(N_EXPERTS, dtype=jnp.int32)[:, None, None]
    dest = rnk + jnp.sum(jnp.where(eid[None] == expert_ids, row_start[:, None, None], 0), axis=0)
    dest_chunks = dest.reshape(TOP_K, t // SC_INDEX_WINDOW, SC_INDEX_WINDOW).transpose(1, 0, 2)
    xs = _sc_scatter_rows(hp, dest_chunks, n_tiles * tmx)
    yp = _experts_call(tile_expert, n_used, xs, wg, wu, wd, tmx)
    yg = _sc_gather_rows(yp, dest_chunks)
    return _combine_call(wt, h2, yg, sg, su, sd, g3, b3, tm_comb)


def _prep_w_in_t(w_in):
    pad = jnp.zeros((D_MODEL, V7X_LANES - MLA_ROPE_DIM), w_in.dtype)
    return jnp.concatenate([w_in, pad], axis=1).T.astype(BF16)


def _prep_w_uq_t(w_uq):
    w = w_uq.reshape(MLA_Q_RANK, MLA_HEADS, MLA_NOPE_DIM + MLA_ROPE_DIM)
    pad = jnp.zeros((MLA_Q_RANK, MLA_HEADS, MLA_QK_PAD - MLA_NOPE_DIM - MLA_ROPE_DIM), w.dtype)
    return jnp.concatenate([w, pad], axis=2).reshape(MLA_Q_RANK, MLA_HEADS * MLA_QK_PAD).T.astype(BF16)


def _prep_w_ukv_t(w_ukv):
    w = w_ukv.reshape(MLA_KV_RANK, MLA_HEADS, MLA_NOPE_DIM + MLA_V_DIM)
    k = w[:, :, :MLA_NOPE_DIM].reshape(MLA_KV_RANK, MLA_HEADS * MLA_NOPE_DIM)
    v = w[:, :, MLA_NOPE_DIM:].reshape(MLA_KV_RANK, MLA_HEADS * MLA_V_DIM)
    return jnp.concatenate([k, v], axis=1).T.astype(BF16)


def _rope_inv_freq(dim):
    return (ROPE_THETA ** (-jnp.arange(0, dim, 2, dtype=F32) / dim)).reshape(-1, 1)


def _layer(x2, mem2, pos_row, b, s, l, w_in, lam_q1, lam_k1, lam_q2, lam_k2, diff_subln_g, mla_q_norm_g,
           mla_w_uq, mla_kv_norm_g, mla_w_ukv, mla_out_norm_g, w_o, ln1_g, ln1_b, mem_w_q, mem_w_k, mem_w_v,
           mem_w_o, ln2_g, ln2_b, router_w, router_bias, exp_w_gate, exp_w_up, exp_w_down, sh_w_gate,
           sh_w_up, sh_w_down, ln3_g, ln3_b):
    row = lambda a: a.reshape(1, -1)
    col = lambda a: a.reshape(-1, 1)
    dqt, dk, dvt, mqt, mk, mvt = _proj_call(
        x2, pos_row, _rope_inv_freq(DIFF_ROT_DIM), _rope_inv_freq(MLA_ROPE_DIM), _prep_w_in_t(w_in),
        col(mla_q_norm_g), _prep_w_uq_t(mla_w_uq), col(mla_kv_norm_g), _prep_w_ukv_t(mla_w_ukv),
        tm=min(TOKEN_TILE, s))

    lam_init = 0.8 - 0.6 * math.exp(-0.3 * l)
    lam_vecs = jnp.stack([lam_q1, lam_k1, lam_q2, lam_k2]).astype(F32)
    diff_o, mla_o = _mixer_attn_call(dqt, dk, dvt, mqt, mk, mvt, lam_vecs, col(diff_subln_g), col(mla_out_norm_g),
                                     b, s, min(ATTN_QUERY_TILE, s), lam_init)

    kmem, vmem = _memkv_call(mem2, mem_w_k.astype(BF16), mem_w_v.astype(BF16))
    h2, hp = _post_call(diff_o, mla_o, x2, w_o.astype(BF16), row(ln1_g), row(ln1_b), mem_w_q.astype(BF16),
                        kmem, vmem, mem_w_o.astype(BF16), row(ln2_g), row(ln2_b), s, tm=min(TOKEN_TILE, s))

    return _moe(h2, hp, router_w, router_bias, exp_w_gate, exp_w_up, exp_w_down,
                sh_w_gate.astype(BF16), sh_w_up.astype(BF16), sh_w_down.astype(BF16), row(ln3_g), row(ln3_b),
                tm_route=min(TOKEN_TILE, s), tmx=EXPERT_ROW_TILE, tm_comb=min(COMBINE_TOKEN_TILE, s))


def kernel(x, mem, positions, w_in, lam_q1, lam_k1, lam_q2, lam_k2, diff_subln_g, mla_q_norm_g, mla_w_uq,
           mla_kv_norm_g, mla_w_ukv, mla_out_norm_g, w_o, ln1_g, ln1_b, mem_w_q, mem_w_k, mem_w_v, mem_w_o,
           ln2_g, ln2_b, router_w, router_bias, exp_w_gate, exp_w_up, exp_w_down, sh_w_gate, sh_w_up,
           sh_w_down, ln3_g, ln3_b):
    b, s, d = x.shape
    h = x.reshape(b * s, d)
    mem2 = mem.reshape(b * mem.shape[1], d)
    pos_row = positions.reshape(1, b * s)
    params = (w_in, lam_q1, lam_k1, lam_q2, lam_k2, diff_subln_g, mla_q_norm_g, mla_w_uq, mla_kv_norm_g,
              mla_w_ukv, mla_out_norm_g, w_o, ln1_g, ln1_b, mem_w_q, mem_w_k, mem_w_v, mem_w_o, ln2_g, ln2_b,
              router_w, router_bias, exp_w_gate, exp_w_up, exp_w_down, sh_w_gate, sh_w_up, sh_w_down,
              ln3_g, ln3_b)
    for l in range(w_in.shape[0]):
        h = _layer(h, mem2, pos_row, b, s, l, *[p[l] for p in params])
    return h.reshape(b, s, d)
```

```python
import functools
import math

import jax
import jax.numpy as jnp
from jax import lax
from jax.experimental import pallas as pl
from jax.experimental.pallas import tpu as pltpu
from jax.experimental.pallas import tpu_sc as plsc

F32 = jnp.float32
BF16 = jnp.bfloat16

D_MODEL = 1024
N_MEM = 256
ROPE_THETA = 500000.0
NEG_INF = -1e30
LN_EPS = 1e-5
RMS_EPS = 1e-6
DIFF_HEADS = 4
DIFF_QK_DIM = 64
DIFF_V_DIM = 128
DIFF_ROT_DIM = DIFF_QK_DIM // 4
MLA_HEADS = 4
MLA_Q_RANK = 256
MLA_KV_RANK = 128
MLA_NOPE_DIM = 128
MLA_ROPE_DIM = 64
MLA_V_DIM = 128
MLA_QK_PAD = 256
MEM_HEADS = 4
MEM_HEAD_DIM = 128
N_EXPERTS = 64
TOP_K = 8
N_GROUPS = 8
TOPK_GROUPS = 4
GROUP_SIZE = N_EXPERTS // N_GROUPS
EXPERT_DIM = 256
ROUTED_SCALE = 2.5
DEPTH = 1
DEEPNORM_ALPHA = (2.0 * DEPTH) ** 0.25
LOG2E = math.log2(math.e)

V7X_LANES = 128
V7X_VMEM_LIMIT_BYTES = 56 * 1024 * 1024
V7X_SC_CORES = 2
V7X_SC_SUBCORES = 16
SC_INDEX_WINDOW = 128
V7X_BF16_SUBLANES = 16
V_AUG_DIM = DIFF_V_DIM + V7X_BF16_SUBLANES

TOKEN_TILE = 1024
ATTN_QUERY_TILE = 512
EXPERT_ROW_TILE = 1024
COMBINE_TOKEN_TILE = 512
ATTN_HEADS_PER_STEP = 2

N_DQ = DIFF_HEADS * 2 * DIFF_QK_DIM
N_DV = DIFF_HEADS * DIFF_V_DIM
O_DK = N_DQ
O_DV = 2 * N_DQ
O_CQ = O_DV + N_DV
O_CKV = O_CQ + MLA_Q_RANK
O_KR = O_CKV + MLA_KV_RANK


def _cparams(sem):
    return pltpu.CompilerParams(dimension_semantics=sem, vmem_limit_bytes=V7X_VMEM_LIMIT_BYTES)


def _dot(a, b):
    return jnp.dot(a, b, preferred_element_type=F32)


def _dot_nt(a, b):
    return lax.dot_general(a, b, (((1,), (1,)), ((), ())), preferred_element_type=F32)


def _rms_rows(x, g):
    return x * lax.rsqrt(jnp.mean(jnp.square(x), axis=0, keepdims=True) + RMS_EPS) * g


def _layer_norm(x, g, b):
    mu = jnp.mean(x, axis=-1, keepdims=True)
    xc = x - mu
    var = jnp.mean(jnp.square(xc), axis=-1, keepdims=True)
    return xc * lax.rsqrt(var + LN_EPS) * g + b


def _with_ones_rows(vt, heads):
    dv = vt.shape[0] // heads
    ones = jnp.ones((V_AUG_DIM - dv, vt.shape[1]), vt.dtype)
    return jnp.concatenate([p for h in range(heads) for p in (vt[h * dv:(h + 1) * dv], ones)], axis=0)


def _rope_rows(x, cos, sin, period):
    h = cos.shape[0]
    pieces = []
    for base in range(0, x.shape[0], period):
        x1 = x[base:base + h]
        x2 = x[base + h:base + 2 * h]
        pieces.append(x1 * cos - x2 * sin)
        pieces.append(x2 * cos + x1 * sin)
        if period > 2 * h:
            pieces.append(x[base + 2 * h:base + period])
    return jnp.concatenate(pieces, axis=0)


def _proj_kernel(x_ref, pos_ref, fd_ref, fm_ref, w_in_ref, gq_ref, wuq_ref, gkv_ref, wukv_ref,
                 dq_ref, dk_ref, dv_ref, mq_ref, mk_ref, mv_ref):
    xb = x_ref[...].astype(BF16)
    pos = pos_ref[...].astype(F32)
    ang_d = fd_ref[...] * pos
    ang_m = fm_ref[...] * pos
    cos_d, sin_d = jnp.cos(ang_d), jnp.sin(ang_d)
    cos_m, sin_m = jnp.cos(ang_m), jnp.sin(ang_m)

    dq = _rope_rows(_dot_nt(w_in_ref[0:N_DQ, :], xb), cos_d, sin_d, DIFF_QK_DIM)
    dq_ref[...] = (dq * (DIFF_QK_DIM ** -0.5 * LOG2E)).astype(BF16)
    dk = _rope_rows(_dot_nt(w_in_ref[O_DK:O_DK + N_DQ, :], xb), cos_d, sin_d, DIFF_QK_DIM)
    dk_ref[...] = dk.T.astype(BF16)
    dv_ref[...] = _with_ones_rows(_dot_nt(w_in_ref[O_DV:O_DV + N_DV, :], xb), DIFF_HEADS).astype(BF16)

    c_q = _dot_nt(w_in_ref[O_CQ:O_CQ + MLA_Q_RANK, :], xb)
    q = _dot(wuq_ref[...], _rms_rows(c_q, gq_ref[...]).astype(BF16))
    q = q * ((MLA_NOPE_DIM + MLA_ROPE_DIM) ** -0.5 * LOG2E)
    pieces = []
    for h in range(MLA_HEADS):
        o = h * MLA_QK_PAD
        pieces.append(q[o:o + MLA_NOPE_DIM])
        pieces.append(_rope_rows(q[o + MLA_NOPE_DIM:o + MLA_NOPE_DIM + MLA_ROPE_DIM], cos_m, sin_m, MLA_ROPE_DIM))
        pieces.append(q[o + MLA_NOPE_DIM + MLA_ROPE_DIM:o + MLA_QK_PAD])
    mq_ref[...] = jnp.concatenate(pieces, axis=0).astype(BF16)

    c_kv = _dot_nt(w_in_ref[O_CKV:O_CKV + MLA_KV_RANK, :], xb)
    kv = _dot(wukv_ref[...], _rms_rows(c_kv, gkv_ref[...]).astype(BF16))
    n_kn = MLA_HEADS * MLA_NOPE_DIM
    mv_ref[...] = _with_ones_rows(kv[n_kn:], MLA_HEADS).astype(BF16)
    k_rope = _dot_nt(w_in_ref[O_KR:O_KR + V7X_LANES, :], xb)
    k_pe = jnp.concatenate([_rope_rows(k_rope[:MLA_ROPE_DIM], cos_m, sin_m, MLA_ROPE_DIM),
                            k_rope[MLA_ROPE_DIM:]], axis=0)
    k_nope_t = kv[:n_kn].T.astype(BF16)
    k_pe_t = k_pe.T.astype(BF16)
    for h in range(MLA_HEADS):
        o = h * MLA_QK_PAD
        mk_ref[:, o:o + MLA_NOPE_DIM] = k_nope_t[:, h * MLA_NOPE_DIM:(h + 1) * MLA_NOPE_DIM]
        mk_ref[:, o + MLA_NOPE_DIM:o + MLA_QK_PAD] = k_pe_t


def _proj_call(x2, pos_row, fd, fm, w_in_t, gq, wuq_t, gkv, wukv_t, tm):
    t = x2.shape[0]
    full = lambda a: pl.BlockSpec(a.shape, lambda i: (0, 0))
    col_blk = lambda r: pl.BlockSpec((r, tm), lambda i: (0, i))
    row_blk = lambda w: pl.BlockSpec((tm, w), lambda i: (i, 0))
    n_mq = MLA_HEADS * MLA_QK_PAD
    return pl.pallas_call(
        _proj_kernel,
        out_shape=[jax.ShapeDtypeStruct((N_DQ, t), BF16), jax.ShapeDtypeStruct((t, N_DQ), BF16),
                   jax.ShapeDtypeStruct((DIFF_HEADS * V_AUG_DIM, t), BF16), jax.ShapeDtypeStruct((n_mq, t), BF16),
                   jax.ShapeDtypeStruct((t, n_mq), BF16), jax.ShapeDtypeStruct((MLA_HEADS * V_AUG_DIM, t), BF16)],
        grid=(t // tm,),
        in_specs=[row_blk(D_MODEL), col_blk(1), full(fd), full(fm), full(w_in_t), full(gq), full(wuq_t),
                  full(gkv), full(wukv_t)],
        out_specs=[col_blk(N_DQ), row_blk(N_DQ), col_blk(DIFF_HEADS * V_AUG_DIM), col_blk(n_mq), row_blk(n_mq),
                   col_blk(MLA_HEADS * V_AUG_DIM)],
        compiler_params=_cparams(("parallel",)),
        name="proj_rope",
    )(x2, pos_row, fd, fm, w_in_t, gq, wuq_t, gkv, wukv_t)


class _AttnStream:
    def __init__(self, qq, k_ref, vt_ref, scratch, i, tq):
        self.qq, self.k_ref, self.vt_ref, self.i, self.tq = qq, k_ref, vt_ref, i, tq
        self.s_refs, self.p_refs, self.acc_ref, self.m_ref = scratch[0:2], scratch[2:4], scratch[4], scratch[5]
        self.n = qq.shape[1]
        self.tk = self.s_refs[0].shape[0]
        assert tq == 2 * self.tk
        self.q_pos = i * tq + (lax.broadcasted_iota(jnp.int32, (self.tk, self.n), 1) & (tq - 1))
        self.key_row = lax.broadcasted_iota(jnp.int32, (self.tk, self.n), 0)
        self.late = [slice(a + tq // 2, a + tq) for a in range(0, self.n, tq)]

    def _chunk(self, c):
        return pl.ds(pl.multiple_of(c * self.tk, self.tk), self.tk)

    def scores(self, c, slot):
        s = _dot(self.k_ref[self._chunk(c), :], self.qq)
        self.s_refs[slot][...] = s
        return jnp.max(s, axis=0, keepdims=True)

    def softmax(self, c, slot, cmax, m, masked):
        s = self.s_refs[slot][...]
        if masked:
            s = jnp.where(self.key_row + c * self.tk <= self.q_pos, s, NEG_INF)
            cmax = jnp.max(s, axis=0, keepdims=True)
        m_new = jnp.maximum(m, cmax)
        alpha = jnp.exp2(m - m_new)
        self.p_refs[slot][...] = jnp.exp2((s - m_new).astype(BF16))
        return m_new, alpha

    def values(self, c, slot, alpha):
        self.acc_ref[...] = alpha * self.acc_ref[...] + _dot(self.vt_ref[:, self._chunk(c)], self.p_refs[slot][...])

    def scores_late(self, c, slot):
        kc = self.k_ref[self._chunk(c), :]
        for sl in self.late:
            self.s_refs[slot][:, sl] = _dot(kc, self.qq[:, sl])

    def softmax_late(self, c, slot, m):
        self.m_ref[...] = m
        alphas = []
        for sl in self.late:
            shape = (self.tk, sl.stop - sl.start)
            key_pos = lax.broadcasted_iota(jnp.int32, shape, 0) + c * self.tk
            query_pos = self.i * self.tq + ((lax.broadcasted_iota(jnp.int32, shape, 1) + sl.start) & (self.tq - 1))
            s = jnp.where(key_pos <= query_pos, self.s_refs[slot][:, sl], NEG_INF)
            m_old = self.m_ref[:, sl]
            m_new = jnp.maximum(m_old, jnp.max(s, axis=0, keepdims=True))
            alphas.append(jnp.exp2(m_old - m_new))
            self.p_refs[slot][:, sl] = jnp.exp2((s - m_new).astype(BF16))
        return alphas

    def values_late(self, c, slot, alphas):
        vc = self.vt_ref[:, self._chunk(c)]
        for sl, alpha in zip(self.late, alphas):
            self.acc_ref[:, sl] = alpha * self.acc_ref[:, sl] + _dot(vc, self.p_refs[slot][:, sl])

    def result(self):
        dv = self.vt_ref.shape[0] - V7X_BF16_SUBLANES
        return self.acc_ref[0:dv, :] / self.acc_ref[dv:dv + 1, :]


def _flash_streams(streams, i):
    def pair(u, carry, masked_second):
        c = 2 * u
        cm0 = [st.scores(c, 0) for st in streams]
        for st, (_, alpha0, _) in zip(streams, carry):
            st.values(c - 2, 0, alpha0)
        mid = [st.softmax(c - 1, 1, cm1, m, False) for st, (cm1, _, m) in zip(streams, carry)]
        if masked_second:
            for st in streams:
                st.scores_late(c + 1, 1)
            cm1s = [cm1 for cm1, _, _ in carry]
        else:
            cm1s = [st.scores(c + 1, 1) for st in streams]
        for st, (_, alpha1) in zip(streams, mid):
            st.values(c - 1, 1, alpha1)
        out = [st.softmax(c, 0, cm, m, masked_second) for st, cm, (m, _) in zip(streams, cm0, mid)]
        return tuple((cm1, alpha0, m) for cm1, (m, alpha0) in zip(cm1s, out))

    def drain(state):
        for st, (alpha0, _) in zip(streams, state):
            st.values(2 * i, 0, alpha0)
        lates = [st.softmax_late(2 * i + 1, 1, m) for st, (_, m) in zip(streams, state)]
        for st, alphas in zip(streams, lates):
            st.values_late(2 * i + 1, 1, alphas)

    for st in streams:
        st.acc_ref[...] = jnp.zeros(st.acc_ref.shape, F32)
    m_init = [jnp.full((1, st.n), NEG_INF, F32) for st in streams]

    @pl.when(i == 0)
    def _():
        for st in streams:
            st.scores(0, 0)
            st.scores_late(1, 1)
        state = [st.softmax(0, 0, None, m0, True) for st, m0 in zip(streams, m_init)]
        drain([(alpha0, m) for m, alpha0 in state])

    @pl.when(i > 0)
    def _():
        cm0 = [st.scores(0, 0) for st in streams]
        cm1 = [st.scores(1, 1) for st in streams]
        first = [st.softmax(0, 0, cm, m0, False) for st, cm, m0 in zip(streams, cm0, m_init)]
        carry = tuple((c1, alpha0, m) for c1, (m, alpha0) in zip(cm1, first))
        carry = lax.fori_loop(1, i, functools.partial(pair, masked_second=False), carry)
        carry = pair(i, carry, True)
        drain([(alpha0, m) for _, alpha0, m in carry])

    return [st.result() for st in streams]


def _mixer_attn_kernel(dq_ref, dk_ref, dvt_ref, mq_ref, mk_ref, mvt_ref, lam_ref, gd_ref, gm_ref, od_ref, om_ref,
                       *scratch, tq, lam_init):
    i = pl.program_id(2)
    dqk, mqk = 2 * DIFF_QK_DIM, MLA_QK_PAD
    streams = []
    for hh in range(ATTN_HEADS_PER_STEP):
        qt = dq_ref[hh * dqk:(hh + 1) * dqk, :]
        row = lax.broadcasted_iota(jnp.int32, qt.shape, 0)
        zero = jnp.zeros_like(qt)
        qq = jnp.concatenate([jnp.where(row < DIFF_QK_DIM, qt, zero), jnp.where(row >= DIFF_QK_DIM, qt, zero)],
                             axis=1)
        sc = scratch[12 * hh:12 * (hh + 1)]
        streams.append(_AttnStream(qq, dk_ref.at[:, pl.ds(hh * dqk, dqk)],
                                   dvt_ref.at[pl.ds(hh * V_AUG_DIM, V_AUG_DIM), :], sc[0:6], i, tq))
        streams.append(_AttnStream(mq_ref[hh * mqk:(hh + 1) * mqk, :], mk_ref.at[:, pl.ds(hh * mqk, mqk)],
                                   mvt_ref.at[pl.ds(hh * V_AUG_DIM, V_AUG_DIM), :], sc[6:12], i, tq))
    outs = _flash_streams(streams, i)
    lv = lam_ref[...]
    lam = (jnp.exp(jnp.sum(lv[0:1] * lv[1:2], axis=1, keepdims=True))
           - jnp.exp(jnp.sum(lv[2:3] * lv[3:4], axis=1, keepdims=True)) + lam_init)
    for hh in range(ATTN_HEADS_PER_STEP):
        o_diff, o_mla = outs[2 * hh], outs[2 * hh + 1]
        d = o_diff[:, :tq] - lam * o_diff[:, tq:]
        cols = slice(hh * DIFF_V_DIM, (hh + 1) * DIFF_V_DIM)
        od_ref[:, cols] = (_rms_rows(d, gd_ref[...]) * (1.0 - lam_init)).T.astype(od_ref.dtype)
        om_ref[:, cols] = _rms_rows(o_mla, gm_ref[...]).T.astype(om_ref.dtype)


def _attn_scratch(tq, n):
    tk = tq // 2
    return [pltpu.VMEM((tk, n), F32), pltpu.VMEM((tk, n), F32), pltpu.VMEM((tk, n), BF16), pltpu.VMEM((tk, n), BF16),
            pltpu.VMEM((V_AUG_DIM, n), F32), pltpu.VMEM((1, n), F32)]


def _mixer_attn_call(dqt, dk, dvt, mqt, mk, mvt, lam_vecs, g_diff, g_mla, b, s, tq, lam_init):
    assert DIFF_HEADS == MLA_HEADS and DIFF_V_DIM == MLA_V_DIM
    nq = s // tq
    t = b * s
    hps = ATTN_HEADS_PER_STEP
    q_blk = lambda w: pl.BlockSpec((hps * w, tq), lambda bi, h, i: (h, bi * nq + i))
    k_blk = lambda w: pl.BlockSpec((s, hps * w), lambda bi, h, i: (bi, h))
    v_blk = pl.BlockSpec((hps * V_AUG_DIM, s), lambda bi, h, i: (h, bi))
    small = lambda a: pl.BlockSpec(a.shape, lambda bi, h, i: (0, 0))
    o_blk = pl.BlockSpec((tq, hps * DIFF_V_DIM), lambda bi, h, i: (bi * nq + i, h))
    return pl.pallas_call(
        functools.partial(_mixer_attn_kernel, tq=tq, lam_init=lam_init),
        out_shape=[jax.ShapeDtypeStruct((t, DIFF_HEADS * DIFF_V_DIM), BF16),
                   jax.ShapeDtypeStruct((t, MLA_HEADS * MLA_V_DIM), BF16)],
        grid=(b, DIFF_HEADS // hps, nq),
        in_specs=[q_blk(2 * DIFF_QK_DIM), k_blk(2 * DIFF_QK_DIM), v_blk, q_blk(MLA_QK_PAD), k_blk(MLA_QK_PAD), v_blk,
                  small(lam_vecs), small(g_diff), small(g_mla)],
        out_specs=[o_blk, o_blk],
        scratch_shapes=(_attn_scratch(tq, 2 * tq) + _attn_scratch(tq, tq)) * hps,
        compiler_params=_cparams(("parallel", "parallel", "arbitrary")),
        name="mixer_attn",
    )(dqt, dk, dvt, mqt, mk, mvt, lam_vecs, g_diff, g_mla)


def _memkv_kernel(mem_ref, wk_ref, wv_ref, k_ref, v_ref):
    mb = mem_ref[...].astype(BF16)
    k_ref[...] = _dot(mb, wk_ref[...]).astype(BF16)
    v_ref[...] = _dot(mb, wv_ref[...]).astype(BF16)


def _memkv_call(mem2, wk, wv):
    n = mem2.shape[0]
    w = wk.shape[1]
    tm = N_MEM
    return pl.pallas_call(
        _memkv_kernel,
        out_shape=[jax.ShapeDtypeStruct((n, w), BF16)] * 2,
        grid=(n // tm,),
        in_specs=[pl.BlockSpec((tm, D_MODEL), lambda i: (i, 0)),
                  pl.BlockSpec(wk.shape, lambda i: (0, 0)),
                  pl.BlockSpec(wv.shape, lambda i: (0, 0))],
        out_specs=[pl.BlockSpec((tm, w), lambda i: (i, 0))] * 2,
        compiler_params=_cparams(("parallel",)),
        name="mem_kv",
    )(mem2, wk, wv)


def _pack_bf16_pairs(x):
    half = x.shape[1] // 2
    r = pltpu.bitcast(x.astype(BF16).astype(F32), jnp.uint32)
    return (r[:, :half] >> 16) | (r[:, half:] & jnp.uint32(0xFFFF0000))


def _unpack_bf16_pairs(w, dtype=BF16):
    lo = pltpu.bitcast(w << 16, F32)
    hi = pltpu.bitcast(w & jnp.uint32(0xFFFF0000), F32)
    return lo.astype(dtype), hi.astype(dtype)


def _post_kernel(da_ref, ma_ref, x_ref, wo_ref, g1_ref, b1_ref, wq_ref, km_ref, vm_ref, wmo_ref,
                 g2_ref, b2_ref, h_ref, hp_ref):
    n_d = DIFF_HEADS * DIFF_V_DIM
    mix = _dot(da_ref[...], wo_ref[0:n_d, :]) + _dot(ma_ref[...], wo_ref[n_d:, :])
    h1 = _layer_norm(DEEPNORM_ALPHA * x_ref[...] + mix, g1_ref[...], b1_ref[...])
    q = (_dot(h1.astype(BF16), wq_ref[...]) * (MEM_HEAD_DIM ** -0.5)).astype(BF16)
    outs = []
    for h in range(MEM_HEADS):
        sl = slice(h * MEM_HEAD_DIM, (h + 1) * MEM_HEAD_DIM)
        s = _dot_nt(q[:, sl], km_ref[:, sl])
        m = jnp.max(s, axis=1, keepdims=True)
        p = jnp.exp(s - m)
        p = p / jnp.sum(p, axis=1, keepdims=True)
        outs.append(_dot(p.astype(BF16), vm_ref[:, sl]))
    o = jnp.concatenate(outs, axis=1).astype(BF16)
    xat = _dot(o, wmo_ref[...])
    h2 = _layer_norm(DEEPNORM_ALPHA * h1 + xat, g2_ref[...], b2_ref[...])
    h_ref[...] = h2
    hp_ref[...] = _pack_bf16_pairs(h2)


def _post_call(diff_o, mla_o, x2, wo, g1, b1, wq, kmem, vmem, wmo, g2, b2, s, tm):
    t = x2.shape[0]
    per_b = s // tm
    row = lambda w: pl.BlockSpec((tm, w), lambda i: (i, 0))
    full = lambda a: pl.BlockSpec(a.shape, lambda i: (0, 0))
    memb = pl.BlockSpec((N_MEM, kmem.shape[1]), lambda i: (i // per_b, 0))
    return pl.pallas_call(
        _post_kernel,
        out_shape=[jax.ShapeDtypeStruct((t, D_MODEL), F32), jax.ShapeDtypeStruct((t, D_MODEL // 2), jnp.uint32)],
        grid=(t // tm,),
        in_specs=[row(diff_o.shape[1]), row(mla_o.shape[1]), row(D_MODEL), full(wo), full(g1), full(b1),
                  full(wq), memb, memb, full(wmo), full(g2), full(b2)],
        out_specs=[row(D_MODEL), row(D_MODEL // 2)],
        compiler_params=_cparams(("parallel",)),
        name="post_mix_xattn",
    )(diff_o, mla_o, x2, wo, g1, b1, wq, kmem, vmem, wmo, g2, b2)


def _router_kernel(h_ref, rwt_ref, bias_ref, tri_ref, eid_ref, rnk_ref, wt_ref, cnt_ref):
    tm = h_ref.shape[0]

    @pl.when(pl.program_id(0) == 0)
    def _():
        cnt_ref[...] = jnp.zeros(cnt_ref.shape, cnt_ref.dtype)

    h = h_ref[...]
    h_hi = h.astype(BF16)
    h_lo = (h - h_hi.astype(F32)).astype(BF16)
    w = rwt_ref[...]
    w_hi = w.astype(BF16)
    w_lo = (w - w_hi.astype(F32)).astype(BF16)
    both = _dot_nt(jnp.concatenate([w_hi, w_lo], axis=0), h_hi)
    logits = both[:N_EXPERTS] + (both[N_EXPERTS:] + _dot_nt(w_hi, h_lo))
    scores = jax.nn.sigmoid(logits)
    choice = scores + bias_ref[...]
    neg = float("-inf")

    c3 = choice.reshape(N_GROUPS, GROUP_SIZE, tm)
    mi = lax.broadcasted_iota(jnp.int32, c3.shape, 1)
    top1 = jnp.max(c3, axis=1, keepdims=True)
    first = jnp.min(jnp.where(c3 == top1, mi, GROUP_SIZE), axis=1, keepdims=True)
    top2 = jnp.max(jnp.where(mi == first, neg, c3), axis=1, keepdims=True)
    gs = (top1 + top2).reshape(N_GROUPS, tm)

    gi = lax.broadcasted_iota(jnp.int32, gs.shape, 0)
    rank = jnp.zeros(gs.shape, jnp.int32)
    for g in range(N_GROUPS):
        rowg = gs[g:g + 1, :]
        beats = (rowg > gs) | ((rowg == gs) & (g < gi))
        rank = rank + beats.astype(jnp.int32)
    gmask = rank < TOPK_GROUPS
    emask = jnp.broadcast_to(gmask.reshape(N_GROUPS, 1, tm), c3.shape).reshape(N_EXPERTS, tm)

    work = jnp.where(emask, choice, NEG_INF)
    ei = lax.broadcasted_iota(jnp.int32, work.shape, 0)
    sel = jnp.zeros(work.shape, jnp.bool_)
    picks = []
    for _ in range(TOP_K):
        mx = jnp.max(work, axis=0, keepdims=True)
        fi = jnp.min(jnp.where(work == mx, ei, N_EXPERTS), axis=0, keepdims=True)
        pick = ei == fi
        picks.append((fi, pick))
        sel = sel | pick
        work = jnp.where(pick, neg, work)
    top_w = jnp.where(sel, scores, 0.0)
    gates = top_w / jnp.sum(top_w, axis=0, keepdims=True) * ROUTED_SCALE

    sel_b = jnp.where(sel, 1.0, 0.0).astype(BF16)
    before = _dot(sel_b, tri_ref[...]).astype(jnp.int32)
    run = cnt_ref[:, 0:1]
    pos = run + before
    cnt_ref[...] = jnp.broadcast_to(run + jnp.sum(sel.astype(jnp.int32), axis=1, keepdims=True), cnt_ref.shape)

    eid_ref[...] = jnp.concatenate([fi for fi, _ in picks], axis=0)
    rnk_ref[...] = jnp.concatenate(
        [jnp.sum(jnp.where(pick, pos, 0), axis=0, keepdims=True) for _, pick in picks], axis=0)
    w_rows = [jnp.sum(jnp.where(pick, gates, 0.0), axis=0, keepdims=True) for _, pick in picks]
    w_pad = jnp.concatenate(w_rows + [jnp.zeros((V7X_LANES - TOP_K, tm), F32)], axis=0)
    wt_ref[...] = w_pad.T


def _router_call(h2, rwt, bias, tm):
    t = h2.shape[0]
    tri = jnp.triu(jnp.ones((tm, tm), BF16), k=1)
    full = lambda a: pl.BlockSpec(a.shape, lambda i: (0, 0))
    return pl.pallas_call(
        _router_kernel,
        out_shape=[jax.ShapeDtypeStruct((TOP_K, t), jnp.int32), jax.ShapeDtypeStruct((TOP_K, t), jnp.int32),
                   jax.ShapeDtypeStruct((t, V7X_LANES), F32), jax.ShapeDtypeStruct((N_EXPERTS, V7X_LANES), jnp.int32)],
        grid=(t // tm,),
        in_specs=[pl.BlockSpec((tm, D_MODEL), lambda i: (i, 0)), full(rwt), full(bias), full(tri)],
        out_specs=[pl.BlockSpec((TOP_K, tm), lambda i: (0, i)), pl.BlockSpec((TOP_K, tm), lambda i: (0, i)),
                   pl.BlockSpec((tm, V7X_LANES), lambda i: (i, 0)),
                   pl.BlockSpec((N_EXPERTS, V7X_LANES), lambda i: (0, 0))],
        compiler_params=_cparams(("arbitrary",)),
        name="router",
    )(h2, rwt, bias, tri)


_SC_MESH_AXES = ("core", "subcore")


def _sc_mesh():
    return plsc.VectorSubcoreMesh(core_axis_name=_SC_MESH_AXES[0], subcore_axis_name=_SC_MESH_AXES[1])


def _sc_worker_chunks(n_chunks):
    n_workers = V7X_SC_CORES * V7X_SC_SUBCORES
    assert n_chunks % n_workers == 0
    per_worker = n_chunks // n_workers
    first = (lax.axis_index(_SC_MESH_AXES[1]) * V7X_SC_CORES + lax.axis_index(_SC_MESH_AXES[0])) * per_worker
    return first, per_worker


def _sc_scatter_rows(x, idx, n_rows):
    t, w = x.shape
    n_chunks, n_k, window = idx.shape

    @functools.partial(pl.kernel, out_type=jax.ShapeDtypeStruct((n_rows, w), x.dtype), mesh=_sc_mesh(),
                       scratch_types=[pltpu.VMEM((n_k, window), jnp.int32), pltpu.VMEM((window, w), x.dtype)],
                       name="moe_dispatch_sc")
    def scatter(x_hbm, i_hbm, o_hbm, idx_v, rows_v):
        first, per_worker = _sc_worker_chunks(n_chunks)

        @pl.loop(0, per_worker)
        def _(c):
            chunk = first + c
            pltpu.sync_copy(i_hbm.at[chunk], idx_v)
            pltpu.sync_copy(x_hbm.at[pl.ds(chunk * window, window)], rows_v)
            for k in range(n_k):
                pltpu.sync_copy(rows_v, o_hbm.at[idx_v.at[k]])

    return scatter(x, idx)


def _sc_gather_rows(y, idx):
    n_chunks, n_k, window = idx.shape
    w = y.shape[1]

    half = window // 2
    steps = [(k, h) for k in range(n_k) for h in range(2)]

    @functools.partial(pl.kernel, out_type=jax.ShapeDtypeStruct((n_k, n_chunks * window, w), y.dtype),
                       mesh=_sc_mesh(),
                       scratch_types=[pltpu.VMEM((n_k, window), jnp.int32), pltpu.VMEM((half, w), y.dtype),
                                      pltpu.VMEM((half, w), y.dtype), pltpu.SemaphoreType.DMA,
                                      pltpu.SemaphoreType.DMA],
                       name="moe_combine_sc")
    def gather(y_hbm, i_hbm, o_hbm, idx_v, rows_a, rows_b, sem_a, sem_b):
        first, per_worker = _sc_worker_chunks(n_chunks)
        bufs = ((rows_a, sem_a), (rows_b, sem_b))

        @pl.loop(0, per_worker)
        def _(c):
            chunk = first + c
            pltpu.sync_copy(i_hbm.at[chunk], idx_v)

            def start(j):
                k, h = steps[j]
                buf, sem = bufs[j % 2]
                return pltpu.async_copy(y_hbm.at[idx_v.at[k, pl.ds(h * half, half)]], buf, sem)

            pending = start(0)
            for j, (k, h) in enumerate(steps):
                following = start(j + 1) if j + 1 < len(steps) else None
                pending.wait()
                pltpu.sync_copy(bufs[j % 2][0], o_hbm.at[k, pl.ds(chunk * window + h * half, half)])
                pending = following

    return gather(y, idx)


def _experts_kernel(te_ref, nu_ref, xs_ref, wg_ref, wu_ref, wd_ref, y_ref, wg_b, wu_b, wd_b):
    i = pl.program_id(0)

    @pl.when(i < nu_ref[0])
    def _():
        @pl.when((i == 0) | (te_ref[i] != te_ref[jnp.maximum(i - 1, 0)]))
        def _():
            wg_b[...] = wg_ref[0].astype(BF16)
            wu_b[...] = wu_ref[0].astype(BF16)
            wd_b[...] = wd_ref[0].astype(BF16)

        half = D_MODEL // 2
        lo, hi = _unpack_bf16_pairs(xs_ref[...])
        a = _dot(lo, wg_b[:half, :]) + _dot(hi, wg_b[half:, :])
        u = _dot(lo, wu_b[:half, :]) + _dot(hi, wu_b[half:, :])
        y_ref[...] = _pack_bf16_pairs(_dot((jax.nn.silu(a) * u).astype(BF16), wd_b[...]))


def _experts_call(tile_expert, n_used, xs, wg, wu, wd, tmx):
    n_rows, w = xs.shape
    tile = lambda i, te, nu: jnp.minimum(i, nu[0] - 1)
    wspec = lambda shp: pl.BlockSpec((1,) + shp, lambda i, te, nu: (te[tile(i, te, nu)], 0, 0))
    return pl.pallas_call(
        _experts_kernel,
        out_shape=jax.ShapeDtypeStruct((n_rows, w), xs.dtype),
        grid_spec=pltpu.PrefetchScalarGridSpec(
            num_scalar_prefetch=2,
            grid=(n_rows // tmx,),
            in_specs=[pl.BlockSpec((tmx, w), lambda i, te, nu: (tile(i, te, nu), 0)),
                      wspec((D_MODEL, EXPERT_DIM)), wspec((D_MODEL, EXPERT_DIM)), wspec((EXPERT_DIM, D_MODEL))],
            out_specs=pl.BlockSpec((tmx, w), lambda i, te, nu: (tile(i, te, nu), 0)),
            scratch_shapes=[pltpu.VMEM((D_MODEL, EXPERT_DIM), BF16), pltpu.VMEM((D_MODEL, EXPERT_DIM), BF16),
                            pltpu.VMEM((EXPERT_DIM, D_MODEL), BF16)],
        ),
        compiler_params=_cparams(("arbitrary",)),
        name="moe_experts",
    )(tile_expert, n_used, xs, wg, wu, wd)


def _combine_kernel(wt_ref, h_ref, yg_ref, sg_ref, su_ref, sd_ref, g3_ref, b3_ref, o_ref):
    h = h_ref[...]
    hb = h.astype(BF16)
    acc = _dot((jax.nn.silu(_dot(hb, sg_ref[...])) * _dot(hb, su_ref[...])).astype(BF16), sd_ref[...])
    wt = wt_ref[...]
    lo_sum = None
    hi_sum = None
    for k in range(TOP_K):
        lo, hi = _unpack_bf16_pairs(yg_ref[k], F32)
        wk = wt[:, k:k + 1]
        lo_sum = wk * lo if lo_sum is None else lo_sum + wk * lo
        hi_sum = wk * hi if hi_sum is None else hi_sum + wk * hi
    acc = acc + jnp.concatenate([lo_sum, hi_sum], axis=1)
    o_ref[...] = _layer_norm(DEEPNORM_ALPHA * h + acc, g3_ref[...], b3_ref[...])


def _combine_call(wt, h2, yg, sg, su, sd, g3, b3, tm):
    t = h2.shape[0]
    full = lambda a: pl.BlockSpec(a.shape, lambda i: (0, 0))
    return pl.pallas_call(
        _combine_kernel,
        out_shape=jax.ShapeDtypeStruct((t, D_MODEL), F32),
        grid=(t // tm,),
        in_specs=[pl.BlockSpec((tm, V7X_LANES), lambda i: (i, 0)),
                  pl.BlockSpec((tm, D_MODEL), lambda i: (i, 0)),
                  pl.BlockSpec((TOP_K, tm, yg.shape[2]), lambda i: (0, i, 0)),
                  full(sg), full(su), full(sd), full(g3), full(b3)],
        out_specs=pl.BlockSpec((tm, D_MODEL), lambda i: (i, 0)),
        compiler_params=_cparams(("parallel",)),
        name="moe_combine",
    )(wt, h2, yg, sg, su, sd, g3, b3)


def _moe(h2, hp, router_w, router_bias, wg, wu, wd, sg, su, sd, g3, b3, tm_route, tmx, tm_comb):
    t = h2.shape[0]
    eid, rnk, wt, cnt = _router_call(h2, router_w.T, router_bias.reshape(-1, 1), tm_route)
    counts = cnt[:, 0]
    tiles_per_e = (counts + (tmx - 1)) // tmx
    tile_end = jnp.cumsum(tiles_per_e)
    row_start = ((tile_end - tiles_per_e) * tmx).astype(jnp.int32)
    n_tiles = (t * TOP_K) // tmx + N_EXPERTS
    n_used = tile_end[-1:].astype(jnp.int32)
    tile_expert = jnp.minimum(
        jnp.sum(tile_end[None, :] <= jnp.arange(n_tiles, dtype=jnp.int32)[:, None], axis=1), N_EXPERTS - 1
    ).astype(jnp.int32)
    expert_ids = jnp.arange(N_EXPERTS, dtype=jnp.int32)[:, None, None]
    dest = rnk + jnp.sum(jnp.where(eid[None] == expert_ids, row_start[:, None, None], 0), axis=0)
    dest_chunks = dest.reshape(TOP_K, t // SC_INDEX_WINDOW, SC_INDEX_WINDOW).transpose(1, 0, 2)
    xs = _sc_scatter_rows(hp, dest_chunks, n_tiles * tmx)
    yp = _experts_call(tile_expert, n_used, xs, wg, wu, wd, tmx)
    yg = _sc_gather_rows(yp, dest_chunks)
    return _combine_call(wt, h2, yg, sg, su, sd, g3, b3, tm_comb)


def _prep_w_in_t(w_in):
    pad = jnp.zeros((D_MODEL, V7X_LANES - MLA_ROPE_DIM), w_in.dtype)
    return jnp.concatenate([w_in, pad], axis=1).T.astype(BF16)


def _prep_w_uq_t(w_uq):
    w = w_uq.reshape(MLA_Q_RANK, MLA_HEADS, MLA_NOPE_DIM + MLA_ROPE_DIM)
    pad = jnp.zeros((MLA_Q_RANK, MLA_HEADS, MLA_QK_PAD - MLA_NOPE_DIM - MLA_ROPE_DIM), w.dtype)
    return jnp.concatenate([w, pad], axis=2).reshape(MLA_Q_RANK, MLA_HEADS * MLA_QK_PAD).T.astype(BF16)


def _prep_w_ukv_t(w_ukv):
    w = w_ukv.reshape(MLA_KV_RANK, MLA_HEADS, MLA_NOPE_DIM + MLA_V_DIM)
    k = w[:, :, :MLA_NOPE_DIM].reshape(MLA_KV_RANK, MLA_HEADS * MLA_NOPE_DIM)
    v = w[:, :, MLA_NOPE_DIM:].reshape(MLA_KV_RANK, MLA_HEADS * MLA_V_DIM)
    return jnp.concatenate([k, v], axis=1).T.astype(BF16)


def _rope_inv_freq(dim):
    return (ROPE_THETA ** (-jnp.arange(0, dim, 2, dtype=F32) / dim)).reshape(-1, 1)


def _layer(x2, mem2, pos_row, b, s, l, w_in, lam_q1, lam_k1, lam_q2, lam_k2, diff_subln_g, mla_q_norm_g,
           mla_w_uq, mla_kv_norm_g, mla_w_ukv, mla_out_norm_g, w_o, ln1_g, ln1_b, mem_w_q, mem_w_k, mem_w_v,
           mem_w_o, ln2_g, ln2_b, router_w, router_bias, exp_w_gate, exp_w_up, exp_w_down, sh_w_gate,
           sh_w_up, sh_w_down, ln3_g, ln3_b):
    row = lambda a: a.reshape(1, -1)
    col = lambda a: a.reshape(-1, 1)
    dqt, dk, dvt, mqt, mk, mvt = _proj_call(
        x2, pos_row, _rope_inv_freq(DIFF_ROT_DIM), _rope_inv_freq(MLA_ROPE_DIM), _prep_w_in_t(w_in),
        col(mla_q_norm_g), _prep_w_uq_t(mla_w_uq), col(mla_kv_norm_g), _prep_w_ukv_t(mla_w_ukv),
        tm=min(TOKEN_TILE, s))

    lam_init = 0.8 - 0.6 * math.exp(-0.3 * l)
    lam_vecs = jnp.stack([lam_q1, lam_k1, lam_q2, lam_k2]).astype(F32)
    diff_o, mla_o = _mixer_attn_call(dqt, dk, dvt, mqt, mk, mvt, lam_vecs, col(diff_subln_g), col(mla_out_norm_g),
                                     b, s, min(ATTN_QUERY_TILE, s), lam_init)

    kmem, vmem = _memkv_call(mem2, mem_w_k.astype(BF16), mem_w_v.astype(BF16))
    h2, hp = _post_call(diff_o, mla_o, x2, w_o.astype(BF16), row(ln1_g), row(ln1_b), mem_w_q.astype(BF16),
                        kmem, vmem, mem_w_o.astype(BF16), row(ln2_g), row(ln2_b), s, tm=min(TOKEN_TILE, s))

    return _moe(h2, hp, router_w, router_bias, exp_w_gate, exp_w_up, exp_w_down,
                sh_w_gate.astype(BF16), sh_w_up.astype(BF16), sh_w_down.astype(BF16), row(ln3_g), row(ln3_b),
                tm_route=min(TOKEN_TILE, s), tmx=EXPERT_ROW_TILE, tm_comb=min(COMBINE_TOKEN_TILE, s))


def kernel(x, mem, positions, w_in, lam_q1, lam_k1, lam_q2, lam_k2, diff_subln_g, mla_q_norm_g, mla_w_uq,
           mla_kv_norm_g, mla_w_ukv, mla_out_norm_g, w_o, ln1_g, ln1_b, mem_w_q, mem_w_k, mem_w_v, mem_w_o,
           ln2_g, ln2_b, router_w, router_bias, exp_w_gate, exp_w_up, exp_w_down, sh_w_gate, sh_w_up,
           sh_w_down, ln3_g, ln3_b):
    b, s, d = x.shape
    h = x.reshape(b * s, d)
    mem2 = mem.reshape(b * mem.shape[1], d)
    pos_row = positions.reshape(1, b * s)
    params = (w_in, lam_q1, lam_k1, lam_q2, lam_k2, diff_subln_g, mla_q_norm_g, mla_w_uq, mla_kv_norm_g,
              mla_w_ukv, mla_out_norm_g, w_o, ln1_g, ln1_b, mem_w_q, mem_w_k, mem_w_v, mem_w_o, ln2_g, ln2_b,
              router_w, router_bias, exp_w_gate, exp_w_up, exp_w_down, sh_w_gate, sh_w_up, sh_w_down,
              ln3_g, ln3_b)
    for l in range(w_in.shape[0]):
        h = _layer(h, mem2, pos_row, b, s, l, *[p[l] for p in params])
    return h.reshape(b, s, d)
```

```python
import functools
import math

import jax
import jax.numpy as jnp
from jax import lax
from jax.experimental import pallas as pl
from jax.experimental.pallas import tpu as pltpu
from jax.experimental.pallas import tpu_sc as plsc

F32 = jnp.float32
BF16 = jnp.bfloat16

D_MODEL = 1024
N_MEM = 256
ROPE_THETA = 500000.0
NEG_INF = -1e30
LN_EPS = 1e-5
RMS_EPS = 1e-6
DIFF_HEADS = 4
DIFF_QK_DIM = 64
DIFF_V_DIM = 128
DIFF_ROT_DIM = DIFF_QK_DIM // 4
MLA_HEADS = 4
MLA_Q_RANK = 256
MLA_KV_RANK = 128
MLA_NOPE_DIM = 128
MLA_ROPE_DIM = 64
MLA_V_DIM = 128
MLA_QK_PAD = 256
V_AUG_DIM = 128 + 16
MEM_HEADS = 4
MEM_HEAD_DIM = 128
N_EXPERTS = 64
TOP_K = 8
N_GROUPS = 8
TOPK_GROUPS = 4
GROUP_SIZE = N_EXPERTS // N_GROUPS
EXPERT_DIM = 256
ROUTED_SCALE = 2.5
DEPTH = 1
DEEPNORM_ALPHA = (2.0 * DEPTH) ** 0.25
LOG2E = math.log2(math.e)

V7X_LANES = 128
V7X_VMEM_LIMIT_BYTES = 56 * 1024 * 1024
V7X_SC_CORES = 2
V7X_SC_SUBCORES = 16
SC_INDEX_WINDOW = 128
V7X_BF16_SUBLANES = 16

TOKEN_TILE = 1024
ATTN_QUERY_TILE = 512
ATTN_HEADS_PER_STEP = 2
EXPERT_ROW_TILE = 1024
COMBINE_TOKEN_TILE = 512

N_DQ = DIFF_HEADS * 2 * DIFF_QK_DIM
N_DV = DIFF_HEADS * DIFF_V_DIM
O_DK = N_DQ
O_DV = 2 * N_DQ
O_CQ = O_DV + N_DV
O_CKV = O_CQ + MLA_Q_RANK
O_KR = O_CKV + MLA_KV_RANK


def _cparams(sem):
    return pltpu.CompilerParams(dimension_semantics=sem, vmem_limit_bytes=V7X_VMEM_LIMIT_BYTES)


def _dot(a, b):
    return jnp.dot(a, b, preferred_element_type=F32)


def _dot_nt(a, b):
    return lax.dot_general(a, b, (((1,), (1,)), ((), ())), preferred_element_type=F32)


def _rms_rows(x, g):
    return x * lax.rsqrt(jnp.mean(jnp.square(x), axis=0, keepdims=True) + RMS_EPS) * g


def _layer_norm(x, g, b):
    mu = jnp.mean(x, axis=-1, keepdims=True)
    xc = x - mu
    var = jnp.mean(jnp.square(xc), axis=-1, keepdims=True)
    return xc * lax.rsqrt(var + LN_EPS) * g + b


def _with_ones_rows(vt, heads):
    dv = vt.shape[0] // heads
    ones = jnp.ones((V_AUG_DIM - dv, vt.shape[1]), vt.dtype)
    return jnp.concatenate([p for h in range(heads) for p in (vt[h * dv:(h + 1) * dv], ones)], axis=0)


def _rope_rows(x, cos, sin, period):
    h = cos.shape[0]
    pieces = []
    for base in range(0, x.shape[0], period):
        x1 = x[base:base + h]
        x2 = x[base + h:base + 2 * h]
        pieces.append(x1 * cos - x2 * sin)
        pieces.append(x2 * cos + x1 * sin)
        if period > 2 * h:
            pieces.append(x[base + 2 * h:base + period])
    return jnp.concatenate(pieces, axis=0)


def _proj_kernel(x_ref, pos_ref, fd_ref, fm_ref, w_in_ref, gq_ref, wuq_ref, gkv_ref, wukv_ref,
                 dq_ref, dk_ref, dv_ref, mq_ref, mk_ref, mv_ref):
    xb = x_ref[...].astype(BF16)
    pos = pos_ref[...].astype(F32)
    ang_d = fd_ref[...] * pos
    ang_m = fm_ref[...] * pos
    cos_d, sin_d = jnp.cos(ang_d), jnp.sin(ang_d)
    cos_m, sin_m = jnp.cos(ang_m), jnp.sin(ang_m)

    dq = _rope_rows(_dot_nt(w_in_ref[0:N_DQ, :], xb), cos_d, sin_d, DIFF_QK_DIM)
    dq_ref[...] = (dq * (DIFF_QK_DIM ** -0.5 * LOG2E)).astype(BF16)
    dk = _rope_rows(_dot_nt(w_in_ref[O_DK:O_DK + N_DQ, :], xb), cos_d, sin_d, DIFF_QK_DIM)
    dk_ref[...] = dk.T.astype(BF16)
    dv_ref[...] = _with_ones_rows(_dot_nt(w_in_ref[O_DV:O_DV + N_DV, :], xb), DIFF_HEADS).astype(BF16)

    c_q = _dot_nt(w_in_ref[O_CQ:O_CQ + MLA_Q_RANK, :], xb)
    q = _dot(wuq_ref[...], _rms_rows(c_q, gq_ref[...]).astype(BF16))
    q = q * ((MLA_NOPE_DIM + MLA_ROPE_DIM) ** -0.5 * LOG2E)
    pieces = []
    for h in range(MLA_HEADS):
        o = h * MLA_QK_PAD
        pieces.append(q[o:o + MLA_NOPE_DIM])
        pieces.append(_rope_rows(q[o + MLA_NOPE_DIM:o + MLA_NOPE_DIM + MLA_ROPE_DIM], cos_m, sin_m, MLA_ROPE_DIM))
        pieces.append(q[o + MLA_NOPE_DIM + MLA_ROPE_DIM:o + MLA_QK_PAD])
    mq_ref[...] = jnp.concatenate(pieces, axis=0).astype(BF16)

    c_kv = _dot_nt(w_in_ref[O_CKV:O_CKV + MLA_KV_RANK, :], xb)
    kv = _dot(wukv_ref[...], _rms_rows(c_kv, gkv_ref[...]).astype(BF16))
    n_kn = MLA_HEADS * MLA_NOPE_DIM
    mv_ref[...] = _with_ones_rows(kv[n_kn:], MLA_HEADS).astype(BF16)
    k_rope = _dot_nt(w_in_ref[O_KR:O_KR + V7X_LANES, :], xb)
    k_pe = jnp.concatenate([_rope_rows(k_rope[:MLA_ROPE_DIM], cos_m, sin_m, MLA_ROPE_DIM),
                            k_rope[MLA_ROPE_DIM:]], axis=0)
    k_nope_t = kv[:n_kn].T.astype(BF16)
    k_pe_t = k_pe.T.astype(BF16)
    for h in range(MLA_HEADS):
        o = h * MLA_QK_PAD
        mk_ref[:, o:o + MLA_NOPE_DIM] = k_nope_t[:, h * MLA_NOPE_DIM:(h + 1) * MLA_NOPE_DIM]
        mk_ref[:, o + MLA_NOPE_DIM:o + MLA_QK_PAD] = k_pe_t


def _proj_call(x2, pos_row, fd, fm, w_in_t, gq, wuq_t, gkv, wukv_t, tm):
    t = x2.shape[0]
    full = lambda a: pl.BlockSpec(a.shape, lambda i: (0, 0))
    col_blk = lambda r: pl.BlockSpec((r, tm), lambda i: (0, i))
    row_blk = lambda w: pl.BlockSpec((tm, w), lambda i: (i, 0))
    n_mq = MLA_HEADS * MLA_QK_PAD
    return pl.pallas_call(
        _proj_kernel,
        out_shape=[jax.ShapeDtypeStruct((N_DQ, t), BF16), jax.ShapeDtypeStruct((t, N_DQ), BF16),
                   jax.ShapeDtypeStruct((DIFF_HEADS * V_AUG_DIM, t), BF16), jax.ShapeDtypeStruct((n_mq, t), BF16),
                   jax.ShapeDtypeStruct((t, n_mq), BF16), jax.ShapeDtypeStruct((MLA_HEADS * V_AUG_DIM, t), BF16)],
        grid=(t // tm,),
        in_specs=[row_blk(D_MODEL), col_blk(1), full(fd), full(fm), full(w_in_t), full(gq), full(wuq_t),
                  full(gkv), full(wukv_t)],
        out_specs=[col_blk(N_DQ), row_blk(N_DQ), col_blk(DIFF_HEADS * V_AUG_DIM), col_blk(n_mq), row_blk(n_mq),
                   col_blk(MLA_HEADS * V_AUG_DIM)],
        compiler_params=_cparams(("parallel",)),
        name="proj_rope",
    )(x2, pos_row, fd, fm, w_in_t, gq, wuq_t, gkv, wukv_t)


class _AttnStream:
    def __init__(self, qq, k_ref, vt_ref, scratch, i, tq):
        self.qq, self.k_ref, self.vt_ref, self.i, self.tq = qq, k_ref, vt_ref, i, tq
        self.s_refs, self.p_refs, self.acc_ref, self.m_ref = scratch[0:2], scratch[2:4], scratch[4], scratch[5]
        self.n = qq.shape[1]
        self.tk = self.s_refs[0].shape[0]
        assert tq == 2 * self.tk
        self.q_pos = i * tq + (lax.broadcasted_iota(jnp.int32, (self.tk, self.n), 1) & (tq - 1))
        self.key_row = lax.broadcasted_iota(jnp.int32, (self.tk, self.n), 0)
        self.late = [slice(a + tq // 2, a + tq) for a in range(0, self.n, tq)]

    def _chunk(self, c):
        return pl.ds(pl.multiple_of(c * self.tk, self.tk), self.tk)

    def scores(self, c, slot):
        s = _dot(self.k_ref[self._chunk(c), :], self.qq)
        self.s_refs[slot][...] = s
        return jnp.max(s, axis=0, keepdims=True)

    def softmax(self, c, slot, cmax, m, masked):
        s = self.s_refs[slot][...]
        if masked:
            s = jnp.where(self.key_row + c * self.tk <= self.q_pos, s, NEG_INF)
            cmax = jnp.max(s, axis=0, keepdims=True)
        m_new = jnp.maximum(m, cmax)
        alpha = jnp.exp2(m - m_new)
        self.p_refs[slot][...] = jnp.exp2((s - m_new).astype(BF16))
        return m_new, alpha

    def values(self, c, slot, alpha):
        self.acc_ref[...] = alpha * self.acc_ref[...] + _dot(self.vt_ref[:, self._chunk(c)], self.p_refs[slot][...])

    def scores_late(self, c, slot):
        kc = self.k_ref[self._chunk(c), :]
        for sl in self.late:
            self.s_refs[slot][:, sl] = _dot(kc, self.qq[:, sl])

    def softmax_late(self, c, slot, m):
        self.m_ref[...] = m
        alphas = []
        for sl in self.late:
            shape = (self.tk, sl.stop - sl.start)
            key_pos = lax.broadcasted_iota(jnp.int32, shape, 0) + c * self.tk
            query_pos = self.i * self.tq + ((lax.broadcasted_iota(jnp.int32, shape, 1) + sl.start) & (self.tq - 1))
            s = jnp.where(key_pos <= query_pos, self.s_refs[slot][:, sl], NEG_INF)
            m_old = self.m_ref[:, sl]
            m_new = jnp.maximum(m_old, jnp.max(s, axis=0, keepdims=True))
            alphas.append(jnp.exp2(m_old - m_new))
            self.p_refs[slot][:, sl] = jnp.exp2((s - m_new).astype(BF16))
        return alphas

    def values_late(self, c, slot, alphas):
        vc = self.vt_ref[:, self._chunk(c)]
        for sl, alpha in zip(self.late, alphas):
            self.acc_ref[:, sl] = alpha * self.acc_ref[:, sl] + _dot(vc, self.p_refs[slot][:, sl])

    def result(self):
        dv = self.vt_ref.shape[0] - V7X_BF16_SUBLANES
        return self.acc_ref[0:dv, :] / self.acc_ref[dv:dv + 1, :]


def _flash_streams(streams, i):
    def pair(u, carry, masked_second):
        c = 2 * u
        cm0 = [st.scores(c, 0) for st in streams]
        for st, (_, alpha0, _) in zip(streams, carry):
            st.values(c - 2, 0, alpha0)
        mid = [st.softmax(c - 1, 1, cm1, m, False) for st, (cm1, _, m) in zip(streams, carry)]
        if masked_second:
            for st in streams:
                st.scores_late(c + 1, 1)
            cm1s = [cm1 for cm1, _, _ in carry]
        else:
            cm1s = [st.scores(c + 1, 1) for st in streams]
        for st, (_, alpha1) in zip(streams, mid):
            st.values(c - 1, 1, alpha1)
        out = [st.softmax(c, 0, cm, m, masked_second) for st, cm, (m, _) in zip(streams, cm0, mid)]
        return tuple((cm1, alpha0, m) for cm1, (m, alpha0) in zip(cm1s, out))

    def drain(state):
        for st, (alpha0, _) in zip(streams, state):
            st.values(2 * i, 0, alpha0)
        lates = [st.softmax_late(2 * i + 1, 1, m) for st, (_, m) in zip(streams, state)]
        for st, alphas in zip(streams, lates):
            st.values_late(2 * i + 1, 1, alphas)

    for st in streams:
        st.acc_ref[...] = jnp.zeros(st.acc_ref.shape, F32)
    m_init = [jnp.full((1, st.n), NEG_INF, F32) for st in streams]

    @pl.when(i == 0)
    def _():
        for st in streams:
            st.scores(0, 0)
            st.scores_late(1, 1)
        state = [st.softmax(0, 0, None, m0, True) for st, m0 in zip(streams, m_init)]
        drain([(alpha0, m) for m, alpha0 in state])

    @pl.when(i > 0)
    def _():
        cm0 = [st.scores(0, 0) for st in streams]
        cm1 = [st.scores(1, 1) for st in streams]
        first = [st.softmax(0, 0, cm, m0, False) for st, cm, m0 in zip(streams, cm0, m_init)]
        carry = tuple((c1, alpha0, m) for c1, (m, alpha0) in zip(cm1, first))
        carry = lax.fori_loop(1, i, functools.partial(pair, masked_second=False), carry)
        carry = pair(i, carry, True)
        drain([(alpha0, m) for _, alpha0, m in carry])

    return [st.result() for st in streams]


def _mixer_attn_kernel(dq_ref, dk_ref, dvt_ref, mq_ref, mk_ref, mvt_ref, lam_ref, gd_ref, gm_ref, od_ref, om_ref,
                       *scratch, tq, lam_init):
    i = pl.program_id(2)
    dqk, mqk = 2 * DIFF_QK_DIM, MLA_QK_PAD
    diffs, mlas = [], []
    for hh in range(ATTN_HEADS_PER_STEP):
        qt = dq_ref[hh * dqk:(hh + 1) * dqk, :]
        row = lax.broadcasted_iota(jnp.int32, qt.shape, 0)
        zero = jnp.zeros_like(qt)
        qq = jnp.concatenate([jnp.where(row < DIFF_QK_DIM, qt, zero), jnp.where(row >= DIFF_QK_DIM, qt, zero)],
                             axis=1)
        sc = scratch[12 * hh:12 * (hh + 1)]
        diffs.append(_AttnStream(qq, dk_ref.at[:, pl.ds(hh * dqk, dqk)],
                                 dvt_ref.at[pl.ds(hh * V_AUG_DIM, V_AUG_DIM), :], sc[0:6], i, tq))
        mlas.append(_AttnStream(mq_ref[hh * mqk:(hh + 1) * mqk, :], mk_ref.at[:, pl.ds(hh * mqk, mqk)],
                                mvt_ref.at[pl.ds(hh * V_AUG_DIM, V_AUG_DIM), :], sc[6:12], i, tq))
    outs = _flash_streams(diffs + mlas, i)
    lv = lam_ref[...]
    lam = (jnp.exp(jnp.sum(lv[0:1] * lv[1:2], axis=1, keepdims=True))
           - jnp.exp(jnp.sum(lv[2:3] * lv[3:4], axis=1, keepdims=True)) + lam_init)
    for hh in range(ATTN_HEADS_PER_STEP):
        o_diff, o_mla = outs[hh], outs[ATTN_HEADS_PER_STEP + hh]
        d = o_diff[:, :tq] - lam * o_diff[:, tq:]
        cols = slice(hh * DIFF_V_DIM, (hh + 1) * DIFF_V_DIM)
        od_ref[:, cols] = (_rms_rows(d, gd_ref[...]) * (1.0 - lam_init)).T.astype(od_ref.dtype)
        om_ref[:, cols] = _rms_rows(o_mla, gm_ref[...]).T.astype(om_ref.dtype)


def _attn_scratch(tq, n):
    tk = tq // 2
    return [pltpu.VMEM((tk, n), F32), pltpu.VMEM((tk, n), F32), pltpu.VMEM((tk, n), BF16), pltpu.VMEM((tk, n), BF16),
            pltpu.VMEM((V_AUG_DIM, n), F32), pltpu.VMEM((1, n), F32)]


def _mixer_attn_call(dqt, dk, dvt, mqt, mk, mvt, lam_vecs, g_diff, g_mla, b, s, tq, lam_init):
    assert DIFF_HEADS == MLA_HEADS and DIFF_V_DIM == MLA_V_DIM
    nq = s // tq
    t = b * s
    hps = ATTN_HEADS_PER_STEP
    q_blk = lambda w: pl.BlockSpec((hps * w, tq), lambda bi, h, i: (h, bi * nq + i))
    k_blk = lambda w: pl.BlockSpec((s, hps * w), lambda bi, h, i: (bi, h))
    v_blk = pl.BlockSpec((hps * V_AUG_DIM, s), lambda bi, h, i: (h, bi))
    small = lambda a: pl.BlockSpec(a.shape, lambda bi, h, i: (0, 0))
    o_blk = pl.BlockSpec((tq, hps * DIFF_V_DIM), lambda bi, h, i: (bi * nq + i, h))
    return pl.pallas_call(
        functools.partial(_mixer_attn_kernel, tq=tq, lam_init=lam_init),
        out_shape=[jax.ShapeDtypeStruct((t, DIFF_HEADS * DIFF_V_DIM), BF16),
                   jax.ShapeDtypeStruct((t, MLA_HEADS * MLA_V_DIM), BF16)],
        grid=(b, DIFF_HEADS // hps, nq),
        in_specs=[q_blk(2 * DIFF_QK_DIM), k_blk(2 * DIFF_QK_DIM), v_blk, q_blk(MLA_QK_PAD), k_blk(MLA_QK_PAD), v_blk,
                  small(lam_vecs), small(g_diff), small(g_mla)],
        out_specs=[o_blk, o_blk],
        scratch_shapes=[spec for _ in range(hps) for spec in _attn_scratch(tq, 2 * tq) + _attn_scratch(tq, tq)],
        compiler_params=_cparams(("parallel", "parallel", "arbitrary")),
        name="mixer_attn",
    )(dqt, dk, dvt, mqt, mk, mvt, lam_vecs, g_diff, g_mla)


def _memkv_kernel(mem_ref, wk_ref, wv_ref, k_ref, v_ref):
    mb = mem_ref[...].astype(BF16)
    k_ref[...] = _dot(mb, wk_ref[...]).astype(BF16)
    v_ref[...] = _dot(mb, wv_ref[...]).astype(BF16)


def _memkv_call(mem2, wk, wv):
    n = mem2.shape[0]
    w = wk.shape[1]
    tm = N_MEM
    return pl.pallas_call(
        _memkv_kernel,
        out_shape=[jax.ShapeDtypeStruct((n, w), BF16)] * 2,
        grid=(n // tm,),
        in_specs=[pl.BlockSpec((tm, D_MODEL), lambda i: (i, 0)),
                  pl.BlockSpec(wk.shape, lambda i: (0, 0)),
                  pl.BlockSpec(wv.shape, lambda i: (0, 0))],
        out_specs=[pl.BlockSpec((tm, w), lambda i: (i, 0))] * 2,
        compiler_params=_cparams(("parallel",)),
        name="mem_kv",
    )(mem2, wk, wv)


def _pack_bf16_pairs(x):
    half = x.shape[1] // 2
    r = pltpu.bitcast(x.astype(BF16).astype(F32), jnp.uint32)
    return (r[:, :half] >> 16) | (r[:, half:] & jnp.uint32(0xFFFF0000))


def _unpack_bf16_pairs(w, dtype=BF16):
    lo = pltpu.bitcast(w << 16, F32)
    hi = pltpu.bitcast(w & jnp.uint32(0xFFFF0000), F32)
    return lo.astype(dtype), hi.astype(dtype)


def _post_kernel(da_ref, ma_ref, x_ref, wo_ref, g1_ref, b1_ref, wq_ref, km_ref, vm_ref, wmo_ref,
                 g2_ref, b2_ref, h_ref, hp_ref):
    n_d = DIFF_HEADS * DIFF_V_DIM
    mix = _dot(da_ref[...], wo_ref[0:n_d, :]) + _dot(ma_ref[...], wo_ref[n_d:, :])
    h1 = _layer_norm(DEEPNORM_ALPHA * x_ref[...] + mix, g1_ref[...], b1_ref[...])
    q = (_dot(h1.astype(BF16), wq_ref[...]) * (MEM_HEAD_DIM ** -0.5)).astype(BF16)
    outs = []
    for h in range(MEM_HEADS):
        sl = slice(h * MEM_HEAD_DIM, (h + 1) * MEM_HEAD_DIM)
        s = _dot_nt(q[:, sl], km_ref[:, sl])
        m = jnp.max(s, axis=1, keepdims=True)
        p = jnp.exp(s - m)
        p = p / jnp.sum(p, axis=1, keepdims=True)
        outs.append(_dot(p.astype(BF16), vm_ref[:, sl]))
    o = jnp.concatenate(outs, axis=1).astype(BF16)
    xat = _dot(o, wmo_ref[...])
    h2 = _layer_norm(DEEPNORM_ALPHA * h1 + xat, g2_ref[...], b2_ref[...])
    h_ref[...] = h2
    hp_ref[...] = _pack_bf16_pairs(h2)


def _post_call(diff_o, mla_o, x2, wo, g1, b1, wq, kmem, vmem, wmo, g2, b2, s, tm):
    t = x2.shape[0]
    per_b = s // tm
    row = lambda w: pl.BlockSpec((tm, w), lambda i: (i, 0))
    full = lambda a: pl.BlockSpec(a.shape, lambda i: (0, 0))
    memb = pl.BlockSpec((N_MEM, kmem.shape[1]), lambda i: (i // per_b, 0))
    return pl.pallas_call(
        _post_kernel,
        out_shape=[jax.ShapeDtypeStruct((t, D_MODEL), F32), jax.ShapeDtypeStruct((t, D_MODEL // 2), jnp.uint32)],
        grid=(t // tm,),
        in_specs=[row(diff_o.shape[1]), row(mla_o.shape[1]), row(D_MODEL), full(wo), full(g1), full(b1),
                  full(wq), memb, memb, full(wmo), full(g2), full(b2)],
        out_specs=[row(D_MODEL), row(D_MODEL // 2)],
        compiler_params=_cparams(("parallel",)),
        name="post_mix_xattn",
    )(diff_o, mla_o, x2, wo, g1, b1, wq, kmem, vmem, wmo, g2, b2)


def _router_kernel(h_ref, rwt_ref, bias_ref, tri_ref, eid_ref, rnk_ref, wt_ref, cnt_ref):
    tm = h_ref.shape[0]

    @pl.when(pl.program_id(0) == 0)
    def _():
        cnt_ref[...] = jnp.zeros(cnt_ref.shape, cnt_ref.dtype)

    h = h_ref[...]
    h_hi = h.astype(BF16)
    h_lo = (h - h_hi.astype(F32)).astype(BF16)
    w = rwt_ref[...]
    w_hi = w.astype(BF16)
    w_lo = (w - w_hi.astype(F32)).astype(BF16)
    both = _dot_nt(jnp.concatenate([w_hi, w_lo], axis=0), h_hi)
    logits = both[:N_EXPERTS] + (both[N_EXPERTS:] + _dot_nt(w_hi, h_lo))
    scores = jax.nn.sigmoid(logits)
    choice = scores + bias_ref[...]
    neg = float("-inf")

    c3 = choice.reshape(N_GROUPS, GROUP_SIZE, tm)
    mi = lax.broadcasted_iota(jnp.int32, c3.shape, 1)
    top1 = jnp.max(c3, axis=1, keepdims=True)
    first = jnp.min(jnp.where(c3 == top1, mi, GROUP_SIZE), axis=1, keepdims=True)
    top2 = jnp.max(jnp.where(mi == first, neg, c3), axis=1, keepdims=True)
    gs = (top1 + top2).reshape(N_GROUPS, tm)

    gi = lax.broadcasted_iota(jnp.int32, gs.shape, 0)
    rank = jnp.zeros(gs.shape, jnp.int32)
    for g in range(N_GROUPS):
        rowg = gs[g:g + 1, :]
        beats = (rowg > gs) | ((rowg == gs) & (g < gi))
        rank = rank + beats.astype(jnp.int32)
    gmask = rank < TOPK_GROUPS
    emask = jnp.broadcast_to(gmask.reshape(N_GROUPS, 1, tm), c3.shape).reshape(N_EXPERTS, tm)

    work = jnp.where(emask, choice, NEG_INF)
    ei = lax.broadcasted_iota(jnp.int32, work.shape, 0)
    sel = jnp.zeros(work.shape, jnp.bool_)
    picks = []
    for _ in range(TOP_K):
        mx = jnp.max(work, axis=0, keepdims=True)
        fi = jnp.min(jnp.where(work == mx, ei, N_EXPERTS), axis=0, keepdims=True)
        pick = ei == fi
        picks.append((fi, pick))
        sel = sel | pick
        work = jnp.where(pick, neg, work)
    top_w = jnp.where(sel, scores, 0.0)
    gates = top_w / jnp.sum(top_w, axis=0, keepdims=True) * ROUTED_SCALE

    sel_b = jnp.where(sel, 1.0, 0.0).astype(BF16)
    before = _dot(sel_b, tri_ref[...]).astype(jnp.int32)
    run = cnt_ref[:, 0:1]
    pos = run + before
    cnt_ref[...] = jnp.broadcast_to(run + jnp.sum(sel.astype(jnp.int32), axis=1, keepdims=True), cnt_ref.shape)

    eid_ref[...] = jnp.concatenate([fi for fi, _ in picks], axis=0)
    rnk_ref[...] = jnp.concatenate(
        [jnp.sum(jnp.where(pick, pos, 0), axis=0, keepdims=True) for _, pick in picks], axis=0)
    w_rows = [jnp.sum(jnp.where(pick, gates, 0.0), axis=0, keepdims=True) for _, pick in picks]
    w_pad = jnp.concatenate(w_rows + [jnp.zeros((V7X_LANES - TOP_K, tm), F32)], axis=0)
    wt_ref[...] = w_pad.T


def _router_call(h2, rwt, bias, tm):
    t = h2.shape[0]
    tri = jnp.triu(jnp.ones((tm, tm), BF16), k=1)
    full = lambda a: pl.BlockSpec(a.shape, lambda i: (0, 0))
    return pl.pallas_call(
        _router_kernel,
        out_shape=[jax.ShapeDtypeStruct((TOP_K, t), jnp.int32), jax.ShapeDtypeStruct((TOP_K, t), jnp.int32),
                   jax.ShapeDtypeStruct((t, V7X_LANES), F32), jax.ShapeDtypeStruct((N_EXPERTS, V7X_LANES), jnp.int32)],
        grid=(t // tm,),
        in_specs=[pl.BlockSpec((tm, D_MODEL), lambda i: (i, 0)), full(rwt), full(bias), full(tri)],
        out_specs=[pl.BlockSpec((TOP_K, tm), lambda i: (0, i)), pl.BlockSpec((TOP_K, tm), lambda i: (0, i)),
                   pl.BlockSpec((tm, V7X_LANES), lambda i: (i, 0)),
                   pl.BlockSpec((N_EXPERTS, V7X_LANES), lambda i: (0, 0))],
        compiler_params=_cparams(("arbitrary",)),
        name="router",
    )(h2, rwt, bias, tri)


_SC_MESH_AXES = ("core", "subcore")


def _sc_mesh():
    return plsc.VectorSubcoreMesh(core_axis_name=_SC_MESH_AXES[0], subcore_axis_name=_SC_MESH_AXES[1])


def _sc_worker_chunks(n_chunks):
    n_workers = V7X_SC_CORES * V7X_SC_SUBCORES
    assert n_chunks % n_workers == 0
    per_worker = n_chunks // n_workers
    first = (lax.axis_index(_SC_MESH_AXES[1]) * V7X_SC_CORES + lax.axis_index(_SC_MESH_AXES[0])) * per_worker
    return first, per_worker


def _sc_scatter_rows(x, idx, n_rows):
    t, w = x.shape
    n_chunks, n_k, window = idx.shape

    @functools.partial(pl.kernel, out_type=jax.ShapeDtypeStruct((n_rows, w), x.dtype), mesh=_sc_mesh(),
                       scratch_types=[pltpu.VMEM((n_k, window), jnp.int32), pltpu.VMEM((window, w), x.dtype)],
                       name="moe_dispatch_sc")
    def scatter(x_hbm, i_hbm, o_hbm, idx_v, rows_v):
        first, per_worker = _sc_worker_chunks(n_chunks)

        @pl.loop(0, per_worker)
        def _(c):
            chunk = first + c
            pltpu.sync_copy(i_hbm.at[chunk], idx_v)
            pltpu.sync_copy(x_hbm.at[pl.ds(chunk * window, window)], rows_v)
            for k in range(n_k):
                pltpu.sync_copy(rows_v, o_hbm.at[idx_v.at[k]])

    return scatter(x, idx)


def _sc_gather_rows(y, idx):
    n_chunks, n_k, window = idx.shape
    w = y.shape[1]

    half = window // 2
    steps = [(k, h) for k in range(n_k) for h in range(2)]

    @functools.partial(pl.kernel, out_type=jax.ShapeDtypeStruct((n_k, n_chunks * window, w), y.dtype),
                       mesh=_sc_mesh(),
                       scratch_types=[pltpu.VMEM((n_k, window), jnp.int32), pltpu.VMEM((half, w), y.dtype),
                                      pltpu.VMEM((half, w), y.dtype), pltpu.SemaphoreType.DMA,
                                      pltpu.SemaphoreType.DMA],
                       name="moe_combine_sc")
    def gather(y_hbm, i_hbm, o_hbm, idx_v, rows_a, rows_b, sem_a, sem_b):
        first, per_worker = _sc_worker_chunks(n_chunks)
        bufs = ((rows_a, sem_a), (rows_b, sem_b))

        @pl.loop(0, per_worker)
        def _(c):
            chunk = first + c
            pltpu.sync_copy(i_hbm.at[chunk], idx_v)

            def start(j):
                k, h = steps[j]
                buf, sem = bufs[j % 2]
                return pltpu.async_copy(y_hbm.at[idx_v.at[k, pl.ds(h * half, half)]], buf, sem)

            pending = start(0)
            for j, (k, h) in enumerate(steps):
                following = start(j + 1) if j + 1 < len(steps) else None
                pending.wait()
                pltpu.sync_copy(bufs[j % 2][0], o_hbm.at[k, pl.ds(chunk * window + h * half, half)])
                pending = following

    return gather(y, idx)


def _experts_kernel(te_ref, nu_ref, xs_ref, wg_ref, wu_ref, wd_ref, y_ref, wg_b, wu_b, wd_b):
    i = pl.program_id(0)

    @pl.when(i < nu_ref[0])
    def _():
        @pl.when((i == 0) | (te_ref[i] != te_ref[jnp.maximum(i - 1, 0)]))
        def _():
            wg_b[...] = wg_ref[0].astype(BF16)
            wu_b[...] = wu_ref[0].astype(BF16)
            wd_b[...] = wd_ref[0].astype(BF16)

        half = D_MODEL // 2
        lo, hi = _unpack_bf16_pairs(xs_ref[...])
        a = _dot(lo, wg_b[:half, :]) + _dot(hi, wg_b[half:, :])
        u = _dot(lo, wu_b[:half, :]) + _dot(hi, wu_b[half:, :])
        y_ref[...] = _pack_bf16_pairs(_dot((jax.nn.silu(a) * u).astype(BF16), wd_b[...]))


def _experts_call(tile_expert, n_used, xs, wg, wu, wd, tmx):
    n_rows, w = xs.shape
    tile = lambda i, te, nu: jnp.minimum(i, nu[0] - 1)
    wspec = lambda shp: pl.BlockSpec((1,) + shp, lambda i, te, nu: (te[tile(i, te, nu)], 0, 0))
    return pl.pallas_call(
        _experts_kernel,
        out_shape=jax.ShapeDtypeStruct((n_rows, w), xs.dtype),
        grid_spec=pltpu.PrefetchScalarGridSpec(
            num_scalar_prefetch=2,
            grid=(n_rows // tmx,),
            in_specs=[pl.BlockSpec((tmx, w), lambda i, te, nu: (tile(i, te, nu), 0)),
                      wspec((D_MODEL, EXPERT_DIM)), wspec((D_MODEL, EXPERT_DIM)), wspec((EXPERT_DIM, D_MODEL))],
            out_specs=pl.BlockSpec((tmx, w), lambda i, te, nu: (tile(i, te, nu), 0)),
            scratch_shapes=[pltpu.VMEM((D_MODEL, EXPERT_DIM), BF16), pltpu.VMEM((D_MODEL, EXPERT_DIM), BF16),
                            pltpu.VMEM((EXPERT_DIM, D_MODEL), BF16)],
        ),
        compiler_params=_cparams(("arbitrary",)),
        name="moe_experts",
    )(tile_expert, n_used, xs, wg, wu, wd)


def _combine_kernel(wt_ref, h_ref, yg_ref, sg_ref, su_ref, sd_ref, g3_ref, b3_ref, o_ref):
    h = h_ref[...]
    hb = h.astype(BF16)
    acc = _dot((jax.nn.silu(_dot(hb, sg_ref[...])) * _dot(hb, su_ref[...])).astype(BF16), sd_ref[...])
    wt = wt_ref[...]
    lo_sum = None
    hi_sum = None
    for k in range(TOP_K):
        lo, hi = _unpack_bf16_pairs(yg_ref[k], F32)
        wk = wt[:, k:k + 1]
        lo_sum = wk * lo if lo_sum is None else lo_sum + wk * lo
        hi_sum = wk * hi if hi_sum is None else hi_sum + wk * hi
    acc = acc + jnp.concatenate([lo_sum, hi_sum], axis=1)
    o_ref[...] = _layer_norm(DEEPNORM_ALPHA * h + acc, g3_ref[...], b3_ref[...])


def _combine_call(wt, h2, yg, sg, su, sd, g3, b3, tm):
    t = h2.shape[0]
    full = lambda a: pl.BlockSpec(a.shape, lambda i: (0, 0))
    return pl.pallas_call(
        _combine_kernel,
        out_shape=jax.ShapeDtypeStruct((t, D_MODEL), F32),
        grid=(t // tm,),
        in_specs=[pl.BlockSpec((tm, V7X_LANES), lambda i: (i, 0)),
                  pl.BlockSpec((tm, D_MODEL), lambda i: (i, 0)),
                  pl.BlockSpec((TOP_K, tm, yg.shape[2]), lambda i: (0, i, 0)),
                  full(sg), full(su), full(sd), full(g3), full(b3)],
        out_specs=pl.BlockSpec((tm, D_MODEL), lambda i: (i, 0)),
        compiler_params=_cparams(("parallel",)),
        name="moe_combine",
    )(wt, h2, yg, sg, su, sd, g3, b3)


def _moe(h2, hp, router_w, router_bias, wg, wu, wd, sg, su, sd, g3, b3, tm_route, tmx, tm_comb):
    t = h2.shape[0]
    eid, rnk, wt, cnt = _router_call(h2, router_w.T, router_bias.reshape(-1, 1), tm_route)
    counts = cnt[:, 0]
    tiles_per_e = (counts + (tmx - 1)) // tmx
    tile_end = jnp.cumsum(tiles_per_e)
    row_start = ((tile_end - tiles_per_e) * tmx).astype(jnp.int32)
    n_tiles = (t * TOP_K) // tmx + N_EXPERTS
    n_used = tile_end[-1:].astype(jnp.int32)
    tile_expert = jnp.minimum(
        jnp.sum(tile_end[None, :] <= jnp.arange(n_tiles, dtype=jnp.int32)[:, None], axis=1), N_EXPERTS - 1
    ).astype(jnp.int32)
    expert_ids = jnp.arange(N_EXPERTS, dtype=jnp.int32)[:, None, None]
    dest = rnk + jnp.sum(jnp.where(eid[None] == expert_ids, row_start[:, None, None], 0), axis=0)
    dest_chunks = dest.reshape(TOP_K, t // SC_INDEX_WINDOW, SC_INDEX_WINDOW).transpose(1, 0, 2)
    xs = _sc_scatter_rows(hp, dest_chunks, n_tiles * tmx)
    yp = _experts_call(tile_expert, n_used, xs, wg, wu, wd, tmx)
    yg = _sc_gather_rows(yp, dest_chunks)
    return _combine_call(wt, h2, yg, sg, su, sd, g3, b3, tm_comb)


def _prep_w_in_t(w_in):
    pad = jnp.zeros((D_MODEL, V7X_LANES - MLA_ROPE_DIM), w_in.dtype)
    return jnp.concatenate([w_in, pad], axis=1).T.astype(BF16)


def _prep_w_uq_t(w_uq):
    w = w_uq.reshape(MLA_Q_RANK, MLA_HEADS, MLA_NOPE_DIM + MLA_ROPE_DIM)
    pad = jnp.zeros((MLA_Q_RANK, MLA_HEADS, MLA_QK_PAD - MLA_NOPE_DIM - MLA_ROPE_DIM), w.dtype)
    return jnp.concatenate([w, pad], axis=2).reshape(MLA_Q_RANK, MLA_HEADS * MLA_QK_PAD).T.astype(BF16)


def _prep_w_ukv_t(w_ukv):
    w = w_ukv.reshape(MLA_KV_RANK, MLA_HEADS, MLA_NOPE_DIM + MLA_V_DIM)
    k = w[:, :, :MLA_NOPE_DIM].reshape(MLA_KV_RANK, MLA_HEADS * MLA_NOPE_DIM)
    v = w[:, :, MLA_NOPE_DIM:].reshape(MLA_KV_RANK, MLA_HEADS * MLA_V_DIM)
    return jnp.concatenate([k, v], axis=1).T.astype(BF16)


def _rope_inv_freq(dim):
    return (ROPE_THETA ** (-jnp.arange(0, dim, 2, dtype=F32) / dim)).reshape(-1, 1)


def _layer(x2, mem2, pos_row, b, s, l, w_in, lam_q1, lam_k1, lam_q2, lam_k2, diff_subln_g, mla_q_norm_g,
           mla_w_uq, mla_kv_norm_g, mla_w_ukv, mla_out_norm_g, w_o, ln1_g, ln1_b, mem_w_q, mem_w_k, mem_w_v,
           mem_w_o, ln2_g, ln2_b, router_w, router_bias, exp_w_gate, exp_w_up, exp_w_down, sh_w_gate,
           sh_w_up, sh_w_down, ln3_g, ln3_b):
    row = lambda a: a.reshape(1, -1)
    col = lambda a: a.reshape(-1, 1)
    dqt, dk, dvt, mqt, mk, mvt = _proj_call(
        x2, pos_row, _rope_inv_freq(DIFF_ROT_DIM), _rope_inv_freq(MLA_ROPE_DIM), _prep_w_in_t(w_in),
        col(mla_q_norm_g), _prep_w_uq_t(mla_w_uq), col(mla_kv_norm_g), _prep_w_ukv_t(mla_w_ukv),
        tm=min(TOKEN_TILE, s))

    lam_init = 0.8 - 0.6 * math.exp(-0.3 * l)
    lam_vecs = jnp.stack([lam_q1, lam_k1, lam_q2, lam_k2]).astype(F32)
    diff_o, mla_o = _mixer_attn_call(dqt, dk, dvt, mqt, mk, mvt, lam_vecs, col(diff_subln_g), col(mla_out_norm_g),
                                     b, s, min(ATTN_QUERY_TILE, s), lam_init)

    kmem, vmem = _memkv_call(mem2, mem_w_k.astype(BF16), mem_w_v.astype(BF16))
    h2, hp = _post_call(diff_o, mla_o, x2, w_o.astype(BF16), row(ln1_g), row(ln1_b), mem_w_q.astype(BF16),
                        kmem, vmem, mem_w_o.astype(BF16), row(ln2_g), row(ln2_b), s, tm=min(TOKEN_TILE, s))

    return _moe(h2, hp, router_w, router_bias, exp_w_gate, exp_w_up, exp_w_down,
                sh_w_gate.astype(BF16), sh_w_up.astype(BF16), sh_w_down.astype(BF16), row(ln3_g), row(ln3_b),
                tm_route=min(TOKEN_TILE, s), tmx=EXPERT_ROW_TILE, tm_comb=min(COMBINE_TOKEN_TILE, s))


def kernel(x, mem, positions, w_in, lam_q1, lam_k1, lam_q2, lam_k2, diff_subln_g, mla_q_norm_g, mla_w_uq,
           mla_kv_norm_g, mla_w_ukv, mla_out_norm_g, w_o, ln1_g, ln1_b, mem_w_q, mem_w_k, mem_w_v, mem_w_o,
           ln2_g, ln2_b, router_w, router_bias, exp_w_gate, exp_w_up, exp_w_down, sh_w_gate, sh_w_up,
           sh_w_down, ln3_g, ln3_b):
    b, s, d = x.shape
    h = x.reshape(b * s, d)
    mem2 = mem.reshape(b * mem.shape[1], d)
    pos_row = positions.reshape(1, b * s)
    params = (w_in, lam_q1, lam_k1, lam_q2, lam_k2, diff_subln_g, mla_q_norm_g, mla_w_uq, mla_kv_norm_g,
              mla_w_ukv, mla_out_norm_g, w_o, ln1_g, ln1_b, mem_w_q, mem_w_k, mem_w_v, mem_w_o, ln2_g, ln2_b,
              router_w, router_bias, exp_w_gate, exp_w_up, exp_w_down, sh_w_gate, sh_w_up, sh_w_down,
              ln3_g, ln3_b)
    for l in range(w_in.shape[0]):
        h = _layer(h, mem2, pos_row, b, s, l, *[p[l] for p in params])
    return h.reshape(b, s, d)
```

```python
import functools
import math

import jax
import jax.numpy as jnp
from jax import lax
from jax.experimental import pallas as pl
from jax.experimental.pallas import tpu as pltpu
from jax.experimental.pallas import tpu_sc as plsc

F32 = jnp.float32
BF16 = jnp.bfloat16

D_MODEL = 1024
N_MEM = 256
ROPE_THETA = 500000.0
NEG_INF = -1e30
LN_EPS = 1e-5
RMS_EPS = 1e-6
DIFF_HEADS = 4
DIFF_QK_DIM = 64
DIFF_V_DIM = 128
DIFF_ROT_DIM = DIFF_QK_DIM // 4
MLA_HEADS = 4
MLA_Q_RANK = 256
MLA_KV_RANK = 128
MLA_NOPE_DIM = 128
MLA_ROPE_DIM = 64
MLA_V_DIM = 128
MLA_QK_PAD = 256
MEM_HEADS = 4
MEM_HEAD_DIM = 128
N_EXPERTS = 64
TOP_K = 8
N_GROUPS = 8
TOPK_GROUPS = 4
GROUP_SIZE = N_EXPERTS // N_GROUPS
EXPERT_DIM = 256
ROUTED_SCALE = 2.5
DEPTH = 1
DEEPNORM_ALPHA = (2.0 * DEPTH) ** 0.25
LOG2E = math.log2(math.e)

V7X_LANES = 128
V7X_VMEM_LIMIT_BYTES = 56 * 1024 * 1024
V7X_SC_CORES = 2
V7X_SC_SUBCORES = 16
SC_INDEX_WINDOW = 128
V7X_BF16_SUBLANES = 16
V_AUG_DIM = DIFF_V_DIM + V7X_BF16_SUBLANES

TOKEN_TILE = 1024
ATTN_QUERY_TILE = 512
ATTN_HEADS_PER_STEP = 2
EXPERT_ROW_TILE = 1024
COMBINE_TOKEN_TILE = 512

N_DQ = DIFF_HEADS * 2 * DIFF_QK_DIM
N_DV = DIFF_HEADS * DIFF_V_DIM
O_DK = N_DQ
O_DV = 2 * N_DQ
O_CQ = O_DV + N_DV
O_CKV = O_CQ + MLA_Q_RANK
O_KR = O_CKV + MLA_KV_RANK


def _cparams(sem):
    return pltpu.CompilerParams(dimension_semantics=sem, vmem_limit_bytes=V7X_VMEM_LIMIT_BYTES)


def _dot(a, b):
    return jnp.dot(a, b, preferred_element_type=F32)


def _dot_nt(a, b):
    return lax.dot_general(a, b, (((1,), (1,)), ((), ())), preferred_element_type=F32)


def _rms_rows(x, g):
    return x * lax.rsqrt(jnp.mean(jnp.square(x), axis=0, keepdims=True) + RMS_EPS) * g


def _layer_norm(x, g, b):
    mu = jnp.mean(x, axis=-1, keepdims=True)
    xc = x - mu
    var = jnp.mean(jnp.square(xc), axis=-1, keepdims=True)
    return xc * lax.rsqrt(var + LN_EPS) * g + b


def _with_ones_rows(vt, heads):
    dv = vt.shape[0] // heads
    ones = jnp.ones((V_AUG_DIM - dv, vt.shape[1]), vt.dtype)
    return jnp.concatenate([p for h in range(heads) for p in (vt[h * dv:(h + 1) * dv], ones)], axis=0)


def _rope_rows(x, cos, sin, period):
    h = cos.shape[0]
    pieces = []
    for base in range(0, x.shape[0], period):
        x1 = x[base:base + h]
        x2 = x[base + h:base + 2 * h]
        pieces.append(x1 * cos - x2 * sin)
        pieces.append(x2 * cos + x1 * sin)
        if period > 2 * h:
            pieces.append(x[base + 2 * h:base + period])
    return jnp.concatenate(pieces, axis=0)


def _proj_kernel(x_ref, pos_ref, fd_ref, fm_ref, w_in_ref, gq_ref, wuq_ref, gkv_ref, wukv_ref,
                 dq_ref, dk_ref, dv_ref, mq_ref, mk_ref, mv_ref):
    xb = x_ref[...].astype(BF16)
    pos = pos_ref[...].astype(F32)
    ang_d = fd_ref[...] * pos
    ang_m = fm_ref[...] * pos
    cos_d, sin_d = jnp.cos(ang_d), jnp.sin(ang_d)
    cos_m, sin_m = jnp.cos(ang_m), jnp.sin(ang_m)

    dq = _rope_rows(_dot_nt(w_in_ref[0:N_DQ, :], xb), cos_d, sin_d, DIFF_QK_DIM)
    dq_ref[...] = (dq * (DIFF_QK_DIM ** -0.5 * LOG2E)).astype(BF16)
    dk = _rope_rows(_dot_nt(w_in_ref[O_DK:O_DK + N_DQ, :], xb), cos_d, sin_d, DIFF_QK_DIM)
    dk_ref[...] = dk.T.astype(BF16)
    dv_ref[...] = _with_ones_rows(_dot_nt(w_in_ref[O_DV:O_DV + N_DV, :], xb), DIFF_HEADS).astype(BF16)

    c_q = _dot_nt(w_in_ref[O_CQ:O_CQ + MLA_Q_RANK, :], xb)
    q = _dot(wuq_ref[...], _rms_rows(c_q, gq_ref[...]).astype(BF16))
    q = q * ((MLA_NOPE_DIM + MLA_ROPE_DIM) ** -0.5 * LOG2E)
    pieces = []
    for h in range(MLA_HEADS):
        o = h * MLA_QK_PAD
        pieces.append(q[o:o + MLA_NOPE_DIM])
        pieces.append(_rope_rows(q[o + MLA_NOPE_DIM:o + MLA_NOPE_DIM + MLA_ROPE_DIM], cos_m, sin_m, MLA_ROPE_DIM))
        pieces.append(q[o + MLA_NOPE_DIM + MLA_ROPE_DIM:o + MLA_QK_PAD])
    mq_ref[...] = jnp.concatenate(pieces, axis=0).astype(BF16)

    c_kv = _dot_nt(w_in_ref[O_CKV:O_CKV + MLA_KV_RANK, :], xb)
    kv = _dot(wukv_ref[...], _rms_rows(c_kv, gkv_ref[...]).astype(BF16))
    n_kn = MLA_HEADS * MLA_NOPE_DIM
    mv_ref[...] = _with_ones_rows(kv[n_kn:], MLA_HEADS).astype(BF16)
    k_rope = _dot_nt(w_in_ref[O_KR:O_KR + V7X_LANES, :], xb)
    k_pe = jnp.concatenate([_rope_rows(k_rope[:MLA_ROPE_DIM], cos_m, sin_m, MLA_ROPE_DIM),
                            k_rope[MLA_ROPE_DIM:]], axis=0)
    k_nope_t = kv[:n_kn].T.astype(BF16)
    k_pe_t = k_pe.T.astype(BF16)
    for h in range(MLA_HEADS):
        o = h * MLA_QK_PAD
        mk_ref[:, o:o + MLA_NOPE_DIM] = k_nope_t[:, h * MLA_NOPE_DIM:(h + 1) * MLA_NOPE_DIM]
        mk_ref[:, o + MLA_NOPE_DIM:o + MLA_QK_PAD] = k_pe_t


def _proj_call(x2, pos_row, fd, fm, w_in_t, gq, wuq_t, gkv, wukv_t, tm):
    t = x2.shape[0]
    full = lambda a: pl.BlockSpec(a.shape, lambda i: (0, 0))
    col_blk = lambda r: pl.BlockSpec((r, tm), lambda i: (0, i))
    row_blk = lambda w: pl.BlockSpec((tm, w), lambda i: (i, 0))
    n_mq = MLA_HEADS * MLA_QK_PAD
    return pl.pallas_call(
        _proj_kernel,
        out_shape=[jax.ShapeDtypeStruct((N_DQ, t), BF16), jax.ShapeDtypeStruct((t, N_DQ), BF16),
                   jax.ShapeDtypeStruct((DIFF_HEADS * V_AUG_DIM, t), BF16), jax.ShapeDtypeStruct((n_mq, t), BF16),
                   jax.ShapeDtypeStruct((t, n_mq), BF16), jax.ShapeDtypeStruct((MLA_HEADS * V_AUG_DIM, t), BF16)],
        grid=(t // tm,),
        in_specs=[row_blk(D_MODEL), col_blk(1), full(fd), full(fm), full(w_in_t), full(gq), full(wuq_t),
                  full(gkv), full(wukv_t)],
        out_specs=[col_blk(N_DQ), row_blk(N_DQ), col_blk(DIFF_HEADS * V_AUG_DIM), col_blk(n_mq), row_blk(n_mq),
                   col_blk(MLA_HEADS * V_AUG_DIM)],
        compiler_params=_cparams(("parallel",)),
        name="proj_rope",
    )(x2, pos_row, fd, fm, w_in_t, gq, wuq_t, gkv, wukv_t)


class _AttnStream:
    def __init__(self, qq, k_ref, vt_ref, scratch, i, tq):
        self.qq, self.k_ref, self.vt_ref, self.i, self.tq = qq, k_ref, vt_ref, i, tq
        self.s_refs, self.p_refs, self.acc_ref, self.m_ref = scratch[0:2], scratch[2:4], scratch[4], scratch[5]
        self.n = qq.shape[1]
        self.tk = self.s_refs[0].shape[0]
        assert tq == 2 * self.tk
        self.q_pos = i * tq + (lax.broadcasted_iota(jnp.int32, (self.tk, self.n), 1) & (tq - 1))
        self.key_row = lax.broadcasted_iota(jnp.int32, (self.tk, self.n), 0)
        self.late = [slice(a + tq // 2, a + tq) for a in range(0, self.n, tq)]

    def _chunk(self, c):
        return pl.ds(pl.multiple_of(c * self.tk, self.tk), self.tk)

    def scores(self, c, slot):
        s = _dot(self.k_ref[self._chunk(c), :], self.qq)
        self.s_refs[slot][...] = s
        return jnp.max(s, axis=0, keepdims=True)

    def softmax(self, c, slot, cmax, m, masked):
        s = self.s_refs[slot][...]
        if masked:
            s = jnp.where(self.key_row + c * self.tk <= self.q_pos, s, NEG_INF)
            cmax = jnp.max(s, axis=0, keepdims=True)
        m_new = jnp.maximum(m, cmax)
        alpha = jnp.exp2(m - m_new)
        self.p_refs[slot][...] = jnp.exp2((s - m_new).astype(BF16))
        return m_new, alpha

    def values(self, c, slot, alpha):
        self.acc_ref[...] = alpha * self.acc_ref[...] + _dot(self.vt_ref[:, self._chunk(c)], self.p_refs[slot][...])

    def scores_late(self, c, slot):
        kc = self.k_ref[self._chunk(c), :]
        for sl in self.late:
            self.s_refs[slot][:, sl] = _dot(kc, self.qq[:, sl])

    def softmax_late(self, c, slot, m):
        self.m_ref[...] = m
        alphas = []
        for sl in self.late:
            shape = (self.tk, sl.stop - sl.start)
            key_pos = lax.broadcasted_iota(jnp.int32, shape, 0) + c * self.tk
            query_pos = self.i * self.tq + ((lax.broadcasted_iota(jnp.int32, shape, 1) + sl.start) & (self.tq - 1))
            s = jnp.where(key_pos <= query_pos, self.s_refs[slot][:, sl], NEG_INF)
            m_old = self.m_ref[:, sl]
            m_new = jnp.maximum(m_old, jnp.max(s, axis=0, keepdims=True))
            alphas.append(jnp.exp2(m_old - m_new))
            self.p_refs[slot][:, sl] = jnp.exp2((s - m_new).astype(BF16))
        return alphas

    def values_late(self, c, slot, alphas):
        vc = self.vt_ref[:, self._chunk(c)]
        for sl, alpha in zip(self.late, alphas):
            self.acc_ref[:, sl] = alpha * self.acc_ref[:, sl] + _dot(vc, self.p_refs[slot][:, sl])

    def result(self):
        dv = self.vt_ref.shape[0] - V7X_BF16_SUBLANES
        return self.acc_ref[0:dv, :] / self.acc_ref[dv:dv + 1, :]


def _flash_streams(streams, i):
    def pair(u, carry, masked_second):
        c = 2 * u
        cm0 = [st.scores(c, 0) for st in streams]
        for st, (_, alpha0, _) in zip(streams, carry):
            st.values(c - 2, 0, alpha0)
        mid = [st.softmax(c - 1, 1, cm1, m, False) for st, (cm1, _, m) in zip(streams, carry)]
        if masked_second:
            for st in streams:
                st.scores_late(c + 1, 1)
            cm1s = [cm1 for cm1, _, _ in carry]
        else:
            cm1s = [st.scores(c + 1, 1) for st in streams]
        for st, (_, alpha1) in zip(streams, mid):
            st.values(c - 1, 1, alpha1)
        out = [st.softmax(c, 0, cm, m, masked_second) for st, cm, (m, _) in zip(streams, cm0, mid)]
        return tuple((cm1, alpha0, m) for cm1, (m, alpha0) in zip(cm1s, out))

    def drain(state):
        for st, (alpha0, _) in zip(streams, state):
            st.values(2 * i, 0, alpha0)
        lates = [st.softmax_late(2 * i + 1, 1, m) for st, (_, m) in zip(streams, state)]
        for st, alphas in zip(streams, lates):
            st.values_late(2 * i + 1, 1, alphas)

    for st in streams:
        st.acc_ref[...] = jnp.zeros(st.acc_ref.shape, F32)
    m_init = [jnp.full((1, st.n), NEG_INF, F32) for st in streams]

    @pl.when(i == 0)
    def _():
        for st in streams:
            st.scores(0, 0)
            st.scores_late(1, 1)
        state = [st.softmax(0, 0, None, m0, True) for st, m0 in zip(streams, m_init)]
        drain([(alpha0, m) for m, alpha0 in state])

    @pl.when(i > 0)
    def _():
        cm0 = [st.scores(0, 0) for st in streams]
        cm1 = [st.scores(1, 1) for st in streams]
        first = [st.softmax(0, 0, cm, m0, False) for st, cm, m0 in zip(streams, cm0, m_init)]
        carry = tuple((c1, alpha0, m) for c1, (m, alpha0) in zip(cm1, first))
        carry = lax.fori_loop(1, i, functools.partial(pair, masked_second=False), carry)
        carry = pair(i, carry, True)
        drain([(alpha0, m) for _, alpha0, m in carry])

    return [st.result() for st in streams]


def _mixer_attn_kernel(dq_ref, dk_ref, dvt_ref, mq_ref, mk_ref, mvt_ref, lam_ref, gd_ref, gm_ref, od_ref, om_ref,
                       *scratch, tq, lam_init):
    i = pl.program_id(2)
    dqk, mqk = 2 * DIFF_QK_DIM, MLA_QK_PAD
    diffs, mlas = [], []
    for hh in range(ATTN_HEADS_PER_STEP):
        qt = dq_ref[hh * dqk:(hh + 1) * dqk, :]
        row = lax.broadcasted_iota(jnp.int32, qt.shape, 0)
        zero = jnp.zeros_like(qt)
        qq = jnp.concatenate([jnp.where(row < DIFF_QK_DIM, qt, zero), jnp.where(row >= DIFF_QK_DIM, qt, zero)],
                             axis=1)
        sc = scratch[12 * hh:12 * (hh + 1)]
        diffs.append(_AttnStream(qq, dk_ref.at[:, pl.ds(hh * dqk, dqk)],
                                 dvt_ref.at[pl.ds(hh * V_AUG_DIM, V_AUG_DIM), :], sc[0:6], i, tq))
        mlas.append(_AttnStream(mq_ref[hh * mqk:(hh + 1) * mqk, :], mk_ref.at[:, pl.ds(hh * mqk, mqk)],
                                mvt_ref.at[pl.ds(hh * V_AUG_DIM, V_AUG_DIM), :], sc[6:12], i, tq))
    outs = _flash_streams(diffs + mlas, i)
    lv = lam_ref[...]
    lam = (jnp.exp(jnp.sum(lv[0:1] * lv[1:2], axis=1, keepdims=True))
           - jnp.exp(jnp.sum(lv[2:3] * lv[3:4], axis=1, keepdims=True)) + lam_init)
    for hh in range(ATTN_HEADS_PER_STEP):
        o_diff, o_mla = outs[hh], outs[ATTN_HEADS_PER_STEP + hh]
        d = o_diff[:, :tq] - lam * o_diff[:, tq:]
        cols = slice(hh * DIFF_V_DIM, (hh + 1) * DIFF_V_DIM)
        od_ref[:, cols] = (_rms_rows(d, gd_ref[...]) * (1.0 - lam_init)).T.astype(od_ref.dtype)
        om_ref[:, cols] = _rms_rows(o_mla, gm_ref[...]).T.astype(om_ref.dtype)


def _attn_scratch(tq, n):
    tk = tq // 2
    return [pltpu.VMEM((tk, n), F32), pltpu.VMEM((tk, n), F32), pltpu.VMEM((tk, n), BF16), pltpu.VMEM((tk, n), BF16),
            pltpu.VMEM((V_AUG_DIM, n), F32), pltpu.VMEM((1, n), F32)]


def _mixer_attn_call(dqt, dk, dvt, mqt, mk, mvt, lam_vecs, g_diff, g_mla, b, s, tq, lam_init):
    assert DIFF_HEADS == MLA_HEADS and DIFF_V_DIM == MLA_V_DIM
    nq = s // tq
    t = b * s
    hps = ATTN_HEADS_PER_STEP
    q_blk = lambda w: pl.BlockSpec((hps * w, tq), lambda bi, h, i: (h, bi * nq + i))
    k_blk = lambda w: pl.BlockSpec((s, hps * w), lambda bi, h, i: (bi, h))
    v_blk = pl.BlockSpec((hps * V_AUG_DIM, s), lambda bi, h, i: (h, bi))
    small = lambda a: pl.BlockSpec(a.shape, lambda bi, h, i: (0, 0))
    o_blk = pl.BlockSpec((tq, hps * DIFF_V_DIM), lambda bi, h, i: (bi * nq + i, h))
    return pl.pallas_call(
        functools.partial(_mixer_attn_kernel, tq=tq, lam_init=lam_init),
        out_shape=[jax.ShapeDtypeStruct((t, DIFF_HEADS * DIFF_V_DIM), BF16),
                   jax.ShapeDtypeStruct((t, MLA_HEADS * MLA_V_DIM), BF16)],
        grid=(b, DIFF_HEADS // hps, nq),
        in_specs=[q_blk(2 * DIFF_QK_DIM), k_blk(2 * DIFF_QK_DIM), v_blk, q_blk(MLA_QK_PAD), k_blk(MLA_QK_PAD), v_blk,
                  small(lam_vecs), small(g_diff), small(g_mla)],
        out_specs=[o_blk, o_blk],
        scratch_shapes=[spec for _ in range(hps) for spec in _attn_scratch(tq, 2 * tq) + _attn_scratch(tq, tq)],
        compiler_params=_cparams(("parallel", "parallel", "arbitrary")),
        name="mixer_attn",
    )(dqt, dk, dvt, mqt, mk, mvt, lam_vecs, g_diff, g_mla)


def _memkv_kernel(mem_ref, wk_ref, wv_ref, k_ref, v_ref):
    mb = mem_ref[...].astype(BF16)
    k_ref[...] = _dot(mb, wk_ref[...]).astype(BF16)
    v_ref[...] = _dot(mb, wv_ref[...]).astype(BF16)


def _memkv_call(mem2, wk, wv):
    n = mem2.shape[0]
    w = wk.shape[1]
    tm = N_MEM
    return pl.pallas_call(
        _memkv_kernel,
        out_shape=[jax.ShapeDtypeStruct((n, w), BF16)] * 2,
        grid=(n // tm,),
        in_specs=[pl.BlockSpec((tm, D_MODEL), lambda i: (i, 0)),
                  pl.BlockSpec(wk.shape, lambda i: (0, 0)),
                  pl.BlockSpec(wv.shape, lambda i: (0, 0))],
        out_specs=[pl.BlockSpec((tm, w), lambda i: (i, 0))] * 2,
        compiler_params=_cparams(("parallel",)),
        name="mem_kv",
    )(mem2, wk, wv)


def _pack_bf16_pairs(x):
    half = x.shape[1] // 2
    r = pltpu.bitcast(x.astype(BF16).astype(F32), jnp.uint32)
    return (r[:, :half] >> 16) | (r[:, half:] & jnp.uint32(0xFFFF0000))


def _unpack_bf16_pairs(w, dtype=BF16):
    lo = pltpu.bitcast(w << 16, F32)
    hi = pltpu.bitcast(w & jnp.uint32(0xFFFF0000), F32)
    return lo.astype(dtype), hi.astype(dtype)


def _post_kernel(da_ref, ma_ref, x_ref, wo_ref, g1_ref, b1_ref, wq_ref, km_ref, vm_ref, wmo_ref,
                 g2_ref, b2_ref, h_ref, hp_ref):
    n_d = DIFF_HEADS * DIFF_V_DIM
    mix = _dot(da_ref[...], wo_ref[0:n_d, :]) + _dot(ma_ref[...], wo_ref[n_d:, :])
    h1 = _layer_norm(DEEPNORM_ALPHA * x_ref[...] + mix, g1_ref[...], b1_ref[...])
    q = (_dot(h1.astype(BF16), wq_ref[...]) * (MEM_HEAD_DIM ** -0.5)).astype(BF16)
    outs = []
    for h in range(MEM_HEADS):
        sl = slice(h * MEM_HEAD_DIM, (h + 1) * MEM_HEAD_DIM)
        s = _dot_nt(q[:, sl], km_ref[:, sl])
        m = jnp.max(s, axis=1, keepdims=True)
        p = jnp.exp(s - m)
        p = p / jnp.sum(p, axis=1, keepdims=True)
        outs.append(_dot(p.astype(BF16), vm_ref[:, sl]))
    o = jnp.concatenate(outs, axis=1).astype(BF16)
    xat = _dot(o, wmo_ref[...])
    h2 = _layer_norm(DEEPNORM_ALPHA * h1 + xat, g2_ref[...], b2_ref[...])
    h_ref[...] = h2
    hp_ref[...] = _pack_bf16_pairs(h2)


def _post_call(diff_o, mla_o, x2, wo, g1, b1, wq, kmem, vmem, wmo, g2, b2, s, tm):
    t = x2.shape[0]
    per_b = s // tm
    row = lambda w: pl.BlockSpec((tm, w), lambda i: (i, 0))
    full = lambda a: pl.BlockSpec(a.shape, lambda i: (0, 0))
    memb = pl.BlockSpec((N_MEM, kmem.shape[1]), lambda i: (i // per_b, 0))
    return pl.pallas_call(
        _post_kernel,
        out_shape=[jax.ShapeDtypeStruct((t, D_MODEL), F32), jax.ShapeDtypeStruct((t, D_MODEL // 2), jnp.uint32)],
        grid=(t // tm,),
        in_specs=[row(diff_o.shape[1]), row(mla_o.shape[1]), row(D_MODEL), full(wo), full(g1), full(b1),
                  full(wq), memb, memb, full(wmo), full(g2), full(b2)],
        out_specs=[row(D_MODEL), row(D_MODEL // 2)],
        compiler_params=_cparams(("parallel",)),
        name="post_mix_xattn",
    )(diff_o, mla_o, x2, wo, g1, b1, wq, kmem, vmem, wmo, g2, b2)


def _router_kernel(h_ref, rwt_ref, bias_ref, tri_ref, eid_ref, rnk_ref, wt_ref, cnt_ref):
    tm = h_ref.shape[0]

    @pl.when(pl.program_id(0) == 0)
    def _():
        cnt_ref[...] = jnp.zeros(cnt_ref.shape, cnt_ref.dtype)

    h = h_ref[...]
    h_hi = h.astype(BF16)
    h_lo = (h - h_hi.astype(F32)).astype(BF16)
    w = rwt_ref[...]
    w_hi = w.astype(BF16)
    w_lo = (w - w_hi.astype(F32)).astype(BF16)
    both = _dot_nt(jnp.concatenate([w_hi, w_lo], axis=0), h_hi)
    logits = both[:N_EXPERTS] + (both[N_EXPERTS:] + _dot_nt(w_hi, h_lo))
    scores = jax.nn.sigmoid(logits)
    choice = scores + bias_ref[...]
    neg = float("-inf")

    c3 = choice.reshape(N_GROUPS, GROUP_SIZE, tm)
    mi = lax.broadcasted_iota(jnp.int32, c3.shape, 1)
    top1 = jnp.max(c3, axis=1, keepdims=True)
    first = jnp.min(jnp.where(c3 == top1, mi, GROUP_SIZE), axis=1, keepdims=True)
    top2 = jnp.max(jnp.where(mi == first, neg, c3), axis=1, keepdims=True)
    gs = (top1 + top2).reshape(N_GROUPS, tm)

    gi = lax.broadcasted_iota(jnp.int32, gs.shape, 0)
    rank = jnp.zeros(gs.shape, jnp.int32)
    for g in range(N_GROUPS):
        rowg = gs[g:g + 1, :]
        beats = (rowg > gs) | ((rowg == gs) & (g < gi))
        rank = rank + beats.astype(jnp.int32)
    gmask = rank < TOPK_GROUPS
    emask = jnp.broadcast_to(gmask.reshape(N_GROUPS, 1, tm), c3.shape).reshape(N_EXPERTS, tm)

    work = jnp.where(emask, choice, NEG_INF)
    ei = lax.broadcasted_iota(jnp.int32, work.shape, 0)
    sel = jnp.zeros(work.shape, jnp.bool_)
    picks = []
    for _ in range(TOP_K):
        mx = jnp.max(work, axis=0, keepdims=True)
        fi = jnp.min(jnp.where(work == mx, ei, N_EXPERTS), axis=0, keepdims=True)
        pick = ei == fi
        picks.append((fi, pick))
        sel = sel | pick
        work = jnp.where(pick, neg, work)
    top_w = jnp.where(sel, scores, 0.0)
    gates = top_w / jnp.sum(top_w, axis=0, keepdims=True) * ROUTED_SCALE

    sel_b = jnp.where(sel, 1.0, 0.0).astype(BF16)
    before = _dot(sel_b, tri_ref[...]).astype(jnp.int32)
    run = cnt_ref[:, 0:1]
    pos = run + before
    cnt_ref[...] = jnp.broadcast_to(run + jnp.sum(sel.astype(jnp.int32), axis=1, keepdims=True), cnt_ref.shape)

    eid_ref[...] = jnp.concatenate([fi for fi, _ in picks], axis=0)
    rnk_ref[...] = jnp.concatenate(
        [jnp.sum(jnp.where(pick, pos, 0), axis=0, keepdims=True) for _, pick in picks], axis=0)
    w_rows = [jnp.sum(jnp.where(pick, gates, 0.0), axis=0, keepdims=True) for _, pick in picks]
    w_pad = jnp.concatenate(w_rows + [jnp.zeros((V7X_LANES - TOP_K, tm), F32)], axis=0)
    wt_ref[...] = w_pad.T


def _router_call(h2, rwt, bias, tm):
    t = h2.shape[0]
    tri = jnp.triu(jnp.ones((tm, tm), BF16), k=1)
    full = lambda a: pl.BlockSpec(a.shape, lambda i: (0, 0))
    return pl.pallas_call(
        _router_kernel,
        out_shape=[jax.ShapeDtypeStruct((TOP_K, t), jnp.int32), jax.ShapeDtypeStruct((TOP_K, t), jnp.int32),
                   jax.ShapeDtypeStruct((t, V7X_LANES), F32), jax.ShapeDtypeStruct((N_EXPERTS, V7X_LANES), jnp.int32)],
        grid=(t // tm,),
        in_specs=[pl.BlockSpec((tm, D_MODEL), lambda i: (i, 0)), full(rwt), full(bias), full(tri)],
        out_specs=[pl.BlockSpec((TOP_K, tm), lambda i: (0, i)), pl.BlockSpec((TOP_K, tm), lambda i: (0, i)),
                   pl.BlockSpec((tm, V7X_LANES), lambda i: (i, 0)),
                   pl.BlockSpec((N_EXPERTS, V7X_LANES), lambda i: (0, 0))],
        compiler_params=_cparams(("arbitrary",)),
        name="router",
    )(h2, rwt, bias, tri)


_SC_MESH_AXES = ("core", "subcore")


def _sc_mesh():
    return plsc.VectorSubcoreMesh(core_axis_name=_SC_MESH_AXES[0], subcore_axis_name=_SC_MESH_AXES[1])


def _sc_worker_chunks(n_chunks):
    n_workers = V7X_SC_CORES * V7X_SC_SUBCORES
    assert n_chunks % n_workers == 0
    per_worker = n_chunks // n_workers
    first = (lax.axis_index(_SC_MESH_AXES[1]) * V7X_SC_CORES + lax.axis_index(_SC_MESH_AXES[0])) * per_worker
    return first, per_worker


def _sc_scatter_rows(x, idx, n_rows):
    t, w = x.shape
    n_chunks, n_k, window = idx.shape

    @functools.partial(pl.kernel, out_type=jax.ShapeDtypeStruct((n_rows, w), x.dtype), mesh=_sc_mesh(),
                       scratch_types=[pltpu.VMEM((n_k, window), jnp.int32), pltpu.VMEM((window, w), x.dtype)],
                       name="moe_dispatch_sc")
    def scatter(x_hbm, i_hbm, o_hbm, idx_v, rows_v):
        first, per_worker = _sc_worker_chunks(n_chunks)

        @pl.loop(0, per_worker)
        def _(c):
            chunk = first + c
            pltpu.sync_copy(i_hbm.at[chunk], idx_v)
            pltpu.sync_copy(x_hbm.at[pl.ds(chunk * window, window)], rows_v)
            for k in range(n_k):
                pltpu.sync_copy(rows_v, o_hbm.at[idx_v.at[k]])

    return scatter(x, idx)


def _sc_gather_rows(y, idx):
    n_chunks, n_k, window = idx.shape
    w = y.shape[1]

    half = window // 2
    steps = [(k, h) for k in range(n_k) for h in range(2)]

    @functools.partial(pl.kernel, out_type=jax.ShapeDtypeStruct((n_k, n_chunks * window, w), y.dtype),
                       mesh=_sc_mesh(),
                       scratch_types=[pltpu.VMEM((n_k, window), jnp.int32), pltpu.VMEM((half, w), y.dtype),
                                      pltpu.VMEM((half, w), y.dtype), pltpu.SemaphoreType.DMA,
                                      pltpu.SemaphoreType.DMA],
                       name="moe_combine_sc")
    def gather(y_hbm, i_hbm, o_hbm, idx_v, rows_a, rows_b, sem_a, sem_b):
        first, per_worker = _sc_worker_chunks(n_chunks)
        bufs = ((rows_a, sem_a), (rows_b, sem_b))

        @pl.loop(0, per_worker)
        def _(c):
            chunk = first + c
            pltpu.sync_copy(i_hbm.at[chunk], idx_v)

            def start(j):
                k, h = steps[j]
                buf, sem = bufs[j % 2]
                return pltpu.async_copy(y_hbm.at[idx_v.at[k, pl.ds(h * half, half)]], buf, sem)

            pending = start(0)
            for j, (k, h) in enumerate(steps):
                following = start(j + 1) if j + 1 < len(steps) else None
                pending.wait()
                pltpu.sync_copy(bufs[j % 2][0], o_hbm.at[k, pl.ds(chunk * window + h * half, half)])
                pending = following

    return gather(y, idx)


def _experts_kernel(te_ref, nu_ref, xs_ref, wg_ref, wu_ref, wd_ref, y_ref, wg_b, wu_b, wd_b):
    i = pl.program_id(0)

    @pl.when(i < nu_ref[0])
    def _():
        @pl.when((i == 0) | (te_ref[i] != te_ref[jnp.maximum(i - 1, 0)]))
        def _():
            wg_b[...] = wg_ref[0].astype(BF16)
            wu_b[...] = wu_ref[0].astype(BF16)
            wd_b[...] = wd_ref[0].astype(BF16)

        half = D_MODEL // 2
        lo, hi = _unpack_bf16_pairs(xs_ref[...])
        a = _dot(lo, wg_b[:half, :]) + _dot(hi, wg_b[half:, :])
        u = _dot(lo, wu_b[:half, :]) + _dot(hi, wu_b[half:, :])
        y_ref[...] = _pack_bf16_pairs(_dot((jax.nn.silu(a) * u).astype(BF16), wd_b[...]))


def _experts_call(tile_expert, n_used, xs, wg, wu, wd, tmx):
    n_rows, w = xs.shape
    tile = lambda i, te, nu: jnp.minimum(i, nu[0] - 1)
    wspec = lambda shp: pl.BlockSpec((1,) + shp, lambda i, te, nu: (te[tile(i, te, nu)], 0, 0))
    return pl.pallas_call(
        _experts_kernel,
        out_shape=jax.ShapeDtypeStruct((n_rows, w), xs.dtype),
        grid_spec=pltpu.PrefetchScalarGridSpec(
            num_scalar_prefetch=2,
            grid=(n_rows // tmx,),
            in_specs=[pl.BlockSpec((tmx, w), lambda i, te, nu: (tile(i, te, nu), 0)),
                      wspec((D_MODEL, EXPERT_DIM)), wspec((D_MODEL, EXPERT_DIM)), wspec((EXPERT_DIM, D_MODEL))],
            out_specs=pl.BlockSpec((tmx, w), lambda i, te, nu: (tile(i, te, nu), 0)),
            scratch_shapes=[pltpu.VMEM((D_MODEL, EXPERT_DIM), BF16), pltpu.VMEM((D_MODEL, EXPERT_DIM), BF16),
                            pltpu.VMEM((EXPERT_DIM, D_MODEL), BF16)],
        ),
        compiler_params=_cparams(("arbitrary",)),
        name="moe_experts",
    )(tile_expert, n_used, xs, wg, wu, wd)


def _combine_kernel(wt_ref, h_ref, yg_ref, sg_ref, su_ref, sd_ref, g3_ref, b3_ref, o_ref):
    h = h_ref[...]
    hb = h.astype(BF16)
    acc = _dot((jax.nn.silu(_dot(hb, sg_ref[...])) * _dot(hb, su_ref[...])).astype(BF16), sd_ref[...])
    wt = wt_ref[...]
    lo_sum = None
    hi_sum = None
    for k in range(TOP_K):
        lo, hi = _unpack_bf16_pairs(yg_ref[k], F32)
        wk = wt[:, k:k + 1]
        lo_sum = wk * lo if lo_sum is None else lo_sum + wk * lo
        hi_sum = wk * hi if hi_sum is None else hi_sum + wk * hi
    acc = acc + jnp.concatenate([lo_sum, hi_sum], axis=1)
    o_ref[...] = _layer_norm(DEEPNORM_ALPHA * h + acc, g3_ref[...], b3_ref[...])


def _combine_call(wt, h2, yg, sg, su, sd, g3, b3, tm):
    t = h2.shape[0]
    full = lambda a: pl.BlockSpec(a.shape, lambda i: (0, 0))
    return pl.pallas_call(
        _combine_kernel,
        out_shape=jax.ShapeDtypeStruct((t, D_MODEL), F32),
        grid=(t // tm,),
        in_specs=[pl.BlockSpec((tm, V7X_LANES), lambda i: (i, 0)),
                  pl.BlockSpec((tm, D_MODEL), lambda i: (i, 0)),
                  pl.BlockSpec((TOP_K, tm, yg.shape[2]), lambda i: (0, i, 0)),
                  full(sg), full(su), full(sd), full(g3), full(b3)],
        out_specs=pl.BlockSpec((tm, D_MODEL), lambda i: (i, 0)),
        compiler_params=_cparams(("parallel",)),
        name="moe_combine",
    )(wt, h2, yg, sg, su, sd, g3, b3)


def _moe(h2, hp, router_w, router_bias, wg, wu, wd, sg, su, sd, g3, b3, tm_route, tmx, tm_comb):
    t = h2.shape[0]
    eid, rnk, wt, cnt = _router_call(h2, router_w.T, router_bias.reshape(-1, 1), tm_route)
    counts = cnt[:, 0]
    tiles_per_e = (counts + (tmx - 1)) // tmx
    tile_end = jnp.cumsum(tiles_per_e)
    row_start = ((tile_end - tiles_per_e) * tmx).astype(jnp.int32)
    n_tiles = (t * TOP_K) // tmx + N_EXPERTS
    n_used = tile_end[-1:].astype(jnp.int32)
    tile_expert = jnp.minimum(
        jnp.sum(tile_end[None, :] <= jnp.arange(n_tiles, dtype=jnp.int32)[:, None], axis=1), N_EXPERTS - 1
    ).astype(jnp.int32)
    expert_ids = jnp.arange(N_EXPERTS, dtype=jnp.int32)[:, None, None]
    dest = rnk + jnp.sum(jnp.where(eid[None] == expert_ids, row_start[:, None, None], 0), axis=0)
    dest_chunks = dest.reshape(TOP_K, t // SC_INDEX_WINDOW, SC_INDEX_WINDOW).transpose(1, 0, 2)
    xs = _sc_scatter_rows(hp, dest_chunks, n_tiles * tmx)
    yp = _experts_call(tile_expert, n_used, xs, wg, wu, wd, tmx)
    yg = _sc_gather_rows(yp, dest_chunks)
    return _combine_call(wt, h2, yg, sg, su, sd, g3, b3, tm_comb)


def _prep_w_in_t(w_in):
    pad = jnp.zeros((D_MODEL, V7X_LANES - MLA_ROPE_DIM), w_in.dtype)
    return jnp.concatenate([w_in, pad], axis=1).T.astype(BF16)


def _prep_w_uq_t(w_uq):
    w = w_uq.reshape(MLA_Q_RANK, MLA_HEADS, MLA_NOPE_DIM + MLA_ROPE_DIM)
    pad = jnp.zeros((MLA_Q_RANK, MLA_HEADS, MLA_QK_PAD - MLA_NOPE_DIM - MLA_ROPE_DIM), w.dtype)
    return jnp.concatenate([w, pad], axis=2).reshape(MLA_Q_RANK, MLA_HEADS * MLA_QK_PAD).T.astype(BF16)


def _prep_w_ukv_t(w_ukv):
    w = w_ukv.reshape(MLA_KV_RANK, MLA_HEADS, MLA_NOPE_DIM + MLA_V_DIM)
    k = w[:, :, :MLA_NOPE_DIM].reshape(MLA_KV_RANK, MLA_HEADS * MLA_NOPE_DIM)
    v = w[:, :, MLA_NOPE_DIM:].reshape(MLA_KV_RANK, MLA_HEADS * MLA_V_DIM)
    return jnp.concatenate([k, v], axis=1).T.astype(BF16)


def _rope_inv_freq(dim):
    return (ROPE_THETA ** (-jnp.arange(0, dim, 2, dtype=F32) / dim)).reshape(-1, 1)


def _layer(x2, mem2, pos_row, b, s, l, w_in, lam_q1, lam_k1, lam_q2, lam_k2, diff_subln_g, mla_q_norm_g,
           mla_w_uq, mla_kv_norm_g, mla_w_ukv, mla_out_norm_g, w_o, ln1_g, ln1_b, mem_w_q, mem_w_k, mem_w_v,
           mem_w_o, ln2_g, ln2_b, router_w, router_bias, exp_w_gate, exp_w_up, exp_w_down, sh_w_gate,
           sh_w_up, sh_w_down, ln3_g, ln3_b):
    row = lambda a: a.reshape(1, -1)
    col = lambda a: a.reshape(-1, 1)
    dqt, dk, dvt, mqt, mk, mvt = _proj_call(
        x2, pos_row, _rope_inv_freq(DIFF_ROT_DIM), _rope_inv_freq(MLA_ROPE_DIM), _prep_w_in_t(w_in),
        col(mla_q_norm_g), _prep_w_uq_t(mla_w_uq), col(mla_kv_norm_g), _prep_w_ukv_t(mla_w_ukv),
        tm=min(TOKEN_TILE, s))

    lam_init = 0.8 - 0.6 * math.exp(-0.3 * l)
    lam_vecs = jnp.stack([lam_q1, lam_k1, lam_q2, lam_k2]).astype(F32)
    diff_o, mla_o = _mixer_attn_call(dqt, dk, dvt, mqt, mk, mvt, lam_vecs, col(diff_subln_g), col(mla_out_norm_g),
                                     b, s, min(ATTN_QUERY_TILE, s), lam_init)

    kmem, vmem = _memkv_call(mem2, mem_w_k.astype(BF16), mem_w_v.astype(BF16))
    h2, hp = _post_call(diff_o, mla_o, x2, w_o.astype(BF16), row(ln1_g), row(ln1_b), mem_w_q.astype(BF16),
                        kmem, vmem, mem_w_o.astype(BF16), row(ln2_g), row(ln2_b), s, tm=min(TOKEN_TILE, s))

    return _moe(h2, hp, router_w, router_bias, exp_w_gate, exp_w_up, exp_w_down,
                sh_w_gate.astype(BF16), sh_w_up.astype(BF16), sh_w_down.astype(BF16), row(ln3_g), row(ln3_b),
                tm_route=min(TOKEN_TILE, s), tmx=EXPERT_ROW_TILE, tm_comb=min(COMBINE_TOKEN_TILE, s))


def kernel(x, mem, positions, w_in, lam_q1, lam_k1, lam_q2, lam_k2, diff_subln_g, mla_q_norm_g, mla_w_uq,
           mla_kv_norm_g, mla_w_ukv, mla_out_norm_g, w_o, ln1_g, ln1_b, mem_w_q, mem_w_k, mem_w_v, mem_w_o,
           ln2_g, ln2_b, router_w, router_bias, exp_w_gate, exp_w_up, exp_w_down, sh_w_gate, sh_w_up,
           sh_w_down, ln3_g, ln3_b):
    b, s, d = x.shape
    h = x.reshape(b * s, d)
    mem2 = mem.reshape(b * mem.shape[1], d)
    pos_row = positions.reshape(1, b * s)
    params = (w_in, lam_q1, lam_k1, lam_q2, lam_k2, diff_subln_g, mla_q_norm_g, mla_w_uq, mla_kv_norm_g,
              mla_w_ukv, mla_out_norm_g, w_o, ln1_g, ln1_b, mem_w_q, mem_w_k, mem_w_v, mem_w_o, ln2_g, ln2_b,
              router_w, router_bias, exp_w_gate, exp_w_up, exp_w_down, sh_w_gate, sh_w_up, sh_w_down,
              ln3_g, ln3_b)
    for l in range(w_in.shape[0]):
        h = _layer(h, mem2, pos_row, b, s, l, *[p[l] for p in params])
    return h.reshape(b, s, d)
```

```python
import functools
import math

import jax
import jax.numpy as jnp
from jax import lax
from jax.experimental import pallas as pl
from jax.experimental.pallas import tpu as pltpu
from jax.experimental.pallas import tpu_sc as plsc

F32 = jnp.float32
BF16 = jnp.bfloat16

D_MODEL = 1024
N_MEM = 256
ROPE_THETA = 500000.0
NEG_INF = -1e30
LN_EPS = 1e-5
RMS_EPS = 1e-6
DIFF_HEADS = 4
DIFF_QK_DIM = 64
DIFF_V_DIM = 128
DIFF_ROT_DIM = DIFF_QK_DIM // 4
MLA_HEADS = 4
MLA_Q_RANK = 256
MLA_KV_RANK = 128
MLA_NOPE_DIM = 128
MLA_ROPE_DIM = 64
MLA_V_DIM = 128
MLA_QK_PAD = 256
MEM_HEADS = 4
MEM_HEAD_DIM = 128
N_EXPERTS = 64
TOP_K = 8
N_GROUPS = 8
TOPK_GROUPS = 4
GROUP_SIZE = N_EXPERTS // N_GROUPS
EXPERT_DIM = 256
ROUTED_SCALE = 2.5
DEPTH = 1
DEEPNORM_ALPHA = (2.0 * DEPTH) ** 0.25
LOG2E = math.log2(math.e)

V7X_LANES = 128
V7X_VMEM_LIMIT_BYTES = 56 * 1024 * 1024
V7X_SC_CORES = 2
V7X_SC_SUBCORES = 16
SC_INDEX_WINDOW = 128
V7X_BF16_SUBLANES = 16
V_AUG_DIM = DIFF_V_DIM + V7X_BF16_SUBLANES

TOKEN_TILE = 1024
ATTN_QUERY_TILE = 512
ATTN_HEADS_PER_STEP = 2
EXPERT_ROW_TILE = 1024
COMBINE_TOKEN_TILE = 512

N_DQ = DIFF_HEADS * 2 * DIFF_QK_DIM
N_DV = DIFF_HEADS * DIFF_V_DIM
O_DK = N_DQ
O_DV = 2 * N_DQ
O_CQ = O_DV + N_DV
O_CKV = O_CQ + MLA_Q_RANK
O_KR = O_CKV + MLA_KV_RANK


def _cparams(sem):
    return pltpu.CompilerParams(dimension_semantics=sem, vmem_limit_bytes=V7X_VMEM_LIMIT_BYTES)


def _dot(a, b):
    return jnp.dot(a, b, preferred_element_type=F32)


def _dot_nt(a, b):
    return lax.dot_general(a, b, (((1,), (1,)), ((), ())), preferred_element_type=F32)


def _rms_rows(x, g):
    return x * lax.rsqrt(jnp.mean(jnp.square(x), axis=0, keepdims=True) + RMS_EPS) * g


def _layer_norm(x, g, b):
    mu = jnp.mean(x, axis=-1, keepdims=True)
    xc = x - mu
    var = jnp.mean(jnp.square(xc), axis=-1, keepdims=True)
    return xc * lax.rsqrt(var + LN_EPS) * g + b


def _with_ones_rows(vt, heads):
    dv = vt.shape[0] // heads
    ones = jnp.ones((V_AUG_DIM - dv, vt.shape[1]), vt.dtype)
    return jnp.concatenate([p for h in range(heads) for p in (vt[h * dv:(h + 1) * dv], ones)], axis=0)


def _rope_rows(x, cos, sin, period):
    h = cos.shape[0]
    pieces = []
    for base in range(0, x.shape[0], period):
        x1 = x[base:base + h]
        x2 = x[base + h:base + 2 * h]
        pieces.append(x1 * cos - x2 * sin)
        pieces.append(x2 * cos + x1 * sin)
        if period > 2 * h:
            pieces.append(x[base + 2 * h:base + period])
    return jnp.concatenate(pieces, axis=0)


def _proj_kernel(x_ref, pos_ref, fd_ref, fm_ref, w_in_ref, gq_ref, wuq_ref, gkv_ref, wukv_ref,
                 dq_ref, dk_ref, dv_ref, mq_ref, mk_ref, mv_ref):
    xb = x_ref[...].astype(BF16)
    pos = pos_ref[...].astype(F32)
    ang_d = fd_ref[...] * pos
    ang_m = fm_ref[...] * pos
    cos_d, sin_d = jnp.cos(ang_d), jnp.sin(ang_d)
    cos_m, sin_m = jnp.cos(ang_m), jnp.sin(ang_m)

    dq = _rope_rows(_dot_nt(w_in_ref[0:N_DQ, :], xb), cos_d, sin_d, DIFF_QK_DIM)
    dq_ref[...] = (dq * (DIFF_QK_DIM ** -0.5 * LOG2E)).astype(BF16)
    dk = _rope_rows(_dot_nt(w_in_ref[O_DK:O_DK + N_DQ, :], xb), cos_d, sin_d, DIFF_QK_DIM)
    dk_ref[...] = dk.T.astype(BF16)
    dv_ref[...] = _with_ones_rows(_dot_nt(w_in_ref[O_DV:O_DV + N_DV, :], xb), DIFF_HEADS).astype(BF16)

    c_q = _dot_nt(w_in_ref[O_CQ:O_CQ + MLA_Q_RANK, :], xb)
    q = _dot(wuq_ref[...], _rms_rows(c_q, gq_ref[...]).astype(BF16))
    q = q * ((MLA_NOPE_DIM + MLA_ROPE_DIM) ** -0.5 * LOG2E)
    pieces = []
    for h in range(MLA_HEADS):
        o = h * MLA_QK_PAD
        pieces.append(q[o:o + MLA_NOPE_DIM])
        pieces.append(_rope_rows(q[o + MLA_NOPE_DIM:o + MLA_NOPE_DIM + MLA_ROPE_DIM], cos_m, sin_m, MLA_ROPE_DIM))
        pieces.append(q[o + MLA_NOPE_DIM + MLA_ROPE_DIM:o + MLA_QK_PAD])
    mq_ref[...] = jnp.concatenate(pieces, axis=0).astype(BF16)

    c_kv = _dot_nt(w_in_ref[O_CKV:O_CKV + MLA_KV_RANK, :], xb)
    kv = _dot(wukv_ref[...], _rms_rows(c_kv, gkv_ref[...]).astype(BF16))
    n_kn = MLA_HEADS * MLA_NOPE_DIM
    mv_ref[...] = _with_ones_rows(kv[n_kn:], MLA_HEADS).astype(BF16)
    k_rope = _dot_nt(w_in_ref[O_KR:O_KR + V7X_LANES, :], xb)
    k_pe = jnp.concatenate([_rope_rows(k_rope[:MLA_ROPE_DIM], cos_m, sin_m, MLA_ROPE_DIM),
                            k_rope[MLA_ROPE_DIM:]], axis=0)
    k_nope_t = kv[:n_kn].T.astype(BF16)
    k_pe_t = k_pe.T.astype(BF16)
    for h in range(MLA_HEADS):
        o = h * MLA_QK_PAD
        mk_ref[:, o:o + MLA_NOPE_DIM] = k_nope_t[:, h * MLA_NOPE_DIM:(h + 1) * MLA_NOPE_DIM]
        mk_ref[:, o + MLA_NOPE_DIM:o + MLA_QK_PAD] = k_pe_t


def _proj_call(x2, pos_row, fd, fm, w_in_t, gq, wuq_t, gkv, wukv_t, tm):
    t = x2.shape[0]
    full = lambda a: pl.BlockSpec(a.shape, lambda i: (0, 0))
    col_blk = lambda r: pl.BlockSpec((r, tm), lambda i: (0, i))
    row_blk = lambda w: pl.BlockSpec((tm, w), lambda i: (i, 0))
    n_mq = MLA_HEADS * MLA_QK_PAD
    return pl.pallas_call(
        _proj_kernel,
        out_shape=[jax.ShapeDtypeStruct((N_DQ, t), BF16), jax.ShapeDtypeStruct((t, N_DQ), BF16),
                   jax.ShapeDtypeStruct((DIFF_HEADS * V_AUG_DIM, t), BF16), jax.ShapeDtypeStruct((n_mq, t), BF16),
                   jax.ShapeDtypeStruct((t, n_mq), BF16), jax.ShapeDtypeStruct((MLA_HEADS * V_AUG_DIM, t), BF16)],
        grid=(t // tm,),
        in_specs=[row_blk(D_MODEL), col_blk(1), full(fd), full(fm), full(w_in_t), full(gq), full(wuq_t),
                  full(gkv), full(wukv_t)],
        out_specs=[col_blk(N_DQ), row_blk(N_DQ), col_blk(DIFF_HEADS * V_AUG_DIM), col_blk(n_mq), row_blk(n_mq),
                   col_blk(MLA_HEADS * V_AUG_DIM)],
        compiler_params=_cparams(("parallel",)),
        name="proj_rope",
    )(x2, pos_row, fd, fm, w_in_t, gq, wuq_t, gkv, wukv_t)


class _AttnStream:
    def __init__(self, qq, k_ref, vt_ref, scratch, i, tq):
        self.qq, self.k_ref, self.vt_ref, self.i, self.tq = qq, k_ref, vt_ref, i, tq
        self.s_refs, self.p_refs, self.acc_ref, self.m_ref = scratch[0:2], scratch[2:4], scratch[4], scratch[5]
        self.n = qq.shape[1]
        self.tk = self.s_refs[0].shape[0]
        assert tq == 2 * self.tk
        self.q_pos = i * tq + (lax.broadcasted_iota(jnp.int32, (self.tk, self.n), 1) & (tq - 1))
        self.key_row = lax.broadcasted_iota(jnp.int32, (self.tk, self.n), 0)
        self.late = [slice(a + tq // 2, a + tq) for a in range(0, self.n, tq)]

    def _chunk(self, c):
        return pl.ds(pl.multiple_of(c * self.tk, self.tk), self.tk)

    def scores(self, c, slot):
        s = _dot(self.k_ref[self._chunk(c), :], self.qq)
        self.s_refs[slot][...] = s
        return jnp.max(s, axis=0, keepdims=True)

    def softmax(self, c, slot, cmax, m, masked):
        s = self.s_refs[slot][...]
        if masked:
            s = jnp.where(self.key_row + c * self.tk <= self.q_pos, s, NEG_INF)
            cmax = jnp.max(s, axis=0, keepdims=True)
        m_new = jnp.maximum(m, cmax)
        alpha = jnp.exp2(m - m_new)
        self.p_refs[slot][...] = jnp.exp2((s - m_new).astype(BF16))
        return m_new, alpha

    def values(self, c, slot, alpha):
        self.acc_ref[...] = alpha * self.acc_ref[...] + _dot(self.vt_ref[:, self._chunk(c)], self.p_refs[slot][...])

    def scores_late(self, c, slot):
        kc = self.k_ref[self._chunk(c), :]
        for sl in self.late:
            self.s_refs[slot][:, sl] = _dot(kc, self.qq[:, sl])

    def softmax_late(self, c, slot, m):
        self.m_ref[...] = m
        alphas = []
        for sl in self.late:
            shape = (self.tk, sl.stop - sl.start)
            key_pos = lax.broadcasted_iota(jnp.int32, shape, 0) + c * self.tk
            query_pos = self.i * self.tq + ((lax.broadcasted_iota(jnp.int32, shape, 1) + sl.start) & (self.tq - 1))
            s = jnp.where(key_pos <= query_pos, self.s_refs[slot][:, sl], NEG_INF)
            m_old = self.m_ref[:, sl]
            m_new = jnp.maximum(m_old, jnp.max(s, axis=0, keepdims=True))
            alphas.append(jnp.exp2(m_old - m_new))
            self.p_refs[slot][:, sl] = jnp.exp2((s - m_new).astype(BF16))
        return alphas

    def values_late(self, c, slot, alphas):
        vc = self.vt_ref[:, self._chunk(c)]
        for sl, alpha in zip(self.late, alphas):
            self.acc_ref[:, sl] = alpha * self.acc_ref[:, sl] + _dot(vc, self.p_refs[slot][:, sl])

    def result(self):
        dv = self.vt_ref.shape[0] - V7X_BF16_SUBLANES
        return self.acc_ref[0:dv, :] / self.acc_ref[dv:dv + 1, :]


def _flash_streams(streams, i):
    def pair(u, carry, masked_second):
        c = 2 * u
        cm0 = [st.scores(c, 0) for st in streams]
        for st, (_, alpha0, _) in zip(streams, carry):
            st.values(c - 2, 0, alpha0)
        mid = [st.softmax(c - 1, 1, cm1, m, False) for st, (cm1, _, m) in zip(streams, carry)]
        if masked_second:
            for st in streams:
                st.scores_late(c + 1, 1)
            cm1s = [cm1 for cm1, _, _ in carry]
        else:
            cm1s = [st.scores(c + 1, 1) for st in streams]
        for st, (_, alpha1) in zip(streams, mid):
            st.values(c - 1, 1, alpha1)
        out = [st.softmax(c, 0, cm, m, masked_second) for st, cm, (m, _) in zip(streams, cm0, mid)]
        return tuple((cm1, alpha0, m) for cm1, (m, alpha0) in zip(cm1s, out))

    def drain(state):
        for st, (alpha0, _) in zip(streams, state):
            st.values(2 * i, 0, alpha0)
        lates = [st.softmax_late(2 * i + 1, 1, m) for st, (_, m) in zip(streams, state)]
        for st, alphas in zip(streams, lates):
            st.values_late(2 * i + 1, 1, alphas)

    for st in streams:
        st.acc_ref[...] = jnp.zeros(st.acc_ref.shape, F32)
    m_init = [jnp.full((1, st.n), NEG_INF, F32) for st in streams]

    @pl.when(i == 0)
    def _():
        for st in streams:
            st.scores(0, 0)
            st.scores_late(1, 1)
        state = [st.softmax(0, 0, None, m0, True) for st, m0 in zip(streams, m_init)]
        drain([(alpha0, m) for m, alpha0 in state])

    @pl.when(i > 0)
    def _():
        cm0 = [st.scores(0, 0) for st in streams]
        cm1 = [st.scores(1, 1) for st in streams]
        first = [st.softmax(0, 0, cm, m0, False) for st, cm, m0 in zip(streams, cm0, m_init)]
        carry = tuple((c1, alpha0, m) for c1, (m, alpha0) in zip(cm1, first))
        carry = lax.fori_loop(1, i, functools.partial(pair, masked_second=False), carry)
        carry = pair(i, carry, True)
        drain([(alpha0, m) for _, alpha0, m in carry])

    return [st.result() for st in streams]


def _mixer_attn_kernel(dq_ref, dk_ref, dvt_ref, mq_ref, mk_ref, mvt_ref, lam_ref, gd_ref, gm_ref, od_ref, om_ref,
                       *scratch, tq, lam_init):
    i = pl.program_id(2)
    dqk, mqk = 2 * DIFF_QK_DIM, MLA_QK_PAD
    diffs, mlas = [], []
    for hh in range(ATTN_HEADS_PER_STEP):
        qt = dq_ref[hh * dqk:(hh + 1) * dqk, :]
        row = lax.broadcasted_iota(jnp.int32, qt.shape, 0)
        zero = jnp.zeros_like(qt)
        qq = jnp.concatenate([jnp.where(row < DIFF_QK_DIM, qt, zero), jnp.where(row >= DIFF_QK_DIM, qt, zero)],
                             axis=1)
        sc = scratch[12 * hh:12 * (hh + 1)]
        diffs.append(_AttnStream(qq, dk_ref.at[:, pl.ds(hh * dqk, dqk)],
                                 dvt_ref.at[pl.ds(hh * V_AUG_DIM, V_AUG_DIM), :], sc[0:6], i, tq))
        mlas.append(_AttnStream(mq_ref[hh * mqk:(hh + 1) * mqk, :], mk_ref.at[:, pl.ds(hh * mqk, mqk)],
                                mvt_ref.at[pl.ds(hh * V_AUG_DIM, V_AUG_DIM), :], sc[6:12], i, tq))
    outs = _flash_streams(diffs + mlas, i)
    lv = lam_ref[...]
    lam = (jnp.exp(jnp.sum(lv[0:1] * lv[1:2], axis=1, keepdims=True))
           - jnp.exp(jnp.sum(lv[2:3] * lv[3:4], axis=1, keepdims=True)) + lam_init)
    for hh in range(ATTN_HEADS_PER_STEP):
        o_diff, o_mla = outs[hh], outs[ATTN_HEADS_PER_STEP + hh]
        d = o_diff[:, :tq] - lam * o_diff[:, tq:]
        cols = slice(hh * DIFF_V_DIM, (hh + 1) * DIFF_V_DIM)
        od_ref[:, cols] = (_rms_rows(d, gd_ref[...]) * (1.0 - lam_init)).T.astype(od_ref.dtype)
        om_ref[:, cols] = _rms_rows(o_mla, gm_ref[...]).T.astype(om_ref.dtype)


def _attn_scratch(tq, n):
    tk = tq // 2
    return [pltpu.VMEM((tk, n), F32), pltpu.VMEM((tk, n), F32), pltpu.VMEM((tk, n), BF16), pltpu.VMEM((tk, n), BF16),
            pltpu.VMEM((V_AUG_DIM, n), F32), pltpu.VMEM((1, n), F32)]


def _mixer_attn_call(dqt, dk, dvt, mqt, mk, mvt, lam_vecs, g_diff, g_mla, b, s, tq, lam_init):
    assert DIFF_HEADS == MLA_HEADS and DIFF_V_DIM == MLA_V_DIM
    nq = s // tq
    t = b * s
    hps = ATTN_HEADS_PER_STEP
    q_blk = lambda w: pl.BlockSpec((hps * w, tq), lambda bi, h, i: (h, bi * nq + i))
    k_blk = lambda w: pl.BlockSpec((s, hps * w), lambda bi, h, i: (bi, h))
    v_blk = pl.BlockSpec((hps * V_AUG_DIM, s), lambda bi, h, i: (h, bi))
    small = lambda a: pl.BlockSpec(a.shape, lambda bi, h, i: (0, 0))
    o_blk = pl.BlockSpec((tq, hps * DIFF_V_DIM), lambda bi, h, i: (bi * nq + i, h))
    return pl.pallas_call(
        functools.partial(_mixer_attn_kernel, tq=tq, lam_init=lam_init),
        out_shape=[jax.ShapeDtypeStruct((t, DIFF_HEADS * DIFF_V_DIM), BF16),
                   jax.ShapeDtypeStruct((t, MLA_HEADS * MLA_V_DIM), BF16)],
        grid=(b, DIFF_HEADS // hps, nq),
        in_specs=[q_blk(2 * DIFF_QK_DIM), k_blk(2 * DIFF_QK_DIM), v_blk, q_blk(MLA_QK_PAD), k_blk(MLA_QK_PAD), v_blk,
                  small(lam_vecs), small(g_diff), small(g_mla)],
        out_specs=[o_blk, o_blk],
        scratch_shapes=[spec for _ in range(hps) for spec in _attn_scratch(tq, 2 * tq) + _attn_scratch(tq, tq)],
        compiler_params=_cparams(("parallel", "parallel", "arbitrary")),
        name="mixer_attn",
    )(dqt, dk, dvt, mqt, mk, mvt, lam_vecs, g_diff, g_mla)


def _memkv_kernel(mem_ref, wk_ref, wv_ref, k_ref, v_ref):
    mb = mem_ref[...].astype(BF16)
    k_ref[...] = _dot(mb, wk_ref[...]).astype(BF16)
    v_ref[...] = _dot(mb, wv_ref[...]).astype(BF16)


def _memkv_call(mem2, wk, wv):
    n = mem2.shape[0]
    w = wk.shape[1]
    tm = N_MEM
    return pl.pallas_call(
        _memkv_kernel,
        out_shape=[jax.ShapeDtypeStruct((n, w), BF16)] * 2,
        grid=(n // tm,),
        in_specs=[pl.BlockSpec((tm, D_MODEL), lambda i: (i, 0)),
                  pl.BlockSpec(wk.shape, lambda i: (0, 0)),
                  pl.BlockSpec(wv.shape, lambda i: (0, 0))],
        out_specs=[pl.BlockSpec((tm, w), lambda i: (i, 0))] * 2,
        compiler_params=_cparams(("parallel",)),
        name="mem_kv",
    )(mem2, wk, wv)


def _pack_bf16_pairs(x):
    half = x.shape[1] // 2
    r = pltpu.bitcast(x.astype(BF16).astype(F32), jnp.uint32)
    return (r[:, :half] >> 16) | (r[:, half:] & jnp.uint32(0xFFFF0000))


def _unpack_bf16_pairs(w, dtype=BF16):
    lo = pltpu.bitcast(w << 16, F32)
    hi = pltpu.bitcast(w & jnp.uint32(0xFFFF0000), F32)
    return lo.astype(dtype), hi.astype(dtype)


def _post_kernel(da_ref, ma_ref, x_ref, wo_ref, g1_ref, b1_ref, wq_ref, km_ref, vm_ref, wmo_ref,
                 g2_ref, b2_ref, h_ref, hp_ref):
    n_d = DIFF_HEADS * DIFF_V_DIM
    mix = _dot(da_ref[...], wo_ref[0:n_d, :]) + _dot(ma_ref[...], wo_ref[n_d:, :])
    h1 = _layer_norm(DEEPNORM_ALPHA * x_ref[...] + mix, g1_ref[...], b1_ref[...])
    q = (_dot(h1.astype(BF16), wq_ref[...]) * (MEM_HEAD_DIM ** -0.5 * LOG2E)).astype(BF16)
    outs = []
    for h in range(MEM_HEADS):
        sl = slice(h * MEM_HEAD_DIM, (h + 1) * MEM_HEAD_DIM)
        s = _dot_nt(q[:, sl], km_ref[:, sl])
        m = jnp.max(s, axis=1, keepdims=True)
        p = jnp.exp2((s - m).astype(BF16))
        l = jnp.sum(p.astype(F32), axis=1, keepdims=True)
        outs.append(_dot(p, vm_ref[:, sl]) / l)
    o = jnp.concatenate(outs, axis=1).astype(BF16)
    xat = _dot(o, wmo_ref[...])
    h2 = _layer_norm(DEEPNORM_ALPHA * h1 + xat, g2_ref[...], b2_ref[...])
    h_ref[...] = h2
    hp_ref[...] = _pack_bf16_pairs(h2)


def _post_call(diff_o, mla_o, x2, wo, g1, b1, wq, kmem, vmem, wmo, g2, b2, s, tm):
    t = x2.shape[0]
    per_b = s // tm
    row = lambda w: pl.BlockSpec((tm, w), lambda i: (i, 0))
    full = lambda a: pl.BlockSpec(a.shape, lambda i: (0, 0))
    memb = pl.BlockSpec((N_MEM, kmem.shape[1]), lambda i: (i // per_b, 0))
    return pl.pallas_call(
        _post_kernel,
        out_shape=[jax.ShapeDtypeStruct((t, D_MODEL), F32), jax.ShapeDtypeStruct((t, D_MODEL // 2), jnp.uint32)],
        grid=(t // tm,),
        in_specs=[row(diff_o.shape[1]), row(mla_o.shape[1]), row(D_MODEL), full(wo), full(g1), full(b1),
                  full(wq), memb, memb, full(wmo), full(g2), full(b2)],
        out_specs=[row(D_MODEL), row(D_MODEL // 2)],
        compiler_params=_cparams(("parallel",)),
        name="post_mix_xattn",
    )(diff_o, mla_o, x2, wo, g1, b1, wq, kmem, vmem, wmo, g2, b2)


def _router_kernel(h_ref, rwt_ref, bias_ref, tri_ref, eid_ref, rnk_ref, wt_ref, cnt_ref):
    tm = h_ref.shape[0]

    @pl.when(pl.program_id(0) == 0)
    def _():
        cnt_ref[...] = jnp.zeros(cnt_ref.shape, cnt_ref.dtype)

    h = h_ref[...]
    h_hi = h.astype(BF16)
    h_lo = (h - h_hi.astype(F32)).astype(BF16)
    w = rwt_ref[...]
    w_hi = w.astype(BF16)
    w_lo = (w - w_hi.astype(F32)).astype(BF16)
    both = _dot_nt(jnp.concatenate([w_hi, w_lo], axis=0), h_hi)
    logits = both[:N_EXPERTS] + (both[N_EXPERTS:] + _dot_nt(w_hi, h_lo))
    scores = jax.nn.sigmoid(logits)
    choice = scores + bias_ref[...]
    neg = float("-inf")

    c3 = choice.reshape(N_GROUPS, GROUP_SIZE, tm)
    mi = lax.broadcasted_iota(jnp.int32, c3.shape, 1)
    top1 = jnp.max(c3, axis=1, keepdims=True)
    first = jnp.min(jnp.where(c3 == top1, mi, GROUP_SIZE), axis=1, keepdims=True)
    top2 = jnp.max(jnp.where(mi == first, neg, c3), axis=1, keepdims=True)
    gs = (top1 + top2).reshape(N_GROUPS, tm)

    gi = lax.broadcasted_iota(jnp.int32, gs.shape, 0)
    rank = jnp.zeros(gs.shape, jnp.int32)
    for g in range(N_GROUPS):
        rowg = gs[g:g + 1, :]
        beats = (rowg > gs) | ((rowg == gs) & (g < gi))
        rank = rank + beats.astype(jnp.int32)
    gmask = rank < TOPK_GROUPS
    emask = jnp.broadcast_to(gmask.reshape(N_GROUPS, 1, tm), c3.shape).reshape(N_EXPERTS, tm)

    work = jnp.where(emask, choice, NEG_INF)
    ei = lax.broadcasted_iota(jnp.int32, work.shape, 0)
    sel = jnp.zeros(work.shape, jnp.bool_)
    picks = []
    for _ in range(TOP_K):
        mx = jnp.max(work, axis=0, keepdims=True)
        fi = jnp.min(jnp.where(work == mx, ei, N_EXPERTS), axis=0, keepdims=True)
        pick = ei == fi
        picks.append((fi, pick))
        sel = sel | pick
        work = jnp.where(pick, neg, work)
    top_w = jnp.where(sel, scores, 0.0)
    gates = top_w / jnp.sum(top_w, axis=0, keepdims=True) * ROUTED_SCALE

    sel_b = jnp.where(sel, 1.0, 0.0).astype(BF16)
    before = _dot(sel_b, tri_ref[...]).astype(jnp.int32)
    run = cnt_ref[:, 0:1]
    pos = run + before
    cnt_ref[...] = jnp.broadcast_to(run + jnp.sum(sel.astype(jnp.int32), axis=1, keepdims=True), cnt_ref.shape)

    eid_ref[...] = jnp.concatenate([fi for fi, _ in picks], axis=0)
    rnk_ref[...] = jnp.concatenate(
        [jnp.sum(jnp.where(pick, pos, 0), axis=0, keepdims=True) for _, pick in picks], axis=0)
    w_rows = [jnp.sum(jnp.where(pick, gates, 0.0), axis=0, keepdims=True) for _, pick in picks]
    w_pad = jnp.concatenate(w_rows + [jnp.zeros((V7X_LANES - TOP_K, tm), F32)], axis=0)
    wt_ref[...] = w_pad.T


def _router_call(h2, rwt, bias, tm):
    t = h2.shape[0]
    tri = jnp.triu(jnp.ones((tm, tm), BF16), k=1)
    full = lambda a: pl.BlockSpec(a.shape, lambda i: (0, 0))
    return pl.pallas_call(
        _router_kernel,
        out_shape=[jax.ShapeDtypeStruct((TOP_K, t), jnp.int32), jax.ShapeDtypeStruct((TOP_K, t), jnp.int32),
                   jax.ShapeDtypeStruct((t, V7X_LANES), F32), jax.ShapeDtypeStruct((N_EXPERTS, V7X_LANES), jnp.int32)],
        grid=(t // tm,),
        in_specs=[pl.BlockSpec((tm, D_MODEL), lambda i: (i, 0)), full(rwt), full(bias), full(tri)],
        out_specs=[pl.BlockSpec((TOP_K, tm), lambda i: (0, i)), pl.BlockSpec((TOP_K, tm), lambda i: (0, i)),
                   pl.BlockSpec((tm, V7X_LANES), lambda i: (i, 0)),
                   pl.BlockSpec((N_EXPERTS, V7X_LANES), lambda i: (0, 0))],
        compiler_params=_cparams(("arbitrary",)),
        name="router",
    )(h2, rwt, bias, tri)


_SC_MESH_AXES = ("core", "subcore")


def _sc_mesh():
    return plsc.VectorSubcoreMesh(core_axis_name=_SC_MESH_AXES[0], subcore_axis_name=_SC_MESH_AXES[1])


def _sc_worker_chunks(n_chunks):
    n_workers = V7X_SC_CORES * V7X_SC_SUBCORES
    assert n_chunks % n_workers == 0
    per_worker = n_chunks // n_workers
    first = (lax.axis_index(_SC_MESH_AXES[1]) * V7X_SC_CORES + lax.axis_index(_SC_MESH_AXES[0])) * per_worker
    return first, per_worker


def _sc_scatter_rows(x, idx, n_rows):
    t, w = x.shape
    n_chunks, n_k, window = idx.shape

    @functools.partial(pl.kernel, out_type=jax.ShapeDtypeStruct((n_rows, w), x.dtype), mesh=_sc_mesh(),
                       scratch_types=[pltpu.VMEM((n_k, window), jnp.int32), pltpu.VMEM((window, w), x.dtype)],
                       name="moe_dispatch_sc")
    def scatter(x_hbm, i_hbm, o_hbm, idx_v, rows_v):
        first, per_worker = _sc_worker_chunks(n_chunks)

        @pl.loop(0, per_worker)
        def _(c):
            chunk = first + c
            pltpu.sync_copy(i_hbm.at[chunk], idx_v)
            pltpu.sync_copy(x_hbm.at[pl.ds(chunk * window, window)], rows_v)
            for k in range(n_k):
                pltpu.sync_copy(rows_v, o_hbm.at[idx_v.at[k]])

    return scatter(x, idx)


def _sc_gather_rows(y, idx):
    n_chunks, n_k, window = idx.shape
    w = y.shape[1]

    half = window // 2
    steps = [(k, h) for k in range(n_k) for h in range(2)]

    @functools.partial(pl.kernel, out_type=jax.ShapeDtypeStruct((n_k, n_chunks * window, w), y.dtype),
                       mesh=_sc_mesh(),
                       scratch_types=[pltpu.VMEM((n_k, window), jnp.int32), pltpu.VMEM((half, w), y.dtype),
                                      pltpu.VMEM((half, w), y.dtype), pltpu.SemaphoreType.DMA,
                                      pltpu.SemaphoreType.DMA],
                       name="moe_combine_sc")
    def gather(y_hbm, i_hbm, o_hbm, idx_v, rows_a, rows_b, sem_a, sem_b):
        first, per_worker = _sc_worker_chunks(n_chunks)
        bufs = ((rows_a, sem_a), (rows_b, sem_b))

        @pl.loop(0, per_worker)
        def _(c):
            chunk = first + c
            pltpu.sync_copy(i_hbm.at[chunk], idx_v)

            def start(j):
                k, h = steps[j]
                buf, sem = bufs[j % 2]
                return pltpu.async_copy(y_hbm.at[idx_v.at[k, pl.ds(h * half, half)]], buf, sem)

            pending = start(0)
            for j, (k, h) in enumerate(steps):
                following = start(j + 1) if j + 1 < len(steps) else None
                pending.wait()
                pltpu.sync_copy(bufs[j % 2][0], o_hbm.at[k, pl.ds(chunk * window + h * half, half)])
                pending = following

    return gather(y, idx)


def _experts_kernel(te_ref, nu_ref, xs_ref, wg_ref, wu_ref, wd_ref, y_ref, wg_b, wu_b, wd_b):
    i = pl.program_id(0)

    @pl.when(i < nu_ref[0])
    def _():
        @pl.when((i == 0) | (te_ref[i] != te_ref[jnp.maximum(i - 1, 0)]))
        def _():
            wg_b[...] = wg_ref[0].astype(BF16)
            wu_b[...] = wu_ref[0].astype(BF16)
            wd_b[...] = wd_ref[0].astype(BF16)

        half = D_MODEL // 2
        lo, hi = _unpack_bf16_pairs(xs_ref[...])
        a = _dot(lo, wg_b[:half, :]) + _dot(hi, wg_b[half:, :])
        u = _dot(lo, wu_b[:half, :]) + _dot(hi, wu_b[half:, :])
        y_ref[...] = _pack_bf16_pairs(_dot((jax.nn.silu(a) * u).astype(BF16), wd_b[...]))


def _experts_call(tile_expert, n_used, xs, wg, wu, wd, tmx):
    n_rows, w = xs.shape
    tile = lambda i, te, nu: jnp.minimum(i, nu[0] - 1)
    wspec = lambda shp: pl.BlockSpec((1,) + shp, lambda i, te, nu: (te[tile(i, te, nu)], 0, 0))
    return pl.pallas_call(
        _experts_kernel,
        out_shape=jax.ShapeDtypeStruct((n_rows, w), xs.dtype),
        grid_spec=pltpu.PrefetchScalarGridSpec(
            num_scalar_prefetch=2,
            grid=(n_rows // tmx,),
            in_specs=[pl.BlockSpec((tmx, w), lambda i, te, nu: (tile(i, te, nu), 0)),
                      wspec((D_MODEL, EXPERT_DIM)), wspec((D_MODEL, EXPERT_DIM)), wspec((EXPERT_DIM, D_MODEL))],
            out_specs=pl.BlockSpec((tmx, w), lambda i, te, nu: (tile(i, te, nu), 0)),
            scratch_shapes=[pltpu.VMEM((D_MODEL, EXPERT_DIM), BF16), pltpu.VMEM((D_MODEL, EXPERT_DIM), BF16),
                            pltpu.VMEM((EXPERT_DIM, D_MODEL), BF16)],
        ),
        compiler_params=_cparams(("arbitrary",)),
        name="moe_experts",
    )(tile_expert, n_used, xs, wg, wu, wd)


def _combine_kernel(wt_ref, h_ref, yg_ref, sg_ref, su_ref, sd_ref, g3_ref, b3_ref, o_ref):
    h = h_ref[...]
    hb = h.astype(BF16)
    acc = _dot((jax.nn.silu(_dot(hb, sg_ref[...])) * _dot(hb, su_ref[...])).astype(BF16), sd_ref[...])
    wt = wt_ref[...]
    lo_sum = None
    hi_sum = None
    for k in range(TOP_K):
        lo, hi = _unpack_bf16_pairs(yg_ref[k], F32)
        wk = wt[:, k:k + 1]
        lo_sum = wk * lo if lo_sum is None else lo_sum + wk * lo
        hi_sum = wk * hi if hi_sum is None else hi_sum + wk * hi
    acc = acc + jnp.concatenate([lo_sum, hi_sum], axis=1)
    o_ref[...] = _layer_norm(DEEPNORM_ALPHA * h + acc, g3_ref[...], b3_ref[...])


def _combine_call(wt, h2, yg, sg, su, sd, g3, b3, tm):
    t = h2.shape[0]
    full = lambda a: pl.BlockSpec(a.shape, lambda i: (0, 0))
    return pl.pallas_call(
        _combine_kernel,
        out_shape=jax.ShapeDtypeStruct((t, D_MODEL), F32),
        grid=(t // tm,),
        in_specs=[pl.BlockSpec((tm, V7X_LANES), lambda i: (i, 0)),
                  pl.BlockSpec((tm, D_MODEL), lambda i: (i, 0)),
                  pl.BlockSpec((TOP_K, tm, yg.shape[2]), lambda i: (0, i, 0)),
                  full(sg), full(su), full(sd), full(g3), full(b3)],
        out_specs=pl.BlockSpec((tm, D_MODEL), lambda i: (i, 0)),
        compiler_params=_cparams(("parallel",)),
        name="moe_combine",
    )(wt, h2, yg, sg, su, sd, g3, b3)


def _moe(h2, hp, router_w, router_bias, wg, wu, wd, sg, su, sd, g3, b3, tm_route, tmx, tm_comb):
    t = h2.shape[0]
    eid, rnk, wt, cnt = _router_call(h2, router_w.T, router_bias.reshape(-1, 1), tm_route)
    counts = cnt[:, 0]
    tiles_per_e = (counts + (tmx - 1)) // tmx
    tile_end = jnp.cumsum(tiles_per_e)
    row_start = ((tile_end - tiles_per_e) * tmx).astype(jnp.int32)
    n_tiles = (t * TOP_K) // tmx + N_EXPERTS
    n_used = tile_end[-1:].astype(jnp.int32)
    tile_expert = jnp.minimum(
        jnp.sum(tile_end[None, :] <= jnp.arange(n_tiles, dtype=jnp.int32)[:, None], axis=1), N_EXPERTS - 1
    ).astype(jnp.int32)
    expert_ids = jnp.arange(N_EXPERTS, dtype=jnp.int32)[:, None, None]
    dest = rnk + jnp.sum(jnp.where(eid[None] == expert_ids, row_start[:, None, None], 0), axis=0)
    dest_chunks = dest.reshape(TOP_K, t // SC_INDEX_WINDOW, SC_INDEX_WINDOW).transpose(1, 0, 2)
    xs = _sc_scatter_rows(hp, dest_chunks, n_tiles * tmx)
    yp = _experts_call(tile_expert, n_used, xs, wg, wu, wd, tmx)
    yg = _sc_gather_rows(yp, dest_chunks)
    return _combine_call(wt, h2, yg, sg, su, sd, g3, b3, tm_comb)


def _prep_w_in_t(w_in):
    pad = jnp.zeros((D_MODEL, V7X_LANES - MLA_ROPE_DIM), w_in.dtype)
    return jnp.concatenate([w_in, pad], axis=1).T.astype(BF16)


def _prep_w_uq_t(w_uq):
    w = w_uq.reshape(MLA_Q_RANK, MLA_HEADS, MLA_NOPE_DIM + MLA_ROPE_DIM)
    pad = jnp.zeros((MLA_Q_RANK, MLA_HEADS, MLA_QK_PAD - MLA_NOPE_DIM - MLA_ROPE_DIM), w.dtype)
    return jnp.concatenate([w, pad], axis=2).reshape(MLA_Q_RANK, MLA_HEADS * MLA_QK_PAD).T.astype(BF16)


def _prep_w_ukv_t(w_ukv):
    w = w_ukv.reshape(MLA_KV_RANK, MLA_HEADS, MLA_NOPE_DIM + MLA_V_DIM)
    k = w[:, :, :MLA_NOPE_DIM].reshape(MLA_KV_RANK, MLA_HEADS * MLA_NOPE_DIM)
    v = w[:, :, MLA_NOPE_DIM:].reshape(MLA_KV_RANK, MLA_HEADS * MLA_V_DIM)
    return jnp.concatenate([k, v], axis=1).T.astype(BF16)


def _rope_inv_freq(dim):
    return (ROPE_THETA ** (-jnp.arange(0, dim, 2, dtype=F32) / dim)).reshape(-1, 1)


def _layer(x2, mem2, pos_row, b, s, l, w_in, lam_q1, lam_k1, lam_q2, lam_k2, diff_subln_g, mla_q_norm_g,
           mla_w_uq, mla_kv_norm_g, mla_w_ukv, mla_out_norm_g, w_o, ln1_g, ln1_b, mem_w_q, mem_w_k, mem_w_v,
           mem_w_o, ln2_g, ln2_b, router_w, router_bias, exp_w_gate, exp_w_up, exp_w_down, sh_w_gate,
           sh_w_up, sh_w_down, ln3_g, ln3_b):
    row = lambda a: a.reshape(1, -1)
    col = lambda a: a.reshape(-1, 1)
    dqt, dk, dvt, mqt, mk, mvt = _proj_call(
        x2, pos_row, _rope_inv_freq(DIFF_ROT_DIM), _rope_inv_freq(MLA_ROPE_DIM), _prep_w_in_t(w_in),
        col(mla_q_norm_g), _prep_w_uq_t(mla_w_uq), col(mla_kv_norm_g), _prep_w_ukv_t(mla_w_ukv),
        tm=min(TOKEN_TILE, s))

    lam_init = 0.8 - 0.6 * math.exp(-0.3 * l)
    lam_vecs = jnp.stack([lam_q1, lam_k1, lam_q2, lam_k2]).astype(F32)
    diff_o, mla_o = _mixer_attn_call(dqt, dk, dvt, mqt, mk, mvt, lam_vecs, col(diff_subln_g), col(mla_out_norm_g),
                                     b, s, min(ATTN_QUERY_TILE, s), lam_init)

    kmem, vmem = _memkv_call(mem2, mem_w_k.astype(BF16), mem_w_v.astype(BF16))
    h2, hp = _post_call(diff_o, mla_o, x2, w_o.astype(BF16), row(ln1_g), row(ln1_b), mem_w_q.astype(BF16),
                        kmem, vmem, mem_w_o.astype(BF16), row(ln2_g), row(ln2_b), s, tm=min(TOKEN_TILE, s))

    return _moe(h2, hp, router_w, router_bias, exp_w_gate, exp_w_up, exp_w_down,
                sh_w_gate.astype(BF16), sh_w_up.astype(BF16), sh_w_down.astype(BF16), row(ln3_g), row(ln3_b),
                tm_route=min(TOKEN_TILE, s), tmx=EXPERT_ROW_TILE, tm_comb=min(COMBINE_TOKEN_TILE, s))


def kernel(x, mem, positions, w_in, lam_q1, lam_k1, lam_q2, lam_k2, diff_subln_g, mla_q_norm_g, mla_w_uq,
           mla_kv_norm_g, mla_w_ukv, mla_out_norm_g, w_o, ln1_g, ln1_b, mem_w_q, mem_w_k, mem_w_v, mem_w_o,
           ln2_g, ln2_b, router_w, router_bias, exp_w_gate, exp_w_up, exp_w_down, sh_w_gate, sh_w_up,
           sh_w_down, ln3_g, ln3_b):
    b, s, d = x.shape
    h = x.reshape(b * s, d)
    mem2 = mem.reshape(b * mem.shape[1], d)
    pos_row = positions.reshape(1, b * s)
    params = (w_in, lam_q1, lam_k1, lam_q2, lam_k2, diff_subln_g, mla_q_norm_g, mla_w_uq, mla_kv_norm_g,
              mla_w_ukv, mla_out_norm_g, w_o, ln1_g, ln1_b, mem_w_q, mem_w_k, mem_w_v, mem_w_o, ln2_g, ln2_b,
              router_w, router_bias, exp_w_gate, exp_w_up, exp_w_down, sh_w_gate, sh_w_up, sh_w_down,
              ln3_g, ln3_b)
    for l in range(w_in.shape[0]):
        h = _layer(h, mem2, pos_row, b, s, l, *[p[l] for p in params])
    return h.reshape(b, s, d)
```

```python
import functools
import math

import jax
import jax.numpy as jnp
from jax import lax
from jax.experimental import pallas as pl
from jax.experimental.pallas import tpu as pltpu
from jax.experimental.pallas import tpu_sc as plsc

F32 = jnp.float32
BF16 = jnp.bfloat16

D_MODEL = 1024
N_MEM = 256
ROPE_THETA = 500000.0
NEG_INF = -1e30
LN_EPS = 1e-5
RMS_EPS = 1e-6
DIFF_HEADS = 4
DIFF_QK_DIM = 64
DIFF_V_DIM = 128
DIFF_ROT_DIM = DIFF_QK_DIM // 4
MLA_HEADS = 4
MLA_Q_RANK = 256
MLA_KV_RANK = 128
MLA_NOPE_DIM = 128
MLA_ROPE_DIM = 64
MLA_V_DIM = 128
MLA_QK_PAD = 256
MEM_HEADS = 4
MEM_HEAD_DIM = 128
N_EXPERTS = 64
TOP_K = 8
N_GROUPS = 8
TOPK_GROUPS = 4
GROUP_SIZE = N_EXPERTS // N_GROUPS
EXPERT_DIM = 256
ROUTED_SCALE = 2.5
DEPTH = 1
DEEPNORM_ALPHA = (2.0 * DEPTH) ** 0.25
LOG2E = math.log2(math.e)

V7X_LANES = 128
V7X_VMEM_LIMIT_BYTES = 56 * 1024 * 1024
V7X_SC_CORES = 2
V7X_SC_SUBCORES = 16
SC_INDEX_WINDOW = 128
V7X_BF16_SUBLANES = 16
V_AUG_DIM = DIFF_V_DIM + V7X_BF16_SUBLANES

TOKEN_TILE = 1024
ATTN_QUERY_TILE = 512
ATTN_HEADS_PER_STEP = 2
EXPERT_ROW_TILE = 1024
COMBINE_TOKEN_TILE = 512
STREAM_BUFFERS = 3

N_DQ = DIFF_HEADS * 2 * DIFF_QK_DIM
N_DV = DIFF_HEADS * DIFF_V_DIM
O_DK = N_DQ
O_DV = 2 * N_DQ
O_CQ = O_DV + N_DV
O_CKV = O_CQ + MLA_Q_RANK
O_KR = O_CKV + MLA_KV_RANK


def _cparams(sem):
    return pltpu.CompilerParams(dimension_semantics=sem, vmem_limit_bytes=V7X_VMEM_LIMIT_BYTES)


def _dot(a, b):
    return jnp.dot(a, b, preferred_element_type=F32)


def _dot_nt(a, b):
    return lax.dot_general(a, b, (((1,), (1,)), ((), ())), preferred_element_type=F32)


def _rms_rows(x, g):
    return x * lax.rsqrt(jnp.mean(jnp.square(x), axis=0, keepdims=True) + RMS_EPS) * g


def _layer_norm(x, g, b):
    mu = jnp.mean(x, axis=-1, keepdims=True)
    xc = x - mu
    var = jnp.mean(jnp.square(xc), axis=-1, keepdims=True)
    return xc * lax.rsqrt(var + LN_EPS) * g + b


def _with_ones_rows(vt, heads):
    dv = vt.shape[0] // heads
    ones = jnp.ones((V_AUG_DIM - dv, vt.shape[1]), vt.dtype)
    return jnp.concatenate([p for h in range(heads) for p in (vt[h * dv:(h + 1) * dv], ones)], axis=0)


def _rope_rows(x, cos, sin, period):
    h = cos.shape[0]
    pieces = []
    for base in range(0, x.shape[0], period):
        x1 = x[base:base + h]
        x2 = x[base + h:base + 2 * h]
        pieces.append(x1 * cos - x2 * sin)
        pieces.append(x2 * cos + x1 * sin)
        if period > 2 * h:
            pieces.append(x[base + 2 * h:base + period])
    return jnp.concatenate(pieces, axis=0)


def _proj_kernel(x_ref, pos_ref, fd_ref, fm_ref, w_in_ref, gq_ref, wuq_ref, gkv_ref, wukv_ref,
                 dq_ref, dk_ref, dv_ref, mq_ref, mk_ref, mv_ref):
    xb = x_ref[...].astype(BF16)
    pos = pos_ref[...].astype(F32)
    ang_d = fd_ref[...] * pos
    ang_m = fm_ref[...] * pos
    cos_d, sin_d = jnp.cos(ang_d), jnp.sin(ang_d)
    cos_m, sin_m = jnp.cos(ang_m), jnp.sin(ang_m)

    dq = _rope_rows(_dot_nt(w_in_ref[0:N_DQ, :], xb), cos_d, sin_d, DIFF_QK_DIM)
    dq_ref[...] = (dq * (DIFF_QK_DIM ** -0.5 * LOG2E)).astype(BF16)
    dk = _rope_rows(_dot_nt(w_in_ref[O_DK:O_DK + N_DQ, :], xb), cos_d, sin_d, DIFF_QK_DIM)
    dk_ref[...] = dk.T.astype(BF16)
    dv_ref[...] = _with_ones_rows(_dot_nt(w_in_ref[O_DV:O_DV + N_DV, :], xb), DIFF_HEADS).astype(BF16)

    c_q = _dot_nt(w_in_ref[O_CQ:O_CQ + MLA_Q_RANK, :], xb)
    q = _dot(wuq_ref[...], _rms_rows(c_q, gq_ref[...]).astype(BF16))
    q = q * ((MLA_NOPE_DIM + MLA_ROPE_DIM) ** -0.5 * LOG2E)
    pieces = []
    for h in range(MLA_HEADS):
        o = h * MLA_QK_PAD
        pieces.append(q[o:o + MLA_NOPE_DIM])
        pieces.append(_rope_rows(q[o + MLA_NOPE_DIM:o + MLA_NOPE_DIM + MLA_ROPE_DIM], cos_m, sin_m, MLA_ROPE_DIM))
        pieces.append(q[o + MLA_NOPE_DIM + MLA_ROPE_DIM:o + MLA_QK_PAD])
    mq_ref[...] = jnp.concatenate(pieces, axis=0).astype(BF16)

    c_kv = _dot_nt(w_in_ref[O_CKV:O_CKV + MLA_KV_RANK, :], xb)
    kv = _dot(wukv_ref[...], _rms_rows(c_kv, gkv_ref[...]).astype(BF16))
    n_kn = MLA_HEADS * MLA_NOPE_DIM
    mv_ref[...] = _with_ones_rows(kv[n_kn:], MLA_HEADS).astype(BF16)
    k_rope = _dot_nt(w_in_ref[O_KR:O_KR + V7X_LANES, :], xb)
    k_pe = jnp.concatenate([_rope_rows(k_rope[:MLA_ROPE_DIM], cos_m, sin_m, MLA_ROPE_DIM),
                            k_rope[MLA_ROPE_DIM:]], axis=0)
    k_nope_t = kv[:n_kn].T.astype(BF16)
    k_pe_t = k_pe.T.astype(BF16)
    for h in range(MLA_HEADS):
        o = h * MLA_QK_PAD
        mk_ref[:, o:o + MLA_NOPE_DIM] = k_nope_t[:, h * MLA_NOPE_DIM:(h + 1) * MLA_NOPE_DIM]
        mk_ref[:, o + MLA_NOPE_DIM:o + MLA_QK_PAD] = k_pe_t


def _proj_call(x2, pos_row, fd, fm, w_in_t, gq, wuq_t, gkv, wukv_t, tm):
    t = x2.shape[0]
    full = lambda a: pl.BlockSpec(a.shape, lambda i: (0, 0))
    col_blk = lambda r: pl.BlockSpec((r, tm), lambda i: (0, i))
    row_blk = lambda w: pl.BlockSpec((tm, w), lambda i: (i, 0))
    n_mq = MLA_HEADS * MLA_QK_PAD
    return pl.pallas_call(
        _proj_kernel,
        out_shape=[jax.ShapeDtypeStruct((N_DQ, t), BF16), jax.ShapeDtypeStruct((t, N_DQ), BF16),
                   jax.ShapeDtypeStruct((DIFF_HEADS * V_AUG_DIM, t), BF16), jax.ShapeDtypeStruct((n_mq, t), BF16),
                   jax.ShapeDtypeStruct((t, n_mq), BF16), jax.ShapeDtypeStruct((MLA_HEADS * V_AUG_DIM, t), BF16)],
        grid=(t // tm,),
        in_specs=[row_blk(D_MODEL), col_blk(1), full(fd), full(fm), full(w_in_t), full(gq), full(wuq_t),
                  full(gkv), full(wukv_t)],
        out_specs=[col_blk(N_DQ), row_blk(N_DQ), col_blk(DIFF_HEADS * V_AUG_DIM), col_blk(n_mq), row_blk(n_mq),
                   col_blk(MLA_HEADS * V_AUG_DIM)],
        compiler_params=_cparams(("parallel",)),
        name="proj_rope",
    )(x2, pos_row, fd, fm, w_in_t, gq, wuq_t, gkv, wukv_t)


class _AttnStream:
    def __init__(self, qq, k_ref, vt_ref, scratch, i, tq):
        self.qq, self.k_ref, self.vt_ref, self.i, self.tq = qq, k_ref, vt_ref, i, tq
        self.s_refs, self.p_refs, self.acc_ref, self.m_ref = scratch[0:2], scratch[2:4], scratch[4], scratch[5]
        self.n = qq.shape[1]
        self.tk = self.s_refs[0].shape[0]
        assert tq == 2 * self.tk
        self.q_pos = i * tq + (lax.broadcasted_iota(jnp.int32, (self.tk, self.n), 1) & (tq - 1))
        self.key_row = lax.broadcasted_iota(jnp.int32, (self.tk, self.n), 0)
        self.late = [slice(a + tq // 2, a + tq) for a in range(0, self.n, tq)]

    def _chunk(self, c):
        return pl.ds(pl.multiple_of(c * self.tk, self.tk), self.tk)

    def scores(self, c, slot):
        s = _dot(self.k_ref[self._chunk(c), :], self.qq)
        self.s_refs[slot][...] = s
        return jnp.max(s, axis=0, keepdims=True)

    def softmax(self, c, slot, cmax, m, masked):
        s = self.s_refs[slot][...]
        if masked:
            s = jnp.where(self.key_row + c * self.tk <= self.q_pos, s, NEG_INF)
            cmax = jnp.max(s, axis=0, keepdims=True)
        m_new = jnp.maximum(m, cmax)
        alpha = jnp.exp2(m - m_new)
        self.p_refs[slot][...] = jnp.exp2((s - m_new).astype(BF16))
        return m_new, alpha

    def values(self, c, slot, alpha):
        self.acc_ref[...] = alpha * self.acc_ref[...] + _dot(self.vt_ref[:, self._chunk(c)], self.p_refs[slot][...])

    def scores_late(self, c, slot):
        kc = self.k_ref[self._chunk(c), :]
        for sl in self.late:
            self.s_refs[slot][:, sl] = _dot(kc, self.qq[:, sl])

    def softmax_late(self, c, slot, m):
        self.m_ref[...] = m
        alphas = []
        for sl in self.late:
            shape = (self.tk, sl.stop - sl.start)
            key_pos = lax.broadcasted_iota(jnp.int32, shape, 0) + c * self.tk
            query_pos = self.i * self.tq + ((lax.broadcasted_iota(jnp.int32, shape, 1) + sl.start) & (self.tq - 1))
            s = jnp.where(key_pos <= query_pos, self.s_refs[slot][:, sl], NEG_INF)
            m_old = self.m_ref[:, sl]
            m_new = jnp.maximum(m_old, jnp.max(s, axis=0, keepdims=True))
            alphas.append(jnp.exp2(m_old - m_new))
            self.p_refs[slot][:, sl] = jnp.exp2((s - m_new).astype(BF16))
        return alphas

    def values_late(self, c, slot, alphas):
        vc = self.vt_ref[:, self._chunk(c)]
        for sl, alpha in zip(self.late, alphas):
            self.acc_ref[:, sl] = alpha * self.acc_ref[:, sl] + _dot(vc, self.p_refs[slot][:, sl])

    def result(self):
        dv = self.vt_ref.shape[0] - V7X_BF16_SUBLANES
        return self.acc_ref[0:dv, :] / self.acc_ref[dv:dv + 1, :]


def _flash_streams(streams, i):
    def pair(u, carry, masked_second):
        c = 2 * u
        cm0 = [st.scores(c, 0) for st in streams]
        for st, (_, alpha0, _) in zip(streams, carry):
            st.values(c - 2, 0, alpha0)
        mid = [st.softmax(c - 1, 1, cm1, m, False) for st, (cm1, _, m) in zip(streams, carry)]
        if masked_second:
            for st in streams:
                st.scores_late(c + 1, 1)
            cm1s = [cm1 for cm1, _, _ in carry]
        else:
            cm1s = [st.scores(c + 1, 1) for st in streams]
        for st, (_, alpha1) in zip(streams, mid):
            st.values(c - 1, 1, alpha1)
        out = [st.softmax(c, 0, cm, m, masked_second) for st, cm, (m, _) in zip(streams, cm0, mid)]
        return tuple((cm1, alpha0, m) for cm1, (m, alpha0) in zip(cm1s, out))

    def drain(state):
        for st, (alpha0, _) in zip(streams, state):
            st.values(2 * i, 0, alpha0)
        lates = [st.softmax_late(2 * i + 1, 1, m) for st, (_, m) in zip(streams, state)]
        for st, alphas in zip(streams, lates):
            st.values_late(2 * i + 1, 1, alphas)

    for st in streams:
        st.acc_ref[...] = jnp.zeros(st.acc_ref.shape, F32)
    m_init = [jnp.full((1, st.n), NEG_INF, F32) for st in streams]

    @pl.when(i == 0)
    def _():
        for st in streams:
            st.scores(0, 0)
            st.scores_late(1, 1)
        state = [st.softmax(0, 0, None, m0, True) for st, m0 in zip(streams, m_init)]
        drain([(alpha0, m) for m, alpha0 in state])

    @pl.when(i > 0)
    def _():
        cm0 = [st.scores(0, 0) for st in streams]
        cm1 = [st.scores(1, 1) for st in streams]
        first = [st.softmax(0, 0, cm, m0, False) for st, cm, m0 in zip(streams, cm0, m_init)]
        carry = tuple((c1, alpha0, m) for c1, (m, alpha0) in zip(cm1, first))
        carry = lax.fori_loop(1, i, functools.partial(pair, masked_second=False), carry)
        carry = pair(i, carry, True)
        drain([(alpha0, m) for _, alpha0, m in carry])

    return [st.result() for st in streams]


def _mixer_attn_kernel(dq_ref, dk_ref, dvt_ref, mq_ref, mk_ref, mvt_ref, lam_ref, gd_ref, gm_ref, od_ref, om_ref,
                       *scratch, tq, lam_init):
    i = pl.program_id(2)
    dqk, mqk = 2 * DIFF_QK_DIM, MLA_QK_PAD
    diffs, mlas = [], []
    for hh in range(ATTN_HEADS_PER_STEP):
        qt = dq_ref[hh * dqk:(hh + 1) * dqk, :]
        row = lax.broadcasted_iota(jnp.int32, qt.shape, 0)
        zero = jnp.zeros_like(qt)
        qq = jnp.concatenate([jnp.where(row < DIFF_QK_DIM, qt, zero), jnp.where(row >= DIFF_QK_DIM, qt, zero)],
                             axis=1)
        sc = scratch[12 * hh:12 * (hh + 1)]
        diffs.append(_AttnStream(qq, dk_ref.at[:, pl.ds(hh * dqk, dqk)],
                                 dvt_ref.at[pl.ds(hh * V_AUG_DIM, V_AUG_DIM), :], sc[0:6], i, tq))
        mlas.append(_AttnStream(mq_ref[hh * mqk:(hh + 1) * mqk, :], mk_ref.at[:, pl.ds(hh * mqk, mqk)],
                                mvt_ref.at[pl.ds(hh * V_AUG_DIM, V_AUG_DIM), :], sc[6:12], i, tq))
    outs = _flash_streams(diffs + mlas, i)
    lv = lam_ref[...]
    lam = (jnp.exp(jnp.sum(lv[0:1] * lv[1:2], axis=1, keepdims=True))
           - jnp.exp(jnp.sum(lv[2:3] * lv[3:4], axis=1, keepdims=True)) + lam_init)
    for hh in range(ATTN_HEADS_PER_STEP):
        o_diff, o_mla = outs[hh], outs[ATTN_HEADS_PER_STEP + hh]
        d = o_diff[:, :tq] - lam * o_diff[:, tq:]
        cols = slice(hh * DIFF_V_DIM, (hh + 1) * DIFF_V_DIM)
        od_ref[:, cols] = (_rms_rows(d, gd_ref[...]) * (1.0 - lam_init)).T.astype(od_ref.dtype)
        om_ref[:, cols] = _rms_rows(o_mla, gm_ref[...]).T.astype(om_ref.dtype)


def _attn_scratch(tq, n):
    tk = tq // 2
    return [pltpu.VMEM((tk, n), F32), pltpu.VMEM((tk, n), F32), pltpu.VMEM((tk, n), BF16), pltpu.VMEM((tk, n), BF16),
            pltpu.VMEM((V_AUG_DIM, n), F32), pltpu.VMEM((1, n), F32)]


def _mixer_attn_call(dqt, dk, dvt, mqt, mk, mvt, lam_vecs, g_diff, g_mla, b, s, tq, lam_init):
    assert DIFF_HEADS == MLA_HEADS and DIFF_V_DIM == MLA_V_DIM
    nq = s // tq
    t = b * s
    hps = ATTN_HEADS_PER_STEP
    q_blk = lambda w: pl.BlockSpec((hps * w, tq), lambda bi, h, i: (h, bi * nq + i))
    k_blk = lambda w: pl.BlockSpec((s, hps * w), lambda bi, h, i: (bi, h))
    v_blk = pl.BlockSpec((hps * V_AUG_DIM, s), lambda bi, h, i: (h, bi))
    small = lambda a: pl.BlockSpec(a.shape, lambda bi, h, i: (0, 0))
    o_blk = pl.BlockSpec((tq, hps * DIFF_V_DIM), lambda bi, h, i: (bi * nq + i, h))
    return pl.pallas_call(
        functools.partial(_mixer_attn_kernel, tq=tq, lam_init=lam_init),
        out_shape=[jax.ShapeDtypeStruct((t, DIFF_HEADS * DIFF_V_DIM), BF16),
                   jax.ShapeDtypeStruct((t, MLA_HEADS * MLA_V_DIM), BF16)],
        grid=(b, DIFF_HEADS // hps, nq),
        in_specs=[q_blk(2 * DIFF_QK_DIM), k_blk(2 * DIFF_QK_DIM), v_blk, q_blk(MLA_QK_PAD), k_blk(MLA_QK_PAD), v_blk,
                  small(lam_vecs), small(g_diff), small(g_mla)],
        out_specs=[o_blk, o_blk],
        scratch_shapes=[spec for _ in range(hps) for spec in _attn_scratch(tq, 2 * tq) + _attn_scratch(tq, tq)],
        compiler_params=_cparams(("parallel", "parallel", "arbitrary")),
        name="mixer_attn",
    )(dqt, dk, dvt, mqt, mk, mvt, lam_vecs, g_diff, g_mla)


def _memkv_kernel(mem_ref, wk_ref, wv_ref, k_ref, v_ref):
    mb = mem_ref[...].astype(BF16)
    k_ref[...] = _dot(mb, wk_ref[...]).astype(BF16)
    v_ref[...] = _dot(mb, wv_ref[...]).astype(BF16)


def _memkv_call(mem2, wk, wv):
    n = mem2.shape[0]
    w = wk.shape[1]
    tm = N_MEM
    return pl.pallas_call(
        _memkv_kernel,
        out_shape=[jax.ShapeDtypeStruct((n, w), BF16)] * 2,
        grid=(n // tm,),
        in_specs=[pl.BlockSpec((tm, D_MODEL), lambda i: (i, 0)),
                  pl.BlockSpec(wk.shape, lambda i: (0, 0)),
                  pl.BlockSpec(wv.shape, lambda i: (0, 0))],
        out_specs=[pl.BlockSpec((tm, w), lambda i: (i, 0))] * 2,
        compiler_params=_cparams(("parallel",)),
        name="mem_kv",
    )(mem2, wk, wv)


def _pack_bf16_pairs(x):
    half = x.shape[1] // 2
    r = pltpu.bitcast(x.astype(BF16).astype(F32), jnp.uint32)
    return (r[:, :half] >> 16) | (r[:, half:] & jnp.uint32(0xFFFF0000))


def _unpack_bf16_pairs(w, dtype=BF16):
    lo = pltpu.bitcast(w << 16, F32)
    hi = pltpu.bitcast(w & jnp.uint32(0xFFFF0000), F32)
    return lo.astype(dtype), hi.astype(dtype)


def _post_kernel(da_ref, ma_ref, x_ref, wo_ref, g1_ref, b1_ref, wq_ref, km_ref, vm_ref, wmo_ref,
                 g2_ref, b2_ref, h_ref, hp_ref):
    n_d = DIFF_HEADS * DIFF_V_DIM
    mix = _dot(da_ref[...], wo_ref[0:n_d, :]) + _dot(ma_ref[...], wo_ref[n_d:, :])
    h1 = _layer_norm(DEEPNORM_ALPHA * x_ref[...] + mix, g1_ref[...], b1_ref[...])
    q = (_dot(h1.astype(BF16), wq_ref[...]) * (MEM_HEAD_DIM ** -0.5 * LOG2E)).astype(BF16)
    outs = []
    for h in range(MEM_HEADS):
        sl = slice(h * MEM_HEAD_DIM, (h + 1) * MEM_HEAD_DIM)
        s = _dot_nt(q[:, sl], km_ref[:, sl])
        m = jnp.max(s, axis=1, keepdims=True)
        p = jnp.exp2((s - m).astype(BF16))
        l = jnp.sum(p.astype(F32), axis=1, keepdims=True)
        outs.append(_dot(p, vm_ref[:, sl]) / l)
    o = jnp.concatenate(outs, axis=1).astype(BF16)
    xat = _dot(o, wmo_ref[...])
    h2 = _layer_norm(DEEPNORM_ALPHA * h1 + xat, g2_ref[...], b2_ref[...])
    h_ref[...] = h2
    hp_ref[...] = _pack_bf16_pairs(h2)


def _post_call(diff_o, mla_o, x2, wo, g1, b1, wq, kmem, vmem, wmo, g2, b2, s, tm):
    t = x2.shape[0]
    per_b = s // tm
    row = lambda w: pl.BlockSpec((tm, w), lambda i: (i, 0))
    full = lambda a: pl.BlockSpec(a.shape, lambda i: (0, 0))
    memb = pl.BlockSpec((N_MEM, kmem.shape[1]), lambda i: (i // per_b, 0))
    return pl.pallas_call(
        _post_kernel,
        out_shape=[jax.ShapeDtypeStruct((t, D_MODEL), F32), jax.ShapeDtypeStruct((t, D_MODEL // 2), jnp.uint32)],
        grid=(t // tm,),
        in_specs=[row(diff_o.shape[1]), row(mla_o.shape[1]), row(D_MODEL), full(wo), full(g1), full(b1),
                  full(wq), memb, memb, full(wmo), full(g2), full(b2)],
        out_specs=[row(D_MODEL), row(D_MODEL // 2)],
        compiler_params=_cparams(("parallel",)),
        name="post_mix_xattn",
    )(diff_o, mla_o, x2, wo, g1, b1, wq, kmem, vmem, wmo, g2, b2)


def _router_kernel(h_ref, rwt_ref, bias_ref, tri_ref, eid_ref, rnk_ref, wt_ref, cnt_ref):
    tm = h_ref.shape[0]

    @pl.when(pl.program_id(0) == 0)
    def _():
        cnt_ref[...] = jnp.zeros(cnt_ref.shape, cnt_ref.dtype)

    h = h_ref[...]
    h_hi = h.astype(BF16)
    h_lo = (h - h_hi.astype(F32)).astype(BF16)
    w = rwt_ref[...]
    w_hi = w.astype(BF16)
    w_lo = (w - w_hi.astype(F32)).astype(BF16)
    both = _dot_nt(jnp.concatenate([w_hi, w_lo], axis=0), h_hi)
    logits = both[:N_EXPERTS] + (both[N_EXPERTS:] + _dot_nt(w_hi, h_lo))
    scores = jax.nn.sigmoid(logits)
    choice = scores + bias_ref[...]
    neg = float("-inf")

    c3 = choice.reshape(N_GROUPS, GROUP_SIZE, tm)
    mi = lax.broadcasted_iota(jnp.int32, c3.shape, 1)
    top1 = jnp.max(c3, axis=1, keepdims=True)
    first = jnp.min(jnp.where(c3 == top1, mi, GROUP_SIZE), axis=1, keepdims=True)
    top2 = jnp.max(jnp.where(mi == first, neg, c3), axis=1, keepdims=True)
    gs = (top1 + top2).reshape(N_GROUPS, tm)

    gi = lax.broadcasted_iota(jnp.int32, gs.shape, 0)
    rank = jnp.zeros(gs.shape, jnp.int32)
    for g in range(N_GROUPS):
        rowg = gs[g:g + 1, :]
        beats = (rowg > gs) | ((rowg == gs) & (g < gi))
        rank = rank + beats.astype(jnp.int32)
    gmask = rank < TOPK_GROUPS
    emask = jnp.broadcast_to(gmask.reshape(N_GROUPS, 1, tm), c3.shape).reshape(N_EXPERTS, tm)

    work = jnp.where(emask, choice, NEG_INF)
    ei = lax.broadcasted_iota(jnp.int32, work.shape, 0)
    sel = jnp.zeros(work.shape, jnp.bool_)
    picks = []
    for _ in range(TOP_K):
        mx = jnp.max(work, axis=0, keepdims=True)
        fi = jnp.min(jnp.where(work == mx, ei, N_EXPERTS), axis=0, keepdims=True)
        pick = ei == fi
        picks.append((fi, pick))
        sel = sel | pick
        work = jnp.where(pick, neg, work)
    top_w = jnp.where(sel, scores, 0.0)
    gates = top_w / jnp.sum(top_w, axis=0, keepdims=True) * ROUTED_SCALE

    sel_b = jnp.where(sel, 1.0, 0.0).astype(BF16)
    before = _dot(sel_b, tri_ref[...]).astype(jnp.int32)
    run = cnt_ref[:, 0:1]
    pos = run + before
    cnt_ref[...] = jnp.broadcast_to(run + jnp.sum(sel.astype(jnp.int32), axis=1, keepdims=True), cnt_ref.shape)

    eid_ref[...] = jnp.concatenate([fi for fi, _ in picks], axis=0)
    rnk_ref[...] = jnp.concatenate(
        [jnp.sum(jnp.where(pick, pos, 0), axis=0, keepdims=True) for _, pick in picks], axis=0)
    w_rows = [jnp.sum(jnp.where(pick, gates, 0.0), axis=0, keepdims=True) for _, pick in picks]
    w_pad = jnp.concatenate(w_rows + [jnp.zeros((V7X_LANES - TOP_K, tm), F32)], axis=0)
    wt_ref[...] = w_pad.T


def _router_call(h2, rwt, bias, tm):
    t = h2.shape[0]
    tri = jnp.triu(jnp.ones((tm, tm), BF16), k=1)
    full = lambda a: pl.BlockSpec(a.shape, lambda i: (0, 0))
    return pl.pallas_call(
        _router_kernel,
        out_shape=[jax.ShapeDtypeStruct((TOP_K, t), jnp.int32), jax.ShapeDtypeStruct((TOP_K, t), jnp.int32),
                   jax.ShapeDtypeStruct((t, V7X_LANES), F32), jax.ShapeDtypeStruct((N_EXPERTS, V7X_LANES), jnp.int32)],
        grid=(t // tm,),
        in_specs=[pl.BlockSpec((tm, D_MODEL), lambda i: (i, 0)), full(rwt), full(bias), full(tri)],
        out_specs=[pl.BlockSpec((TOP_K, tm), lambda i: (0, i)), pl.BlockSpec((TOP_K, tm), lambda i: (0, i)),
                   pl.BlockSpec((tm, V7X_LANES), lambda i: (i, 0)),
                   pl.BlockSpec((N_EXPERTS, V7X_LANES), lambda i: (0, 0))],
        compiler_params=_cparams(("arbitrary",)),
        name="router",
    )(h2, rwt, bias, tri)


_SC_MESH_AXES = ("core", "subcore")


def _sc_mesh():
    return plsc.VectorSubcoreMesh(core_axis_name=_SC_MESH_AXES[0], subcore_axis_name=_SC_MESH_AXES[1])


def _sc_worker_chunks(n_chunks):
    n_workers = V7X_SC_CORES * V7X_SC_SUBCORES
    assert n_chunks % n_workers == 0
    per_worker = n_chunks // n_workers
    first = (lax.axis_index(_SC_MESH_AXES[1]) * V7X_SC_CORES + lax.axis_index(_SC_MESH_AXES[0])) * per_worker
    return first, per_worker


def _sc_scatter_rows(x, idx, n_rows):
    t, w = x.shape
    n_chunks, n_k, window = idx.shape

    @functools.partial(pl.kernel, out_type=jax.ShapeDtypeStruct((n_rows, w), x.dtype), mesh=_sc_mesh(),
                       scratch_types=[pltpu.VMEM((n_k, window), jnp.int32), pltpu.VMEM((window, w), x.dtype)],
                       name="moe_dispatch_sc")
    def scatter(x_hbm, i_hbm, o_hbm, idx_v, rows_v):
        first, per_worker = _sc_worker_chunks(n_chunks)

        @pl.loop(0, per_worker)
        def _(c):
            chunk = first + c
            pltpu.sync_copy(i_hbm.at[chunk], idx_v)
            pltpu.sync_copy(x_hbm.at[pl.ds(chunk * window, window)], rows_v)
            for k in range(n_k):
                pltpu.sync_copy(rows_v, o_hbm.at[idx_v.at[k]])

    return scatter(x, idx)


def _sc_gather_rows(y, idx):
    n_chunks, n_k, window = idx.shape
    w = y.shape[1]

    half = window // 2
    steps = [(k, h) for k in range(n_k) for h in range(2)]

    @functools.partial(pl.kernel, out_type=jax.ShapeDtypeStruct((n_k, n_chunks * window, w), y.dtype),
                       mesh=_sc_mesh(),
                       scratch_types=[pltpu.VMEM((n_k, window), jnp.int32), pltpu.VMEM((half, w), y.dtype),
                                      pltpu.VMEM((half, w), y.dtype), pltpu.SemaphoreType.DMA,
                                      pltpu.SemaphoreType.DMA],
                       name="moe_combine_sc")
    def gather(y_hbm, i_hbm, o_hbm, idx_v, rows_a, rows_b, sem_a, sem_b):
        first, per_worker = _sc_worker_chunks(n_chunks)
        bufs = ((rows_a, sem_a), (rows_b, sem_b))

        @pl.loop(0, per_worker)
        def _(c):
            chunk = first + c
            pltpu.sync_copy(i_hbm.at[chunk], idx_v)

            def start(j):
                k, h = steps[j]
                buf, sem = bufs[j % 2]
                return pltpu.async_copy(y_hbm.at[idx_v.at[k, pl.ds(h * half, half)]], buf, sem)

            pending = start(0)
            for j, (k, h) in enumerate(steps):
                following = start(j + 1) if j + 1 < len(steps) else None
                pending.wait()
                pltpu.sync_copy(bufs[j % 2][0], o_hbm.at[k, pl.ds(chunk * window + h * half, half)])
                pending = following

    return gather(y, idx)


def _experts_kernel(te_ref, nu_ref, xs_ref, wg_ref, wu_ref, wd_ref, y_ref, wg_b, wu_b, wd_b):
    i = pl.program_id(0)

    @pl.when(i < nu_ref[0])
    def _():
        @pl.when((i == 0) | (te_ref[i] != te_ref[jnp.maximum(i - 1, 0)]))
        def _():
            wg_b[...] = wg_ref[0].astype(BF16)
            wu_b[...] = wu_ref[0].astype(BF16)
            wd_b[...] = wd_ref[0].astype(BF16)

        half = D_MODEL // 2
        lo, hi = _unpack_bf16_pairs(xs_ref[...])
        a = _dot(lo, wg_b[:half, :]) + _dot(hi, wg_b[half:, :])
        u = _dot(lo, wu_b[:half, :]) + _dot(hi, wu_b[half:, :])
        y_ref[...] = _pack_bf16_pairs(_dot((jax.nn.silu(a) * u).astype(BF16), wd_b[...]))


def _experts_call(tile_expert, n_used, xs, wg, wu, wd, tmx):
    n_rows, w = xs.shape
    tile = lambda i, te, nu: jnp.minimum(i, nu[0] - 1)
    wspec = lambda shp: pl.BlockSpec((1,) + shp, lambda i, te, nu: (te[tile(i, te, nu)], 0, 0))
    return pl.pallas_call(
        _experts_kernel,
        out_shape=jax.ShapeDtypeStruct((n_rows, w), xs.dtype),
        grid_spec=pltpu.PrefetchScalarGridSpec(
            num_scalar_prefetch=2,
            grid=(n_rows // tmx,),
            in_specs=[pl.BlockSpec((tmx, w), lambda i, te, nu: (tile(i, te, nu), 0)),
                      wspec((D_MODEL, EXPERT_DIM)), wspec((D_MODEL, EXPERT_DIM)), wspec((EXPERT_DIM, D_MODEL))],
            out_specs=pl.BlockSpec((tmx, w), lambda i, te, nu: (tile(i, te, nu), 0)),
            scratch_shapes=[pltpu.VMEM((D_MODEL, EXPERT_DIM), BF16), pltpu.VMEM((D_MODEL, EXPERT_DIM), BF16),
                            pltpu.VMEM((EXPERT_DIM, D_MODEL), BF16)],
        ),
        compiler_params=_cparams(("arbitrary",)),
        name="moe_experts",
    )(tile_expert, n_used, xs, wg, wu, wd)


def _combine_kernel(wt_ref, h_ref, yg_hbm, sg_ref, su_ref, sd_ref, g3_ref, b3_ref, o_ref, ring, sems):
    s = pl.program_id(0)
    n = pl.num_programs(0)
    tm = h_ref.shape[0]
    ahead = STREAM_BUFFERS - 1

    def block_copy(step):
        slot = step % STREAM_BUFFERS
        return pltpu.make_async_copy(yg_hbm.at[:, pl.ds(step * tm, tm), :], ring.at[slot], sems.at[slot])

    @pl.when(s == 0)
    def _():
        for j in range(ahead):
            block_copy(j).start()

    @pl.when(s + ahead < n)
    def _():
        block_copy(s + ahead).start()

    h = h_ref[...]
    hb = h.astype(BF16)
    acc = _dot((jax.nn.silu(_dot(hb, sg_ref[...])) * _dot(hb, su_ref[...])).astype(BF16), sd_ref[...])
    wt = wt_ref[...]
    block_copy(s).wait()
    yg_ref = ring.at[s % STREAM_BUFFERS]
    lo_sum = None
    hi_sum = None
    for k in range(TOP_K):
        lo, hi = _unpack_bf16_pairs(yg_ref[k], F32)
        wk = wt[:, k:k + 1]
        lo_sum = wk * lo if lo_sum is None else lo_sum + wk * lo
        hi_sum = wk * hi if hi_sum is None else hi_sum + wk * hi
    acc = acc + jnp.concatenate([lo_sum, hi_sum], axis=1)
    o_ref[...] = _layer_norm(DEEPNORM_ALPHA * h + acc, g3_ref[...], b3_ref[...])


def _combine_call(wt, h2, yg, sg, su, sd, g3, b3, tm):
    t = h2.shape[0]
    full = lambda a: pl.BlockSpec(a.shape, lambda i: (0, 0))
    return pl.pallas_call(
        _combine_kernel,
        out_shape=jax.ShapeDtypeStruct((t, D_MODEL), F32),
        grid=(t // tm,),
        in_specs=[pl.BlockSpec((tm, V7X_LANES), lambda i: (i, 0)),
                  pl.BlockSpec((tm, D_MODEL), lambda i: (i, 0)),
                  pl.BlockSpec(memory_space=pl.ANY),
                  full(sg), full(su), full(sd), full(g3), full(b3)],
        out_specs=pl.BlockSpec((tm, D_MODEL), lambda i: (i, 0)),
        scratch_shapes=[pltpu.VMEM((STREAM_BUFFERS, TOP_K, tm, yg.shape[2]), yg.dtype),
                        pltpu.SemaphoreType.DMA((STREAM_BUFFERS,))],
        compiler_params=_cparams(("arbitrary",)),
        name="moe_combine",
    )(wt, h2, yg, sg, su, sd, g3, b3)


def _moe(h2, hp, router_w, router_bias, wg, wu, wd, sg, su, sd, g3, b3, tm_route, tmx, tm_comb):
    t = h2.shape[0]
    eid, rnk, wt, cnt = _router_call(h2, router_w.T, router_bias.reshape(-1, 1), tm_route)
    counts = cnt[:, 0]
    tiles_per_e = (counts + (tmx - 1)) // tmx
    tile_end = jnp.cumsum(tiles_per_e)
    row_start = ((tile_end - tiles_per_e) * tmx).astype(jnp.int32)
    n_tiles = (t * TOP_K) // tmx + N_EXPERTS
    n_used = tile_end[-1:].astype(jnp.int32)
    tile_expert = jnp.minimum(
        jnp.sum(tile_end[None, :] <= jnp.arange(n_tiles, dtype=jnp.int32)[:, None], axis=1), N_EXPERTS - 1
    ).astype(jnp.int32)
    expert_ids = jnp.arange(N_EXPERTS, dtype=jnp.int32)[:, None, None]
    dest = rnk + jnp.sum(jnp.where(eid[None] == expert_ids, row_start[:, None, None], 0), axis=0)
    dest_chunks = dest.reshape(TOP_K, t // SC_INDEX_WINDOW, SC_INDEX_WINDOW).transpose(1, 0, 2)
    xs = _sc_scatter_rows(hp, dest_chunks, n_tiles * tmx)
    yp = _experts_call(tile_expert, n_used, xs, wg, wu, wd, tmx)
    yg = _sc_gather_rows(yp, dest_chunks)
    return _combine_call(wt, h2, yg, sg, su, sd, g3, b3, tm_comb)


def _prep_w_in_t(w_in):
    pad = jnp.zeros((D_MODEL, V7X_LANES - MLA_ROPE_DIM), w_in.dtype)
    return jnp.concatenate([w_in, pad], axis=1).T.astype(BF16)


def _prep_w_uq_t(w_uq):
    w = w_uq.reshape(MLA_Q_RANK, MLA_HEADS, MLA_NOPE_DIM + MLA_ROPE_DIM)
    pad = jnp.zeros((MLA_Q_RANK, MLA_HEADS, MLA_QK_PAD - MLA_NOPE_DIM - MLA_ROPE_DIM), w.dtype)
    return jnp.concatenate([w, pad], axis=2).reshape(MLA_Q_RANK, MLA_HEADS * MLA_QK_PAD).T.astype(BF16)


def _prep_w_ukv_t(w_ukv):
    w = w_ukv.reshape(MLA_KV_RANK, MLA_HEADS, MLA_NOPE_DIM + MLA_V_DIM)
    k = w[:, :, :MLA_NOPE_DIM].reshape(MLA_KV_RANK, MLA_HEADS * MLA_NOPE_DIM)
    v = w[:, :, MLA_NOPE_DIM:].reshape(MLA_KV_RANK, MLA_HEADS * MLA_V_DIM)
    return jnp.concatenate([k, v], axis=1).T.astype(BF16)


def _rope_inv_freq(dim):
    return (ROPE_THETA ** (-jnp.arange(0, dim, 2, dtype=F32) / dim)).reshape(-1, 1)


def _layer(x2, mem2, pos_row, b, s, l, w_in, lam_q1, lam_k1, lam_q2, lam_k2, diff_subln_g, mla_q_norm_g,
           mla_w_uq, mla_kv_norm_g, mla_w_ukv, mla_out_norm_g, w_o, ln1_g, ln1_b, mem_w_q, mem_w_k, mem_w_v,
           mem_w_o, ln2_g, ln2_b, router_w, router_bias, exp_w_gate, exp_w_up, exp_w_down, sh_w_gate,
           sh_w_up, sh_w_down, ln3_g, ln3_b):
    row = lambda a: a.reshape(1, -1)
    col = lambda a: a.reshape(-1, 1)
    dqt, dk, dvt, mqt, mk, mvt = _proj_call(
        x2, pos_row, _rope_inv_freq(DIFF_ROT_DIM), _rope_inv_freq(MLA_ROPE_DIM), _prep_w_in_t(w_in),
        col(mla_q_norm_g), _prep_w_uq_t(mla_w_uq), col(mla_kv_norm_g), _prep_w_ukv_t(mla_w_ukv),
        tm=min(TOKEN_TILE, s))

    lam_init = 0.8 - 0.6 * math.exp(-0.3 * l)
    lam_vecs = jnp.stack([lam_q1, lam_k1, lam_q2, lam_k2]).astype(F32)
    diff_o, mla_o = _mixer_attn_call(dqt, dk, dvt, mqt, mk, mvt, lam_vecs, col(diff_subln_g), col(mla_out_norm_g),
                                     b, s, min(ATTN_QUERY_TILE, s), lam_init)

    kmem, vmem = _memkv_call(mem2, mem_w_k.astype(BF16), mem_w_v.astype(BF16))
    h2, hp = _post_call(diff_o, mla_o, x2, w_o.astype(BF16), row(ln1_g), row(ln1_b), mem_w_q.astype(BF16),
                        kmem, vmem, mem_w_o.astype(BF16), row(ln2_g), row(ln2_b), s, tm=min(TOKEN_TILE, s))

    return _moe(h2, hp, router_w, router_bias, exp_w_gate, exp_w_up, exp_w_down,
                sh_w_gate.astype(BF16), sh_w_up.astype(BF16), sh_w_down.astype(BF16), row(ln3_g), row(ln3_b),
                tm_route=min(TOKEN_TILE, s), tmx=EXPERT_ROW_TILE, tm_comb=min(COMBINE_TOKEN_TILE, s))


def kernel(x, mem, positions, w_in, lam_q1, lam_k1, lam_q2, lam_k2, diff_subln_g, mla_q_norm_g, mla_w_uq,
           mla_kv_norm_g, mla_w_ukv, mla_out_norm_g, w_o, ln1_g, ln1_b, mem_w_q, mem_w_k, mem_w_v, mem_w_o,
           ln2_g, ln2_b, router_w, router_bias, exp_w_gate, exp_w_up, exp_w_down, sh_w_gate, sh_w_up,
           sh_w_down, ln3_g, ln3_b):
    b, s, d = x.shape
    h = x.reshape(b * s, d)
    mem2 = mem.reshape(b * mem.shape[1], d)
    pos_row = positions.reshape(1, b * s)
    params = (w_in, lam_q1, lam_k1, lam_q2, lam_k2, diff_subln_g, mla_q_norm_g, mla_w_uq, mla_kv_norm_g,
              mla_w_ukv, mla_out_norm_g, w_o, ln1_g, ln1_b, mem_w_q, mem_w_k, mem_w_v, mem_w_o, ln2_g, ln2_b,
              router_w, router_bias, exp_w_gate, exp_w_up, exp_w_down, sh_w_gate, sh_w_up, sh_w_down,
              ln3_g, ln3_b)
    for l in range(w_in.shape[0]):
        h = _layer(h, mem2, pos_row, b, s, l, *[p[l] for p in params])
    return h.reshape(b, s, d)
```
